```python
import math
import jax, jax.numpy as jnp
from jax import lax
import numpy as np

D_MODEL = 1024
BATCH = 8
SEQ = 4096
DEPTH = 1

HEAD_DIM = 64
N_MOBA_HEADS = 6
N_SB_HEADS = 6
N_MEM_HEADS = 4
MOBA_WIDTH = N_MOBA_HEADS * HEAD_DIM
SB_WIDTH = N_SB_HEADS * HEAD_DIM
MEM_WIDTH = N_MEM_HEADS * HEAD_DIM
N_MEM = 256
MOBA_BLOCK = 256
MOBA_TOPK = 3
MOBA_QCHUNK = 64
SB_QBLOCK = 128
N_BUCKETS = 32
MAX_DISTANCE = 128
N_GROUPS = 4
EXPERTS_PER_GROUP = 8
N_EXPERTS = N_GROUPS * EXPERTS_PER_GROUP
EXPERT_TOPK = 2
D_EXPERT = 512
MOE_BLOCK = 256
N_BRANCH = 3
PROJ_WIDTH = 3 * MOBA_WIDTH + 3 * SB_WIDTH + MEM_WIDTH + N_BRANCH * D_MODEL
DEEPNORM_ALPHA = (2.0 * DEPTH) ** 0.25
DEEPNORM_BETA = (8.0 * DEPTH) ** -0.25
LN_EPS = 1e-5
NEG = -1e30

kernel_name = "hybrid_moba_stickbreak_memxattn_hiermoe_deepnorm"


def layer_norm(x, g, b):
    xf = x.astype(jnp.float32)
    mu = xf.mean(-1, keepdims=True)
    var = jnp.square(xf - mu).mean(-1, keepdims=True)
    return ((xf - mu) * lax.rsqrt(var + LN_EPS) * g + b).astype(x.dtype)


def t5_bucket(rel):
    rel = jnp.maximum(rel, 0)
    max_exact = N_BUCKETS // 2
    rel_f = jnp.maximum(rel, 1).astype(jnp.float32)
    large = max_exact + (jnp.log(rel_f / max_exact) / math.log(MAX_DISTANCE / max_exact)
                         * (N_BUCKETS - max_exact)).astype(jnp.int32)
    large = jnp.minimum(large, N_BUCKETS - 1)
    return jnp.where(rel < max_exact, rel, large)


def split_heads(t, n_heads):
    B, S, _ = t.shape
    return t.reshape(B, S, n_heads, HEAD_DIM).transpose(0, 2, 1, 3)


def merge_heads(t):
    B, H, S, dh = t.shape
    return t.transpose(0, 2, 1, 3).reshape(B, S, H * dh)


def moba_attention(q, k, v, rel_bias):
    B, H, S, dh = q.shape
    dtype = q.dtype
    q, k, v = (t.astype(jnp.float32) for t in (q, k, v))
    scale = dh ** -0.5
    nb = -(-S // MOBA_BLOCK)
    pad = nb * MOBA_BLOCK - S
    k = jnp.pad(k, ((0, 0), (0, 0), (0, pad), (0, 0)))
    v = jnp.pad(v, ((0, 0), (0, 0), (0, pad), (0, 0)))
    kb = k.reshape(B, H, nb, MOBA_BLOCK, dh)
    vb = v.reshape(B, H, nb, MOBA_BLOCK, dh)
    k_mean = kb.mean(axis=3)
    topk = min(MOBA_TOPK, nb - 1)
    bias_ht = rel_bias.T.astype(jnp.float32)
    blk_ar = jnp.arange(MOBA_BLOCK)
    b_idx = jnp.arange(B)[:, None, None]
    h_idx = jnp.arange(H)[None, :, None]
    h_idx4 = jnp.arange(H)[None, :, None, None]

    def partial_attn(s, vals, eq):
        m = s.max(-1)
        p = jnp.exp(s - m[..., None])
        return m, p.sum(-1), jnp.einsum(eq, p, vals)

    def chunk(args):
        c, qc = args
        start = c * MOBA_QCHUNK
        q_pos = start + jnp.arange(MOBA_QCHUNK)
        own = start // MOBA_BLOCK
        k_own = lax.dynamic_index_in_dim(kb, own, axis=2, keepdims=False)
        v_own = lax.dynamic_index_in_dim(vb, own, axis=2, keepdims=False)
        rel = q_pos[:, None] - (own * MOBA_BLOCK + blk_ar)[None, :]
        s = jnp.einsum('bhqd,bhkd->bhqk', qc, k_own) * scale + bias_ht[:, t5_bucket(rel)][None]
        s = jnp.where(rel >= 0, s, NEG)
        stats = [partial_attn(s, v_own, 'bhqk,bhkd->bhqd')]
        if topk > 0:
            gate = jnp.einsum('bhqd,bhnd->bhqn', qc, k_mean)
            gate = jnp.where(jnp.arange(nb) < own, gate, NEG)
            _, sel = lax.top_k(gate, topk)
            for r in range(topk):
                idx = sel[..., r]
                k_sel = kb[b_idx, h_idx, idx]
                v_sel = vb[b_idx, h_idx, idx]
                rel = q_pos[:, None] - (idx[..., None] * MOBA_BLOCK + blk_ar)
                s = (jnp.einsum('bhqd,bhqkd->bhqk', qc, k_sel) * scale
                     + bias_ht[h_idx4, t5_bucket(rel)])
                s = jnp.where(r < own, s, NEG)
                stats.append(partial_attn(s, v_sel, 'bhqk,bhqkd->bhqd'))
        m = jnp.stack([st[0] for st in stats])
        l = jnp.stack([st[1] for st in stats])
        o = jnp.stack([st[2] for st in stats])
        w = jnp.exp(m - m.max(0))
        return (w[..., None] * o).sum(0) / (w * l).sum(0)[..., None]

    nc = S // MOBA_QCHUNK
    qcs = q.reshape(B, H, nc, MOBA_QCHUNK, dh).transpose(2, 0, 1, 3, 4)
    out = lax.map(chunk, (jnp.arange(nc), qcs))
    return out.transpose(1, 2, 0, 3, 4).reshape(B, H, S, dh).astype(dtype)


def stick_breaking_attention(q, k, v):
    B, H, S, dh = q.shape
    dtype = q.dtype
    scale = dh ** -0.5
    kf = k.astype(jnp.float32)
    vf = v.astype(jnp.float32)
    nqb = S // SB_QBLOCK
    qbs = q.astype(jnp.float32).reshape(B, H, nqb, SB_QBLOCK, dh).transpose(2, 0, 1, 3, 4)
    k_pos = jnp.arange(S)

    def block(args):
        i, qb = args
        q_pos = i * SB_QBLOCK + jnp.arange(SB_QBLOCK)
        strict = k_pos[None, :] < q_pos[:, None]
        z = jnp.einsum('bhqd,bhkd->bhqk', qb, kf) * scale
        log_keep = jnp.where(strict, jax.nn.log_sigmoid(-z), 0.0)
        log_stick = lax.cumsum(log_keep, axis=3, reverse=True) - log_keep
        a = jnp.where(strict, jnp.exp(jax.nn.log_sigmoid(z) + log_stick), 0.0)
        return jnp.einsum('bhqk,bhkd->bhqd', a, vf)

    out = lax.map(block, (jnp.arange(nqb), qbs))
    return out.transpose(1, 2, 0, 3, 4).reshape(B, H, S, dh).astype(dtype)


def memory_attention(q, k, v):
    scale = q.shape[-1] ** -0.5
    s = jnp.einsum('bhqd,bhmd->bhqm', q.astype(jnp.float32), k.astype(jnp.float32)) * scale
    p = jax.nn.softmax(s, axis=-1)
    return jnp.einsum('bhqm,bhmd->bhqd', p, v.astype(jnp.float32)).astype(q.dtype)


def hybrid_mixer(x, mem, w_in, w_mem_kv, rel_bias, w_br_moba, w_br_sb, w_br_mem, w_out):
    sizes = (MOBA_WIDTH, MOBA_WIDTH, MOBA_WIDTH, SB_WIDTH, SB_WIDTH, SB_WIDTH, MEM_WIDTH)
    offsets = []
    acc = 0
    for sz in sizes:
        acc += sz
        offsets.append(acc)
    proj = x @ w_in
    q_a, k_a, v_a, q_b, k_b, v_b, q_m, gate_logits = jnp.split(proj, offsets, axis=-1)
    y_a = merge_heads(moba_attention(split_heads(q_a, N_MOBA_HEADS), split_heads(k_a, N_MOBA_HEADS),
                                     split_heads(v_a, N_MOBA_HEADS), rel_bias))
    y_b = merge_heads(stick_breaking_attention(split_heads(q_b, N_SB_HEADS), split_heads(k_b, N_SB_HEADS),
                                               split_heads(v_b, N_SB_HEADS)))
    k_m, v_m = jnp.split(mem @ w_mem_kv, 2, axis=-1)
    y_m = merge_heads(memory_attention(split_heads(q_m, N_MEM_HEADS), split_heads(k_m, N_MEM_HEADS),
                                       split_heads(v_m, N_MEM_HEADS)))
    B, S, _ = x.shape
    gates = jax.nn.sigmoid(gate_logits.reshape(B, S, N_BRANCH, D_MODEL))
    merged = (gates[:, :, 0] * (y_a @ w_br_moba)
              + gates[:, :, 1] * (y_b @ w_br_sb)
              + gates[:, :, 2] * (y_m @ w_br_mem))
    return merged @ w_out


def hierarchical_moe(x, w_router_group, b_router_group, w_router_expert, b_router_expert,
                     w_gate, w_up, w_down):
    B, S, D = x.shape
    xt = x.reshape(-1, D)
    N = xt.shape[0]
    g_prob = jax.nn.softmax((xt @ w_router_group + b_router_group).astype(jnp.float32), axis=-1)
    g_p, g_idx = lax.top_k(g_prob, 1)
    e_logits = (xt @ w_router_expert + b_router_expert).astype(jnp.float32)
    e_logits = e_logits.reshape(N, N_GROUPS, EXPERTS_PER_GROUP)
    e_in_group = jnp.take_along_axis(e_logits, g_idx[:, :, None], axis=1)[:, 0]
    top_l, top_i = lax.top_k(e_in_group, EXPERT_TOPK)
    gate = g_p * jax.nn.softmax(top_l, axis=-1)
    expert = g_idx * EXPERTS_PER_GROUP + top_i

    M = N * EXPERT_TOPK
    flat_e = expert.reshape(M)
    flat_tok = jnp.arange(M) // EXPERT_TOPK
    flat_gate = gate.reshape(M)
    order = jnp.argsort(flat_e)
    e_sorted = flat_e[order]
    tok_sorted = flat_tok[order]
    gate_sorted = flat_gate[order]
    counts = jnp.bincount(flat_e, length=N_EXPERTS)
    padded = (counts + MOE_BLOCK - 1) // MOE_BLOCK * MOE_BLOCK
    start = jnp.cumsum(counts) - counts
    pstart = jnp.cumsum(padded) - padded
    dest = pstart[e_sorted] + (jnp.arange(M) - start[e_sorted])
    n_blocks = -(-M // MOE_BLOCK) + N_EXPERTS
    P = n_blocks * MOE_BLOCK
    x_pad = jnp.zeros((P, D), xt.dtype).at[dest].set(xt[tok_sorted])
    block_expert = jnp.clip(jnp.searchsorted(pstart + padded, jnp.arange(n_blocks) * MOE_BLOCK, side='right'),
                            0, N_EXPERTS - 1)

    def expert_block(args):
        xb, e = args
        h = jax.nn.silu(xb @ w_gate[e]) * (xb @ w_up[e])
        return h @ w_down[e]

    y_pad = lax.map(expert_block, (x_pad.reshape(n_blocks, MOE_BLOCK, D), block_expert)).reshape(P, D)
    y_assign = y_pad[dest] * gate_sorted[:, None].astype(y_pad.dtype)
    out = jax.ops.segment_sum(y_assign, tok_sorted, num_segments=N)
    return out.reshape(B, S, D)


def setup_inputs(seed: int = 0) -> dict:
    key = jax.random.key(seed)
    ks = jax.random.split(key, 20)
    f32 = jnp.float32
    L, D = DEPTH, D_MODEL

    def nrm(k, shape, scale):
        return jax.random.normal(k, shape, f32) * scale

    return {
        "x": nrm(ks[0], (BATCH, SEQ, D), 1.0),
        "mem": nrm(ks[1], (BATCH, N_MEM, D), 1.0),
        "w_in": nrm(ks[2], (L, D, PROJ_WIDTH), D ** -0.5),
        "w_mem_kv": nrm(ks[3], (L, D, 2 * MEM_WIDTH), D ** -0.5),
        "rel_bias": nrm(ks[4], (N_BUCKETS, N_MOBA_HEADS), 0.5),
        "w_br_moba": nrm(ks[5], (L, MOBA_WIDTH, D), MOBA_WIDTH ** -0.5),
        "w_br_sb": nrm(ks[6], (L, SB_WIDTH, D), SB_WIDTH ** -0.5),
        "w_br_mem": nrm(ks[7], (L, MEM_WIDTH, D), MEM_WIDTH ** -0.5),
        "w_out": nrm(ks[8], (L, D, D), D ** -0.5 * DEEPNORM_BETA),
        "ln1_g": 1.0 + nrm(ks[9], (L, D), 0.02),
        "ln1_b": nrm(ks[10], (L, D), 0.02),
        "w_router_group": nrm(ks[11], (L, D, N_GROUPS), D ** -0.5),
        "b_router_group": nrm(ks[12], (L, N_GROUPS), 0.01),
        "w_router_expert": nrm(ks[13], (L, D, N_EXPERTS), D ** -0.5),
        "b_router_expert": nrm(ks[14], (L, N_EXPERTS), 0.01),
        "w_gate": nrm(ks[15], (L, N_EXPERTS, D, D_EXPERT), D ** -0.5),
        "w_up": nrm(ks[16], (L, N_EXPERTS, D, D_EXPERT), D ** -0.5),
        "w_down": nrm(ks[17], (L, N_EXPERTS, D_EXPERT, D), D_EXPERT ** -0.5 * DEEPNORM_BETA),
        "ln2_g": 1.0 + nrm(ks[18], (L, D), 0.02),
        "ln2_b": nrm(ks[19], (L, D), 0.02),
    }


def reference(x, mem, w_in, w_mem_kv, rel_bias, w_br_moba, w_br_sb, w_br_mem, w_out, ln1_g, ln1_b,
              w_router_group, b_router_group, w_router_expert, b_router_expert,
              w_gate, w_up, w_down, ln2_g, ln2_b):
    for l in range(DEPTH):
        mix = hybrid_mixer(x, mem, w_in[l], w_mem_kv[l], rel_bias, w_br_moba[l], w_br_sb[l],
                           w_br_mem[l], w_out[l])
        x = layer_norm(DEEPNORM_ALPHA * x + mix, ln1_g[l], ln1_b[l])
        ffn = hierarchical_moe(x, w_router_group[l], b_router_group[l], w_router_expert[l],
                               b_router_expert[l], w_gate[l], w_up[l], w_down[l])
        x = layer_norm(DEEPNORM_ALPHA * x + ffn, ln2_g[l], ln2_b[l])
    return x
```

```python
import functools
import math

import jax
import jax.numpy as jnp
from jax import lax
from jax.experimental import pallas as pl
from jax.experimental.pallas import tpu as pltpu

F32, BF16, I32 = jnp.float32, jnp.bfloat16, jnp.int32

HEAD_DIM = 64
N_MOBA_HEADS = 6
N_SB_HEADS = 6
N_MEM_HEADS = 4
MOBA_BLOCK = 256
MOBA_TOPK = 3
N_BUCKETS = 32
MAX_DISTANCE = 128
N_GROUPS = 4
EXPERTS_PER_GROUP = 8
MOE_BLOCK = 256
LN_EPS = 1e-5
NEG = -1e30

LANES = 128
VMEM_BYTES = 64 * 1024 * 1024
PAIR = LANES // HEAD_DIM

SB_ZERO_LOG = 110.0

PROJ_ROWS = 512
MERGE_ROWS = 512
ATTN_ROWS = 256
COMBINE_ROWS = 256


def _params(semantics, vmem_mb):
    return pltpu.CompilerParams(dimension_semantics=semantics,
                                vmem_limit_bytes=min(vmem_mb * 1024 * 1024, VMEM_BYTES))


def _dot(a, b):
    return jnp.dot(a, b, preferred_element_type=F32)


def _dot_nt(a, b):
    return lax.dot_general(a, b, (((1,), (1,)), ((), ())), preferred_element_type=F32)


def _layer_norm(h, g, b):
    mu = jnp.mean(h, axis=-1, keepdims=True)
    d = h - mu
    var = jnp.mean(d * d, axis=-1, keepdims=True)
    return d * lax.rsqrt(var + LN_EPS) * g + b


def _sigmoid(x):
    return 1.0 / (1.0 + jnp.exp(-x))


def _head_select(q, lane, hh, scale):
    keep = (lane >= hh * HEAD_DIM) & (lane < (hh + 1) * HEAD_DIM)
    return jnp.where(keep, q.astype(F32) * scale, 0.0).astype(BF16)


def _proj_kernel(x_ref, wq_ref, wg_ref, qkv_ref, gate_ref):
    xb = x_ref[...].astype(BF16)
    qkv_ref[...] = _dot(xb, wq_ref[...]).astype(BF16)
    gate_ref[...] = _dot(xb, wg_ref[...])


def _proj(x2, w_qkv, w_gates):
    n, d = x2.shape
    n_qkv, n_gates = w_qkv.shape[1], w_gates.shape[1]
    tm = PROJ_ROWS
    return pl.pallas_call(
        _proj_kernel,
        grid=(n // tm,),
        in_specs=[pl.BlockSpec((tm, d), lambda i: (i, 0)),
                  pl.BlockSpec((d, n_qkv), lambda i: (0, 0)),
                  pl.BlockSpec((d, n_gates), lambda i: (0, 0))],
        out_specs=[pl.BlockSpec((tm, n_qkv), lambda i: (i, 0)),
                   pl.BlockSpec((tm, n_gates), lambda i: (i, 0))],
        out_shape=[jax.ShapeDtypeStruct((n, n_qkv), BF16),
                   jax.ShapeDtypeStruct((n, n_gates), F32)],
        compiler_params=_params(("arbitrary",), 56),
        name="proj",
    )(x2, w_qkv, w_gates)


def _t5_bucket(rel):
    rel = jnp.maximum(rel, 0)
    max_exact = N_BUCKETS // 2
    rel_f = jnp.maximum(rel, 1).astype(F32)
    large = max_exact + (jnp.log(rel_f / max_exact) / math.log(MAX_DISTANCE / max_exact)
                         * (N_BUCKETS - max_exact)).astype(I32)
    large = jnp.minimum(large, N_BUCKETS - 1)
    return jnp.where(rel < max_exact, rel, large)


def _bias_table_kernel(rb_ref, o_ref):
    h = pl.program_id(0)
    blk = o_ref.shape[1]
    i = lax.broadcasted_iota(I32, (blk, 2 * blk), 0)
    j = lax.broadcasted_iota(I32, (blk, 2 * blk), 1)
    bucket = _t5_bucket(blk + i - j)
    acc = jnp.zeros((blk, 2 * blk), F32)
    for b in range(N_BUCKETS):
        acc = jnp.where(bucket == b, rb_ref[b, h], acc)
    o_ref[0] = acc


def _bias_table(rel_bias):
    n_heads = rel_bias.shape[1]
    blk = MOBA_BLOCK
    return pl.pallas_call(
        _bias_table_kernel,
        grid=(n_heads,),
        in_specs=[pl.BlockSpec(memory_space=pltpu.SMEM)],
        out_specs=pl.BlockSpec((1, blk, 2 * blk), lambda h: (h, 0, 0)),
        out_shape=jax.ShapeDtypeStruct((n_heads, blk, 2 * blk), F32),
        compiler_params=_params(("arbitrary",), 32),
        name="bias_table",
    )(rel_bias)


def _moba_kernel(rb_ref, q_ref, k_ref, v_ref, tbl_ref, o_ref, kmean_sc):
    pair = pl.program_id(1)
    own = pl.program_id(2)
    blk = q_ref.shape[1]
    nb = k_ref.shape[1] // blk
    scale = HEAD_DIM ** -0.5

    @pl.when(own == 0)
    def _():
        for n in range(nb):
            kb = k_ref[0, n * blk:(n + 1) * blk, :].astype(F32)
            kmean_sc[n:n + 1, :] = jnp.mean(kb, axis=0, keepdims=True)

    lane = lax.broadcasted_iota(I32, (blk, LANES), 1)
    col = lax.broadcasted_iota(I32, (blk, nb), 1)
    causal = (lax.broadcasted_iota(I32, (blk, blk), 1) <= lax.broadcasted_iota(I32, (blk, blk), 0))
    q = q_ref[0]
    kmean = kmean_sc[...].astype(BF16)
    valid = col < own
    outs = []
    for hh in range(PAIR):
        far = rb_ref[N_BUCKETS - 1, pair * PAIR + hh]
        qg = _head_select(q, lane, hh, 1.0)
        qs = _head_select(q, lane, hh, scale)

        gate = jnp.where(valid, _dot_nt(qg, kmean), NEG)
        rank = jnp.zeros((blk, nb), I32)
        for m in range(nb):
            gc = gate[:, m:m + 1]
            beats = (gc > gate) | ((gc == gate) & (col > m))
            rank = rank + beats.astype(I32)
        sel_bias = jnp.where(valid & (rank < MOBA_TOPK), 0.0, NEG)

        off = pl.multiple_of(own * blk, blk)
        s = _dot_nt(qs, k_ref[0, pl.ds(off, blk), :]) + tbl_ref[hh, :, blk:]
        s = jnp.where(causal, s, NEG)
        m0 = jnp.max(s, axis=1, keepdims=True)
        p = jnp.exp(s - m0)
        l0 = jnp.sum(p, axis=1, keepdims=True)
        acc0 = _dot(p.astype(BF16), v_ref[0, pl.ds(off, blk), :])

        def past_block(n, carry, bias):
            m_run, l_run, acc = carry
            offn = pl.multiple_of(n * blk, blk)
            picked = jnp.sum(jnp.where(col == n, sel_bias, 0.0), axis=1, keepdims=True)
            sn = _dot_nt(qs, k_ref[0, pl.ds(offn, blk), :]) + bias + picked
            m_new = jnp.maximum(m_run, jnp.max(sn, axis=1, keepdims=True))
            a = jnp.exp(m_run - m_new)
            pn = jnp.exp(sn - m_new)
            l_new = a * l_run + jnp.sum(pn, axis=1, keepdims=True)
            acc_new = a * acc + _dot(pn.astype(BF16), v_ref[0, pl.ds(offn, blk), :])
            return m_new, l_new, acc_new

        carry = lax.fori_loop(jnp.maximum(own - 1, 0), own,
                              lambda n, c: past_block(n, c, tbl_ref[hh, :, :blk]), (m0, l0, acc0))
        _, l_fin, acc_fin = lax.fori_loop(0, own - 1, lambda n, c: past_block(n, c, far), carry)
        outs.append(acc_fin / l_fin)
    out = outs[0]
    for hh in range(1, PAIR):
        out = jnp.where(lane >= hh * HEAD_DIM, outs[hh], out)
    o_ref[0] = out.astype(o_ref.dtype)


def _moba(qkv3, tbl, rel_bias, col0):
    bsz, seq, _ = qkv3.shape
    assert MAX_DISTANCE <= MOBA_BLOCK and seq % MOBA_BLOCK == 0
    n_pairs = N_MOBA_HEADS // PAIR
    blk = MOBA_BLOCK
    return pl.pallas_call(
        _moba_kernel,
        grid=(bsz, n_pairs, seq // blk),
        in_specs=[pl.BlockSpec(memory_space=pltpu.SMEM),
                  pl.BlockSpec((1, blk, LANES), lambda b, p, i: (b, i, col0 + p)),
                  pl.BlockSpec((1, seq, LANES), lambda b, p, i: (b, 0, col0 + n_pairs + p)),
                  pl.BlockSpec((1, seq, LANES), lambda b, p, i: (b, 0, col0 + 2 * n_pairs + p)),
                  pl.BlockSpec((PAIR, blk, 2 * blk), lambda b, p, i: (p, 0, 0))],
        out_specs=pl.BlockSpec((1, blk, LANES), lambda b, p, i: (b, i, p)),
        out_shape=jax.ShapeDtypeStruct((bsz, seq, n_pairs * LANES), BF16),
        scratch_shapes=[pltpu.VMEM((seq // blk, LANES), F32)],
        compiler_params=_params(("arbitrary", "arbitrary", "arbitrary"), 40),
        name="moba",
    )(rel_bias, qkv3, qkv3, qkv3, tbl)


def _sb_kernel(q_ref, k_ref, v_ref, o_ref):
    qi = pl.program_id(2)
    t = q_ref.shape[1]
    scale = HEAD_DIM ** -0.5
    lane = lax.broadcasted_iota(I32, (t, LANES), 1)
    r = lax.broadcasted_iota(I32, (t, t), 0)
    c = lax.broadcasted_iota(I32, (t, t), 1)
    strict = c < r
    tri = jnp.where(r >= c, 1.0, 0.0).astype(BF16)
    q = q_ref[0]

    outs = []
    for hh in range(PAIR):
        qs = _head_select(q, lane, hh, scale)

        def block(j, carry, diagonal):
            off = pl.multiple_of(j * t, t)
            z = _dot_nt(qs, k_ref[0, pl.ds(off, t), :])
            sp = jnp.maximum(z, 0.0) + jnp.log1p(jnp.exp(-jnp.abs(z)))
            if diagonal:
                sp = jnp.where(strict, sp, 0.0)
            hi = sp.astype(BF16)
            lo = (sp - hi.astype(F32)).astype(BF16)
            csum = _dot(hi, tri) + _dot(lo, tri)
            a = jnp.exp(z - (csum + carry))
            if diagonal:
                a = jnp.where(strict, a, 0.0)
            pv = _dot(a.astype(BF16), v_ref[0, pl.ds(off, t), :])
            return pv, csum[:, 0:1]

        acc0, carry0 = block(qi, jnp.zeros((t, 1), F32), True)

        def cond(state):
            j, cmin, _, _ = state
            return (j >= 0) & (cmin < SB_ZERO_LOG)

        def body(state):
            j, _, carry, acc = state
            pv, rowsum = block(j, carry, False)
            carry = carry + rowsum
            return j - 1, jnp.min(carry), carry, acc + pv

        state = lax.while_loop(cond, body, (qi - 1, jnp.min(carry0), carry0, acc0))
        outs.append(state[3])
    out = outs[0]
    for hh in range(1, PAIR):
        out = jnp.where(lane >= hh * HEAD_DIM, outs[hh], out)
    o_ref[0] = out.astype(o_ref.dtype)


def _stickbreak(qkv3, col0):
    bsz, seq, _ = qkv3.shape
    n_pairs = N_SB_HEADS // PAIR
    t = ATTN_ROWS
    return pl.pallas_call(
        _sb_kernel,
        grid=(bsz, n_pairs, seq // t),
        in_specs=[pl.BlockSpec((1, t, LANES), lambda b, p, i: (b, i, col0 + p)),
                  pl.BlockSpec((1, seq, LANES), lambda b, p, i: (b, 0, col0 + n_pairs + p)),
                  pl.BlockSpec((1, seq, LANES), lambda b, p, i: (b, 0, col0 + 2 * n_pairs + p))],
        out_specs=pl.BlockSpec((1, t, LANES), lambda b, p, i: (b, i, p)),
        out_shape=jax.ShapeDtypeStruct((bsz, seq, n_pairs * LANES), BF16),
        compiler_params=_params(("arbitrary", "arbitrary", "arbitrary"), 40),
        name="stickbreak",
    )(qkv3, qkv3, qkv3)


def _memkv_kernel(m_ref, w_ref, o_ref):
    o_ref[...] = _dot(m_ref[...].astype(BF16), w_ref[...]).astype(BF16)


def _memkv(mem2, w_bf16):
    n, d = mem2.shape
    width = w_bf16.shape[1]
    tm = min(n, 512)
    return pl.pallas_call(
        _memkv_kernel,
        grid=(n // tm,),
        in_specs=[pl.BlockSpec((tm, d), lambda i: (i, 0)),
                  pl.BlockSpec((d, width), lambda i: (0, 0))],
        out_specs=pl.BlockSpec((tm, width), lambda i: (i, 0)),
        out_shape=jax.ShapeDtypeStruct((n, width), BF16),
        compiler_params=_params(("arbitrary",), 32),
        name="memkv",
    )(mem2, w_bf16)


def _mem_kernel(q_ref, k_ref, v_ref, o_ref):
    t, width = q_ref.shape[1], q_ref.shape[2]
    scale = HEAD_DIM ** -0.5
    lane = lax.broadcasted_iota(I32, (t, width), 1)
    q = q_ref[0]
    k = k_ref[0]
    v = v_ref[0]
    out = jnp.zeros((t, width), F32)
    for h in range(width // HEAD_DIM):
        qs = _head_select(q, lane, h, scale)
        s = _dot_nt(qs, k)
        e = jnp.exp(s - jnp.max(s, axis=1, keepdims=True))
        p = e / jnp.sum(e, axis=1, keepdims=True)
        pv = _dot(p.astype(BF16), v)
        out = jnp.where((lane >= h * HEAD_DIM) & (lane < (h + 1) * HEAD_DIM), pv, out)
    o_ref[0] = out.astype(o_ref.dtype)


def _mem_attention(qkv3, kv3, qcol):
    bsz, seq, _ = qkv3.shape
    n_mem = kv3.shape[1]
    width = N_MEM_HEADS * HEAD_DIM
    t = ATTN_ROWS
    return pl.pallas_call(
        _mem_kernel,
        grid=(bsz, seq // t),
        in_specs=[pl.BlockSpec((1, t, width), lambda b, i: (b, i, qcol)),
                  pl.BlockSpec((1, n_mem, width), lambda b, i: (b, 0, 0)),
                  pl.BlockSpec((1, n_mem, width), lambda b, i: (b, 0, 1))],
        out_specs=pl.BlockSpec((1, t, width), lambda b, i: (b, i, 0)),
        out_shape=jax.ShapeDtypeStruct((bsz, seq, width), BF16),
        compiler_params=_params(("arbitrary", "arbitrary"), 32),
        name="mem_attention",
    )(qkv3, kv3, kv3)


def _merge_kernel(ya_ref, yb_ref, ym_ref, g_ref, x_ref, wa_ref, wb_ref, wm_ref, wo_ref,
                  lng_ref, lnb_ref, wrh_ref, wrl_ref, br_ref, x1_ref, route_ref, cnt_ref,
                  *, alpha, n_experts):
    step = pl.program_id(0)
    tm, d = x_ref.shape

    @pl.when(step == 0)
    def _():
        cnt_ref[...] = jnp.zeros(cnt_ref.shape, F32)

    merged = (_sigmoid(g_ref[:, 0:d]) * _dot(ya_ref[...], wa_ref[...])
              + _sigmoid(g_ref[:, d:2 * d]) * _dot(yb_ref[...], wb_ref[...])
              + _sigmoid(g_ref[:, 2 * d:3 * d]) * _dot(ym_ref[...], wm_ref[...]))
    mix = _dot(merged.astype(BF16), wo_ref[...])
    x1 = _layer_norm(alpha * x_ref[...] + mix, lng_ref[...], lnb_ref[...])
    x1_ref[...] = x1

    xh = x1.astype(BF16)
    xl = (x1 - xh.astype(F32)).astype(BF16)
    logits = (_dot(xh, wrh_ref[...]) + (_dot(xh, wrl_ref[...]) + _dot(xl, wrh_ref[...]))
              + br_ref[...])

    col = lax.broadcasted_iota(I32, (tm, LANES), 1)
    colf = col.astype(F32)
    big = float(LANES)
    gmask = (col >= n_experts) & (col < n_experts + N_GROUPS)
    lg = jnp.where(gmask, logits, -jnp.inf)
    gmax = jnp.max(lg, axis=1, keepdims=True)
    gidx = jnp.min(jnp.where(lg == gmax, colf, big), axis=1, keepdims=True) - n_experts
    g_p = 1.0 / jnp.sum(jnp.where(gmask, jnp.exp(logits - gmax), 0.0), axis=1, keepdims=True)
    lo_col = gidx * EXPERTS_PER_GROUP
    emask = (colf >= lo_col) & (colf < lo_col + EXPERTS_PER_GROUP)
    le = jnp.where(emask, logits, -jnp.inf)
    l1 = jnp.max(le, axis=1, keepdims=True)
    i1 = jnp.min(jnp.where(le == l1, colf, big), axis=1, keepdims=True)
    le2 = jnp.where(colf == i1, -jnp.inf, le)
    l2 = jnp.max(le2, axis=1, keepdims=True)
    i2 = jnp.min(jnp.where(le2 == l2, colf, big), axis=1, keepdims=True)
    e2 = jnp.exp(l2 - l1)
    gate1 = g_p * (1.0 / (1.0 + e2))
    gate2 = g_p * (e2 / (1.0 + e2))

    oh1 = colf == i1
    oh2 = colf == i2
    cnt = jnp.where(oh1 | oh2, 1.0, 0.0)
    rr = lax.broadcasted_iota(I32, (tm, tm), 0)
    cc = lax.broadcasted_iota(I32, (tm, tm), 1)
    before = jnp.where(cc < rr, 1.0, 0.0).astype(BF16)
    base = _dot(before, cnt.astype(BF16)) + cnt_ref[0:1, :]
    rank1 = jnp.sum(jnp.where(oh1, base, 0.0), axis=1, keepdims=True)
    rank2 = jnp.sum(jnp.where(oh2, base, 0.0), axis=1, keepdims=True)
    cnt_ref[...] = cnt_ref[...] + jnp.sum(cnt, axis=0, keepdims=True)

    route = jnp.zeros((tm, LANES), F32)
    for k, val in enumerate((i1, i2, rank1, rank2, gate1, gate2)):
        route = jnp.where(col == k, val, route)
    route_ref[...] = route


def _merge(ya, yb, ym, gates, x2, wa, wb, wm, wo, ln_g, ln_b, wr_hi, wr_lo, b_r, alpha, n_experts):
    n, d = x2.shape
    tm = MERGE_ROWS
    row = lambda w: pl.BlockSpec((tm, w), lambda i: (i, 0))
    full = lambda a: pl.BlockSpec(a.shape, lambda i: (0,) * a.ndim)
    return pl.pallas_call(
        functools.partial(_merge_kernel, alpha=alpha, n_experts=n_experts),
        grid=(n // tm,),
        in_specs=[row(ya.shape[1]), row(yb.shape[1]), row(ym.shape[1]), row(gates.shape[1]), row(d),
                  full(wa), full(wb), full(wm), full(wo), full(ln_g), full(ln_b),
                  full(wr_hi), full(wr_lo), full(b_r)],
        out_specs=[row(d), row(LANES), pl.BlockSpec((8, LANES), lambda i: (0, 0))],
        out_shape=[jax.ShapeDtypeStruct((n, d), F32),
                   jax.ShapeDtypeStruct((n, LANES), F32),
                   jax.ShapeDtypeStruct((8, LANES), F32)],
        compiler_params=_params(("arbitrary",), 52),
        name="merge",
    )(ya, yb, ym, gates, x2, wa, wb, wm, wo, ln_g, ln_b, wr_hi, wr_lo, b_r)


def _gather_rows(src_hbm, idx_ref, buf, sem, n_rows):
    def issue(r, _):
        pltpu.make_async_copy(src_hbm.at[pl.ds(idx_ref[0, 0, r], 1), :],
                              buf.at[pl.ds(r, 1), :], sem).start()
        return 0
    lax.fori_loop(0, n_rows, issue, 0)


def _wait_rows(src_hbm, buf, sem, n_rows):
    def wait(r, _):
        pltpu.make_async_copy(src_hbm.at[pl.ds(0, 1), :], buf.at[pl.ds(r, 1), :], sem).wait()
        return 0
    lax.fori_loop(0, n_rows, wait, 0)


def _experts_kernel(be_ref, nu_ref, idx_ref, idxn_ref, x_hbm, wg_ref, wu_ref, wd_ref, o_ref,
                    xbuf, sem, wg_sc, wu_sc, wd_sc):
    b = pl.program_id(0)
    n_used = nu_ref[0]
    rows = o_ref.shape[0]
    slot = lax.rem(b, 2)

    @pl.when((b == 0) & (n_used > 0))
    def _():
        _gather_rows(x_hbm, idx_ref, xbuf.at[0], sem.at[0], rows)

    @pl.when(b + 1 < n_used)
    def _():
        _gather_rows(x_hbm, idxn_ref, xbuf.at[1 - slot], sem.at[1 - slot], rows)

    @pl.when(b < n_used)
    def _():
        changed = (b == 0) | (be_ref[b] != be_ref[jnp.maximum(b - 1, 0)])

        @pl.when(changed)
        def _():
            wg_sc[...] = wg_ref[0].astype(BF16)
            wu_sc[...] = wu_ref[0].astype(BF16)
            wd_sc[...] = wd_ref[0].astype(BF16)

        _wait_rows(x_hbm, xbuf.at[slot], sem.at[slot], rows)
        xb = xbuf[slot].astype(BF16)
        g = _dot(xb, wg_sc[...])
        u = _dot(xb, wu_sc[...])
        h = (g * _sigmoid(g) * u).astype(BF16)
        o_ref[...] = _dot(h, wd_sc[...])

    @pl.when(b >= n_used)
    def _():
        o_ref[...] = jnp.zeros(o_ref.shape, F32)


def _experts(x1, tok_of_row, block_expert, n_used, w_gate, w_up, w_down):
    n, d = x1.shape
    n_blocks = block_expert.shape[0]
    rows = MOE_BLOCK
    d_exp = w_gate.shape[-1]
    idx3 = tok_of_row.reshape(n_blocks, 1, rows)
    smem_idx = lambda f: pl.BlockSpec((1, 1, rows), f, memory_space=pltpu.SMEM)
    grid_spec = pltpu.PrefetchScalarGridSpec(
        num_scalar_prefetch=2,
        grid=(n_blocks,),
        in_specs=[smem_idx(lambda b, be, nu: (b, 0, 0)),
                  smem_idx(lambda b, be, nu: (jnp.minimum(b + 1, n_blocks - 1), 0, 0)),
                  pl.BlockSpec(memory_space=pl.ANY),
                  pl.BlockSpec((1, d, d_exp), lambda b, be, nu: (be[b], 0, 0)),
                  pl.BlockSpec((1, d, d_exp), lambda b, be, nu: (be[b], 0, 0)),
                  pl.BlockSpec((1, d_exp, d), lambda b, be, nu: (be[b], 0, 0))],
        out_specs=pl.BlockSpec((rows, d), lambda b, be, nu: (b, 0)),
        scratch_shapes=[pltpu.VMEM((2, rows, d), F32),
                        pltpu.SemaphoreType.DMA((2,)),
                        pltpu.VMEM((d, d_exp), BF16),
                        pltpu.VMEM((d, d_exp), BF16),
                        pltpu.VMEM((d_exp, d), BF16)])
    return pl.pallas_call(
        _experts_kernel,
        grid_spec=grid_spec,
        out_shape=jax.ShapeDtypeStruct((n_blocks * rows, d), F32),
        compiler_params=_params(("arbitrary",), 44),
        name="experts",
    )(block_expert, n_used, idx3, idx3, x1, w_gate, w_up, w_down)


def _combine_kernel(idx_ref, idxn_ref, y_hbm, x1_ref, route_ref, lng_ref, lnb_ref, o_ref,
                    ybuf, sem, *, alpha):
    i = pl.program_id(0)
    n_steps = pl.num_programs(0)
    tm = x1_ref.shape[0]
    slot = lax.rem(i, 2)

    @pl.when(i == 0)
    def _():
        _gather_rows(y_hbm, idx_ref, ybuf.at[0], sem.at[0], 2 * tm)

    @pl.when(i + 1 < n_steps)
    def _():
        _gather_rows(y_hbm, idxn_ref, ybuf.at[1 - slot], sem.at[1 - slot], 2 * tm)

    _wait_rows(y_hbm, ybuf.at[slot], sem.at[slot], 2 * tm)
    ffn = route_ref[:, 4:5] * ybuf[slot, 0:tm, :] + route_ref[:, 5:6] * ybuf[slot, tm:2 * tm, :]
    o_ref[...] = _layer_norm(alpha * x1_ref[...] + ffn, lng_ref[...], lnb_ref[...])


def _combine(y_pad, dest, x1, route, ln_g, ln_b, alpha):
    n, d = x1.shape
    tm = COMBINE_ROWS
    n_steps = n // tm
    idx3 = dest.reshape(n_steps, tm, 2).transpose(0, 2, 1).reshape(n_steps, 1, 2 * tm)
    smem_idx = lambda f: pl.BlockSpec((1, 1, 2 * tm), f, memory_space=pltpu.SMEM)
    return pl.pallas_call(
        functools.partial(_combine_kernel, alpha=alpha),
        grid=(n_steps,),
        in_specs=[smem_idx(lambda i: (i, 0, 0)),
                  smem_idx(lambda i: (jnp.minimum(i + 1, n_steps - 1), 0, 0)),
                  pl.BlockSpec(memory_space=pl.ANY),
                  pl.BlockSpec((tm, d), lambda i: (i, 0)),
                  pl.BlockSpec((tm, LANES), lambda i: (i, 0)),
                  pl.BlockSpec((1, d), lambda i: (0, 0)),
                  pl.BlockSpec((1, d), lambda i: (0, 0))],
        out_specs=pl.BlockSpec((tm, d), lambda i: (i, 0)),
        out_shape=jax.ShapeDtypeStruct((n, d), F32),
        scratch_shapes=[pltpu.VMEM((2, 2 * tm, d), F32), pltpu.SemaphoreType.DMA((2,))],
        compiler_params=_params(("arbitrary",), 32),
        name="combine",
    )(idx3, idx3, y_pad, x1, route, ln_g, ln_b)


def _split_bf16(w):
    hi = w.astype(BF16)
    return hi, (w - hi.astype(F32)).astype(BF16)


def _layer(x, mem, tbl, rel_bias, w_in, w_mem_kv, w_br_moba, w_br_sb, w_br_mem, w_out, ln1_g, ln1_b,
           w_rg, b_rg, w_re, b_re, w_gate, w_up, w_down, ln2_g, ln2_b, alpha):
    bsz, seq, d = x.shape
    n = bsz * seq
    n_experts = w_re.shape[1]
    moba_w, sb_w, mem_w = N_MOBA_HEADS * HEAD_DIM, N_SB_HEADS * HEAD_DIM, N_MEM_HEADS * HEAD_DIM
    n_qkv = 3 * moba_w + 3 * sb_w + mem_w
    assert w_in.shape[1] == n_qkv + 3 * d and n_experts + N_GROUPS <= LANES

    x2 = x.reshape(n, d)
    qkv, gates = _proj(x2, w_in[:, :n_qkv].astype(BF16), w_in[:, n_qkv:].astype(BF16))
    qkv3 = qkv.reshape(bsz, seq, n_qkv)
    y_a = _moba(qkv3, tbl, rel_bias, 0)
    y_b = _stickbreak(qkv3, 3 * moba_w // LANES)
    kv = _memkv(mem.reshape(-1, d), w_mem_kv.astype(BF16)).reshape(bsz, mem.shape[1], 2 * mem_w)
    y_m = _mem_attention(qkv3, kv, (3 * moba_w + 3 * sb_w) // mem_w)

    w_r = jnp.zeros((d, LANES), F32).at[:, :n_experts].set(w_re).at[:, n_experts:n_experts + N_GROUPS].set(w_rg)
    b_r = jnp.zeros((1, LANES), F32).at[0, :n_experts].set(b_re).at[0, n_experts:n_experts + N_GROUPS].set(b_rg)
    wr_hi, wr_lo = _split_bf16(w_r)
    x1, route, counts = _merge(
        y_a.reshape(n, moba_w), y_b.reshape(n, sb_w), y_m.reshape(n, mem_w), gates, x2,
        w_br_moba.astype(BF16), w_br_sb.astype(BF16), w_br_mem.astype(BF16), w_out.astype(BF16),
        ln1_g.reshape(1, d), ln1_b.reshape(1, d), wr_hi, wr_lo, b_r, alpha, n_experts)

    rows = MOE_BLOCK
    expert = route[:, 0:2].astype(I32)
    rank = route[:, 2:4].astype(I32)
    cnt = counts[0, :n_experts].astype(I32)
    padded = (cnt + rows - 1) // rows * rows
    pend = jnp.cumsum(padded)
    pstart = pend - padded
    dest = pstart[expert] + rank
    n_blocks = (2 * n) // rows + n_experts
    tok = jnp.broadcast_to(jnp.arange(n, dtype=I32)[:, None], (n, 2))
    tok_of_row = jnp.zeros((n_blocks * rows,), I32).at[dest.reshape(-1)].set(tok.reshape(-1))
    block_expert = jnp.clip(jnp.searchsorted(pend, jnp.arange(n_blocks, dtype=I32) * rows, side='right'),
                            0, n_experts - 1).astype(I32)
    n_used = (pend[-1] // rows).astype(I32).reshape(1)

    y_pad = _experts(x1, tok_of_row, block_expert, n_used, w_gate, w_up, w_down)
    out = _combine(y_pad, dest, x1, route, ln2_g.reshape(1, d), ln2_b.reshape(1, d), alpha)
    return out.reshape(bsz, seq, d)


def kernel(x, mem, w_in, w_mem_kv, rel_bias, w_br_moba, w_br_sb, w_br_mem, w_out, ln1_g, ln1_b,
           w_router_group, b_router_group, w_router_expert, b_router_expert,
           w_gate, w_up, w_down, ln2_g, ln2_b):
    depth = w_in.shape[0]
    alpha = (2.0 * depth) ** 0.25
    tbl = _bias_table(rel_bias)
    for l in range(depth):
        x = _layer(x, mem, tbl, rel_bias, w_in[l], w_mem_kv[l], w_br_moba[l], w_br_sb[l], w_br_mem[l],
                   w_out[l], ln1_g[l], ln1_b[l], w_router_group[l], b_router_group[l],
                   w_router_expert[l], b_router_expert[l], w_gate[l], w_up[l], w_down[l],
                   ln2_g[l], ln2_b[l], alpha)
    return x
```

```python
import functools
import math

import jax
import jax.numpy as jnp
from jax import lax
from jax.experimental import pallas as pl
from jax.experimental.pallas import tpu as pltpu

F32, BF16, I32 = jnp.float32, jnp.bfloat16, jnp.int32

HEAD_DIM = 64
N_MOBA_HEADS = 6
N_SB_HEADS = 6
N_MEM_HEADS = 4
MOBA_BLOCK = 256
MOBA_TOPK = 3
N_BUCKETS = 32
MAX_DISTANCE = 128
N_GROUPS = 4
EXPERTS_PER_GROUP = 8
MOE_BLOCK = 256
LN_EPS = 1e-5
NEG = -1e30

LANES = 128
VMEM_BYTES = 64 * 1024 * 1024
PAIR = LANES // HEAD_DIM

SB_ZERO_LOG = 110.0

PROJ_ROWS = 512
MERGE_ROWS = 512
ATTN_ROWS = 256
COMBINE_ROWS = 256


def _params(semantics, vmem_mb):
    return pltpu.CompilerParams(dimension_semantics=semantics,
                                vmem_limit_bytes=min(vmem_mb * 1024 * 1024, VMEM_BYTES))


def _dot(a, b):
    return jnp.dot(a, b, preferred_element_type=F32)


def _dot_nt(a, b):
    return lax.dot_general(a, b, (((1,), (1,)), ((), ())), preferred_element_type=F32)


def _layer_norm(h, g, b):
    mu = jnp.mean(h, axis=-1, keepdims=True)
    d = h - mu
    var = jnp.mean(d * d, axis=-1, keepdims=True)
    return d * lax.rsqrt(var + LN_EPS) * g + b


def _sigmoid(x):
    return 1.0 / (1.0 + jnp.exp(-x))


def _head_select(q, lane, hh, scale):
    keep = (lane >= hh * HEAD_DIM) & (lane < (hh + 1) * HEAD_DIM)
    return jnp.where(keep, q.astype(F32) * scale, 0.0).astype(BF16)


def _head_rows(qt, row, hh, scale):
    keep = (row >= hh * HEAD_DIM) & (row < (hh + 1) * HEAD_DIM)
    return jnp.where(keep, qt * scale, 0.0).astype(BF16)


def _transposed_bf16(x):
    return x.astype(F32).T.astype(BF16)


def _proj_kernel(x_ref, wq_ref, wg_ref, qkv_ref, gate_ref):
    xb = x_ref[...].astype(BF16)
    qkv_ref[...] = _dot(xb, wq_ref[...]).astype(BF16)
    gate_ref[...] = _dot(xb, wg_ref[...])


def _proj(x2, w_qkv, w_gates):
    n, d = x2.shape
    n_qkv, n_gates = w_qkv.shape[1], w_gates.shape[1]
    tm = PROJ_ROWS
    return pl.pallas_call(
        _proj_kernel,
        grid=(n // tm,),
        in_specs=[pl.BlockSpec((tm, d), lambda i: (i, 0)),
                  pl.BlockSpec((d, n_qkv), lambda i: (0, 0)),
                  pl.BlockSpec((d, n_gates), lambda i: (0, 0))],
        out_specs=[pl.BlockSpec((tm, n_qkv), lambda i: (i, 0)),
                   pl.BlockSpec((tm, n_gates), lambda i: (i, 0))],
        out_shape=[jax.ShapeDtypeStruct((n, n_qkv), BF16),
                   jax.ShapeDtypeStruct((n, n_gates), F32)],
        compiler_params=_params(("arbitrary",), 56),
        name="proj",
    )(x2, w_qkv, w_gates)


def _t5_bucket(rel):
    rel = jnp.maximum(rel, 0)
    max_exact = N_BUCKETS // 2
    rel_f = jnp.maximum(rel, 1).astype(F32)
    large = max_exact + (jnp.log(rel_f / max_exact) / math.log(MAX_DISTANCE / max_exact)
                         * (N_BUCKETS - max_exact)).astype(I32)
    large = jnp.minimum(large, N_BUCKETS - 1)
    return jnp.where(rel < max_exact, rel, large)


def _bias_table_kernel(rb_ref, o_ref):
    h = pl.program_id(0)
    blk = o_ref.shape[2]
    j = lax.broadcasted_iota(I32, (2 * blk, blk), 0)
    i = lax.broadcasted_iota(I32, (2 * blk, blk), 1)
    bucket = _t5_bucket(blk + i - j)
    acc = jnp.zeros((2 * blk, blk), F32)
    for b in range(N_BUCKETS):
        acc = jnp.where(bucket == b, rb_ref[b, h], acc)
    o_ref[0] = acc


def _bias_table(rel_bias):
    n_heads = rel_bias.shape[1]
    blk = MOBA_BLOCK
    return pl.pallas_call(
        _bias_table_kernel,
        grid=(n_heads,),
        in_specs=[pl.BlockSpec(memory_space=pltpu.SMEM)],
        out_specs=pl.BlockSpec((1, 2 * blk, blk), lambda h: (h, 0, 0)),
        out_shape=jax.ShapeDtypeStruct((n_heads, 2 * blk, blk), F32),
        compiler_params=_params(("arbitrary",), 32),
        name="bias_table",
    )(rel_bias)


def _fold_keys(x, op, final):
    while x.shape[0] > 8 and x.shape[0] % 2 == 0:
        half = x.shape[0] // 2
        x = op(x[:half], x[half:])
    return final(x, axis=0, keepdims=True)


def _head_queries(qt, row, hh, scale):
    p, sub = divmod(hh, PAIR)
    return _head_rows(qt[p * LANES:(p + 1) * LANES, :], row, sub, scale)


def _pair_lanes(hh):
    p = hh // PAIR
    return slice(p * LANES, (p + 1) * LANES)


def _head_dims(hh):
    return slice(hh * HEAD_DIM, (hh + 1) * HEAD_DIM)


def _moba_kernel(rb_ref, q_ref, k_ref, v_ref, tbl_ref, o_ref, kmean_sc, vt_sc, sel_sc):
    own = pl.program_id(1)
    blk, width = q_ref.shape[1], q_ref.shape[2]
    n_heads = width // HEAD_DIM
    nb = k_ref.shape[1] // blk
    scale = HEAD_DIM ** -0.5

    @pl.when(own == 0)
    def _():
        for n in range(nb):
            kb = k_ref[0, n * blk:(n + 1) * blk, :].astype(F32)
            kmean_sc[n:n + 1, :] = jnp.mean(kb, axis=0, keepdims=True)
            vt_sc[n] = _transposed_bf16(v_ref[0, n * blk:(n + 1) * blk, :])

    qt = q_ref[0].astype(F32).T
    row = lax.broadcasted_iota(I32, (LANES, blk), 0)
    blk_id = lax.broadcasted_iota(I32, (nb, blk), 0)
    valid = blk_id < own
    causal = (lax.broadcasted_iota(I32, (blk, blk), 0) <= lax.broadcasted_iota(I32, (blk, blk), 1))
    kmean = kmean_sc[...].astype(BF16)

    qs = [_head_queries(qt, row, hh, scale) for hh in range(n_heads)]
    gates = [_dot(kmean[:, _pair_lanes(hh)], _head_queries(qt, row, hh, 1.0)) for hh in range(n_heads)]

    def scores(n):
        offn = pl.multiple_of(n * blk, blk)
        return [_dot(k_ref[0, pl.ds(offn, blk), _pair_lanes(hh)], qs[hh]) for hh in range(n_heads)]

    own_scores = scores(own)

    for hh in range(n_heads):
        gate = jnp.where(valid, gates[hh], NEG)
        rank = jnp.zeros((nb, blk), I32)
        for m in range(nb):
            gm = gate[m:m + 1, :]
            beats = (gm > gate) | ((gm == gate) & (blk_id > m))
            rank = rank + beats.astype(I32)
        sel_sc[hh] = jnp.where(valid & (rank < MOBA_TOPK), 0.0, NEG)

    stats = []
    for hh, s in enumerate(own_scores):
        s = jnp.where(causal, s + tbl_ref[hh, blk:, :], NEG)
        m0 = _fold_keys(s, jnp.maximum, jnp.max)
        p = jnp.exp(s - m0)
        stats.append((m0, _fold_keys(p, jnp.add, jnp.sum), p.astype(BF16)))
    state = []
    for hh, (m0, l0, p) in enumerate(stats):
        state += [m0, l0, _dot(vt_sc[own, _head_dims(hh), :], p)]

    def past_block(n, state, previous):
        stats = []
        for hh, s in enumerate(scores(n)):
            m_run, l_run, _ = state[3 * hh:3 * hh + 3]
            bias = tbl_ref[hh, :blk, :] if previous else rb_ref[N_BUCKETS - 1, hh]
            sn = s + bias + sel_sc[hh, pl.ds(n, 1), :]
            m_new = jnp.maximum(m_run, _fold_keys(sn, jnp.maximum, jnp.max))
            a = jnp.exp(m_run - m_new)
            pn = jnp.exp(sn - m_new)
            stats.append((m_new, a * l_run + _fold_keys(pn, jnp.add, jnp.sum), a, pn.astype(BF16)))
        new = []
        for hh, (m_new, l_new, a, pn) in enumerate(stats):
            new += [m_new, l_new, a * state[3 * hh + 2] + _dot(vt_sc[n, _head_dims(hh), :], pn)]
        return tuple(new)

    state = lax.fori_loop(jnp.maximum(own - 1, 0), own, lambda n, c: past_block(n, c, True), tuple(state))
    state = lax.fori_loop(0, own - 1, lambda n, c: past_block(n, c, False), state)
    out_t = jnp.concatenate([state[3 * hh + 2] / state[3 * hh + 1] for hh in range(n_heads)], axis=0)
    o_ref[0] = out_t.T.astype(o_ref.dtype)


def _moba(qkv3, tbl, rel_bias, col0):
    bsz, seq, _ = qkv3.shape
    assert MAX_DISTANCE <= MOBA_BLOCK and seq % MOBA_BLOCK == 0
    width = N_MOBA_HEADS * HEAD_DIM
    blk = MOBA_BLOCK
    nb = seq // blk
    return pl.pallas_call(
        _moba_kernel,
        grid=(bsz, nb),
        in_specs=[pl.BlockSpec(memory_space=pltpu.SMEM),
                  pl.BlockSpec((1, blk, width), lambda b, i: (b, i, col0)),
                  pl.BlockSpec((1, seq, width), lambda b, i: (b, 0, col0 + 1)),
                  pl.BlockSpec((1, seq, width), lambda b, i: (b, 0, col0 + 2)),
                  pl.BlockSpec((N_MOBA_HEADS, 2 * blk, blk), lambda b, i: (0, 0, 0))],
        out_specs=pl.BlockSpec((1, blk, width), lambda b, i: (b, i, 0)),
        out_shape=jax.ShapeDtypeStruct((bsz, seq, width), BF16),
        scratch_shapes=[pltpu.VMEM((nb, width), F32),
                        pltpu.VMEM((nb, width, blk), BF16),
                        pltpu.VMEM((N_MOBA_HEADS, nb, blk), F32)],
        compiler_params=_params(("arbitrary", "arbitrary"), 48),
        name="moba",
    )(rel_bias, qkv3, qkv3, qkv3, tbl)


def _sb_kernel(q_ref, k_ref, v_ref, o_ref, vt_sc):
    qi = pl.program_id(1)
    t, width = q_ref.shape[1], q_ref.shape[2]
    n_heads = width // HEAD_DIM
    nb = k_ref.shape[1] // t
    scale = HEAD_DIM ** -0.5

    @pl.when(qi == 0)
    def _():
        for n in range(nb):
            vt_sc[n] = _transposed_bf16(v_ref[0, n * t:(n + 1) * t, :])

    qt = q_ref[0].astype(F32).T
    row = lax.broadcasted_iota(I32, (LANES, t), 0)
    key = lax.broadcasted_iota(I32, (t, t), 0)
    qry = lax.broadcasted_iota(I32, (t, t), 1)
    strict = key < qry
    tri = jnp.where(qry >= key, 1.0, 0.0).astype(BF16)
    qs = [_head_queries(qt, row, hh, scale) for hh in range(n_heads)]

    def block(j, carries, diagonal):
        off = pl.multiple_of(j * t, t)
        zs = [_dot(k_ref[0, pl.ds(off, t), _pair_lanes(hh)], qs[hh]) for hh in range(n_heads)]
        csums = []
        for z in zs:
            sp = jnp.maximum(z, 0.0) + jnp.log1p(jnp.exp(-jnp.abs(z)))
            if diagonal:
                sp = jnp.where(strict, sp, 0.0)
            hi = sp.astype(BF16)
            lo = (sp - hi.astype(F32)).astype(BF16)
            csums.append(_dot(tri, hi) + _dot(tri, lo))
        out = []
        for hh in range(n_heads):
            a = jnp.exp(zs[hh] - (csums[hh] + carries[hh]))
            if diagonal:
                a = jnp.where(strict, a, 0.0)
            pv = _dot(vt_sc[j, _head_dims(hh), :], a.astype(BF16))
            out.append((pv, csums[hh][0:1, :]))
        return out

    zero = jnp.zeros((1, t), F32)
    first = block(qi, [zero] * n_heads, True)
    accs = tuple(pv for pv, _ in first)
    carries = tuple(total for _, total in first)

    def lowest(carries):
        return jnp.min(functools.reduce(jnp.minimum, carries))

    def cond(state):
        j, cmin, _, _ = state
        return (j >= 0) & (cmin < SB_ZERO_LOG)

    def body(state):
        j, _, carries, accs = state
        res = block(j, carries, False)
        carries = tuple(c + total for c, (_, total) in zip(carries, res))
        accs = tuple(acc + pv for acc, (pv, _) in zip(accs, res))
        return j - 1, lowest(carries), carries, accs

    state = lax.while_loop(cond, body, (qi - 1, lowest(carries), carries, accs))
    out_t = jnp.concatenate(list(state[3]), axis=0)
    o_ref[0] = out_t.T.astype(o_ref.dtype)


def _stickbreak(qkv3, col0):
    bsz, seq, _ = qkv3.shape
    width = N_SB_HEADS * HEAD_DIM
    t = ATTN_ROWS
    return pl.pallas_call(
        _sb_kernel,
        grid=(bsz, seq // t),
        in_specs=[pl.BlockSpec((1, t, width), lambda b, i: (b, i, col0)),
                  pl.BlockSpec((1, seq, width), lambda b, i: (b, 0, col0 + 1)),
                  pl.BlockSpec((1, seq, width), lambda b, i: (b, 0, col0 + 2))],
        out_specs=pl.BlockSpec((1, t, width), lambda b, i: (b, i, 0)),
        out_shape=jax.ShapeDtypeStruct((bsz, seq, width), BF16),
        scratch_shapes=[pltpu.VMEM((seq // t, width, t), BF16)],
        compiler_params=_params(("arbitrary", "arbitrary"), 48),
        name="stickbreak",
    )(qkv3, qkv3, qkv3)


def _memkv_kernel(m_ref, w_ref, o_ref):
    o_ref[...] = _dot(m_ref[...].astype(BF16), w_ref[...]).astype(BF16)


def _memkv(mem2, w_bf16):
    n, d = mem2.shape
    width = w_bf16.shape[1]
    tm = min(n, 512)
    return pl.pallas_call(
        _memkv_kernel,
        grid=(n // tm,),
        in_specs=[pl.BlockSpec((tm, d), lambda i: (i, 0)),
                  pl.BlockSpec((d, width), lambda i: (0, 0))],
        out_specs=pl.BlockSpec((tm, width), lambda i: (i, 0)),
        out_shape=jax.ShapeDtypeStruct((n, width), BF16),
        compiler_params=_params(("arbitrary",), 32),
        name="memkv",
    )(mem2, w_bf16)


def _mem_kernel(q_ref, k_ref, v_ref, o_ref):
    t, width = q_ref.shape[1], q_ref.shape[2]
    scale = HEAD_DIM ** -0.5
    lane = lax.broadcasted_iota(I32, (t, width), 1)
    q = q_ref[0]
    k = k_ref[0]
    v = v_ref[0]
    out = jnp.zeros((t, width), F32)
    for h in range(width // HEAD_DIM):
        qs = _head_select(q, lane, h, scale)
        s = _dot_nt(qs, k)
        e = jnp.exp(s - jnp.max(s, axis=1, keepdims=True))
        p = e / jnp.sum(e, axis=1, keepdims=True)
        pv = _dot(p.astype(BF16), v)
        out = jnp.where((lane >= h * HEAD_DIM) & (lane < (h + 1) * HEAD_DIM), pv, out)
    o_ref[0] = out.astype(o_ref.dtype)


def _mem_attention(qkv3, kv3, qcol):
    bsz, seq, _ = qkv3.shape
    n_mem = kv3.shape[1]
    width = N_MEM_HEADS * HEAD_DIM
    t = ATTN_ROWS
    return pl.pallas_call(
        _mem_kernel,
        grid=(bsz, seq // t),
        in_specs=[pl.BlockSpec((1, t, width), lambda b, i: (b, i, qcol)),
                  pl.BlockSpec((1, n_mem, width), lambda b, i: (b, 0, 0)),
                  pl.BlockSpec((1, n_mem, width), lambda b, i: (b, 0, 1))],
        out_specs=pl.BlockSpec((1, t, width), lambda b, i: (b, i, 0)),
        out_shape=jax.ShapeDtypeStruct((bsz, seq, width), BF16),
        compiler_params=_params(("arbitrary", "arbitrary"), 32),
        name="mem_attention",
    )(qkv3, kv3, kv3)


def _merge_kernel(ya_ref, yb_ref, ym_ref, g_ref, x_ref, wa_ref, wb_ref, wm_ref, wo_ref,
                  lng_ref, lnb_ref, wrh_ref, wrl_ref, br_ref, x1_ref, route_ref, cnt_ref,
                  *, alpha, n_experts):
    step = pl.program_id(0)
    tm, d = x_ref.shape

    @pl.when(step == 0)
    def _():
        cnt_ref[...] = jnp.zeros(cnt_ref.shape, F32)

    merged = (_sigmoid(g_ref[:, 0:d]) * _dot(ya_ref[...], wa_ref[...])
              + _sigmoid(g_ref[:, d:2 * d]) * _dot(yb_ref[...], wb_ref[...])
              + _sigmoid(g_ref[:, 2 * d:3 * d]) * _dot(ym_ref[...], wm_ref[...]))
    mix = _dot(merged.astype(BF16), wo_ref[...])
    x1 = _layer_norm(alpha * x_ref[...] + mix, lng_ref[...], lnb_ref[...])
    x1_ref[...] = x1

    xh = x1.astype(BF16)
    xl = (x1 - xh.astype(F32)).astype(BF16)
    logits = (_dot(xh, wrh_ref[...]) + (_dot(xh, wrl_ref[...]) + _dot(xl, wrh_ref[...]))
              + br_ref[...])

    col = lax.broadcasted_iota(I32, (tm, LANES), 1)
    colf = col.astype(F32)
    big = float(LANES)
    gmask = (col >= n_experts) & (col < n_experts + N_GROUPS)
    lg = jnp.where(gmask, logits, -jnp.inf)
    gmax = jnp.max(lg, axis=1, keepdims=True)
    gidx = jnp.min(jnp.where(lg == gmax, colf, big), axis=1, keepdims=True) - n_experts
    g_p = 1.0 / jnp.sum(jnp.where(gmask, jnp.exp(logits - gmax), 0.0), axis=1, keepdims=True)
    lo_col = gidx * EXPERTS_PER_GROUP
    emask = (colf >= lo_col) & (colf < lo_col + EXPERTS_PER_GROUP)
    le = jnp.where(emask, logits, -jnp.inf)
    l1 = jnp.max(le, axis=1, keepdims=True)
    i1 = jnp.min(jnp.where(le == l1, colf, big), axis=1, keepdims=True)
    le2 = jnp.where(colf == i1, -jnp.inf, le)
    l2 = jnp.max(le2, axis=1, keepdims=True)
    i2 = jnp.min(jnp.where(le2 == l2, colf, big), axis=1, keepdims=True)
    e2 = jnp.exp(l2 - l1)
    gate1 = g_p * (1.0 / (1.0 + e2))
    gate2 = g_p * (e2 / (1.0 + e2))

    oh1 = colf == i1
    oh2 = colf == i2
    cnt = jnp.where(oh1 | oh2, 1.0, 0.0)
    rr = lax.broadcasted_iota(I32, (tm, tm), 0)
    cc = lax.broadcasted_iota(I32, (tm, tm), 1)
    before = jnp.where(cc < rr, 1.0, 0.0).astype(BF16)
    base = _dot(before, cnt.astype(BF16)) + cnt_ref[0:1, :]
    rank1 = jnp.sum(jnp.where(oh1, base, 0.0), axis=1, keepdims=True)
    rank2 = jnp.sum(jnp.where(oh2, base, 0.0), axis=1, keepdims=True)
    cnt_ref[...] = cnt_ref[...] + jnp.sum(cnt, axis=0, keepdims=True)

    route = jnp.zeros((tm, LANES), F32)
    for k, val in enumerate((i1, i2, rank1, rank2, gate1, gate2)):
        route = jnp.where(col == k, val, route)
    route_ref[...] = route


def _merge(ya, yb, ym, gates, x2, wa, wb, wm, wo, ln_g, ln_b, wr_hi, wr_lo, b_r, alpha, n_experts):
    n, d = x2.shape
    tm = MERGE_ROWS
    row = lambda w: pl.BlockSpec((tm, w), lambda i: (i, 0))
    full = lambda a: pl.BlockSpec(a.shape, lambda i: (0,) * a.ndim)
    return pl.pallas_call(
        functools.partial(_merge_kernel, alpha=alpha, n_experts=n_experts),
        grid=(n // tm,),
        in_specs=[row(ya.shape[1]), row(yb.shape[1]), row(ym.shape[1]), row(gates.shape[1]), row(d),
                  full(wa), full(wb), full(wm), full(wo), full(ln_g), full(ln_b),
                  full(wr_hi), full(wr_lo), full(b_r)],
        out_specs=[row(d), row(LANES), pl.BlockSpec((8, LANES), lambda i: (0, 0))],
        out_shape=[jax.ShapeDtypeStruct((n, d), F32),
                   jax.ShapeDtypeStruct((n, LANES), F32),
                   jax.ShapeDtypeStruct((8, LANES), F32)],
        compiler_params=_params(("arbitrary",), 52),
        name="merge",
    )(ya, yb, ym, gates, x2, wa, wb, wm, wo, ln_g, ln_b, wr_hi, wr_lo, b_r)


def _row_copy(src_hbm, idx_ref, buf, sem, r):
    return pltpu.make_async_copy(src_hbm.at[pl.ds(idx_ref[0, 0, r], 1), :], buf.at[pl.ds(r, 1), :], sem)


def _gather_rows(src_hbm, idx_ref, buf, sem, n_rows, unrolled):
    if unrolled:
        for r in range(n_rows):
            _row_copy(src_hbm, idx_ref, buf, sem, r).start()
    else:
        def issue(r, _):
            _row_copy(src_hbm, idx_ref, buf, sem, r).start()
            return 0
        lax.fori_loop(0, n_rows, issue, 0)


def _wait_rows(src_hbm, buf, sem, n_rows):
    pltpu.make_async_copy(src_hbm.at[pl.ds(0, n_rows), :], buf, sem).wait()


def _experts_kernel(be_ref, nu_ref, idx_ref, idxn_ref, x_hbm, wg_ref, wu_ref, wd_ref, o_ref,
                    xbuf, sem, wg_sc, wu_sc, wd_sc):
    b = pl.program_id(0)
    n_used = nu_ref[0]
    rows = o_ref.shape[0]
    slot = lax.rem(b, 2)

    @pl.when((b == 0) & (n_used > 0))
    def _():
        _gather_rows(x_hbm, idx_ref, xbuf.at[0], sem.at[0], rows, False)

    @pl.when(b < n_used)
    def _():
        changed = (b == 0) | (be_ref[b] != be_ref[jnp.maximum(b - 1, 0)])

        @pl.when(changed)
        def _():
            wg_sc[...] = wg_ref[0].astype(BF16)
            wu_sc[...] = wu_ref[0].astype(BF16)
            wd_sc[...] = wd_ref[0].astype(BF16)

        _wait_rows(x_hbm, xbuf.at[slot], sem.at[slot], rows)
        _gather_rows(x_hbm, idxn_ref, xbuf.at[1 - slot], sem.at[1 - slot], rows, True)
        xb = xbuf[slot].astype(BF16)
        g = _dot(xb, wg_sc[...])
        u = _dot(xb, wu_sc[...])
        h = (g * _sigmoid(g) * u).astype(BF16)
        o_ref[...] = _dot(h, wd_sc[...])

    @pl.when(b >= n_used)
    def _():
        @pl.when((b == n_used) & (n_used > 0))
        def _():
            _wait_rows(x_hbm, xbuf.at[slot], sem.at[slot], rows)

        o_ref[...] = jnp.zeros(o_ref.shape, F32)


def _experts(x1, tok_of_row, block_expert, n_used, w_gate, w_up, w_down):
    n, d = x1.shape
    n_blocks = block_expert.shape[0]
    rows = MOE_BLOCK
    d_exp = w_gate.shape[-1]
    idx3 = tok_of_row.reshape(n_blocks, 1, rows)
    smem_idx = lambda f: pl.BlockSpec((1, 1, rows), f, memory_space=pltpu.SMEM)
    grid_spec = pltpu.PrefetchScalarGridSpec(
        num_scalar_prefetch=2,
        grid=(n_blocks,),
        in_specs=[smem_idx(lambda b, be, nu: (b, 0, 0)),
                  smem_idx(lambda b, be, nu: (jnp.minimum(b + 1, n_blocks - 1), 0, 0)),
                  pl.BlockSpec(memory_space=pl.ANY),
                  pl.BlockSpec((1, d, d_exp), lambda b, be, nu: (be[b], 0, 0)),
                  pl.BlockSpec((1, d, d_exp), lambda b, be, nu: (be[b], 0, 0)),
                  pl.BlockSpec((1, d_exp, d), lambda b, be, nu: (be[b], 0, 0))],
        out_specs=pl.BlockSpec((rows, d), lambda b, be, nu: (b, 0)),
        scratch_shapes=[pltpu.VMEM((2, rows, d), F32),
                        pltpu.SemaphoreType.DMA((2,)),
                        pltpu.VMEM((d, d_exp), BF16),
                        pltpu.VMEM((d, d_exp), BF16),
                        pltpu.VMEM((d_exp, d), BF16)])
    return pl.pallas_call(
        _experts_kernel,
        grid_spec=grid_spec,
        out_shape=jax.ShapeDtypeStruct((n_blocks * rows, d), F32),
        compiler_params=_params(("arbitrary",), 44),
        name="experts",
    )(block_expert, n_used, idx3, idx3, x1, w_gate, w_up, w_down)


def _combine_kernel(idx_ref, idxn_ref, y_hbm, x1_ref, route_ref, lng_ref, lnb_ref, o_ref,
                    ybuf, sem, *, alpha):
    i = pl.program_id(0)
    n_steps = pl.num_programs(0)
    tm = x1_ref.shape[0]
    slot = lax.rem(i, 2)

    @pl.when(i == 0)
    def _():
        _gather_rows(y_hbm, idx_ref, ybuf.at[0], sem.at[0], 2 * tm, False)

    @pl.when(i + 1 < n_steps)
    def _():
        _gather_rows(y_hbm, idxn_ref, ybuf.at[1 - slot], sem.at[1 - slot], 2 * tm, True)

    _wait_rows(y_hbm, ybuf.at[slot], sem.at[slot], 2 * tm)
    ffn = route_ref[:, 4:5] * ybuf[slot, 0:tm, :] + route_ref[:, 5:6] * ybuf[slot, tm:2 * tm, :]
    o_ref[...] = _layer_norm(alpha * x1_ref[...] + ffn, lng_ref[...], lnb_ref[...])


def _combine(y_pad, dest, x1, route, ln_g, ln_b, alpha):
    n, d = x1.shape
    tm = COMBINE_ROWS
    n_steps = n // tm
    idx3 = dest.reshape(n_steps, tm, 2).transpose(0, 2, 1).reshape(n_steps, 1, 2 * tm)
    smem_idx = lambda f: pl.BlockSpec((1, 1, 2 * tm), f, memory_space=pltpu.SMEM)
    return pl.pallas_call(
        functools.partial(_combine_kernel, alpha=alpha),
        grid=(n_steps,),
        in_specs=[smem_idx(lambda i: (i, 0, 0)),
                  smem_idx(lambda i: (jnp.minimum(i + 1, n_steps - 1), 0, 0)),
                  pl.BlockSpec(memory_space=pl.ANY),
                  pl.BlockSpec((tm, d), lambda i: (i, 0)),
                  pl.BlockSpec((tm, LANES), lambda i: (i, 0)),
                  pl.BlockSpec((1, d), lambda i: (0, 0)),
                  pl.BlockSpec((1, d), lambda i: (0, 0))],
        out_specs=pl.BlockSpec((tm, d), lambda i: (i, 0)),
        out_shape=jax.ShapeDtypeStruct((n, d), F32),
        scratch_shapes=[pltpu.VMEM((2, 2 * tm, d), F32), pltpu.SemaphoreType.DMA((2,))],
        compiler_params=_params(("arbitrary",), 32),
        name="combine",
    )(idx3, idx3, y_pad, x1, route, ln_g, ln_b)


def _split_bf16(w):
    hi = w.astype(BF16)
    return hi, (w - hi.astype(F32)).astype(BF16)


def _layer(x, mem, tbl, rel_bias, w_in, w_mem_kv, w_br_moba, w_br_sb, w_br_mem, w_out, ln1_g, ln1_b,
           w_rg, b_rg, w_re, b_re, w_gate, w_up, w_down, ln2_g, ln2_b, alpha):
    bsz, seq, d = x.shape
    n = bsz * seq
    n_experts = w_re.shape[1]
    moba_w, sb_w, mem_w = N_MOBA_HEADS * HEAD_DIM, N_SB_HEADS * HEAD_DIM, N_MEM_HEADS * HEAD_DIM
    n_qkv = 3 * moba_w + 3 * sb_w + mem_w
    assert w_in.shape[1] == n_qkv + 3 * d and n_experts + N_GROUPS <= LANES and moba_w == sb_w

    x2 = x.reshape(n, d)
    qkv, gates = _proj(x2, w_in[:, :n_qkv].astype(BF16), w_in[:, n_qkv:].astype(BF16))
    qkv3 = qkv.reshape(bsz, seq, n_qkv)
    y_a = _moba(qkv3, tbl, rel_bias, 0)
    y_b = _stickbreak(qkv3, 3 * moba_w // sb_w)
    kv = _memkv(mem.reshape(-1, d), w_mem_kv.astype(BF16)).reshape(bsz, mem.shape[1], 2 * mem_w)
    y_m = _mem_attention(qkv3, kv, (3 * moba_w + 3 * sb_w) // mem_w)

    w_r = jnp.zeros((d, LANES), F32).at[:, :n_experts].set(w_re).at[:, n_experts:n_experts + N_GROUPS].set(w_rg)
    b_r = jnp.zeros((1, LANES), F32).at[0, :n_experts].set(b_re).at[0, n_experts:n_experts + N_GROUPS].set(b_rg)
    wr_hi, wr_lo = _split_bf16(w_r)
    x1, route, counts = _merge(
        y_a.reshape(n, moba_w), y_b.reshape(n, sb_w), y_m.reshape(n, mem_w), gates, x2,
        w_br_moba.astype(BF16), w_br_sb.astype(BF16), w_br_mem.astype(BF16), w_out.astype(BF16),
        ln1_g.reshape(1, d), ln1_b.reshape(1, d), wr_hi, wr_lo, b_r, alpha, n_experts)

    rows = MOE_BLOCK
    expert = route[:, 0:2].astype(I32)
    rank = route[:, 2:4].astype(I32)
    cnt = counts[0, :n_experts].astype(I32)
    padded = (cnt + rows - 1) // rows * rows
    pend = jnp.cumsum(padded)
    pstart = pend - padded
    dest = pstart[expert] + rank
    n_blocks = (2 * n) // rows + n_experts
    tok = jnp.broadcast_to(jnp.arange(n, dtype=I32)[:, None], (n, 2))
    tok_of_row = jnp.zeros((n_blocks * rows,), I32).at[dest.reshape(-1)].set(tok.reshape(-1))
    block_start = jnp.arange(n_blocks, dtype=I32) * rows
    block_expert = jnp.minimum(jnp.sum((pend[None, :] <= block_start[:, None]).astype(I32), axis=1),
                               n_experts - 1)
    n_used = (pend[-1] // rows).astype(I32).reshape(1)

    y_pad = _experts(x1, tok_of_row, block_expert, n_used, w_gate, w_up, w_down)
    out = _combine(y_pad, dest, x1, route, ln2_g.reshape(1, d), ln2_b.reshape(1, d), alpha)
    return out.reshape(bsz, seq, d)


def kernel(x, mem, w_in, w_mem_kv, rel_bias, w_br_moba, w_br_sb, w_br_mem, w_out, ln1_g, ln1_b,
           w_router_group, b_router_group, w_router_expert, b_router_expert,
           w_gate, w_up, w_down, ln2_g, ln2_b):
    depth = w_in.shape[0]
    alpha = (2.0 * depth) ** 0.25
    tbl = _bias_table(rel_bias)
    for l in range(depth):
        x = _layer(x, mem, tbl, rel_bias, w_in[l], w_mem_kv[l], w_br_moba[l], w_br_sb[l], w_br_mem[l],
                   w_out[l], ln1_g[l], ln1_b[l], w_router_group[l], b_router_group[l],
                   w_router_expert[l], b_router_expert[l], w_gate[l], w_up[l], w_down[l],
                   ln2_g[l], ln2_b[l], alpha)
    return x
```

```python
import functools
import math

import jax
import jax.numpy as jnp
from jax import lax
from jax.experimental import pallas as pl
from jax.experimental.pallas import tpu as pltpu

F32, BF16, I32 = jnp.float32, jnp.bfloat16, jnp.int32

HEAD_DIM = 64
N_MOBA_HEADS = 6
N_SB_HEADS = 6
N_MEM_HEADS = 4
MOBA_BLOCK = 256
MOBA_TOPK = 3
N_BUCKETS = 32
MAX_DISTANCE = 128
N_GROUPS = 4
EXPERTS_PER_GROUP = 8
MOE_BLOCK = 256
LN_EPS = 1e-5
NEG = -1e30

LANES = 128
VMEM_BYTES = 64 * 1024 * 1024
PAIR = LANES // HEAD_DIM

SB_ZERO_LOG = 110.0

PROJ_ROWS = 512
MERGE_ROWS = 512
ATTN_ROWS = 256
COMBINE_ROWS = 256


def _params(semantics, vmem_mb):
    return pltpu.CompilerParams(dimension_semantics=semantics,
                                vmem_limit_bytes=min(vmem_mb * 1024 * 1024, VMEM_BYTES))


def _dot(a, b):
    return jnp.dot(a, b, preferred_element_type=F32)


def _layer_norm(h, g, b):
    mu = jnp.mean(h, axis=-1, keepdims=True)
    d = h - mu
    var = jnp.mean(d * d, axis=-1, keepdims=True)
    return d * lax.rsqrt(var + LN_EPS) * g + b


def _sigmoid(x):
    return 1.0 / (1.0 + jnp.exp(-x))


def _head_rows(qt, row, hh, scale):
    keep = (row >= hh * HEAD_DIM) & (row < (hh + 1) * HEAD_DIM)
    return jnp.where(keep, qt * scale, 0.0).astype(BF16)


def _transposed_bf16(x):
    return x.astype(F32).T.astype(BF16)


def _proj_kernel(x_ref, wq_ref, wg_ref, qkv_ref, gate_ref):
    xb = x_ref[...].astype(BF16)
    qkv_ref[...] = _dot(xb, wq_ref[...]).astype(BF16)
    gate_ref[...] = _dot(xb, wg_ref[...])


def _proj(x2, w_qkv, w_gates):
    n, d = x2.shape
    n_qkv, n_gates = w_qkv.shape[1], w_gates.shape[1]
    tm = PROJ_ROWS
    return pl.pallas_call(
        _proj_kernel,
        grid=(n // tm,),
        in_specs=[pl.BlockSpec((tm, d), lambda i: (i, 0)),
                  pl.BlockSpec((d, n_qkv), lambda i: (0, 0)),
                  pl.BlockSpec((d, n_gates), lambda i: (0, 0))],
        out_specs=[pl.BlockSpec((tm, n_qkv), lambda i: (i, 0)),
                   pl.BlockSpec((tm, n_gates), lambda i: (i, 0))],
        out_shape=[jax.ShapeDtypeStruct((n, n_qkv), BF16),
                   jax.ShapeDtypeStruct((n, n_gates), F32)],
        compiler_params=_params(("arbitrary",), 56),
        name="proj",
    )(x2, w_qkv, w_gates)


def _t5_bucket(rel):
    rel = jnp.maximum(rel, 0)
    max_exact = N_BUCKETS // 2
    rel_f = jnp.maximum(rel, 1).astype(F32)
    large = max_exact + (jnp.log(rel_f / max_exact) / math.log(MAX_DISTANCE / max_exact)
                         * (N_BUCKETS - max_exact)).astype(I32)
    large = jnp.minimum(large, N_BUCKETS - 1)
    return jnp.where(rel < max_exact, rel, large)


def _bias_table_kernel(rb_ref, o_ref):
    h = pl.program_id(0)
    blk = o_ref.shape[2]
    j = lax.broadcasted_iota(I32, (2 * blk, blk), 0)
    i = lax.broadcasted_iota(I32, (2 * blk, blk), 1)
    bucket = _t5_bucket(blk + i - j)
    acc = jnp.zeros((2 * blk, blk), F32)
    for b in range(N_BUCKETS):
        acc = jnp.where(bucket == b, rb_ref[b, h], acc)
    o_ref[0] = acc


def _bias_table(rel_bias):
    n_heads = rel_bias.shape[1]
    blk = MOBA_BLOCK
    return pl.pallas_call(
        _bias_table_kernel,
        grid=(n_heads,),
        in_specs=[pl.BlockSpec(memory_space=pltpu.SMEM)],
        out_specs=pl.BlockSpec((1, 2 * blk, blk), lambda h: (h, 0, 0)),
        out_shape=jax.ShapeDtypeStruct((n_heads, 2 * blk, blk), F32),
        compiler_params=_params(("arbitrary",), 32),
        name="bias_table",
    )(rel_bias)


def _fold_keys(x, op, final):
    while x.shape[0] > 8 and x.shape[0] % 2 == 0:
        half = x.shape[0] // 2
        x = op(x[:half], x[half:])
    return final(x, axis=0, keepdims=True)


def _head_queries(qt, row, hh, scale):
    p, sub = divmod(hh, PAIR)
    return _head_rows(qt[p * LANES:(p + 1) * LANES, :], row, sub, scale)


def _pair_lanes(hh):
    p = hh // PAIR
    return slice(p * LANES, (p + 1) * LANES)


def _head_dims(hh):
    return slice(hh * HEAD_DIM, (hh + 1) * HEAD_DIM)


def _moba_kernel(rb_ref, q_ref, k_ref, v_ref, tbl_ref, o_ref, kmean_sc, vt_sc, sel_sc):
    own = pl.program_id(1)
    blk, width = q_ref.shape[1], q_ref.shape[2]
    n_heads = width // HEAD_DIM
    nb = k_ref.shape[1] // blk
    scale = HEAD_DIM ** -0.5

    @pl.when(own == 0)
    def _():
        for n in range(nb):
            kb = k_ref[0, n * blk:(n + 1) * blk, :].astype(F32)
            kmean_sc[n:n + 1, :] = jnp.mean(kb, axis=0, keepdims=True)
            vt_sc[n] = _transposed_bf16(v_ref[0, n * blk:(n + 1) * blk, :])

    qt = q_ref[0].astype(F32).T
    row = lax.broadcasted_iota(I32, (LANES, blk), 0)
    blk_id = lax.broadcasted_iota(I32, (nb, blk), 0)
    valid = blk_id < own
    causal = (lax.broadcasted_iota(I32, (blk, blk), 0) <= lax.broadcasted_iota(I32, (blk, blk), 1))
    kmean = kmean_sc[...].astype(BF16)

    qs = [_head_queries(qt, row, hh, scale) for hh in range(n_heads)]
    gates = [_dot(kmean[:, _pair_lanes(hh)], _head_queries(qt, row, hh, 1.0)) for hh in range(n_heads)]

    def scores(n):
        offn = pl.multiple_of(n * blk, blk)
        return [_dot(k_ref[0, pl.ds(offn, blk), _pair_lanes(hh)], qs[hh]) for hh in range(n_heads)]

    own_scores = scores(own)

    for hh in range(n_heads):
        gate = jnp.where(valid, gates[hh], NEG)
        rank = jnp.zeros((nb, blk), I32)
        for m in range(nb):
            gm = gate[m:m + 1, :]
            beats = (gm > gate) | ((gm == gate) & (blk_id > m))
            rank = rank + beats.astype(I32)
        sel_sc[hh] = jnp.where(valid & (rank < MOBA_TOPK), 0.0, NEG)

    stats = []
    for hh, s in enumerate(own_scores):
        s = jnp.where(causal, s + tbl_ref[hh, blk:, :], NEG)
        m0 = _fold_keys(s, jnp.maximum, jnp.max)
        p = jnp.exp(s - m0)
        stats.append((m0, _fold_keys(p, jnp.add, jnp.sum), p.astype(BF16)))
    state = []
    for hh, (m0, l0, p) in enumerate(stats):
        state += [m0, l0, _dot(vt_sc[own, _head_dims(hh), :], p)]

    def past_block(n, state, previous):
        stats = []
        for hh, s in enumerate(scores(n)):
            m_run, l_run, _ = state[3 * hh:3 * hh + 3]
            bias = tbl_ref[hh, :blk, :] if previous else rb_ref[N_BUCKETS - 1, hh]
            sn = s + bias + sel_sc[hh, pl.ds(n, 1), :]
            m_new = jnp.maximum(m_run, _fold_keys(sn, jnp.maximum, jnp.max))
            a = jnp.exp(m_run - m_new)
            pn = jnp.exp(sn - m_new)
            stats.append((m_new, a * l_run + _fold_keys(pn, jnp.add, jnp.sum), a, pn.astype(BF16)))
        new = []
        for hh, (m_new, l_new, a, pn) in enumerate(stats):
            new += [m_new, l_new, a * state[3 * hh + 2] + _dot(vt_sc[n, _head_dims(hh), :], pn)]
        return tuple(new)

    state = lax.fori_loop(jnp.maximum(own - 1, 0), own, lambda n, c: past_block(n, c, True), tuple(state))
    state = lax.fori_loop(0, own - 1, lambda n, c: past_block(n, c, False), state)
    out_t = jnp.concatenate([state[3 * hh + 2] / state[3 * hh + 1] for hh in range(n_heads)], axis=0)
    o_ref[0] = out_t.T.astype(o_ref.dtype)


def _moba(qkv3, tbl, rel_bias, col0):
    bsz, seq, _ = qkv3.shape
    assert MAX_DISTANCE <= MOBA_BLOCK and seq % MOBA_BLOCK == 0
    width = N_MOBA_HEADS * HEAD_DIM
    blk = MOBA_BLOCK
    nb = seq // blk
    return pl.pallas_call(
        _moba_kernel,
        grid=(bsz, nb),
        in_specs=[pl.BlockSpec(memory_space=pltpu.SMEM),
                  pl.BlockSpec((1, blk, width), lambda b, i: (b, i, col0)),
                  pl.BlockSpec((1, seq, width), lambda b, i: (b, 0, col0 + 1)),
                  pl.BlockSpec((1, seq, width), lambda b, i: (b, 0, col0 + 2)),
                  pl.BlockSpec((N_MOBA_HEADS, 2 * blk, blk), lambda b, i: (0, 0, 0))],
        out_specs=pl.BlockSpec((1, blk, width), lambda b, i: (b, i, 0)),
        out_shape=jax.ShapeDtypeStruct((bsz, seq, width), BF16),
        scratch_shapes=[pltpu.VMEM((nb, width), F32),
                        pltpu.VMEM((nb, width, blk), BF16),
                        pltpu.VMEM((N_MOBA_HEADS, nb, blk), F32)],
        compiler_params=_params(("arbitrary", "arbitrary"), 48),
        name="moba",
    )(rel_bias, qkv3, qkv3, qkv3, tbl)


def _sb_kernel(q_ref, k_ref, v_ref, o_ref, vt_sc):
    qi = pl.program_id(1)
    t, width = q_ref.shape[1], q_ref.shape[2]
    n_heads = width // HEAD_DIM
    nb = k_ref.shape[1] // t
    scale = HEAD_DIM ** -0.5

    @pl.when(qi == 0)
    def _():
        for n in range(nb):
            vt_sc[n] = _transposed_bf16(v_ref[0, n * t:(n + 1) * t, :])

    qt = q_ref[0].astype(F32).T
    row = lax.broadcasted_iota(I32, (LANES, t), 0)
    key = lax.broadcasted_iota(I32, (t, t), 0)
    qry = lax.broadcasted_iota(I32, (t, t), 1)
    strict = key < qry
    tri = jnp.where(qry >= key, 1.0, 0.0).astype(BF16)
    qs = [_head_queries(qt, row, hh, scale) for hh in range(n_heads)]

    def block(j, carries, diagonal):
        off = pl.multiple_of(j * t, t)
        zs = [_dot(k_ref[0, pl.ds(off, t), _pair_lanes(hh)], qs[hh]) for hh in range(n_heads)]
        csums = []
        for z in zs:
            sp = jnp.maximum(z, 0.0) + jnp.log(1.0 + jnp.exp(-jnp.abs(z)))
            if diagonal:
                sp = jnp.where(strict, sp, 0.0)
            hi = sp.astype(BF16)
            lo = (sp - hi.astype(F32)).astype(BF16)
            csums.append(_dot(tri, hi) + _dot(tri, lo))
        out = []
        for hh in range(n_heads):
            a = jnp.exp(zs[hh] - (csums[hh] + carries[hh]))
            if diagonal:
                a = jnp.where(strict, a, 0.0)
            pv = _dot(vt_sc[j, _head_dims(hh), :], a.astype(BF16))
            out.append((pv, csums[hh][0:1, :]))
        return out

    zero = jnp.zeros((1, t), F32)
    first = block(qi, [zero] * n_heads, True)
    accs = tuple(pv for pv, _ in first)
    carries = tuple(total for _, total in first)

    def lowest(carries):
        return jnp.min(functools.reduce(jnp.minimum, carries))

    def cond(state):
        j, cmin, _, _ = state
        return (j >= 0) & (cmin < SB_ZERO_LOG)

    def body(state):
        j, _, carries, accs = state
        res = block(j, carries, False)
        carries = tuple(c + total for c, (_, total) in zip(carries, res))
        accs = tuple(acc + pv for acc, (pv, _) in zip(accs, res))
        return j - 1, lowest(carries), carries, accs

    state = lax.while_loop(cond, body, (qi - 1, lowest(carries), carries, accs))
    out_t = jnp.concatenate(list(state[3]), axis=0)
    o_ref[0] = out_t.T.astype(o_ref.dtype)


def _stickbreak(qkv3, col0):
    bsz, seq, _ = qkv3.shape
    width = N_SB_HEADS * HEAD_DIM
    t = ATTN_ROWS
    return pl.pallas_call(
        _sb_kernel,
        grid=(bsz, seq // t),
        in_specs=[pl.BlockSpec((1, t, width), lambda b, i: (b, i, col0)),
                  pl.BlockSpec((1, seq, width), lambda b, i: (b, 0, col0 + 1)),
                  pl.BlockSpec((1, seq, width), lambda b, i: (b, 0, col0 + 2))],
        out_specs=pl.BlockSpec((1, t, width), lambda b, i: (b, i, 0)),
        out_shape=jax.ShapeDtypeStruct((bsz, seq, width), BF16),
        scratch_shapes=[pltpu.VMEM((seq // t, width, t), BF16)],
        compiler_params=_params(("arbitrary", "arbitrary"), 48),
        name="stickbreak",
    )(qkv3, qkv3, qkv3)


def _memkv_kernel(m_ref, w_ref, o_ref):
    o_ref[...] = _dot(m_ref[...].astype(BF16), w_ref[...]).astype(BF16)


def _memkv(mem2, w_bf16):
    n, d = mem2.shape
    width = w_bf16.shape[1]
    tm = min(n, 512)
    return pl.pallas_call(
        _memkv_kernel,
        grid=(n // tm,),
        in_specs=[pl.BlockSpec((tm, d), lambda i: (i, 0)),
                  pl.BlockSpec((d, width), lambda i: (0, 0))],
        out_specs=pl.BlockSpec((tm, width), lambda i: (i, 0)),
        out_shape=jax.ShapeDtypeStruct((n, width), BF16),
        compiler_params=_params(("arbitrary",), 32),
        name="memkv",
    )(mem2, w_bf16)


def _mem_kernel(q_ref, k_ref, v_ref, o_ref, vt_sc):
    t, width = q_ref.shape[1], q_ref.shape[2]
    n_heads = width // HEAD_DIM
    scale = HEAD_DIM ** -0.5

    @pl.when(pl.program_id(1) == 0)
    def _():
        vt_sc[...] = _transposed_bf16(v_ref[0])

    qt = q_ref[0].astype(F32).T
    row = lax.broadcasted_iota(I32, (LANES, t), 0)
    qs = [_head_queries(qt, row, hh, scale) for hh in range(n_heads)]
    scores = [_dot(k_ref[0, :, _pair_lanes(hh)], qs[hh]) for hh in range(n_heads)]
    probs = []
    for s in scores:
        e = jnp.exp(s - _fold_keys(s, jnp.maximum, jnp.max))
        probs.append((e * (1.0 / _fold_keys(e, jnp.add, jnp.sum))).astype(BF16))
    out_t = jnp.concatenate([_dot(vt_sc[_head_dims(hh), :], probs[hh]) for hh in range(n_heads)], axis=0)
    o_ref[0] = out_t.T.astype(o_ref.dtype)


def _mem_attention(qkv3, kv3, qcol):
    bsz, seq, _ = qkv3.shape
    n_mem = kv3.shape[1]
    width = N_MEM_HEADS * HEAD_DIM
    t = ATTN_ROWS
    return pl.pallas_call(
        _mem_kernel,
        grid=(bsz, seq // t),
        in_specs=[pl.BlockSpec((1, t, width), lambda b, i: (b, i, qcol)),
                  pl.BlockSpec((1, n_mem, width), lambda b, i: (b, 0, 0)),
                  pl.BlockSpec((1, n_mem, width), lambda b, i: (b, 0, 1))],
        out_specs=pl.BlockSpec((1, t, width), lambda b, i: (b, i, 0)),
        out_shape=jax.ShapeDtypeStruct((bsz, seq, width), BF16),
        scratch_shapes=[pltpu.VMEM((width, n_mem), BF16)],
        compiler_params=_params(("arbitrary", "arbitrary"), 32),
        name="mem_attention",
    )(qkv3, kv3, kv3)


def _merge_kernel(ya_ref, yb_ref, ym_ref, g_ref, x_ref, wa_ref, wb_ref, wm_ref, wo_ref,
                  lng_ref, lnb_ref, wrh_ref, wrl_ref, br_ref, x1_ref, route_ref, cnt_ref,
                  *, alpha, n_experts):
    step = pl.program_id(0)
    tm, d = x_ref.shape

    @pl.when(step == 0)
    def _():
        cnt_ref[...] = jnp.zeros(cnt_ref.shape, F32)

    merged = (_sigmoid(g_ref[:, 0:d]) * _dot(ya_ref[...], wa_ref[...])
              + _sigmoid(g_ref[:, d:2 * d]) * _dot(yb_ref[...], wb_ref[...])
              + _sigmoid(g_ref[:, 2 * d:3 * d]) * _dot(ym_ref[...], wm_ref[...]))
    mix = _dot(merged.astype(BF16), wo_ref[...])
    x1 = _layer_norm(alpha * x_ref[...] + mix, lng_ref[...], lnb_ref[...])
    x1_ref[...] = x1

    xh = x1.astype(BF16)
    xl = (x1 - xh.astype(F32)).astype(BF16)
    logits = (_dot(xh, wrh_ref[...]) + (_dot(xh, wrl_ref[...]) + _dot(xl, wrh_ref[...]))
              + br_ref[...])

    col = lax.broadcasted_iota(I32, (tm, LANES), 1)
    colf = col.astype(F32)
    big = float(LANES)
    gmask = (col >= n_experts) & (col < n_experts + N_GROUPS)
    lg = jnp.where(gmask, logits, -jnp.inf)
    gmax = jnp.max(lg, axis=1, keepdims=True)
    gidx = jnp.min(jnp.where(lg == gmax, colf, big), axis=1, keepdims=True) - n_experts
    g_p = 1.0 / jnp.sum(jnp.where(gmask, jnp.exp(logits - gmax), 0.0), axis=1, keepdims=True)
    lo_col = gidx * EXPERTS_PER_GROUP
    emask = (colf >= lo_col) & (colf < lo_col + EXPERTS_PER_GROUP)
    le = jnp.where(emask, logits, -jnp.inf)
    l1 = jnp.max(le, axis=1, keepdims=True)
    i1 = jnp.min(jnp.where(le == l1, colf, big), axis=1, keepdims=True)
    le2 = jnp.where(colf == i1, -jnp.inf, le)
    l2 = jnp.max(le2, axis=1, keepdims=True)
    i2 = jnp.min(jnp.where(le2 == l2, colf, big), axis=1, keepdims=True)
    e2 = jnp.exp(l2 - l1)
    gate1 = g_p * (1.0 / (1.0 + e2))
    gate2 = g_p * (e2 / (1.0 + e2))

    oh1 = colf == i1
    oh2 = colf == i2
    cnt = jnp.where(oh1 | oh2, 1.0, 0.0)
    rr = lax.broadcasted_iota(I32, (tm, tm), 0)
    cc = lax.broadcasted_iota(I32, (tm, tm), 1)
    before = jnp.where(cc < rr, 1.0, 0.0).astype(BF16)
    base = _dot(before, cnt.astype(BF16)) + cnt_ref[0:1, :]
    rank1 = jnp.sum(jnp.where(oh1, base, 0.0), axis=1, keepdims=True)
    rank2 = jnp.sum(jnp.where(oh2, base, 0.0), axis=1, keepdims=True)
    cnt_ref[...] = cnt_ref[...] + jnp.sum(cnt, axis=0, keepdims=True)

    route = jnp.zeros((tm, LANES), F32)
    for k, val in enumerate((i1, i2, rank1, rank2, gate1, gate2)):
        route = jnp.where(col == k, val, route)
    route_ref[...] = route


def _merge(ya, yb, ym, gates, x2, wa, wb, wm, wo, ln_g, ln_b, wr_hi, wr_lo, b_r, alpha, n_experts):
    n, d = x2.shape
    tm = MERGE_ROWS
    row = lambda w: pl.BlockSpec((tm, w), lambda i: (i, 0))
    full = lambda a: pl.BlockSpec(a.shape, lambda i: (0,) * a.ndim)
    return pl.pallas_call(
        functools.partial(_merge_kernel, alpha=alpha, n_experts=n_experts),
        grid=(n // tm,),
        in_specs=[row(ya.shape[1]), row(yb.shape[1]), row(ym.shape[1]), row(gates.shape[1]), row(d),
                  full(wa), full(wb), full(wm), full(wo), full(ln_g), full(ln_b),
                  full(wr_hi), full(wr_lo), full(b_r)],
        out_specs=[row(d), row(LANES), pl.BlockSpec((8, LANES), lambda i: (0, 0))],
        out_shape=[jax.ShapeDtypeStruct((n, d), F32),
                   jax.ShapeDtypeStruct((n, LANES), F32),
                   jax.ShapeDtypeStruct((8, LANES), F32)],
        compiler_params=_params(("arbitrary",), 52),
        name="merge",
    )(ya, yb, ym, gates, x2, wa, wb, wm, wo, ln_g, ln_b, wr_hi, wr_lo, b_r)


def _row_copy(src_hbm, idx_ref, buf, sem, r):
    return pltpu.make_async_copy(src_hbm.at[pl.ds(idx_ref[0, 0, r], 1), :], buf.at[pl.ds(r, 1), :], sem)


def _gather_rows(src_hbm, idx_ref, buf, sem, n_rows, unrolled):
    if unrolled:
        for r in range(n_rows):
            _row_copy(src_hbm, idx_ref, buf, sem, r).start(priority=r % 2)
    else:
        def issue(r, _):
            _row_copy(src_hbm, idx_ref, buf, sem, r).start()
            return 0
        lax.fori_loop(0, n_rows, issue, 0)


def _wait_rows(src_hbm, buf, sem, n_rows):
    pltpu.make_async_copy(src_hbm.at[pl.ds(0, n_rows), :], buf, sem).wait()


def _experts_kernel(be_ref, nu_ref, idx_ref, idxn_ref, x_hbm, wg_ref, wu_ref, wd_ref, o_ref,
                    xbuf, sem, wg_sc, wu_sc, wd_sc):
    b = pl.program_id(0)
    n_used = nu_ref[0]
    rows = o_ref.shape[0]
    slot = lax.rem(b, 2)

    @pl.when((b == 0) & (n_used > 0))
    def _():
        _gather_rows(x_hbm, idx_ref, xbuf.at[0], sem.at[0], rows, False)

    @pl.when(b < n_used)
    def _():
        changed = (b == 0) | (be_ref[b] != be_ref[jnp.maximum(b - 1, 0)])

        @pl.when(changed)
        def _():
            wg_sc[...] = wg_ref[0].astype(BF16)
            wu_sc[...] = wu_ref[0].astype(BF16)
            wd_sc[...] = wd_ref[0].astype(BF16)

        _wait_rows(x_hbm, xbuf.at[slot], sem.at[slot], rows)
        _gather_rows(x_hbm, idxn_ref, xbuf.at[1 - slot], sem.at[1 - slot], rows, True)
        xb = xbuf[slot].astype(BF16)
        g = _dot(xb, wg_sc[...])
        u = _dot(xb, wu_sc[...])
        h = (g * _sigmoid(g) * u).astype(BF16)
        o_ref[...] = _dot(h, wd_sc[...])

    @pl.when(b >= n_used)
    def _():
        @pl.when((b == n_used) & (n_used > 0))
        def _():
            _wait_rows(x_hbm, xbuf.at[slot], sem.at[slot], rows)

        o_ref[...] = jnp.zeros(o_ref.shape, F32)


def _experts(x1, tok_of_row, block_expert, n_used, w_gate, w_up, w_down):
    n, d = x1.shape
    n_blocks = block_expert.shape[0]
    rows = MOE_BLOCK
    d_exp = w_gate.shape[-1]
    idx3 = tok_of_row.reshape(n_blocks, 1, rows)
    smem_idx = lambda f: pl.BlockSpec((1, 1, rows), f, memory_space=pltpu.SMEM)
    grid_spec = pltpu.PrefetchScalarGridSpec(
        num_scalar_prefetch=2,
        grid=(n_blocks,),
        in_specs=[smem_idx(lambda b, be, nu: (b, 0, 0)),
                  smem_idx(lambda b, be, nu: (jnp.minimum(b + 1, n_blocks - 1), 0, 0)),
                  pl.BlockSpec(memory_space=pl.ANY),
                  pl.BlockSpec((1, d, d_exp), lambda b, be, nu: (be[b], 0, 0)),
                  pl.BlockSpec((1, d, d_exp), lambda b, be, nu: (be[b], 0, 0)),
                  pl.BlockSpec((1, d_exp, d), lambda b, be, nu: (be[b], 0, 0))],
        out_specs=pl.BlockSpec((rows, d), lambda b, be, nu: (b, 0)),
        scratch_shapes=[pltpu.VMEM((2, rows, d), F32),
                        pltpu.SemaphoreType.DMA((2,)),
                        pltpu.VMEM((d, d_exp), BF16),
                        pltpu.VMEM((d, d_exp), BF16),
                        pltpu.VMEM((d_exp, d), BF16)])
    return pl.pallas_call(
        _experts_kernel,
        grid_spec=grid_spec,
        out_shape=jax.ShapeDtypeStruct((n_blocks * rows, d), F32),
        compiler_params=_params(("arbitrary",), 44),
        name="experts",
    )(block_expert, n_used, idx3, idx3, x1, w_gate, w_up, w_down)


def _combine_kernel(idx_ref, idxn_ref, y_hbm, x1_ref, route_ref, lng_ref, lnb_ref, o_ref,
                    ybuf, sem, *, alpha):
    i = pl.program_id(0)
    n_steps = pl.num_programs(0)
    tm = x1_ref.shape[0]
    slot = lax.rem(i, 2)

    @pl.when(i == 0)
    def _():
        _gather_rows(y_hbm, idx_ref, ybuf.at[0], sem.at[0], 2 * tm, False)

    @pl.when(i + 1 < n_steps)
    def _():
        _gather_rows(y_hbm, idxn_ref, ybuf.at[1 - slot], sem.at[1 - slot], 2 * tm, True)

    _wait_rows(y_hbm, ybuf.at[slot], sem.at[slot], 2 * tm)
    ffn = route_ref[:, 4:5] * ybuf[slot, 0:tm, :] + route_ref[:, 5:6] * ybuf[slot, tm:2 * tm, :]
    o_ref[...] = _layer_norm(alpha * x1_ref[...] + ffn, lng_ref[...], lnb_ref[...])


def _combine(y_pad, dest, x1, route, ln_g, ln_b, alpha):
    n, d = x1.shape
    tm = COMBINE_ROWS
    n_steps = n // tm
    idx3 = dest.reshape(n_steps, tm, 2).transpose(0, 2, 1).reshape(n_steps, 1, 2 * tm)
    smem_idx = lambda f: pl.BlockSpec((1, 1, 2 * tm), f, memory_space=pltpu.SMEM)
    return pl.pallas_call(
        functools.partial(_combine_kernel, alpha=alpha),
        grid=(n_steps,),
        in_specs=[smem_idx(lambda i: (i, 0, 0)),
                  smem_idx(lambda i: (jnp.minimum(i + 1, n_steps - 1), 0, 0)),
                  pl.BlockSpec(memory_space=pl.ANY),
                  pl.BlockSpec((tm, d), lambda i: (i, 0)),
                  pl.BlockSpec((tm, LANES), lambda i: (i, 0)),
                  pl.BlockSpec((1, d), lambda i: (0, 0)),
                  pl.BlockSpec((1, d), lambda i: (0, 0))],
        out_specs=pl.BlockSpec((tm, d), lambda i: (i, 0)),
        out_shape=jax.ShapeDtypeStruct((n, d), F32),
        scratch_shapes=[pltpu.VMEM((2, 2 * tm, d), F32), pltpu.SemaphoreType.DMA((2,))],
        compiler_params=_params(("arbitrary",), 32),
        name="combine",
    )(idx3, idx3, y_pad, x1, route, ln_g, ln_b)


def _split_bf16(w):
    hi = w.astype(BF16)
    return hi, (w - hi.astype(F32)).astype(BF16)


def _layer(x, mem, tbl, rel_bias, w_in, w_mem_kv, w_br_moba, w_br_sb, w_br_mem, w_out, ln1_g, ln1_b,
           w_rg, b_rg, w_re, b_re, w_gate, w_up, w_down, ln2_g, ln2_b, alpha):
    bsz, seq, d = x.shape
    n = bsz * seq
    n_experts = w_re.shape[1]
    moba_w, sb_w, mem_w = N_MOBA_HEADS * HEAD_DIM, N_SB_HEADS * HEAD_DIM, N_MEM_HEADS * HEAD_DIM
    n_qkv = 3 * moba_w + 3 * sb_w + mem_w
    assert w_in.shape[1] == n_qkv + 3 * d and n_experts + N_GROUPS <= LANES and moba_w == sb_w

    x2 = x.reshape(n, d)
    qkv, gates = _proj(x2, w_in[:, :n_qkv].astype(BF16), w_in[:, n_qkv:].astype(BF16))
    qkv3 = qkv.reshape(bsz, seq, n_qkv)
    y_a = _moba(qkv3, tbl, rel_bias, 0)
    y_b = _stickbreak(qkv3, 3 * moba_w // sb_w)
    kv = _memkv(mem.reshape(-1, d), w_mem_kv.astype(BF16)).reshape(bsz, mem.shape[1], 2 * mem_w)
    y_m = _mem_attention(qkv3, kv, (3 * moba_w + 3 * sb_w) // mem_w)

    w_r = jnp.zeros((d, LANES), F32).at[:, :n_experts].set(w_re).at[:, n_experts:n_experts + N_GROUPS].set(w_rg)
    b_r = jnp.zeros((1, LANES), F32).at[0, :n_experts].set(b_re).at[0, n_experts:n_experts + N_GROUPS].set(b_rg)
    wr_hi, wr_lo = _split_bf16(w_r)
    x1, route, counts = _merge(
        y_a.reshape(n, moba_w), y_b.reshape(n, sb_w), y_m.reshape(n, mem_w), gates, x2,
        w_br_moba.astype(BF16), w_br_sb.astype(BF16), w_br_mem.astype(BF16), w_out.astype(BF16),
        ln1_g.reshape(1, d), ln1_b.reshape(1, d), wr_hi, wr_lo, b_r, alpha, n_experts)

    rows = MOE_BLOCK
    expert = route[:, 0:2].astype(I32)
    rank = route[:, 2:4].astype(I32)
    cnt = counts[0, :n_experts].astype(I32)
    padded = (cnt + rows - 1) // rows * rows
    pend = jnp.cumsum(padded)
    pstart = pend - padded
    dest = pstart[expert] + rank
    n_blocks = (2 * n) // rows + n_experts
    tok = jnp.broadcast_to(jnp.arange(n, dtype=I32)[:, None], (n, 2))
    tok_of_row = jnp.zeros((n_blocks * rows,), I32).at[dest.reshape(-1)].set(
        tok.reshape(-1), unique_indices=True, mode="promise_in_bounds")
    block_start = jnp.arange(n_blocks, dtype=I32) * rows
    block_expert = jnp.minimum(jnp.sum((pend[None, :] <= block_start[:, None]).astype(I32), axis=1),
                               n_experts - 1)
    n_used = (pend[-1] // rows).astype(I32).reshape(1)

    y_pad = _experts(x1, tok_of_row, block_expert, n_used, w_gate, w_up, w_down)
    out = _combine(y_pad, dest, x1, route, ln2_g.reshape(1, d), ln2_b.reshape(1, d), alpha)
    return out.reshape(bsz, seq, d)


def kernel(x, mem, w_in, w_mem_kv, rel_bias, w_br_moba, w_br_sb, w_br_mem, w_out, ln1_g, ln1_b,
           w_router_group, b_router_group, w_router_expert, b_router_expert,
           w_gate, w_up, w_down, ln2_g, ln2_b):
    depth = w_in.shape[0]
    alpha = (2.0 * depth) ** 0.25
    tbl = _bias_table(rel_bias)
    for l in range(depth):
        x = _layer(x, mem, tbl, rel_bias, w_in[l], w_mem_kv[l], w_br_moba[l], w_br_sb[l], w_br_mem[l],
                   w_out[l], ln1_g[l], ln1_b[l], w_router_group[l], b_router_group[l],
                   w_router_expert[l], b_router_expert[l], w_gate[l], w_up[l], w_down[l],
                   ln2_g[l], ln2_b[l], alpha)
    return x
```

```python
import functools
import math

import jax
import jax.numpy as jnp
from jax import lax
from jax.experimental import pallas as pl
from jax.experimental.pallas import tpu as pltpu

F32, BF16, I32 = jnp.float32, jnp.bfloat16, jnp.int32

HEAD_DIM = 64
N_MOBA_HEADS = 6
N_SB_HEADS = 6
N_MEM_HEADS = 4
MOBA_BLOCK = 256
MOBA_TOPK = 3
N_BUCKETS = 32
MAX_DISTANCE = 128
N_GROUPS = 4
EXPERTS_PER_GROUP = 8
MOE_BLOCK = 256
LN_EPS = 1e-5
NEG = -1e30

LANES = 128
VMEM_BYTES = 64 * 1024 * 1024
PAIR = LANES // HEAD_DIM

SB_ZERO_LOG = 110.0

PROJ_ROWS = 512
MERGE_ROWS = 512
ATTN_ROWS = 256
COMBINE_ROWS = 256


def _params(semantics, vmem_mb):
    return pltpu.CompilerParams(dimension_semantics=semantics,
                                vmem_limit_bytes=min(vmem_mb * 1024 * 1024, VMEM_BYTES))


def _dot(a, b):
    return jnp.dot(a, b, preferred_element_type=F32)


def _layer_norm(h, g, b):
    mu = jnp.mean(h, axis=-1, keepdims=True)
    d = h - mu
    var = jnp.mean(d * d, axis=-1, keepdims=True)
    return d * lax.rsqrt(var + LN_EPS) * g + b


def _sigmoid(x):
    return 1.0 / (1.0 + jnp.exp(-x))


def _head_rows(qt, row, hh, scale):
    keep = (row >= hh * HEAD_DIM) & (row < (hh + 1) * HEAD_DIM)
    return jnp.where(keep, qt * scale, 0.0).astype(BF16)


def _transposed_bf16(x):
    return x.astype(F32).T.astype(BF16)


def _proj_kernel(x_ref, wq_ref, wg_ref, qkv_ref, gate_ref):
    xb = x_ref[...].astype(BF16)
    qkv_ref[...] = _dot(xb, wq_ref[...]).astype(BF16)
    gate_ref[...] = _dot(xb, wg_ref[...])


def _proj(x2, w_qkv, w_gates):
    n, d = x2.shape
    n_qkv, n_gates = w_qkv.shape[1], w_gates.shape[1]
    tm = PROJ_ROWS
    return pl.pallas_call(
        _proj_kernel,
        grid=(n // tm,),
        in_specs=[pl.BlockSpec((tm, d), lambda i: (i, 0)),
                  pl.BlockSpec((d, n_qkv), lambda i: (0, 0)),
                  pl.BlockSpec((d, n_gates), lambda i: (0, 0))],
        out_specs=[pl.BlockSpec((tm, n_qkv), lambda i: (i, 0)),
                   pl.BlockSpec((tm, n_gates), lambda i: (i, 0))],
        out_shape=[jax.ShapeDtypeStruct((n, n_qkv), BF16),
                   jax.ShapeDtypeStruct((n, n_gates), F32)],
        compiler_params=_params(("arbitrary",), 56),
        name="proj",
    )(x2, w_qkv, w_gates)


def _t5_bucket(rel):
    rel = jnp.maximum(rel, 0)
    max_exact = N_BUCKETS // 2
    rel_f = jnp.maximum(rel, 1).astype(F32)
    large = max_exact + (jnp.log(rel_f / max_exact) / math.log(MAX_DISTANCE / max_exact)
                         * (N_BUCKETS - max_exact)).astype(I32)
    large = jnp.minimum(large, N_BUCKETS - 1)
    return jnp.where(rel < max_exact, rel, large)


def _bias_table_kernel(rb_ref, o_ref):
    h = pl.program_id(0)
    blk = o_ref.shape[2]
    j = lax.broadcasted_iota(I32, (2 * blk, blk), 0)
    i = lax.broadcasted_iota(I32, (2 * blk, blk), 1)
    bucket = _t5_bucket(blk + i - j)
    acc = jnp.zeros((2 * blk, blk), F32)
    for b in range(N_BUCKETS):
        acc = jnp.where(bucket == b, rb_ref[b, h], acc)
    o_ref[0] = acc


def _bias_table(rel_bias):
    n_heads = rel_bias.shape[1]
    blk = MOBA_BLOCK
    return pl.pallas_call(
        _bias_table_kernel,
        grid=(n_heads,),
        in_specs=[pl.BlockSpec(memory_space=pltpu.SMEM)],
        out_specs=pl.BlockSpec((1, 2 * blk, blk), lambda h: (h, 0, 0)),
        out_shape=jax.ShapeDtypeStruct((n_heads, 2 * blk, blk), F32),
        compiler_params=_params(("arbitrary",), 32),
        name="bias_table",
    )(rel_bias)


def _fold_keys(x, op, final):
    while x.shape[0] > 8 and x.shape[0] % 2 == 0:
        half = x.shape[0] // 2
        x = op(x[:half], x[half:])
    return final(x, axis=0, keepdims=True)


def _head_queries(qt, row, hh, scale):
    p, sub = divmod(hh, PAIR)
    return _head_rows(qt[p * LANES:(p + 1) * LANES, :], row, sub, scale)


def _pair_lanes(hh):
    p = hh // PAIR
    return slice(p * LANES, (p + 1) * LANES)


def _head_dims(hh):
    return slice(hh * HEAD_DIM, (hh + 1) * HEAD_DIM)


def _moba_kernel(rb_ref, q_ref, k_ref, v_ref, tbl_ref, o_ref, kmean_sc, vt_sc, sel_sc):
    own = pl.program_id(1)
    blk, width = q_ref.shape[1], q_ref.shape[2]
    n_heads = width // HEAD_DIM
    nb = k_ref.shape[1] // blk
    scale = HEAD_DIM ** -0.5

    @pl.when(own == 0)
    def _():
        for n in range(nb):
            kb = k_ref[0, n * blk:(n + 1) * blk, :].astype(F32)
            kmean_sc[n:n + 1, :] = jnp.mean(kb, axis=0, keepdims=True)
            vt_sc[n] = _transposed_bf16(v_ref[0, n * blk:(n + 1) * blk, :])

    qt = q_ref[0].astype(F32).T
    row = lax.broadcasted_iota(I32, (LANES, blk), 0)
    blk_id = lax.broadcasted_iota(I32, (nb, blk), 0)
    valid = blk_id < own
    causal = (lax.broadcasted_iota(I32, (blk, blk), 0) <= lax.broadcasted_iota(I32, (blk, blk), 1))
    kmean = kmean_sc[...].astype(BF16)

    qs = [_head_queries(qt, row, hh, scale) for hh in range(n_heads)]
    gates = [_dot(kmean[:, _pair_lanes(hh)], _head_queries(qt, row, hh, 1.0)) for hh in range(n_heads)]

    def scores(n):
        offn = pl.multiple_of(n * blk, blk)
        return [_dot(k_ref[0, pl.ds(offn, blk), _pair_lanes(hh)], qs[hh]) for hh in range(n_heads)]

    own_scores = scores(own)

    for hh in range(n_heads):
        gate = jnp.where(valid, gates[hh], NEG)
        rank = jnp.zeros((nb, blk), I32)
        for m in range(nb):
            gm = gate[m:m + 1, :]
            beats = (gm > gate) | ((gm == gate) & (blk_id > m))
            rank = rank + beats.astype(I32)
        sel_sc[hh] = jnp.where(valid & (rank < MOBA_TOPK), 0.0, NEG)

    stats = []
    for hh, s in enumerate(own_scores):
        s = jnp.where(causal, s + tbl_ref[hh, blk:, :], NEG)
        m0 = _fold_keys(s, jnp.maximum, jnp.max)
        p = jnp.exp(s - m0)
        stats.append((m0, _fold_keys(p, jnp.add, jnp.sum), p.astype(BF16)))
    state = []
    for hh, (m0, l0, p) in enumerate(stats):
        state += [m0, l0, _dot(vt_sc[own, _head_dims(hh), :], p)]

    def past_block(n, state, previous):
        stats = []
        for hh, s in enumerate(scores(n)):
            m_run, l_run, _ = state[3 * hh:3 * hh + 3]
            bias = tbl_ref[hh, :blk, :] if previous else rb_ref[N_BUCKETS - 1, hh]
            sn = s + bias + sel_sc[hh, pl.ds(n, 1), :]
            m_new = jnp.maximum(m_run, _fold_keys(sn, jnp.maximum, jnp.max))
            a = jnp.exp(m_run - m_new)
            pn = jnp.exp(sn - m_new)
            stats.append((m_new, a * l_run + _fold_keys(pn, jnp.add, jnp.sum), a, pn.astype(BF16)))
        new = []
        for hh, (m_new, l_new, a, pn) in enumerate(stats):
            new += [m_new, l_new, a * state[3 * hh + 2] + _dot(vt_sc[n, _head_dims(hh), :], pn)]
        return tuple(new)

    state = lax.fori_loop(jnp.maximum(own - 1, 0), own, lambda n, c: past_block(n, c, True), tuple(state))
    state = lax.fori_loop(0, own - 1, lambda n, c: past_block(n, c, False), state)
    out_t = jnp.concatenate([state[3 * hh + 2] / state[3 * hh + 1] for hh in range(n_heads)], axis=0)
    o_ref[0] = out_t.T.astype(o_ref.dtype)


def _moba(qkv3, tbl, rel_bias, col0):
    bsz, seq, _ = qkv3.shape
    assert MAX_DISTANCE <= MOBA_BLOCK and seq % MOBA_BLOCK == 0
    width = N_MOBA_HEADS * HEAD_DIM
    blk = MOBA_BLOCK
    nb = seq // blk
    return pl.pallas_call(
        _moba_kernel,
        grid=(bsz, nb),
        in_specs=[pl.BlockSpec(memory_space=pltpu.SMEM),
                  pl.BlockSpec((1, blk, width), lambda b, i: (b, i, col0)),
                  pl.BlockSpec((1, seq, width), lambda b, i: (b, 0, col0 + 1)),
                  pl.BlockSpec((1, seq, width), lambda b, i: (b, 0, col0 + 2)),
                  pl.BlockSpec((N_MOBA_HEADS, 2 * blk, blk), lambda b, i: (0, 0, 0))],
        out_specs=pl.BlockSpec((1, blk, width), lambda b, i: (b, i, 0)),
        out_shape=jax.ShapeDtypeStruct((bsz, seq, width), BF16),
        scratch_shapes=[pltpu.VMEM((nb, width), F32),
                        pltpu.VMEM((nb, width, blk), BF16),
                        pltpu.VMEM((N_MOBA_HEADS, nb, blk), F32)],
        compiler_params=_params(("arbitrary", "arbitrary"), 48),
        name="moba",
    )(rel_bias, qkv3, qkv3, qkv3, tbl)


def _sb_kernel(q_ref, k_ref, v_ref, o_ref, vt_sc):
    qi = pl.program_id(1)
    t, width = q_ref.shape[1], q_ref.shape[2]
    n_heads = width // HEAD_DIM
    nb = k_ref.shape[1] // t
    scale = HEAD_DIM ** -0.5

    @pl.when(qi == 0)
    def _():
        for n in range(nb):
            vt_sc[n] = _transposed_bf16(v_ref[0, n * t:(n + 1) * t, :])

    qt = q_ref[0].astype(F32).T
    row = lax.broadcasted_iota(I32, (LANES, t), 0)
    key = lax.broadcasted_iota(I32, (t, t), 0)
    qry = lax.broadcasted_iota(I32, (t, t), 1)
    strict = key < qry
    tri = jnp.where(qry >= key, 1.0, 0.0).astype(BF16)
    qs = [_head_queries(qt, row, hh, scale) for hh in range(n_heads)]

    def block(j, carries, diagonal):
        off = pl.multiple_of(j * t, t)
        zs = [_dot(k_ref[0, pl.ds(off, t), _pair_lanes(hh)], qs[hh]) for hh in range(n_heads)]
        csums = []
        for z in zs:
            sp = jnp.maximum(z, 0.0) + jnp.log(1.0 + jnp.exp(-jnp.abs(z)))
            if diagonal:
                sp = jnp.where(strict, sp, 0.0)
            hi = sp.astype(BF16)
            lo = (sp - hi.astype(F32)).astype(BF16)
            csums.append(_dot(tri, hi) + _dot(tri, lo))
        out = []
        for hh in range(n_heads):
            a = jnp.exp(zs[hh] - (csums[hh] + carries[hh]))
            if diagonal:
                a = jnp.where(strict, a, 0.0)
            pv = _dot(vt_sc[j, _head_dims(hh), :], a.astype(BF16))
            out.append((pv, csums[hh][0:1, :]))
        return out

    zero = jnp.zeros((1, t), F32)
    first = block(qi, [zero] * n_heads, True)
    accs = tuple(pv for pv, _ in first)
    carries = tuple(total for _, total in first)

    def lowest(carries):
        return jnp.min(functools.reduce(jnp.minimum, carries))

    def cond(state):
        j, cmin, _, _ = state
        return (j >= 0) & (cmin < SB_ZERO_LOG)

    def body(state):
        j, _, carries, accs = state
        res = block(j, carries, False)
        carries = tuple(c + total for c, (_, total) in zip(carries, res))
        accs = tuple(acc + pv for acc, (pv, _) in zip(accs, res))
        return j - 1, lowest(carries), carries, accs

    state = lax.while_loop(cond, body, (qi - 1, lowest(carries), carries, accs))
    out_t = jnp.concatenate(list(state[3]), axis=0)
    o_ref[0] = out_t.T.astype(o_ref.dtype)


def _stickbreak(qkv3, col0):
    bsz, seq, _ = qkv3.shape
    width = N_SB_HEADS * HEAD_DIM
    t = ATTN_ROWS
    return pl.pallas_call(
        _sb_kernel,
        grid=(bsz, seq // t),
        in_specs=[pl.BlockSpec((1, t, width), lambda b, i: (b, i, col0)),
                  pl.BlockSpec((1, seq, width), lambda b, i: (b, 0, col0 + 1)),
                  pl.BlockSpec((1, seq, width), lambda b, i: (b, 0, col0 + 2))],
        out_specs=pl.BlockSpec((1, t, width), lambda b, i: (b, i, 0)),
        out_shape=jax.ShapeDtypeStruct((bsz, seq, width), BF16),
        scratch_shapes=[pltpu.VMEM((seq // t, width, t), BF16)],
        compiler_params=_params(("arbitrary", "arbitrary"), 48),
        name="stickbreak",
    )(qkv3, qkv3, qkv3)


def _memkv_kernel(m_ref, w_ref, o_ref):
    o_ref[...] = _dot(m_ref[...].astype(BF16), w_ref[...]).astype(BF16)


def _memkv(mem2, w_bf16):
    n, d = mem2.shape
    width = w_bf16.shape[1]
    tm = min(n, 512)
    return pl.pallas_call(
        _memkv_kernel,
        grid=(n // tm,),
        in_specs=[pl.BlockSpec((tm, d), lambda i: (i, 0)),
                  pl.BlockSpec((d, width), lambda i: (0, 0))],
        out_specs=pl.BlockSpec((tm, width), lambda i: (i, 0)),
        out_shape=jax.ShapeDtypeStruct((n, width), BF16),
        compiler_params=_params(("arbitrary",), 32),
        name="memkv",
    )(mem2, w_bf16)


def _mem_kernel(q_ref, k_ref, v_ref, o_ref, vt_sc):
    t, width = q_ref.shape[1], q_ref.shape[2]
    n_heads = width // HEAD_DIM
    scale = HEAD_DIM ** -0.5

    @pl.when(pl.program_id(1) == 0)
    def _():
        vt_sc[...] = _transposed_bf16(v_ref[0])

    qt = q_ref[0].astype(F32).T
    row = lax.broadcasted_iota(I32, (LANES, t), 0)
    qs = [_head_queries(qt, row, hh, scale) for hh in range(n_heads)]
    scores = [_dot(k_ref[0, :, _pair_lanes(hh)], qs[hh]) for hh in range(n_heads)]
    probs = []
    for s in scores:
        e = jnp.exp(s - _fold_keys(s, jnp.maximum, jnp.max))
        probs.append((e * (1.0 / _fold_keys(e, jnp.add, jnp.sum))).astype(BF16))
    out_t = jnp.concatenate([_dot(vt_sc[_head_dims(hh), :], probs[hh]) for hh in range(n_heads)], axis=0)
    o_ref[0] = out_t.T.astype(o_ref.dtype)


def _mem_attention(qkv3, kv3, qcol):
    bsz, seq, _ = qkv3.shape
    n_mem = kv3.shape[1]
    width = N_MEM_HEADS * HEAD_DIM
    t = ATTN_ROWS
    return pl.pallas_call(
        _mem_kernel,
        grid=(bsz, seq // t),
        in_specs=[pl.BlockSpec((1, t, width), lambda b, i: (b, i, qcol)),
                  pl.BlockSpec((1, n_mem, width), lambda b, i: (b, 0, 0)),
                  pl.BlockSpec((1, n_mem, width), lambda b, i: (b, 0, 1))],
        out_specs=pl.BlockSpec((1, t, width), lambda b, i: (b, i, 0)),
        out_shape=jax.ShapeDtypeStruct((bsz, seq, width), BF16),
        scratch_shapes=[pltpu.VMEM((width, n_mem), BF16)],
        compiler_params=_params(("arbitrary", "arbitrary"), 32),
        name="mem_attention",
    )(qkv3, kv3, kv3)


def _merge_kernel(ya_ref, yb_ref, ym_ref, g_ref, x_ref, wa_ref, wb_ref, wm_ref, wo_ref,
                  lng_ref, lnb_ref, wrh_ref, wrl_ref, br_ref, x1_ref, route_ref, cnt_ref,
                  *, alpha, n_experts):
    step = pl.program_id(0)
    tm, d = x_ref.shape

    @pl.when(step == 0)
    def _():
        cnt_ref[...] = jnp.zeros(cnt_ref.shape, F32)

    merged = (_sigmoid(g_ref[:, 0:d]) * _dot(ya_ref[...], wa_ref[...])
              + _sigmoid(g_ref[:, d:2 * d]) * _dot(yb_ref[...], wb_ref[...])
              + _sigmoid(g_ref[:, 2 * d:3 * d]) * _dot(ym_ref[...], wm_ref[...]))
    mix = _dot(merged.astype(BF16), wo_ref[...])
    x1 = _layer_norm(alpha * x_ref[...] + mix, lng_ref[...], lnb_ref[...])
    x1_ref[...] = x1

    xh = x1.astype(BF16)
    xl = (x1 - xh.astype(F32)).astype(BF16)
    logits = (_dot(xh, wrh_ref[...]) + (_dot(xh, wrl_ref[...]) + _dot(xl, wrh_ref[...]))
              + br_ref[...])

    col = lax.broadcasted_iota(I32, (tm, LANES), 1)
    colf = col.astype(F32)
    big = float(LANES)
    gmask = (col >= n_experts) & (col < n_experts + N_GROUPS)
    lg = jnp.where(gmask, logits, -jnp.inf)
    gmax = jnp.max(lg, axis=1, keepdims=True)
    gidx = jnp.min(jnp.where(lg == gmax, colf, big), axis=1, keepdims=True) - n_experts
    g_p = 1.0 / jnp.sum(jnp.where(gmask, jnp.exp(logits - gmax), 0.0), axis=1, keepdims=True)
    lo_col = gidx * EXPERTS_PER_GROUP
    emask = (colf >= lo_col) & (colf < lo_col + EXPERTS_PER_GROUP)
    le = jnp.where(emask, logits, -jnp.inf)
    l1 = jnp.max(le, axis=1, keepdims=True)
    i1 = jnp.min(jnp.where(le == l1, colf, big), axis=1, keepdims=True)
    le2 = jnp.where(colf == i1, -jnp.inf, le)
    l2 = jnp.max(le2, axis=1, keepdims=True)
    i2 = jnp.min(jnp.where(le2 == l2, colf, big), axis=1, keepdims=True)
    e2 = jnp.exp(l2 - l1)
    gate1 = g_p * (1.0 / (1.0 + e2))
    gate2 = g_p * (e2 / (1.0 + e2))

    oh1 = colf == i1
    oh2 = colf == i2
    cnt = jnp.where(oh1 | oh2, 1.0, 0.0)
    rr = lax.broadcasted_iota(I32, (tm, tm), 0)
    cc = lax.broadcasted_iota(I32, (tm, tm), 1)
    before = jnp.where(cc < rr, 1.0, 0.0).astype(BF16)
    base = _dot(before, cnt.astype(BF16)) + cnt_ref[0:1, :]
    rank1 = jnp.sum(jnp.where(oh1, base, 0.0), axis=1, keepdims=True)
    rank2 = jnp.sum(jnp.where(oh2, base, 0.0), axis=1, keepdims=True)
    cnt_ref[...] = cnt_ref[...] + jnp.sum(cnt, axis=0, keepdims=True)

    route = jnp.zeros((tm, LANES), F32)
    for k, val in enumerate((i1, i2, rank1, rank2, gate1, gate2)):
        route = jnp.where(col == k, val, route)
    route_ref[...] = route


def _merge(ya, yb, ym, gates, x2, wa, wb, wm, wo, ln_g, ln_b, wr_hi, wr_lo, b_r, alpha, n_experts):
    n, d = x2.shape
    tm = MERGE_ROWS
    row = lambda w: pl.BlockSpec((tm, w), lambda i: (i, 0))
    full = lambda a: pl.BlockSpec(a.shape, lambda i: (0,) * a.ndim)
    return pl.pallas_call(
        functools.partial(_merge_kernel, alpha=alpha, n_experts=n_experts),
        grid=(n // tm,),
        in_specs=[row(ya.shape[1]), row(yb.shape[1]), row(ym.shape[1]), row(gates.shape[1]), row(d),
                  full(wa), full(wb), full(wm), full(wo), full(ln_g), full(ln_b),
                  full(wr_hi), full(wr_lo), full(b_r)],
        out_specs=[row(d), row(LANES), pl.BlockSpec((8, LANES), lambda i: (0, 0))],
        out_shape=[jax.ShapeDtypeStruct((n, d), F32),
                   jax.ShapeDtypeStruct((n, LANES), F32),
                   jax.ShapeDtypeStruct((8, LANES), F32)],
        compiler_params=_params(("arbitrary",), 52),
        name="merge",
    )(ya, yb, ym, gates, x2, wa, wb, wm, wo, ln_g, ln_b, wr_hi, wr_lo, b_r)


def _row_copy(src_hbm, row, buf, sem, r):
    return pltpu.make_async_copy(src_hbm.at[pl.ds(row, 1), :], buf.at[pl.ds(r, 1), :], sem)


def _gather_rows(src_hbm, idx_ref, buf, sem, n_rows, unrolled):
    if unrolled:
        for r in range(n_rows):
            _row_copy(src_hbm, idx_ref[0, 0, r], buf, sem, r).start()
    else:
        def issue(r, _):
            _row_copy(src_hbm, idx_ref[0, 0, r], buf, sem, r).start()
            return 0
        lax.fori_loop(0, n_rows, issue, 0)


def _wait_rows(src_hbm, buf, sem, n_rows):
    pltpu.make_async_copy(src_hbm.at[pl.ds(0, n_rows), :], buf, sem).wait()


def _experts_kernel(be_ref, nu_ref, idx_ref, idxn_ref, x_hbm, wg_ref, wu_ref, wd_ref, o_ref,
                    xbuf_even, xbuf_odd, sem, wg_sc, wu_sc, wd_sc):
    b = pl.program_id(0)
    n_used = nu_ref[0]
    rows = o_ref.shape[0]
    bufs = (xbuf_even, xbuf_odd)
    parity = lax.rem(b, 2)

    @pl.when((b == 0) & (n_used > 0))
    def _():
        _gather_rows(x_hbm, idx_ref, bufs[0], sem.at[0], rows, False)

    def compute(cur):
        _wait_rows(x_hbm, bufs[cur], sem.at[cur], rows)
        _gather_rows(x_hbm, idxn_ref, bufs[1 - cur], sem.at[1 - cur], rows, True)
        xb = bufs[cur][...].astype(BF16)
        g = _dot(xb, wg_sc[...])
        u = _dot(xb, wu_sc[...])
        h = (g * _sigmoid(g) * u).astype(BF16)
        o_ref[...] = _dot(h, wd_sc[...])

    @pl.when(b < n_used)
    def _():
        changed = (b == 0) | (be_ref[b] != be_ref[jnp.maximum(b - 1, 0)])

        @pl.when(changed)
        def _():
            wg_sc[...] = wg_ref[0].astype(BF16)
            wu_sc[...] = wu_ref[0].astype(BF16)
            wd_sc[...] = wd_ref[0].astype(BF16)

        for cur in range(2):
            pl.when(parity == cur)(functools.partial(compute, cur))

    @pl.when(b >= n_used)
    def _():
        for cur in range(2):
            @pl.when((b == n_used) & (n_used > 0) & (parity == cur))
            def _():
                _wait_rows(x_hbm, bufs[cur], sem.at[cur], rows)

        o_ref[...] = jnp.zeros(o_ref.shape, F32)


def _experts(x1, tok_of_row, block_expert, n_used, w_gate, w_up, w_down):
    n, d = x1.shape
    n_blocks = block_expert.shape[0]
    rows = MOE_BLOCK
    d_exp = w_gate.shape[-1]
    idx3 = tok_of_row.reshape(n_blocks, 1, rows)
    smem_idx = lambda f: pl.BlockSpec((1, 1, rows), f, memory_space=pltpu.SMEM)
    grid_spec = pltpu.PrefetchScalarGridSpec(
        num_scalar_prefetch=2,
        grid=(n_blocks,),
        in_specs=[smem_idx(lambda b, be, nu: (b, 0, 0)),
                  smem_idx(lambda b, be, nu: (jnp.minimum(b + 1, n_blocks - 1), 0, 0)),
                  pl.BlockSpec(memory_space=pl.ANY),
                  pl.BlockSpec((1, d, d_exp), lambda b, be, nu: (be[b], 0, 0)),
                  pl.BlockSpec((1, d, d_exp), lambda b, be, nu: (be[b], 0, 0)),
                  pl.BlockSpec((1, d_exp, d), lambda b, be, nu: (be[b], 0, 0))],
        out_specs=pl.BlockSpec((rows, d), lambda b, be, nu: (b, 0)),
        scratch_shapes=[pltpu.VMEM((rows, d), F32),
                        pltpu.VMEM((rows, d), F32),
                        pltpu.SemaphoreType.DMA((2,)),
                        pltpu.VMEM((d, d_exp), BF16),
                        pltpu.VMEM((d, d_exp), BF16),
                        pltpu.VMEM((d_exp, d), BF16)])
    return pl.pallas_call(
        _experts_kernel,
        grid_spec=grid_spec,
        out_shape=jax.ShapeDtypeStruct((n_blocks * rows, d), F32),
        compiler_params=_params(("arbitrary",), 44),
        name="experts",
    )(block_expert, n_used, idx3, idx3, x1, w_gate, w_up, w_down)


def _combine_kernel(idx_ref, idxn_ref, y_hbm, x1_ref, route_ref, lng_ref, lnb_ref, o_ref,
                    ybuf_even, ybuf_odd, sem, *, alpha):
    i = pl.program_id(0)
    n_steps = pl.num_programs(0)
    tm = x1_ref.shape[0]
    bufs = (ybuf_even, ybuf_odd)
    parity = lax.rem(i, 2)

    @pl.when(i == 0)
    def _():
        _gather_rows(y_hbm, idx_ref, bufs[0], sem.at[0], 2 * tm, False)

    def compute(cur):
        _wait_rows(y_hbm, bufs[cur], sem.at[cur], 2 * tm)
        _gather_rows(y_hbm, idxn_ref, bufs[1 - cur], sem.at[1 - cur], 2 * tm, True)
        ffn = (route_ref[:, 4:5] * bufs[cur][0:tm, :] + route_ref[:, 5:6] * bufs[cur][tm:2 * tm, :])
        o_ref[...] = _layer_norm(alpha * x1_ref[...] + ffn, lng_ref[...], lnb_ref[...])

        @pl.when(i + 1 == n_steps)
        def _():
            _wait_rows(y_hbm, bufs[1 - cur], sem.at[1 - cur], 2 * tm)

    for cur in range(2):
        pl.when(parity == cur)(functools.partial(compute, cur))


def _combine(y_pad, dest, x1, route, ln_g, ln_b, alpha):
    n, d = x1.shape
    tm = COMBINE_ROWS
    n_steps = n // tm
    idx3 = dest.reshape(n_steps, tm, 2).transpose(0, 2, 1).reshape(n_steps, 1, 2 * tm)
    smem_idx = lambda f: pl.BlockSpec((1, 1, 2 * tm), f, memory_space=pltpu.SMEM)
    return pl.pallas_call(
        functools.partial(_combine_kernel, alpha=alpha),
        grid=(n_steps,),
        in_specs=[smem_idx(lambda i: (i, 0, 0)),
                  smem_idx(lambda i: (jnp.minimum(i + 1, n_steps - 1), 0, 0)),
                  pl.BlockSpec(memory_space=pl.ANY),
                  pl.BlockSpec((tm, d), lambda i: (i, 0)),
                  pl.BlockSpec((tm, LANES), lambda i: (i, 0)),
                  pl.BlockSpec((1, d), lambda i: (0, 0)),
                  pl.BlockSpec((1, d), lambda i: (0, 0))],
        out_specs=pl.BlockSpec((tm, d), lambda i: (i, 0)),
        out_shape=jax.ShapeDtypeStruct((n, d), F32),
        scratch_shapes=[pltpu.VMEM((2 * tm, d), F32), pltpu.VMEM((2 * tm, d), F32),
                        pltpu.SemaphoreType.DMA((2,))],
        compiler_params=_params(("arbitrary",), 32),
        name="combine",
    )(idx3, idx3, y_pad, x1, route, ln_g, ln_b)


def _split_bf16(w):
    hi = w.astype(BF16)
    return hi, (w - hi.astype(F32)).astype(BF16)


def _layer(x, mem, tbl, rel_bias, w_in, w_mem_kv, w_br_moba, w_br_sb, w_br_mem, w_out, ln1_g, ln1_b,
           w_rg, b_rg, w_re, b_re, w_gate, w_up, w_down, ln2_g, ln2_b, alpha):
    bsz, seq, d = x.shape
    n = bsz * seq
    n_experts = w_re.shape[1]
    moba_w, sb_w, mem_w = N_MOBA_HEADS * HEAD_DIM, N_SB_HEADS * HEAD_DIM, N_MEM_HEADS * HEAD_DIM
    n_qkv = 3 * moba_w + 3 * sb_w + mem_w
    assert w_in.shape[1] == n_qkv + 3 * d and n_experts + N_GROUPS <= LANES and moba_w == sb_w

    x2 = x.reshape(n, d)
    qkv, gates = _proj(x2, w_in[:, :n_qkv].astype(BF16), w_in[:, n_qkv:].astype(BF16))
    qkv3 = qkv.reshape(bsz, seq, n_qkv)
    y_a = _moba(qkv3, tbl, rel_bias, 0)
    y_b = _stickbreak(qkv3, 3 * moba_w // sb_w)
    kv = _memkv(mem.reshape(-1, d), w_mem_kv.astype(BF16)).reshape(bsz, mem.shape[1], 2 * mem_w)
    y_m = _mem_attention(qkv3, kv, (3 * moba_w + 3 * sb_w) // mem_w)

    w_r = jnp.zeros((d, LANES), F32).at[:, :n_experts].set(w_re).at[:, n_experts:n_experts + N_GROUPS].set(w_rg)
    b_r = jnp.zeros((1, LANES), F32).at[0, :n_experts].set(b_re).at[0, n_experts:n_experts + N_GROUPS].set(b_rg)
    wr_hi, wr_lo = _split_bf16(w_r)
    x1, route, counts = _merge(
        y_a.reshape(n, moba_w), y_b.reshape(n, sb_w), y_m.reshape(n, mem_w), gates, x2,
        w_br_moba.astype(BF16), w_br_sb.astype(BF16), w_br_mem.astype(BF16), w_out.astype(BF16),
        ln1_g.reshape(1, d), ln1_b.reshape(1, d), wr_hi, wr_lo, b_r, alpha, n_experts)

    rows = MOE_BLOCK
    expert = route[:, 0:2].astype(I32)
    rank = route[:, 2:4].astype(I32)
    cnt = counts[0, :n_experts].astype(I32)
    padded = (cnt + rows - 1) // rows * rows
    pend = jnp.cumsum(padded)
    pstart = pend - padded
    dest = pstart[expert] + rank
    n_blocks = (2 * n) // rows + n_experts
    tok = jnp.broadcast_to(jnp.arange(n, dtype=I32)[:, None], (n, 2))
    tok_of_row = jnp.zeros((n_blocks * rows,), I32).at[dest.reshape(-1)].set(
        tok.reshape(-1), unique_indices=True, mode="promise_in_bounds")
    block_start = jnp.arange(n_blocks, dtype=I32) * rows
    block_expert = jnp.minimum(jnp.sum((pend[None, :] <= block_start[:, None]).astype(I32), axis=1),
                               n_experts - 1)
    n_used = (pend[-1] // rows).astype(I32).reshape(1)

    y_pad = _experts(x1, tok_of_row, block_expert, n_used, w_gate, w_up, w_down)
    out = _combine(y_pad, dest, x1, route, ln2_g.reshape(1, d), ln2_b.reshape(1, d), alpha)
    return out.reshape(bsz, seq, d)


def kernel(x, mem, w_in, w_mem_kv, rel_bias, w_br_moba, w_br_sb, w_br_mem, w_out, ln1_g, ln1_b,
           w_router_group, b_router_group, w_router_expert, b_router_expert,
           w_gate, w_up, w_down, ln2_g, ln2_b):
    depth = w_in.shape[0]
    alpha = (2.0 * depth) ** 0.25
    tbl = _bias_table(rel_bias)
    for l in range(depth):
        x = _layer(x, mem, tbl, rel_bias, w_in[l], w_mem_kv[l], w_br_moba[l], w_br_sb[l], w_br_mem[l],
                   w_out[l], ln1_g[l], ln1_b[l], w_router_group[l], b_router_group[l],
                   w_router_expert[l], b_router_expert[l], w_gate[l], w_up[l], w_down[l],
                   ln2_g[l], ln2_b[l], alpha)
    return x
```

```python
import functools
import math

import jax
import jax.numpy as jnp
from jax import lax
from jax.experimental import pallas as pl
from jax.experimental.pallas import tpu as pltpu

F32, BF16, I32 = jnp.float32, jnp.bfloat16, jnp.int32

HEAD_DIM = 64
N_MOBA_HEADS = 6
N_SB_HEADS = 6
N_MEM_HEADS = 4
MOBA_BLOCK = 256
MOBA_TOPK = 3
N_BUCKETS = 32
MAX_DISTANCE = 128
N_GROUPS = 4
EXPERTS_PER_GROUP = 8
MOE_BLOCK = 256
LN_EPS = 1e-5
NEG = -1e30

LANES = 128
VMEM_BYTES = 64 * 1024 * 1024
PAIR = LANES // HEAD_DIM

SB_ZERO_LOG = 110.0

PROJ_ROWS = 512
MERGE_ROWS = 512
ATTN_ROWS = 256
COMBINE_ROWS = 256


def _params(semantics, vmem_mb):
    return pltpu.CompilerParams(dimension_semantics=semantics,
                                vmem_limit_bytes=min(vmem_mb * 1024 * 1024, VMEM_BYTES))


def _dot(a, b):
    return jnp.dot(a, b, preferred_element_type=F32)


def _layer_norm(h, g, b):
    mu = jnp.mean(h, axis=-1, keepdims=True)
    d = h - mu
    var = jnp.mean(d * d, axis=-1, keepdims=True)
    return d * lax.rsqrt(var + LN_EPS) * g + b


def _sigmoid(x):
    return 1.0 / (1.0 + jnp.exp(-x))


def _head_rows(qt, row, hh, scale):
    keep = (row >= hh * HEAD_DIM) & (row < (hh + 1) * HEAD_DIM)
    return jnp.where(keep, qt * scale, 0.0).astype(BF16)


def _transposed_bf16(x):
    return x.astype(F32).T.astype(BF16)


def _proj_kernel(x_ref, wq_ref, wg_ref, qkv_ref, gate_ref):
    xb = x_ref[...].astype(BF16)
    qkv_ref[...] = _dot(xb, wq_ref[...]).astype(BF16)
    gate_ref[...] = _dot(xb, wg_ref[...])


def _proj(x2, w_qkv, w_gates):
    n, d = x2.shape
    n_qkv, n_gates = w_qkv.shape[1], w_gates.shape[1]
    tm = PROJ_ROWS
    return pl.pallas_call(
        _proj_kernel,
        grid=(n // tm,),
        in_specs=[pl.BlockSpec((tm, d), lambda i: (i, 0)),
                  pl.BlockSpec((d, n_qkv), lambda i: (0, 0)),
                  pl.BlockSpec((d, n_gates), lambda i: (0, 0))],
        out_specs=[pl.BlockSpec((tm, n_qkv), lambda i: (i, 0)),
                   pl.BlockSpec((tm, n_gates), lambda i: (i, 0))],
        out_shape=[jax.ShapeDtypeStruct((n, n_qkv), BF16),
                   jax.ShapeDtypeStruct((n, n_gates), F32)],
        compiler_params=_params(("arbitrary",), 56),
        name="proj",
    )(x2, w_qkv, w_gates)


def _t5_bucket(rel):
    rel = jnp.maximum(rel, 0)
    max_exact = N_BUCKETS // 2
    rel_f = jnp.maximum(rel, 1).astype(F32)
    large = max_exact + (jnp.log(rel_f / max_exact) / math.log(MAX_DISTANCE / max_exact)
                         * (N_BUCKETS - max_exact)).astype(I32)
    large = jnp.minimum(large, N_BUCKETS - 1)
    return jnp.where(rel < max_exact, rel, large)


def _bias_table_kernel(rb_ref, o_ref):
    h = pl.program_id(0)
    blk = o_ref.shape[2]
    j = lax.broadcasted_iota(I32, (2 * blk, blk), 0)
    i = lax.broadcasted_iota(I32, (2 * blk, blk), 1)
    bucket = _t5_bucket(blk + i - j)
    acc = jnp.zeros((2 * blk, blk), F32)
    for b in range(N_BUCKETS):
        acc = jnp.where(bucket == b, rb_ref[b, h], acc)
    o_ref[0] = acc


def _bias_table(rel_bias):
    n_heads = rel_bias.shape[1]
    blk = MOBA_BLOCK
    return pl.pallas_call(
        _bias_table_kernel,
        grid=(n_heads,),
        in_specs=[pl.BlockSpec(memory_space=pltpu.SMEM)],
        out_specs=pl.BlockSpec((1, 2 * blk, blk), lambda h: (h, 0, 0)),
        out_shape=jax.ShapeDtypeStruct((n_heads, 2 * blk, blk), F32),
        compiler_params=_params(("arbitrary",), 32),
        name="bias_table",
    )(rel_bias)


def _fold_keys(x, op, final):
    while x.shape[0] > 8 and x.shape[0] % 2 == 0:
        half = x.shape[0] // 2
        x = op(x[:half], x[half:])
    return final(x, axis=0, keepdims=True)


def _head_queries(qt, row, hh, scale):
    p, sub = divmod(hh, PAIR)
    return _head_rows(qt[p * LANES:(p + 1) * LANES, :], row, sub, scale)


def _pair_lanes(hh):
    p = hh // PAIR
    return slice(p * LANES, (p + 1) * LANES)


def _head_dims(hh):
    return slice(hh * HEAD_DIM, (hh + 1) * HEAD_DIM)


def _moba_kernel(rb_ref, q_ref, k_ref, v_ref, tbl_ref, o_ref, kmean_sc, vt_sc, sel_sc):
    own = pl.program_id(1)
    blk, width = q_ref.shape[1], q_ref.shape[2]
    n_heads = width // HEAD_DIM
    nb = k_ref.shape[1] // blk
    scale = HEAD_DIM ** -0.5

    @pl.when(own == 0)
    def _():
        for n in range(nb):
            kb = k_ref[0, n * blk:(n + 1) * blk, :].astype(F32)
            kmean_sc[n:n + 1, :] = jnp.mean(kb, axis=0, keepdims=True)
            vt_sc[n] = _transposed_bf16(v_ref[0, n * blk:(n + 1) * blk, :])

    qt = q_ref[0].astype(F32).T
    row = lax.broadcasted_iota(I32, (LANES, blk), 0)
    blk_id = lax.broadcasted_iota(I32, (nb, blk), 0)
    valid = blk_id < own
    causal = (lax.broadcasted_iota(I32, (blk, blk), 0) <= lax.broadcasted_iota(I32, (blk, blk), 1))
    kmean = kmean_sc[...].astype(BF16)

    qs = [_head_queries(qt, row, hh, scale) for hh in range(n_heads)]
    gates = [_dot(kmean[:, _pair_lanes(hh)], _head_queries(qt, row, hh, 1.0)) for hh in range(n_heads)]

    def scores(n):
        offn = pl.multiple_of(n * blk, blk)
        return [_dot(k_ref[0, pl.ds(offn, blk), _pair_lanes(hh)], qs[hh]) for hh in range(n_heads)]

    own_scores = scores(own)

    for hh in range(n_heads):
        gate = jnp.where(valid, gates[hh], NEG)
        rank = jnp.zeros((nb, blk), I32)
        for m in range(nb):
            gm = gate[m:m + 1, :]
            beats = (gm > gate) | ((gm == gate) & (blk_id > m))
            rank = rank + beats.astype(I32)
        sel_sc[hh] = jnp.where(valid & (rank < MOBA_TOPK), 0.0, NEG)

    stats = []
    for hh, s in enumerate(own_scores):
        s = jnp.where(causal, s + tbl_ref[hh, blk:, :], NEG)
        m0 = _fold_keys(s, jnp.maximum, jnp.max)
        p = jnp.exp(s - m0)
        stats.append((m0, _fold_keys(p, jnp.add, jnp.sum), p.astype(BF16)))
    state = []
    for hh, (m0, l0, p) in enumerate(stats):
        state += [m0, l0, _dot(vt_sc[own, _head_dims(hh), :], p)]

    def past_block(n, state, previous):
        stats = []
        for hh, s in enumerate(scores(n)):
            m_run, l_run, _ = state[3 * hh:3 * hh + 3]
            bias = tbl_ref[hh, :blk, :] if previous else rb_ref[N_BUCKETS - 1, hh]
            sn = s + bias + sel_sc[hh, pl.ds(n, 1), :]
            m_new = jnp.maximum(m_run, _fold_keys(sn, jnp.maximum, jnp.max))
            a = jnp.exp(m_run - m_new)
            pn = jnp.exp(sn - m_new)
            stats.append((m_new, a * l_run + _fold_keys(pn, jnp.add, jnp.sum), a, pn.astype(BF16)))
        new = []
        for hh, (m_new, l_new, a, pn) in enumerate(stats):
            new += [m_new, l_new, a * state[3 * hh + 2] + _dot(vt_sc[n, _head_dims(hh), :], pn)]
        return tuple(new)

    state = lax.fori_loop(jnp.maximum(own - 1, 0), own, lambda n, c: past_block(n, c, True), tuple(state))
    state = lax.fori_loop(0, own - 1, lambda n, c: past_block(n, c, False), state)
    out_t = jnp.concatenate([state[3 * hh + 2] / state[3 * hh + 1] for hh in range(n_heads)], axis=0)
    o_ref[0] = out_t.T.astype(o_ref.dtype)


def _moba(qkv3, tbl, rel_bias, col0):
    bsz, seq, _ = qkv3.shape
    assert MAX_DISTANCE <= MOBA_BLOCK and seq % MOBA_BLOCK == 0
    width = N_MOBA_HEADS * HEAD_DIM
    blk = MOBA_BLOCK
    nb = seq // blk
    return pl.pallas_call(
        _moba_kernel,
        grid=(bsz, nb),
        in_specs=[pl.BlockSpec(memory_space=pltpu.SMEM),
                  pl.BlockSpec((1, blk, width), lambda b, i: (b, i, col0)),
                  pl.BlockSpec((1, seq, width), lambda b, i: (b, 0, col0 + 1)),
                  pl.BlockSpec((1, seq, width), lambda b, i: (b, 0, col0 + 2)),
                  pl.BlockSpec((N_MOBA_HEADS, 2 * blk, blk), lambda b, i: (0, 0, 0))],
        out_specs=pl.BlockSpec((1, blk, width), lambda b, i: (b, i, 0)),
        out_shape=jax.ShapeDtypeStruct((bsz, seq, width), BF16),
        scratch_shapes=[pltpu.VMEM((nb, width), F32),
                        pltpu.VMEM((nb, width, blk), BF16),
                        pltpu.VMEM((N_MOBA_HEADS, nb, blk), F32)],
        compiler_params=_params(("arbitrary", "arbitrary"), 48),
        name="moba",
    )(rel_bias, qkv3, qkv3, qkv3, tbl)


def _sb_kernel(q_ref, k_ref, v_ref, o_ref, vt_sc):
    qi = pl.program_id(1)
    t, width = q_ref.shape[1], q_ref.shape[2]
    n_heads = width // HEAD_DIM
    nb = k_ref.shape[1] // t
    scale = HEAD_DIM ** -0.5

    @pl.when(qi == 0)
    def _():
        for n in range(nb):
            vt_sc[n] = _transposed_bf16(v_ref[0, n * t:(n + 1) * t, :])

    qt = q_ref[0].astype(F32).T
    row = lax.broadcasted_iota(I32, (LANES, t), 0)
    key = lax.broadcasted_iota(I32, (t, t), 0)
    qry = lax.broadcasted_iota(I32, (t, t), 1)
    strict = key < qry
    tri = jnp.where(qry >= key, 1.0, 0.0).astype(BF16)
    qs = [_head_queries(qt, row, hh, scale) for hh in range(n_heads)]

    def block(j, carries, diagonal):
        off = pl.multiple_of(j * t, t)
        zs = [_dot(k_ref[0, pl.ds(off, t), _pair_lanes(hh)], qs[hh]) for hh in range(n_heads)]
        csums = []
        for z in zs:
            sp = jnp.maximum(z, 0.0) + jnp.log(1.0 + jnp.exp(-jnp.abs(z)))
            if diagonal:
                sp = jnp.where(strict, sp, 0.0)
            hi = sp.astype(BF16)
            lo = (sp - hi.astype(F32)).astype(BF16)
            csums.append(_dot(tri, hi) + _dot(tri, lo))
        out = []
        for hh in range(n_heads):
            a = jnp.exp(zs[hh] - (csums[hh] + carries[hh]))
            if diagonal:
                a = jnp.where(strict, a, 0.0)
            pv = _dot(vt_sc[j, _head_dims(hh), :], a.astype(BF16))
            out.append((pv, csums[hh][0:1, :]))
        return out

    zero = jnp.zeros((1, t), F32)
    first = block(qi, [zero] * n_heads, True)
    accs = tuple(pv for pv, _ in first)
    carries = tuple(total for _, total in first)

    def lowest(carries):
        return jnp.min(functools.reduce(jnp.minimum, carries))

    def cond(state):
        j, cmin, _, _ = state
        return (j >= 0) & (cmin < SB_ZERO_LOG)

    def body(state):
        j, _, carries, accs = state
        res = block(j, carries, False)
        carries = tuple(c + total for c, (_, total) in zip(carries, res))
        accs = tuple(acc + pv for acc, (pv, _) in zip(accs, res))
        return j - 1, lowest(carries), carries, accs

    state = lax.while_loop(cond, body, (qi - 1, lowest(carries), carries, accs))
    out_t = jnp.concatenate(list(state[3]), axis=0)
    o_ref[0] = out_t.T.astype(o_ref.dtype)


def _stickbreak(qkv3, col0):
    bsz, seq, _ = qkv3.shape
    width = N_SB_HEADS * HEAD_DIM
    t = ATTN_ROWS
    return pl.pallas_call(
        _sb_kernel,
        grid=(bsz, seq // t),
        in_specs=[pl.BlockSpec((1, t, width), lambda b, i: (b, i, col0)),
                  pl.BlockSpec((1, seq, width), lambda b, i: (b, 0, col0 + 1)),
                  pl.BlockSpec((1, seq, width), lambda b, i: (b, 0, col0 + 2))],
        out_specs=pl.BlockSpec((1, t, width), lambda b, i: (b, i, 0)),
        out_shape=jax.ShapeDtypeStruct((bsz, seq, width), BF16),
        scratch_shapes=[pltpu.VMEM((seq // t, width, t), BF16)],
        compiler_params=_params(("arbitrary", "arbitrary"), 48),
        name="stickbreak",
    )(qkv3, qkv3, qkv3)


def _memkv_kernel(m_ref, w_ref, o_ref):
    o_ref[...] = _dot(m_ref[...].astype(BF16), w_ref[...]).astype(BF16)


def _memkv(mem2, w_bf16):
    n, d = mem2.shape
    width = w_bf16.shape[1]
    tm = min(n, 512)
    return pl.pallas_call(
        _memkv_kernel,
        grid=(n // tm,),
        in_specs=[pl.BlockSpec((tm, d), lambda i: (i, 0)),
                  pl.BlockSpec((d, width), lambda i: (0, 0))],
        out_specs=pl.BlockSpec((tm, width), lambda i: (i, 0)),
        out_shape=jax.ShapeDtypeStruct((n, width), BF16),
        compiler_params=_params(("arbitrary",), 32),
        name="memkv",
    )(mem2, w_bf16)


def _mem_kernel(q_ref, k_ref, v_ref, o_ref, vt_sc):
    t, width = q_ref.shape[1], q_ref.shape[2]
    n_heads = width // HEAD_DIM
    scale = HEAD_DIM ** -0.5

    @pl.when(pl.program_id(1) == 0)
    def _():
        vt_sc[...] = _transposed_bf16(v_ref[0])

    qt = q_ref[0].astype(F32).T
    row = lax.broadcasted_iota(I32, (LANES, t), 0)
    qs = [_head_queries(qt, row, hh, scale) for hh in range(n_heads)]
    scores = [_dot(k_ref[0, :, _pair_lanes(hh)], qs[hh]) for hh in range(n_heads)]
    probs = []
    for s in scores:
        e = jnp.exp(s - _fold_keys(s, jnp.maximum, jnp.max))
        probs.append((e * (1.0 / _fold_keys(e, jnp.add, jnp.sum))).astype(BF16))
    out_t = jnp.concatenate([_dot(vt_sc[_head_dims(hh), :], probs[hh]) for hh in range(n_heads)], axis=0)
    o_ref[0] = out_t.T.astype(o_ref.dtype)


def _mem_attention(qkv3, kv3, qcol):
    bsz, seq, _ = qkv3.shape
    n_mem = kv3.shape[1]
    width = N_MEM_HEADS * HEAD_DIM
    t = ATTN_ROWS
    return pl.pallas_call(
        _mem_kernel,
        grid=(bsz, seq // t),
        in_specs=[pl.BlockSpec((1, t, width), lambda b, i: (b, i, qcol)),
                  pl.BlockSpec((1, n_mem, width), lambda b, i: (b, 0, 0)),
                  pl.BlockSpec((1, n_mem, width), lambda b, i: (b, 0, 1))],
        out_specs=pl.BlockSpec((1, t, width), lambda b, i: (b, i, 0)),
        out_shape=jax.ShapeDtypeStruct((bsz, seq, width), BF16),
        scratch_shapes=[pltpu.VMEM((width, n_mem), BF16)],
        compiler_params=_params(("arbitrary", "arbitrary"), 32),
        name="mem_attention",
    )(qkv3, kv3, kv3)


def _merge_kernel(ya_ref, yb_ref, ym_ref, g_ref, x_ref, wa_ref, wb_ref, wm_ref, wo_ref,
                  lng_ref, lnb_ref, wrh_ref, wrl_ref, br_ref, x1_ref, route_ref, cnt_ref,
                  *, alpha, n_experts):
    step = pl.program_id(0)
    tm, d = x_ref.shape

    @pl.when(step == 0)
    def _():
        cnt_ref[...] = jnp.zeros(cnt_ref.shape, F32)

    merged = (_sigmoid(g_ref[:, 0:d]) * _dot(ya_ref[...], wa_ref[...])
              + _sigmoid(g_ref[:, d:2 * d]) * _dot(yb_ref[...], wb_ref[...])
              + _sigmoid(g_ref[:, 2 * d:3 * d]) * _dot(ym_ref[...], wm_ref[...]))
    mix = _dot(merged.astype(BF16), wo_ref[...])
    x1 = _layer_norm(alpha * x_ref[...] + mix, lng_ref[...], lnb_ref[...])
    x1_ref[...] = x1

    xh = x1.astype(BF16)
    xl = (x1 - xh.astype(F32)).astype(BF16)
    logits = (_dot(xh, wrh_ref[...]) + (_dot(xh, wrl_ref[...]) + _dot(xl, wrh_ref[...]))
              + br_ref[...])

    col = lax.broadcasted_iota(I32, (tm, LANES), 1)
    colf = col.astype(F32)
    big = float(LANES)
    gmask = (col >= n_experts) & (col < n_experts + N_GROUPS)
    lg = jnp.where(gmask, logits, -jnp.inf)
    gmax = jnp.max(lg, axis=1, keepdims=True)
    gidx = jnp.min(jnp.where(lg == gmax, colf, big), axis=1, keepdims=True) - n_experts
    g_p = 1.0 / jnp.sum(jnp.where(gmask, jnp.exp(logits - gmax), 0.0), axis=1, keepdims=True)
    lo_col = gidx * EXPERTS_PER_GROUP
    emask = (colf >= lo_col) & (colf < lo_col + EXPERTS_PER_GROUP)
    le = jnp.where(emask, logits, -jnp.inf)
    l1 = jnp.max(le, axis=1, keepdims=True)
    i1 = jnp.min(jnp.where(le == l1, colf, big), axis=1, keepdims=True)
    le2 = jnp.where(colf == i1, -jnp.inf, le)
    l2 = jnp.max(le2, axis=1, keepdims=True)
    i2 = jnp.min(jnp.where(le2 == l2, colf, big), axis=1, keepdims=True)
    e2 = jnp.exp(l2 - l1)
    gate1 = g_p * (1.0 / (1.0 + e2))
    gate2 = g_p * (e2 / (1.0 + e2))

    oh1 = colf == i1
    oh2 = colf == i2
    cnt = jnp.where(oh1 | oh2, 1.0, 0.0)
    rr = lax.broadcasted_iota(I32, (tm, tm), 0)
    cc = lax.broadcasted_iota(I32, (tm, tm), 1)
    before = jnp.where(cc < rr, 1.0, 0.0).astype(BF16)
    base = _dot(before, cnt.astype(BF16)) + cnt_ref[0:1, :]
    rank1 = jnp.sum(jnp.where(oh1, base, 0.0), axis=1, keepdims=True)
    rank2 = jnp.sum(jnp.where(oh2, base, 0.0), axis=1, keepdims=True)
    cnt_ref[...] = cnt_ref[...] + jnp.sum(cnt, axis=0, keepdims=True)

    route = jnp.zeros((tm, LANES), F32)
    for k, val in enumerate((i1, i2, rank1, rank2, gate1, gate2)):
        route = jnp.where(col == k, val, route)
    route_ref[...] = route


def _merge(ya, yb, ym, gates, x2, wa, wb, wm, wo, ln_g, ln_b, wr_hi, wr_lo, b_r, alpha, n_experts):
    n, d = x2.shape
    tm = MERGE_ROWS
    row = lambda w: pl.BlockSpec((tm, w), lambda i: (i, 0))
    full = lambda a: pl.BlockSpec(a.shape, lambda i: (0,) * a.ndim)
    return pl.pallas_call(
        functools.partial(_merge_kernel, alpha=alpha, n_experts=n_experts),
        grid=(n // tm,),
        in_specs=[row(ya.shape[1]), row(yb.shape[1]), row(ym.shape[1]), row(gates.shape[1]), row(d),
                  full(wa), full(wb), full(wm), full(wo), full(ln_g), full(ln_b),
                  full(wr_hi), full(wr_lo), full(b_r)],
        out_specs=[row(d), row(LANES), pl.BlockSpec((8, LANES), lambda i: (0, 0))],
        out_shape=[jax.ShapeDtypeStruct((n, d), F32),
                   jax.ShapeDtypeStruct((n, LANES), F32),
                   jax.ShapeDtypeStruct((8, LANES), F32)],
        compiler_params=_params(("arbitrary",), 52),
        name="merge",
    )(ya, yb, ym, gates, x2, wa, wb, wm, wo, ln_g, ln_b, wr_hi, wr_lo, b_r)


def _row_copy(src_hbm, row, buf, sem, r):
    return pltpu.make_async_copy(src_hbm.at[pl.ds(row, 1), :], buf.at[pl.ds(r, 1), :], sem)


def _gather_rows(src_hbm, idx_ref, buf, sem, n_rows, unrolled):
    if unrolled:
        for r in range(n_rows):
            _row_copy(src_hbm, idx_ref[0, 0, r], buf, sem, r).start()
    else:
        def issue(r, _):
            _row_copy(src_hbm, idx_ref[0, 0, r], buf, sem, r).start()
            return 0
        lax.fori_loop(0, n_rows, issue, 0)


def _wait_rows(src_hbm, buf, sem, n_rows):
    pltpu.make_async_copy(src_hbm.at[pl.ds(0, n_rows), :], buf, sem).wait()


DISPATCH_ROWS = 512
CHUNK = 8


def _chunk_copy(buf, xpad_hbm, sem, src_row, dst_row):
    return pltpu.make_async_copy(buf.at[pl.ds(src_row, CHUNK), :], xpad_hbm.at[pl.ds(dst_row, CHUNK), :], sem)


def _dispatch_kernel(nch_ref, off_ref, dst_ref, zrow_ref, zflag_ref, nu_ref, base_ref, route_ref, x_ref,
                     xpad_hbm, dest_ref, sorted_even, sorted_odd, zero_buf, sem, zsem):
    i = pl.program_id(0)
    n_steps = pl.num_programs(0)
    n_exp = base_ref.shape[1]
    td = x_ref.shape[0]
    rb = sorted_even.shape[0]
    zrows = zero_buf.shape[0]
    bufs = (sorted_even, sorted_odd)
    parity = lax.rem(i, 2)

    @pl.when(i == 0)
    def _():
        zero_buf[...] = jnp.zeros(zero_buf.shape, F32)

        def zero_block(row, start):
            cp = pltpu.make_async_copy(zero_buf, xpad_hbm.at[pl.ds(pl.multiple_of(row, zrows), zrows), :], zsem)
            cp.start() if start else cp.wait()

        for start in (True, False):
            for e in range(n_exp):
                pl.when(zflag_ref[e] == 1)(functools.partial(zero_block, zrow_ref[e], start))

            def unused(blk, _):
                zero_block(blk * zrows, start)
                return 0
            lax.fori_loop(nu_ref[0], xpad_hbm.shape[0] // zrows, unused, 0)

    rt = route_ref[...].T
    eid = lax.broadcasted_iota(I32, (n_exp, td), 0).astype(F32)
    rowid = lax.broadcasted_iota(I32, (rb, td), 0).astype(F32)
    hit = None
    dests = []
    for k in range(2):
        mine = eid == rt[k:k + 1, :]
        rank = rt[2 + k:3 + k, :]
        pos = jnp.sum(jnp.where(mine, base_ref[0, :, 0:1], 0.0), axis=0, keepdims=True) + rank
        dests.append(jnp.sum(jnp.where(mine, base_ref[0, :, 1:2], 0.0), axis=0, keepdims=True) + rank)
        hit = (rowid == pos) if hit is None else hit | (rowid == pos)
    dest_ref[0] = jnp.concatenate(dests + [jnp.zeros((8 - len(dests), td), F32)], axis=0).astype(I32)
    perm = jnp.where(hit, 1.0, 0.0).astype(BF16)
    rows_sorted = _dot(perm, x_ref[...].astype(BF16))

    def runs(tile, buf, s, start):
        for e in range(n_exp):
            k = tile * n_exp + e
            src0, dst0 = off_ref[k], dst_ref[k]

            def one(j, _):
                cp = _chunk_copy(buf, xpad_hbm, s, pl.multiple_of(src0 + j * CHUNK, CHUNK),
                                 pl.multiple_of(dst0 + j * CHUNK, CHUNK))
                cp.start() if start else cp.wait()
                return 0
            lax.fori_loop(0, nch_ref[k], one, 0)

    def step(cur):
        bufs[cur][...] = rows_sorted

        @pl.when(i > 0)
        def _():
            runs(i - 1, bufs[1 - cur], sem.at[1 - cur], False)

        runs(i, bufs[cur], sem.at[cur], True)

        @pl.when(i + 1 == n_steps)
        def _():
            runs(i, bufs[cur], sem.at[cur], False)

    for cur in range(2):
        pl.when(parity == cur)(functools.partial(step, cur))


def _dispatch(x1, route, nch, off8, dst, zrow, zflag, n_used, base, n_rows_out):
    n, d = x1.shape
    td = DISPATCH_ROWS
    n_tiles = n // td
    n_exp = base.shape[1]
    rb = 2 * td + n_exp * CHUNK
    grid_spec = pltpu.PrefetchScalarGridSpec(
        num_scalar_prefetch=6,
        grid=(n_tiles,),
        in_specs=[pl.BlockSpec((1, n_exp, LANES), lambda i, *_: (i, 0, 0)),
                  pl.BlockSpec((td, LANES), lambda i, *_: (i, 0)),
                  pl.BlockSpec((td, d), lambda i, *_: (i, 0))],
        out_specs=[pl.BlockSpec(memory_space=pl.ANY),
                   pl.BlockSpec((1, 8, td), lambda i, *_: (i, 0, 0))],
        scratch_shapes=[pltpu.VMEM((rb, d), F32), pltpu.VMEM((rb, d), F32),
                        pltpu.VMEM((MOE_BLOCK, d), F32),
                        pltpu.SemaphoreType.DMA((2,)), pltpu.SemaphoreType.DMA(())])
    return pl.pallas_call(
        _dispatch_kernel,
        grid_spec=grid_spec,
        out_shape=[jax.ShapeDtypeStruct((n_rows_out, d), F32),
                   jax.ShapeDtypeStruct((n_tiles, 8, td), I32)],
        compiler_params=_params(("arbitrary",), 48),
        name="dispatch",
    )(nch, off8, dst, zrow, zflag, n_used, base, route, x1)


def _experts_kernel(be_ref, nu_ref, x_ref, wg_ref, wu_ref, wd_ref, o_ref, wg_sc, wu_sc, wd_sc):
    b = pl.program_id(0)
    n_used = nu_ref[0]

    @pl.when(b < n_used)
    def _():
        changed = (b == 0) | (be_ref[b] != be_ref[jnp.maximum(b - 1, 0)])

        @pl.when(changed)
        def _():
            wg_sc[...] = wg_ref[0].astype(BF16)
            wu_sc[...] = wu_ref[0].astype(BF16)
            wd_sc[...] = wd_ref[0].astype(BF16)

        xb = x_ref[...].astype(BF16)
        g = _dot(xb, wg_sc[...])
        u = _dot(xb, wu_sc[...])
        h = (g * _sigmoid(g) * u).astype(BF16)
        o_ref[...] = _dot(h, wd_sc[...])

    @pl.when(b >= n_used)
    def _():
        o_ref[...] = jnp.zeros(o_ref.shape, F32)


def _experts(x_pad, block_expert, n_used, w_gate, w_up, w_down):
    d = x_pad.shape[1]
    n_blocks = block_expert.shape[0]
    rows = MOE_BLOCK
    d_exp = w_gate.shape[-1]
    grid_spec = pltpu.PrefetchScalarGridSpec(
        num_scalar_prefetch=2,
        grid=(n_blocks,),
        in_specs=[pl.BlockSpec((rows, d), lambda b, be, nu: (jnp.minimum(b, jnp.maximum(nu[0] - 1, 0)), 0)),
                  pl.BlockSpec((1, d, d_exp), lambda b, be, nu: (be[b], 0, 0)),
                  pl.BlockSpec((1, d, d_exp), lambda b, be, nu: (be[b], 0, 0)),
                  pl.BlockSpec((1, d_exp, d), lambda b, be, nu: (be[b], 0, 0))],
        out_specs=pl.BlockSpec((rows, d), lambda b, be, nu: (b, 0)),
        scratch_shapes=[pltpu.VMEM((d, d_exp), BF16),
                        pltpu.VMEM((d, d_exp), BF16),
                        pltpu.VMEM((d_exp, d), BF16)])
    return pl.pallas_call(
        _experts_kernel,
        grid_spec=grid_spec,
        out_shape=jax.ShapeDtypeStruct((n_blocks * rows, d), F32),
        compiler_params=_params(("arbitrary",), 44),
        name="experts",
    )(block_expert, n_used, x_pad, w_gate, w_up, w_down)


def _combine_kernel(idx_ref, idxn_ref, y_hbm, x1_ref, route_ref, lng_ref, lnb_ref, o_ref,
                    ybuf_even, ybuf_odd, sem, *, alpha):
    i = pl.program_id(0)
    n_steps = pl.num_programs(0)
    tm = x1_ref.shape[0]
    bufs = (ybuf_even, ybuf_odd)
    parity = lax.rem(i, 2)

    @pl.when(i == 0)
    def _():
        _gather_rows(y_hbm, idx_ref, bufs[0], sem.at[0], 2 * tm, False)

    def compute(cur):
        _gather_rows(y_hbm, idxn_ref, bufs[1 - cur], sem.at[1 - cur], 2 * tm, True)
        _wait_rows(y_hbm, bufs[cur], sem.at[cur], 2 * tm)
        ffn = (route_ref[:, 4:5] * bufs[cur][0:tm, :] + route_ref[:, 5:6] * bufs[cur][tm:2 * tm, :])
        o_ref[...] = _layer_norm(alpha * x1_ref[...] + ffn, lng_ref[...], lnb_ref[...])

        @pl.when(i + 1 == n_steps)
        def _():
            _wait_rows(y_hbm, bufs[1 - cur], sem.at[1 - cur], 2 * tm)

    for cur in range(2):
        pl.when(parity == cur)(functools.partial(compute, cur))


def _combine(y_pad, dest, x1, route, ln_g, ln_b, alpha):
    n, d = x1.shape
    tm = COMBINE_ROWS
    n_steps = n // tm
    idx3 = dest.reshape(n_steps, tm, 2).transpose(0, 2, 1).reshape(n_steps, 1, 2 * tm)
    smem_idx = lambda f: pl.BlockSpec((1, 1, 2 * tm), f, memory_space=pltpu.SMEM)
    return pl.pallas_call(
        functools.partial(_combine_kernel, alpha=alpha),
        grid=(n_steps,),
        in_specs=[smem_idx(lambda i: (i, 0, 0)),
                  smem_idx(lambda i: (jnp.minimum(i + 1, n_steps - 1), 0, 0)),
                  pl.BlockSpec(memory_space=pl.ANY),
                  pl.BlockSpec((tm, d), lambda i: (i, 0)),
                  pl.BlockSpec((tm, LANES), lambda i: (i, 0)),
                  pl.BlockSpec((1, d), lambda i: (0, 0)),
                  pl.BlockSpec((1, d), lambda i: (0, 0))],
        out_specs=pl.BlockSpec((tm, d), lambda i: (i, 0)),
        out_shape=jax.ShapeDtypeStruct((n, d), F32),
        scratch_shapes=[pltpu.VMEM((2 * tm, d), F32), pltpu.VMEM((2 * tm, d), F32),
                        pltpu.SemaphoreType.DMA((2,))],
        compiler_params=_params(("arbitrary",), 32),
        name="combine",
    )(idx3, idx3, y_pad, x1, route, ln_g, ln_b)


def _split_bf16(w):
    hi = w.astype(BF16)
    return hi, (w - hi.astype(F32)).astype(BF16)


def _layer(x, mem, tbl, rel_bias, w_in, w_mem_kv, w_br_moba, w_br_sb, w_br_mem, w_out, ln1_g, ln1_b,
           w_rg, b_rg, w_re, b_re, w_gate, w_up, w_down, ln2_g, ln2_b, alpha):
    bsz, seq, d = x.shape
    n = bsz * seq
    n_experts = w_re.shape[1]
    moba_w, sb_w, mem_w = N_MOBA_HEADS * HEAD_DIM, N_SB_HEADS * HEAD_DIM, N_MEM_HEADS * HEAD_DIM
    n_qkv = 3 * moba_w + 3 * sb_w + mem_w
    assert w_in.shape[1] == n_qkv + 3 * d and n_experts + N_GROUPS <= LANES and moba_w == sb_w

    x2 = x.reshape(n, d)
    qkv, gates = _proj(x2, w_in[:, :n_qkv].astype(BF16), w_in[:, n_qkv:].astype(BF16))
    qkv3 = qkv.reshape(bsz, seq, n_qkv)
    y_a = _moba(qkv3, tbl, rel_bias, 0)
    y_b = _stickbreak(qkv3, 3 * moba_w // sb_w)
    kv = _memkv(mem.reshape(-1, d), w_mem_kv.astype(BF16)).reshape(bsz, mem.shape[1], 2 * mem_w)
    y_m = _mem_attention(qkv3, kv, (3 * moba_w + 3 * sb_w) // mem_w)

    w_r = jnp.zeros((d, LANES), F32).at[:, :n_experts].set(w_re).at[:, n_experts:n_experts + N_GROUPS].set(w_rg)
    b_r = jnp.zeros((1, LANES), F32).at[0, :n_experts].set(b_re).at[0, n_experts:n_experts + N_GROUPS].set(b_rg)
    wr_hi, wr_lo = _split_bf16(w_r)
    x1, route, counts = _merge(
        y_a.reshape(n, moba_w), y_b.reshape(n, sb_w), y_m.reshape(n, mem_w), gates, x2,
        w_br_moba.astype(BF16), w_br_sb.astype(BF16), w_br_mem.astype(BF16), w_out.astype(BF16),
        ln1_g.reshape(1, d), ln1_b.reshape(1, d), wr_hi, wr_lo, b_r, alpha, n_experts)

    rows, td = MOE_BLOCK, DISPATCH_ROWS
    n_tiles = n // td
    expert = route[:, 0:2].astype(I32)
    tile_cnt = jnp.sum((expert.reshape(n_tiles, 2 * td, 1) == jnp.arange(n_experts, dtype=I32)).astype(I32), axis=1)
    n_chunks = (tile_cnt + CHUNK - 1) // CHUNK
    run_rows = n_chunks * CHUNK
    run_rank = jnp.cumsum(tile_cnt, axis=0) - tile_cnt
    run_off = jnp.cumsum(run_rows, axis=0) - run_rows
    sorted_off = jnp.cumsum(run_rows, axis=1) - run_rows
    expert_rows = jnp.sum(run_rows, axis=0)
    padded = (expert_rows + rows - 1) // rows * rows
    pend = jnp.cumsum(padded)
    pstart = pend - padded
    run_dst = pstart[None, :] + run_off
    n_blocks = (2 * n + n_tiles * n_experts * (CHUNK - 1)) // rows + n_experts
    block_start = jnp.arange(n_blocks, dtype=I32) * rows
    block_expert = jnp.minimum(jnp.sum((pend[None, :] <= block_start[:, None]).astype(I32), axis=1),
                               n_experts - 1)
    n_used = (pend[-1] // rows).astype(I32).reshape(1)
    base = jnp.zeros((n_tiles, n_experts, LANES), F32)
    base = base.at[:, :, 0].set((sorted_off - run_rank).astype(F32)).at[:, :, 1].set((run_dst - run_rank).astype(F32))
    x_pad, dest8 = _dispatch(x1, route, n_chunks.reshape(-1), sorted_off.reshape(-1), run_dst.reshape(-1),
                             jnp.maximum(pend - rows, 0), (padded > 0).astype(I32), n_used, base,
                             n_blocks * rows)
    dest = dest8[:, 0:2, :].transpose(0, 2, 1).reshape(n, 2)

    y_pad = _experts(x_pad, block_expert, n_used, w_gate, w_up, w_down)
    out = _combine(y_pad, dest, x1, route, ln2_g.reshape(1, d), ln2_b.reshape(1, d), alpha)
    return out.reshape(bsz, seq, d)


def kernel(x, mem, w_in, w_mem_kv, rel_bias, w_br_moba, w_br_sb, w_br_mem, w_out, ln1_g, ln1_b,
           w_router_group, b_router_group, w_router_expert, b_router_expert,
           w_gate, w_up, w_down, ln2_g, ln2_b):
    depth = w_in.shape[0]
    alpha = (2.0 * depth) ** 0.25
    tbl = _bias_table(rel_bias)
    for l in range(depth):
        x = _layer(x, mem, tbl, rel_bias, w_in[l], w_mem_kv[l], w_br_moba[l], w_br_sb[l], w_br_mem[l],
                   w_out[l], ln1_g[l], ln1_b[l], w_router_group[l], b_router_group[l],
                   w_router_expert[l], b_router_expert[l], w_gate[l], w_up[l], w_down[l],
                   ln2_g[l], ln2_b[l], alpha)
    return x
```

```python
import functools
import math

import jax
import jax.numpy as jnp
from jax import lax
from jax.experimental import pallas as pl
from jax.experimental.pallas import tpu as pltpu

F32, BF16, I32 = jnp.float32, jnp.bfloat16, jnp.int32

HEAD_DIM = 64
N_MOBA_HEADS = 6
N_SB_HEADS = 6
N_MEM_HEADS = 4
MOBA_BLOCK = 256
MOBA_TOPK = 3
N_BUCKETS = 32
MAX_DISTANCE = 128
N_GROUPS = 4
EXPERTS_PER_GROUP = 8
LN_EPS = 1e-5
NEG = -1e30

LANES = 128
VMEM_BYTES = 64 * 1024 * 1024
PAIR = LANES // HEAD_DIM

SB_ZERO_LOG = 110.0

PROJ_ROWS = 512
MERGE_ROWS = 512
ATTN_ROWS = 256
COMBINE_ROWS = 256
MOE_BLOCK = 512


def _params(semantics, vmem_mb):
    return pltpu.CompilerParams(dimension_semantics=semantics,
                                vmem_limit_bytes=min(vmem_mb * 1024 * 1024, VMEM_BYTES))


def _dot(a, b):
    return jnp.dot(a, b, preferred_element_type=F32)


def _layer_norm(h, g, b):
    mu = jnp.mean(h, axis=-1, keepdims=True)
    d = h - mu
    var = jnp.mean(d * d, axis=-1, keepdims=True)
    return d * lax.rsqrt(var + LN_EPS) * g + b


def _sigmoid(x):
    return 1.0 / (1.0 + jnp.exp(-x))


def _head_rows(qt, row, hh, scale):
    keep = (row >= hh * HEAD_DIM) & (row < (hh + 1) * HEAD_DIM)
    return jnp.where(keep, qt * scale, 0.0).astype(BF16)


def _transposed_bf16(x):
    return x.astype(F32).T.astype(BF16)


def _proj_kernel(x_ref, wq_ref, wg_ref, qkv_ref, gate_ref):
    xb = x_ref[...].astype(BF16)
    qkv_ref[...] = _dot(xb, wq_ref[...]).astype(BF16)
    gate_ref[...] = _dot(xb, wg_ref[...])


def _proj(x2, w_qkv, w_gates):
    n, d = x2.shape
    n_qkv, n_gates = w_qkv.shape[1], w_gates.shape[1]
    tm = PROJ_ROWS
    return pl.pallas_call(
        _proj_kernel,
        grid=(n // tm,),
        in_specs=[pl.BlockSpec((tm, d), lambda i: (i, 0)),
                  pl.BlockSpec((d, n_qkv), lambda i: (0, 0)),
                  pl.BlockSpec((d, n_gates), lambda i: (0, 0))],
        out_specs=[pl.BlockSpec((tm, n_qkv), lambda i: (i, 0)),
                   pl.BlockSpec((tm, n_gates), lambda i: (i, 0))],
        out_shape=[jax.ShapeDtypeStruct((n, n_qkv), BF16),
                   jax.ShapeDtypeStruct((n, n_gates), F32)],
        compiler_params=_params(("arbitrary",), 56),
        name="proj",
    )(x2, w_qkv, w_gates)


def _t5_bucket(rel):
    rel = jnp.maximum(rel, 0)
    max_exact = N_BUCKETS // 2
    rel_f = jnp.maximum(rel, 1).astype(F32)
    large = max_exact + (jnp.log(rel_f / max_exact) / math.log(MAX_DISTANCE / max_exact)
                         * (N_BUCKETS - max_exact)).astype(I32)
    large = jnp.minimum(large, N_BUCKETS - 1)
    return jnp.where(rel < max_exact, rel, large)


def _bias_table_kernel(rb_ref, o_ref):
    h = pl.program_id(0)
    blk = o_ref.shape[2]
    j = lax.broadcasted_iota(I32, (2 * blk, blk), 0)
    i = lax.broadcasted_iota(I32, (2 * blk, blk), 1)
    bucket = _t5_bucket(blk + i - j)
    acc = jnp.zeros((2 * blk, blk), F32)
    for b in range(N_BUCKETS):
        acc = jnp.where(bucket == b, rb_ref[b, h], acc)
    o_ref[0] = acc


def _bias_table(rel_bias):
    n_heads = rel_bias.shape[1]
    blk = MOBA_BLOCK
    return pl.pallas_call(
        _bias_table_kernel,
        grid=(n_heads,),
        in_specs=[pl.BlockSpec(memory_space=pltpu.SMEM)],
        out_specs=pl.BlockSpec((1, 2 * blk, blk), lambda h: (h, 0, 0)),
        out_shape=jax.ShapeDtypeStruct((n_heads, 2 * blk, blk), F32),
        compiler_params=_params(("arbitrary",), 32),
        name="bias_table",
    )(rel_bias)


def _fold_keys(x, op, final):
    while x.shape[0] > 8 and x.shape[0] % 2 == 0:
        half = x.shape[0] // 2
        x = op(x[:half], x[half:])
    return final(x, axis=0, keepdims=True)


def _head_queries(qt, row, hh, scale):
    p, sub = divmod(hh, PAIR)
    return _head_rows(qt[p * LANES:(p + 1) * LANES, :], row, sub, scale)


def _pair_lanes(hh):
    p = hh // PAIR
    return slice(p * LANES, (p + 1) * LANES)


def _head_dims(hh):
    return slice(hh * HEAD_DIM, (hh + 1) * HEAD_DIM)


def _moba_kernel(rb_ref, q_ref, k_ref, v_ref, tbl_ref, o_ref, kmean_sc, vt_sc, sel_sc, s_sc, p_sc):
    own = pl.program_id(1)
    blk, width = q_ref.shape[1], q_ref.shape[2]
    n_heads = width // HEAD_DIM
    nb = k_ref.shape[1] // blk
    scale = HEAD_DIM ** -0.5

    @pl.when(own == 0)
    def _():
        for n in range(nb):
            kb = k_ref[0, n * blk:(n + 1) * blk, :].astype(F32)
            kmean_sc[n:n + 1, :] = jnp.mean(kb, axis=0, keepdims=True)
            vt_sc[n] = _transposed_bf16(v_ref[0, n * blk:(n + 1) * blk, :])

    qt = q_ref[0].astype(F32).T
    row = lax.broadcasted_iota(I32, (LANES, blk), 0)
    blk_id = lax.broadcasted_iota(I32, (nb, blk), 0)
    valid = blk_id < own
    causal = (lax.broadcasted_iota(I32, (blk, blk), 0) <= lax.broadcasted_iota(I32, (blk, blk), 1))
    kmean = kmean_sc[...].astype(BF16)

    qs = [_head_queries(qt, row, hh, scale) for hh in range(n_heads)]
    gates = [_dot(kmean[:, _pair_lanes(hh)], _head_queries(qt, row, hh, 1.0)) for hh in range(n_heads)]

    def scores(n):
        offn = pl.multiple_of(n * blk, blk)
        return [_dot(k_ref[0, pl.ds(offn, blk), _pair_lanes(hh)], qs[hh]) for hh in range(n_heads)]

    own_scores = scores(own)

    for hh in range(n_heads):
        gate = jnp.where(valid, gates[hh], NEG)
        rank = jnp.zeros((nb, blk), I32)
        for m in range(nb):
            gm = gate[m:m + 1, :]
            beats = (gm > gate) | ((gm == gate) & (blk_id > m))
            rank = rank + beats.astype(I32)
        sel_sc[hh] = jnp.where(valid & (rank < MOBA_TOPK), 0.0, NEG)

    stats = []
    for hh, s in enumerate(own_scores):
        s = jnp.where(causal, s + tbl_ref[hh, blk:, :], NEG)
        m0 = _fold_keys(s, jnp.maximum, jnp.max)
        p = jnp.exp(s - m0)
        stats.append((m0, _fold_keys(p, jnp.add, jnp.sum), p.astype(BF16)))
    state = []
    for hh, (m0, l0, p) in enumerate(stats):
        state += [m0, l0, _dot(vt_sc[own, _head_dims(hh), :], p)]

    def softmax_step(hh, sn, m_run, l_run):
        m_new = jnp.maximum(m_run, _fold_keys(sn, jnp.maximum, jnp.max))
        a = jnp.exp(m_run - m_new)
        pn = jnp.exp(sn - m_new)
        return m_new, a * l_run + _fold_keys(pn, jnp.add, jnp.sum), a, pn.astype(BF16)

    def previous_block(n, state):
        stats = [softmax_step(hh, s + tbl_ref[hh, :blk, :] + sel_sc[hh, pl.ds(n, 1), :],
                              state[3 * hh], state[3 * hh + 1]) for hh, s in enumerate(scores(n))]
        new = []
        for hh, (m_new, l_new, a, pn) in enumerate(stats):
            new += [m_new, l_new, a * state[3 * hh + 2] + _dot(vt_sc[n, _head_dims(hh), :], pn)]
        return tuple(new)

    state = lax.fori_loop(jnp.maximum(own - 1, 0), own, previous_block, tuple(state))

    n_far = own - 1
    last_far = jnp.maximum(n_far - 1, 0)
    for hh, s in enumerate(scores(0)):
        s_sc[hh] = s
    p_sc[...] = jnp.zeros(p_sc.shape, BF16)

    def far_block(n, carry):
        state, a_prev = carry[:3 * n_heads], carry[3 * n_heads:]
        n_prev = jnp.maximum(n - 1, 0)
        pv = [_dot(vt_sc[n_prev, _head_dims(hh), :], p_sc[hh]) for hh in range(n_heads)]
        s_next = scores(jnp.minimum(n + 1, last_far))
        stats = [softmax_step(hh, s_sc[hh] + (rb_ref[N_BUCKETS - 1, hh] + sel_sc[hh, pl.ds(n, 1), :]),
                              state[3 * hh], state[3 * hh + 1]) for hh in range(n_heads)]
        new, a_new = [], []
        for hh, (m_new, l_new, a, pn) in enumerate(stats):
            p_sc[hh] = pn
            s_sc[hh] = s_next[hh]
            new += [m_new, l_new, a_prev[hh] * state[3 * hh + 2] + pv[hh]]
            a_new.append(a)
        return tuple(new) + tuple(a_new)

    ones = jnp.ones((1, blk), F32)
    carry = lax.fori_loop(0, n_far, far_block, tuple(state) + (ones,) * n_heads)
    state, a_prev = carry[:3 * n_heads], carry[3 * n_heads:]
    out_t = jnp.concatenate(
        [(a_prev[hh] * state[3 * hh + 2] + _dot(vt_sc[last_far, _head_dims(hh), :], p_sc[hh])) / state[3 * hh + 1]
         for hh in range(n_heads)], axis=0)
    o_ref[0] = out_t.T.astype(o_ref.dtype)


def _moba(qkv3, tbl, rel_bias, col0):
    bsz, seq, _ = qkv3.shape
    assert MAX_DISTANCE <= MOBA_BLOCK and seq % MOBA_BLOCK == 0
    width = N_MOBA_HEADS * HEAD_DIM
    blk = MOBA_BLOCK
    nb = seq // blk
    return pl.pallas_call(
        _moba_kernel,
        grid=(bsz, nb),
        in_specs=[pl.BlockSpec(memory_space=pltpu.SMEM),
                  pl.BlockSpec((1, blk, width), lambda b, i: (b, i, col0)),
                  pl.BlockSpec((1, seq, width), lambda b, i: (b, 0, col0 + 1)),
                  pl.BlockSpec((1, seq, width), lambda b, i: (b, 0, col0 + 2)),
                  pl.BlockSpec((N_MOBA_HEADS, 2 * blk, blk), lambda b, i: (0, 0, 0))],
        out_specs=pl.BlockSpec((1, blk, width), lambda b, i: (b, i, 0)),
        out_shape=jax.ShapeDtypeStruct((bsz, seq, width), BF16),
        scratch_shapes=[pltpu.VMEM((nb, width), F32),
                        pltpu.VMEM((nb, width, blk), BF16),
                        pltpu.VMEM((N_MOBA_HEADS, nb, blk), F32),
                        pltpu.VMEM((N_MOBA_HEADS, blk, blk), F32),
                        pltpu.VMEM((N_MOBA_HEADS, blk, blk), BF16)],
        compiler_params=_params(("arbitrary", "arbitrary"), 48),
        name="moba",
    )(rel_bias, qkv3, qkv3, qkv3, tbl)


def _sb_kernel(q_ref, k_ref, v_ref, o_ref, vt_sc):
    qi = pl.program_id(1)
    t, width = q_ref.shape[1], q_ref.shape[2]
    n_heads = width // HEAD_DIM
    nb = k_ref.shape[1] // t
    scale = HEAD_DIM ** -0.5

    @pl.when(qi == 0)
    def _():
        for n in range(nb):
            vt_sc[n] = _transposed_bf16(v_ref[0, n * t:(n + 1) * t, :])

    qt = q_ref[0].astype(F32).T
    row = lax.broadcasted_iota(I32, (LANES, t), 0)
    key = lax.broadcasted_iota(I32, (t, t), 0)
    qry = lax.broadcasted_iota(I32, (t, t), 1)
    strict = key < qry
    tri = jnp.where(qry >= key, 1.0, 0.0).astype(BF16)
    qs = [_head_queries(qt, row, hh, scale) for hh in range(n_heads)]

    def block(j, carries, diagonal):
        off = pl.multiple_of(j * t, t)
        zs = [_dot(k_ref[0, pl.ds(off, t), _pair_lanes(hh)], qs[hh]) for hh in range(n_heads)]
        csums = []
        for z in zs:
            sp = jnp.maximum(z, 0.0) + jnp.log(1.0 + jnp.exp(-jnp.abs(z)))
            if diagonal:
                sp = jnp.where(strict, sp, 0.0)
            hi = sp.astype(BF16)
            lo = (sp - hi.astype(F32)).astype(BF16)
            csums.append(_dot(tri, hi) + _dot(tri, lo))
        out = []
        for hh in range(n_heads):
            a = jnp.exp(zs[hh] - (csums[hh] + carries[hh]))
            if diagonal:
                a = jnp.where(strict, a, 0.0)
            pv = _dot(vt_sc[j, _head_dims(hh), :], a.astype(BF16))
            out.append((pv, csums[hh][0:1, :]))
        return out

    zero = jnp.zeros((1, t), F32)
    first = block(qi, [zero] * n_heads, True)
    accs = tuple(pv for pv, _ in first)
    carries = tuple(total for _, total in first)

    def lowest(carries):
        return jnp.min(functools.reduce(jnp.minimum, carries))

    def cond(state):
        j, cmin, _, _ = state
        return (j >= 0) & (cmin < SB_ZERO_LOG)

    def body(state):
        j, _, carries, accs = state
        res = block(j, carries, False)
        carries = tuple(c + total for c, (_, total) in zip(carries, res))
        accs = tuple(acc + pv for acc, (pv, _) in zip(accs, res))
        return j - 1, lowest(carries), carries, accs

    state = lax.while_loop(cond, body, (qi - 1, lowest(carries), carries, accs))
    out_t = jnp.concatenate(list(state[3]), axis=0)
    o_ref[0] = out_t.T.astype(o_ref.dtype)


def _stickbreak(qkv3, col0):
    bsz, seq, _ = qkv3.shape
    width = N_SB_HEADS * HEAD_DIM
    t = ATTN_ROWS
    return pl.pallas_call(
        _sb_kernel,
        grid=(bsz, seq // t),
        in_specs=[pl.BlockSpec((1, t, width), lambda b, i: (b, i, col0)),
                  pl.BlockSpec((1, seq, width), lambda b, i: (b, 0, col0 + 1)),
                  pl.BlockSpec((1, seq, width), lambda b, i: (b, 0, col0 + 2))],
        out_specs=pl.BlockSpec((1, t, width), lambda b, i: (b, i, 0)),
        out_shape=jax.ShapeDtypeStruct((bsz, seq, width), BF16),
        scratch_shapes=[pltpu.VMEM((seq // t, width, t), BF16)],
        compiler_params=_params(("arbitrary", "arbitrary"), 48),
        name="stickbreak",
    )(qkv3, qkv3, qkv3)


def _memkv_kernel(m_ref, w_ref, o_ref):
    o_ref[...] = _dot(m_ref[...].astype(BF16), w_ref[...]).astype(BF16)


def _memkv(mem2, w_bf16):
    n, d = mem2.shape
    width = w_bf16.shape[1]
    tm = min(n, 512)
    return pl.pallas_call(
        _memkv_kernel,
        grid=(n // tm,),
        in_specs=[pl.BlockSpec((tm, d), lambda i: (i, 0)),
                  pl.BlockSpec((d, width), lambda i: (0, 0))],
        out_specs=pl.BlockSpec((tm, width), lambda i: (i, 0)),
        out_shape=jax.ShapeDtypeStruct((n, width), BF16),
        compiler_params=_params(("arbitrary",), 32),
        name="memkv",
    )(mem2, w_bf16)


def _mem_kernel(q_ref, k_ref, v_ref, o_ref, vt_sc):
    t, width = q_ref.shape[1], q_ref.shape[2]
    n_heads = width // HEAD_DIM
    scale = HEAD_DIM ** -0.5

    @pl.when(pl.program_id(1) == 0)
    def _():
        vt_sc[...] = _transposed_bf16(v_ref[0])

    qt = q_ref[0].astype(F32).T
    row = lax.broadcasted_iota(I32, (LANES, t), 0)
    qs = [_head_queries(qt, row, hh, scale) for hh in range(n_heads)]
    scores = [_dot(k_ref[0, :, _pair_lanes(hh)], qs[hh]) for hh in range(n_heads)]
    probs = []
    for s in scores:
        e = jnp.exp(s - _fold_keys(s, jnp.maximum, jnp.max))
        probs.append((e * (1.0 / _fold_keys(e, jnp.add, jnp.sum))).astype(BF16))
    out_t = jnp.concatenate([_dot(vt_sc[_head_dims(hh), :], probs[hh]) for hh in range(n_heads)], axis=0)
    o_ref[0] = out_t.T.astype(o_ref.dtype)


def _mem_attention(qkv3, kv3, qcol):
    bsz, seq, _ = qkv3.shape
    n_mem = kv3.shape[1]
    width = N_MEM_HEADS * HEAD_DIM
    t = ATTN_ROWS
    return pl.pallas_call(
        _mem_kernel,
        grid=(bsz, seq // t),
        in_specs=[pl.BlockSpec((1, t, width), lambda b, i: (b, i, qcol)),
                  pl.BlockSpec((1, n_mem, width), lambda b, i: (b, 0, 0)),
                  pl.BlockSpec((1, n_mem, width), lambda b, i: (b, 0, 1))],
        out_specs=pl.BlockSpec((1, t, width), lambda b, i: (b, i, 0)),
        out_shape=jax.ShapeDtypeStruct((bsz, seq, width), BF16),
        scratch_shapes=[pltpu.VMEM((width, n_mem), BF16)],
        compiler_params=_params(("arbitrary", "arbitrary"), 32),
        name="mem_attention",
    )(qkv3, kv3, kv3)


def _merge_kernel(ya_ref, yb_ref, ym_ref, g_ref, x_ref, wa_ref, wb_ref, wm_ref, wo_ref,
                  lng_ref, lnb_ref, wrh_ref, wrl_ref, br_ref, x1_ref, route_ref, cnt_ref,
                  *, alpha, n_experts):
    step = pl.program_id(0)
    tm, d = x_ref.shape

    @pl.when(step == 0)
    def _():
        cnt_ref[...] = jnp.zeros(cnt_ref.shape, F32)

    merged = (_sigmoid(g_ref[:, 0:d]) * _dot(ya_ref[...], wa_ref[...])
              + _sigmoid(g_ref[:, d:2 * d]) * _dot(yb_ref[...], wb_ref[...])
              + _sigmoid(g_ref[:, 2 * d:3 * d]) * _dot(ym_ref[...], wm_ref[...]))
    mix = _dot(merged.astype(BF16), wo_ref[...])
    x1 = _layer_norm(alpha * x_ref[...] + mix, lng_ref[...], lnb_ref[...])
    x1_ref[...] = x1

    xh = x1.astype(BF16)
    xl = (x1 - xh.astype(F32)).astype(BF16)
    logits = (_dot(xh, wrh_ref[...]) + (_dot(xh, wrl_ref[...]) + _dot(xl, wrh_ref[...]))
              + br_ref[...])

    col = lax.broadcasted_iota(I32, (tm, LANES), 1)
    colf = col.astype(F32)
    big = float(LANES)
    gmask = (col >= n_experts) & (col < n_experts + N_GROUPS)
    lg = jnp.where(gmask, logits, -jnp.inf)
    gmax = jnp.max(lg, axis=1, keepdims=True)
    gidx = jnp.min(jnp.where(lg == gmax, colf, big), axis=1, keepdims=True) - n_experts
    g_p = 1.0 / jnp.sum(jnp.where(gmask, jnp.exp(logits - gmax), 0.0), axis=1, keepdims=True)
    lo_col = gidx * EXPERTS_PER_GROUP
    emask = (colf >= lo_col) & (colf < lo_col + EXPERTS_PER_GROUP)
    le = jnp.where(emask, logits, -jnp.inf)
    l1 = jnp.max(le, axis=1, keepdims=True)
    i1 = jnp.min(jnp.where(le == l1, colf, big), axis=1, keepdims=True)
    le2 = jnp.where(colf == i1, -jnp.inf, le)
    l2 = jnp.max(le2, axis=1, keepdims=True)
    i2 = jnp.min(jnp.where(le2 == l2, colf, big), axis=1, keepdims=True)
    e2 = jnp.exp(l2 - l1)
    gate1 = g_p * (1.0 / (1.0 + e2))
    gate2 = g_p * (e2 / (1.0 + e2))

    oh1 = colf == i1
    oh2 = colf == i2
    cnt = jnp.where(oh1 | oh2, 1.0, 0.0)
    rr = lax.broadcasted_iota(I32, (tm, tm), 0)
    cc = lax.broadcasted_iota(I32, (tm, tm), 1)
    before = jnp.where(cc < rr, 1.0, 0.0).astype(BF16)
    base = _dot(before, cnt.astype(BF16)) + cnt_ref[0:1, :]
    rank1 = jnp.sum(jnp.where(oh1, base, 0.0), axis=1, keepdims=True)
    rank2 = jnp.sum(jnp.where(oh2, base, 0.0), axis=1, keepdims=True)
    cnt_ref[...] = cnt_ref[...] + jnp.sum(cnt, axis=0, keepdims=True)

    route = jnp.zeros((tm, LANES), F32)
    for k, val in enumerate((i1, i2, rank1, rank2, gate1, gate2)):
        route = jnp.where(col == k, val, route)
    route_ref[...] = route


def _merge(ya, yb, ym, gates, x2, wa, wb, wm, wo, ln_g, ln_b, wr_hi, wr_lo, b_r, alpha, n_experts):
    n, d = x2.shape
    tm = MERGE_ROWS
    row = lambda w: pl.BlockSpec((tm, w), lambda i: (i, 0))
    full = lambda a: pl.BlockSpec(a.shape, lambda i: (0,) * a.ndim)
    return pl.pallas_call(
        functools.partial(_merge_kernel, alpha=alpha, n_experts=n_experts),
        grid=(n // tm,),
        in_specs=[row(ya.shape[1]), row(yb.shape[1]), row(ym.shape[1]), row(gates.shape[1]), row(d),
                  full(wa), full(wb), full(wm), full(wo), full(ln_g), full(ln_b),
                  full(wr_hi), full(wr_lo), full(b_r)],
        out_specs=[row(d), row(LANES), pl.BlockSpec((8, LANES), lambda i: (0, 0))],
        out_shape=[jax.ShapeDtypeStruct((n, d), F32),
                   jax.ShapeDtypeStruct((n, LANES), F32),
                   jax.ShapeDtypeStruct((8, LANES), F32)],
        compiler_params=_params(("arbitrary",), 52),
        name="merge",
    )(ya, yb, ym, gates, x2, wa, wb, wm, wo, ln_g, ln_b, wr_hi, wr_lo, b_r)


def _row_copy(src_hbm, row, buf, sem, r):
    return pltpu.make_async_copy(src_hbm.at[pl.ds(row, 1), :], buf.at[pl.ds(r, 1), :], sem)


def _gather_rows(src_hbm, idx_ref, buf, sem, n_rows, unrolled):
    if unrolled:
        for r in range(n_rows):
            _row_copy(src_hbm, idx_ref[0, 0, r], buf, sem, r).start()
    else:
        def issue(r, _):
            _row_copy(src_hbm, idx_ref[0, 0, r], buf, sem, r).start()
            return 0
        lax.fori_loop(0, n_rows, issue, 0)


def _wait_rows(src_hbm, buf, sem, n_rows):
    pltpu.make_async_copy(src_hbm.at[pl.ds(0, n_rows), :], buf, sem).wait()


DISPATCH_ROWS = 512
CHUNK = 8


def _chunk_copy(buf, xpad_hbm, sem, src_row, dst_row):
    return pltpu.make_async_copy(buf.at[pl.ds(src_row, CHUNK), :], xpad_hbm.at[pl.ds(dst_row, CHUNK), :], sem)


def _dispatch_kernel(nch_ref, off_ref, dst_ref, zrow_ref, zflag_ref, nu_ref, base_ref, route_ref, x_ref,
                     xpad_hbm, dest_ref, sorted_even, sorted_odd, zero_buf, sem, zsem):
    i = pl.program_id(0)
    n_steps = pl.num_programs(0)
    n_exp = base_ref.shape[1]
    td = x_ref.shape[0]
    rb = sorted_even.shape[0]
    zrows = zero_buf.shape[0]
    bufs = (sorted_even, sorted_odd)
    parity = lax.rem(i, 2)

    @pl.when(i == 0)
    def _():
        zero_buf[...] = jnp.zeros(zero_buf.shape, F32)

        def zero_block(row, start):
            cp = pltpu.make_async_copy(zero_buf, xpad_hbm.at[pl.ds(pl.multiple_of(row, zrows), zrows), :], zsem)
            cp.start() if start else cp.wait()

        for start in (True, False):
            for e in range(n_exp):
                pl.when(zflag_ref[e] == 1)(functools.partial(zero_block, zrow_ref[e], start))

            def unused(blk, _):
                zero_block(blk * zrows, start)
                return 0
            lax.fori_loop(nu_ref[0], xpad_hbm.shape[0] // zrows, unused, 0)

    rt = route_ref[...].T
    eid = lax.broadcasted_iota(I32, (n_exp, td), 0).astype(F32)
    rowid = lax.broadcasted_iota(I32, (rb, td), 0).astype(F32)
    hit = None
    dests = []
    for k in range(2):
        mine = eid == rt[k:k + 1, :]
        rank = rt[2 + k:3 + k, :]
        pos = jnp.sum(jnp.where(mine, base_ref[0, :, 0:1], 0.0), axis=0, keepdims=True) + rank
        dests.append(jnp.sum(jnp.where(mine, base_ref[0, :, 1:2], 0.0), axis=0, keepdims=True) + rank)
        hit = (rowid == pos) if hit is None else hit | (rowid == pos)
    dest_ref[0] = jnp.concatenate(dests + [jnp.zeros((8 - len(dests), td), F32)], axis=0).astype(I32)
    perm = jnp.where(hit, 1.0, 0.0).astype(BF16)
    rows_sorted = _dot(perm, x_ref[...].astype(BF16))

    def runs(tile, buf, s, start):
        for e in range(n_exp):
            k = tile * n_exp + e
            src0, dst0 = off_ref[k], dst_ref[k]

            def one(j, _):
                cp = _chunk_copy(buf, xpad_hbm, s, pl.multiple_of(src0 + j * CHUNK, CHUNK),
                                 pl.multiple_of(dst0 + j * CHUNK, CHUNK))
                cp.start() if start else cp.wait()
                return 0
            lax.fori_loop(0, nch_ref[k], one, 0)

    def step(cur):
        bufs[cur][...] = rows_sorted

        @pl.when(i > 0)
        def _():
            runs(i - 1, bufs[1 - cur], sem.at[1 - cur], False)

        runs(i, bufs[cur], sem.at[cur], True)

        @pl.when(i + 1 == n_steps)
        def _():
            runs(i, bufs[cur], sem.at[cur], False)

    for cur in range(2):
        pl.when(parity == cur)(functools.partial(step, cur))


def _dispatch(x1, route, nch, off8, dst, zrow, zflag, n_used, base, n_rows_out):
    n, d = x1.shape
    td = DISPATCH_ROWS
    n_tiles = n // td
    n_exp = base.shape[1]
    rb = 2 * td + n_exp * CHUNK
    grid_spec = pltpu.PrefetchScalarGridSpec(
        num_scalar_prefetch=6,
        grid=(n_tiles,),
        in_specs=[pl.BlockSpec((1, n_exp, LANES), lambda i, *_: (i, 0, 0)),
                  pl.BlockSpec((td, LANES), lambda i, *_: (i, 0)),
                  pl.BlockSpec((td, d), lambda i, *_: (i, 0))],
        out_specs=[pl.BlockSpec(memory_space=pl.ANY),
                   pl.BlockSpec((1, 8, td), lambda i, *_: (i, 0, 0))],
        scratch_shapes=[pltpu.VMEM((rb, d), F32), pltpu.VMEM((rb, d), F32),
                        pltpu.VMEM((MOE_BLOCK, d), F32),
                        pltpu.SemaphoreType.DMA((2,)), pltpu.SemaphoreType.DMA(())])
    return pl.pallas_call(
        _dispatch_kernel,
        grid_spec=grid_spec,
        out_shape=[jax.ShapeDtypeStruct((n_rows_out, d), F32),
                   jax.ShapeDtypeStruct((n_tiles, 8, td), I32)],
        compiler_params=_params(("arbitrary",), 48),
        name="dispatch",
    )(nch, off8, dst, zrow, zflag, n_used, base, route, x1)


def _experts_kernel(be_ref, nu_ref, x_ref, wg_ref, wu_ref, wd_ref, o_ref, wg_sc, wu_sc, wd_sc):
    b = pl.program_id(0)
    n_used = nu_ref[0]

    @pl.when(b < n_used)
    def _():
        changed = (b == 0) | (be_ref[b] != be_ref[jnp.maximum(b - 1, 0)])

        @pl.when(changed)
        def _():
            wg_sc[...] = wg_ref[0].astype(BF16)
            wu_sc[...] = wu_ref[0].astype(BF16)
            wd_sc[...] = wd_ref[0].astype(BF16)

        xb = x_ref[...].astype(BF16)
        g = _dot(xb, wg_sc[...])
        u = _dot(xb, wu_sc[...])
        h = (g * _sigmoid(g) * u).astype(BF16)
        o_ref[...] = _dot(h, wd_sc[...])

    @pl.when(b >= n_used)
    def _():
        o_ref[...] = jnp.zeros(o_ref.shape, F32)


def _experts(x_pad, block_expert, n_used, w_gate, w_up, w_down):
    d = x_pad.shape[1]
    n_blocks = block_expert.shape[0]
    rows = MOE_BLOCK
    d_exp = w_gate.shape[-1]
    grid_spec = pltpu.PrefetchScalarGridSpec(
        num_scalar_prefetch=2,
        grid=(n_blocks,),
        in_specs=[pl.BlockSpec((rows, d), lambda b, be, nu: (jnp.minimum(b, jnp.maximum(nu[0] - 1, 0)), 0)),
                  pl.BlockSpec((1, d, d_exp), lambda b, be, nu: (be[b], 0, 0)),
                  pl.BlockSpec((1, d, d_exp), lambda b, be, nu: (be[b], 0, 0)),
                  pl.BlockSpec((1, d_exp, d), lambda b, be, nu: (be[b], 0, 0))],
        out_specs=pl.BlockSpec((rows, d), lambda b, be, nu: (b, 0)),
        scratch_shapes=[pltpu.VMEM((d, d_exp), BF16),
                        pltpu.VMEM((d, d_exp), BF16),
                        pltpu.VMEM((d_exp, d), BF16)])
    return pl.pallas_call(
        _experts_kernel,
        grid_spec=grid_spec,
        out_shape=jax.ShapeDtypeStruct((n_blocks * rows, d), F32),
        compiler_params=_params(("arbitrary",), 44),
        name="experts",
    )(block_expert, n_used, x_pad, w_gate, w_up, w_down)


def _combine_kernel(idx_ref, idxn_ref, y_hbm, x1_ref, route_ref, lng_ref, lnb_ref, o_ref,
                    ybuf_even, ybuf_odd, sem, *, alpha):
    i = pl.program_id(0)
    n_steps = pl.num_programs(0)
    tm = x1_ref.shape[0]
    bufs = (ybuf_even, ybuf_odd)
    parity = lax.rem(i, 2)

    @pl.when(i == 0)
    def _():
        _gather_rows(y_hbm, idx_ref, bufs[0], sem.at[0], 2 * tm, False)

    def compute(cur):
        _gather_rows(y_hbm, idxn_ref, bufs[1 - cur], sem.at[1 - cur], 2 * tm, True)
        _wait_rows(y_hbm, bufs[cur], sem.at[cur], 2 * tm)
        ffn = (route_ref[:, 4:5] * bufs[cur][0:tm, :] + route_ref[:, 5:6] * bufs[cur][tm:2 * tm, :])
        o_ref[...] = _layer_norm(alpha * x1_ref[...] + ffn, lng_ref[...], lnb_ref[...])

        @pl.when(i + 1 == n_steps)
        def _():
            _wait_rows(y_hbm, bufs[1 - cur], sem.at[1 - cur], 2 * tm)

    for cur in range(2):
        pl.when(parity == cur)(functools.partial(compute, cur))


def _combine(y_pad, dest, x1, route, ln_g, ln_b, alpha):
    n, d = x1.shape
    tm = COMBINE_ROWS
    n_steps = n // tm
    idx3 = dest.reshape(n_steps, tm, 2).transpose(0, 2, 1).reshape(n_steps, 1, 2 * tm)
    smem_idx = lambda f: pl.BlockSpec((1, 1, 2 * tm), f, memory_space=pltpu.SMEM)
    return pl.pallas_call(
        functools.partial(_combine_kernel, alpha=alpha),
        grid=(n_steps,),
        in_specs=[smem_idx(lambda i: (i, 0, 0)),
                  smem_idx(lambda i: (jnp.minimum(i + 1, n_steps - 1), 0, 0)),
                  pl.BlockSpec(memory_space=pl.ANY),
                  pl.BlockSpec((tm, d), lambda i: (i, 0)),
                  pl.BlockSpec((tm, LANES), lambda i: (i, 0)),
                  pl.BlockSpec((1, d), lambda i: (0, 0)),
                  pl.BlockSpec((1, d), lambda i: (0, 0))],
        out_specs=pl.BlockSpec((tm, d), lambda i: (i, 0)),
        out_shape=jax.ShapeDtypeStruct((n, d), F32),
        scratch_shapes=[pltpu.VMEM((2 * tm, d), F32), pltpu.VMEM((2 * tm, d), F32),
                        pltpu.SemaphoreType.DMA((2,))],
        compiler_params=_params(("arbitrary",), 32),
        name="combine",
    )(idx3, idx3, y_pad, x1, route, ln_g, ln_b)


def _split_bf16(w):
    hi = w.astype(BF16)
    return hi, (w - hi.astype(F32)).astype(BF16)


def _layer(x, mem, tbl, rel_bias, w_in, w_mem_kv, w_br_moba, w_br_sb, w_br_mem, w_out, ln1_g, ln1_b,
           w_rg, b_rg, w_re, b_re, w_gate, w_up, w_down, ln2_g, ln2_b, alpha):
    bsz, seq, d = x.shape
    n = bsz * seq
    n_experts = w_re.shape[1]
    moba_w, sb_w, mem_w = N_MOBA_HEADS * HEAD_DIM, N_SB_HEADS * HEAD_DIM, N_MEM_HEADS * HEAD_DIM
    n_qkv = 3 * moba_w + 3 * sb_w + mem_w
    assert w_in.shape[1] == n_qkv + 3 * d and n_experts + N_GROUPS <= LANES and moba_w == sb_w

    x2 = x.reshape(n, d)
    qkv, gates = _proj(x2, w_in[:, :n_qkv].astype(BF16), w_in[:, n_qkv:].astype(BF16))
    qkv3 = qkv.reshape(bsz, seq, n_qkv)
    y_a = _moba(qkv3, tbl, rel_bias, 0)
    y_b = _stickbreak(qkv3, 3 * moba_w // sb_w)
    kv = _memkv(mem.reshape(-1, d), w_mem_kv.astype(BF16)).reshape(bsz, mem.shape[1], 2 * mem_w)
    y_m = _mem_attention(qkv3, kv, (3 * moba_w + 3 * sb_w) // mem_w)

    w_r = jnp.zeros((d, LANES), F32).at[:, :n_experts].set(w_re).at[:, n_experts:n_experts + N_GROUPS].set(w_rg)
    b_r = jnp.zeros((1, LANES), F32).at[0, :n_experts].set(b_re).at[0, n_experts:n_experts + N_GROUPS].set(b_rg)
    wr_hi, wr_lo = _split_bf16(w_r)
    x1, route, counts = _merge(
        y_a.reshape(n, moba_w), y_b.reshape(n, sb_w), y_m.reshape(n, mem_w), gates, x2,
        w_br_moba.astype(BF16), w_br_sb.astype(BF16), w_br_mem.astype(BF16), w_out.astype(BF16),
        ln1_g.reshape(1, d), ln1_b.reshape(1, d), wr_hi, wr_lo, b_r, alpha, n_experts)

    rows, td = MOE_BLOCK, DISPATCH_ROWS
    n_tiles = n // td
    expert = route[:, 0:2].astype(I32)
    tile_cnt = jnp.sum((expert.reshape(n_tiles, 2 * td, 1) == jnp.arange(n_experts, dtype=I32)).astype(I32), axis=1)
    n_chunks = (tile_cnt + CHUNK - 1) // CHUNK
    run_rows = n_chunks * CHUNK
    run_rank = jnp.cumsum(tile_cnt, axis=0) - tile_cnt
    run_off = jnp.cumsum(run_rows, axis=0) - run_rows
    sorted_off = jnp.cumsum(run_rows, axis=1) - run_rows
    expert_rows = jnp.sum(run_rows, axis=0)
    padded = (expert_rows + rows - 1) // rows * rows
    pend = jnp.cumsum(padded)
    pstart = pend - padded
    run_dst = pstart[None, :] + run_off
    n_blocks = (2 * n + n_tiles * n_experts * (CHUNK - 1)) // rows + n_experts
    block_start = jnp.arange(n_blocks, dtype=I32) * rows
    block_expert = jnp.minimum(jnp.sum((pend[None, :] <= block_start[:, None]).astype(I32), axis=1),
                               n_experts - 1)
    n_used = (pend[-1] // rows).astype(I32).reshape(1)
    base = jnp.zeros((n_tiles, n_experts, LANES), F32)
    base = base.at[:, :, 0].set((sorted_off - run_rank).astype(F32)).at[:, :, 1].set((run_dst - run_rank).astype(F32))
    x_pad, dest8 = _dispatch(x1, route, n_chunks.reshape(-1), sorted_off.reshape(-1), run_dst.reshape(-1),
                             jnp.maximum(pend - rows, 0), (padded > 0).astype(I32), n_used, base,
                             n_blocks * rows)
    dest = dest8[:, 0:2, :].transpose(0, 2, 1).reshape(n, 2)

    y_pad = _experts(x_pad, block_expert, n_used, w_gate, w_up, w_down)
    out = _combine(y_pad, dest, x1, route, ln2_g.reshape(1, d), ln2_b.reshape(1, d), alpha)
    return out.reshape(bsz, seq, d)


def kernel(x, mem, w_in, w_mem_kv, rel_bias, w_br_moba, w_br_sb, w_br_mem, w_out, ln1_g, ln1_b,
           w_router_group, b_router_group, w_router_expert, b_router_expert,
           w_gate, w_up, w_down, ln2_g, ln2_b):
    depth = w_in.shape[0]
    alpha = (2.0 * depth) ** 0.25
    tbl = _bias_table(rel_bias)
    for l in range(depth):
        x = _layer(x, mem, tbl, rel_bias, w_in[l], w_mem_kv[l], w_br_moba[l], w_br_sb[l], w_br_mem[l],
                   w_out[l], ln1_g[l], ln1_b[l], w_router_group[l], b_router_group[l],
                   w_router_expert[l], b_router_expert[l], w_gate[l], w_up[l], w_down[l],
                   ln2_g[l], ln2_b[l], alpha)
    return x
```

```python
import functools
import math

import jax
import jax.numpy as jnp
from jax import lax
from jax.experimental import pallas as pl
from jax.experimental.pallas import tpu as pltpu

F32, BF16, I32 = jnp.float32, jnp.bfloat16, jnp.int32

HEAD_DIM = 64
N_MOBA_HEADS = 6
N_SB_HEADS = 6
N_MEM_HEADS = 4
MOBA_BLOCK = 256
MOBA_TOPK = 3
N_BUCKETS = 32
MAX_DISTANCE = 128
N_GROUPS = 4
EXPERTS_PER_GROUP = 8
LN_EPS = 1e-5
NEG = -1e30

LANES = 128
VMEM_BYTES = 64 * 1024 * 1024
PAIR = LANES // HEAD_DIM

SB_ZERO_LOG = 110.0

PROJ_ROWS = 512
MERGE_ROWS = 512
ATTN_ROWS = 256
COMBINE_ROWS = 256
MOE_BLOCK = 512


def _params(semantics, vmem_mb):
    return pltpu.CompilerParams(dimension_semantics=semantics,
                                vmem_limit_bytes=min(vmem_mb * 1024 * 1024, VMEM_BYTES))


def _dot(a, b):
    return jnp.dot(a, b, preferred_element_type=F32)


def _layer_norm(h, g, b):
    mu = jnp.mean(h, axis=-1, keepdims=True)
    d = h - mu
    var = jnp.mean(d * d, axis=-1, keepdims=True)
    return d * lax.rsqrt(var + LN_EPS) * g + b


def _sigmoid(x):
    return 1.0 / (1.0 + jnp.exp(-x))


def _head_rows(qt, row, hh, scale):
    keep = (row >= hh * HEAD_DIM) & (row < (hh + 1) * HEAD_DIM)
    return jnp.where(keep, qt * scale, 0.0).astype(BF16)


def _transposed_bf16(x):
    return x.astype(F32).T.astype(BF16)


def _proj_kernel(x_ref, wq_ref, qkv_ref):
    qkv_ref[...] = _dot(x_ref[...].astype(BF16), wq_ref[...]).astype(BF16)


def _proj(x2, w_qkv):
    n, d = x2.shape
    n_qkv = w_qkv.shape[1]
    tm = PROJ_ROWS
    return pl.pallas_call(
        _proj_kernel,
        grid=(n // tm,),
        in_specs=[pl.BlockSpec((tm, d), lambda i: (i, 0)),
                  pl.BlockSpec((d, n_qkv), lambda i: (0, 0))],
        out_specs=pl.BlockSpec((tm, n_qkv), lambda i: (i, 0)),
        out_shape=jax.ShapeDtypeStruct((n, n_qkv), BF16),
        compiler_params=_params(("arbitrary",), 40),
        name="proj",
    )(x2, w_qkv)


def _t5_bucket(rel):
    rel = jnp.maximum(rel, 0)
    max_exact = N_BUCKETS // 2
    rel_f = jnp.maximum(rel, 1).astype(F32)
    large = max_exact + (jnp.log(rel_f / max_exact) / math.log(MAX_DISTANCE / max_exact)
                         * (N_BUCKETS - max_exact)).astype(I32)
    large = jnp.minimum(large, N_BUCKETS - 1)
    return jnp.where(rel < max_exact, rel, large)


def _bias_table_kernel(rb_ref, o_ref):
    h = pl.program_id(0)
    blk = o_ref.shape[2]
    j = lax.broadcasted_iota(I32, (2 * blk, blk), 0)
    i = lax.broadcasted_iota(I32, (2 * blk, blk), 1)
    bucket = _t5_bucket(blk + i - j)
    acc = jnp.zeros((2 * blk, blk), F32)
    for b in range(N_BUCKETS):
        acc = jnp.where(bucket == b, rb_ref[b, h], acc)
    o_ref[0] = acc


def _bias_table(rel_bias):
    n_heads = rel_bias.shape[1]
    blk = MOBA_BLOCK
    return pl.pallas_call(
        _bias_table_kernel,
        grid=(n_heads,),
        in_specs=[pl.BlockSpec(memory_space=pltpu.SMEM)],
        out_specs=pl.BlockSpec((1, 2 * blk, blk), lambda h: (h, 0, 0)),
        out_shape=jax.ShapeDtypeStruct((n_heads, 2 * blk, blk), F32),
        compiler_params=_params(("arbitrary",), 32),
        name="bias_table",
    )(rel_bias)


def _fold_keys(x, op, final):
    while x.shape[0] > 8 and x.shape[0] % 2 == 0:
        half = x.shape[0] // 2
        x = op(x[:half], x[half:])
    return final(x, axis=0, keepdims=True)


def _head_queries(qt, row, hh, scale):
    p, sub = divmod(hh, PAIR)
    return _head_rows(qt[p * LANES:(p + 1) * LANES, :], row, sub, scale)


def _pair_lanes(hh):
    p = hh // PAIR
    return slice(p * LANES, (p + 1) * LANES)


def _head_dims(hh):
    return slice(hh * HEAD_DIM, (hh + 1) * HEAD_DIM)


def _moba_kernel(rb_ref, q_ref, k_ref, v_ref, tbl_ref, o_ref, kmean_sc, vt_sc, sel_sc, s_sc, p_sc):
    own = pl.program_id(1)
    blk, width = q_ref.shape[1], q_ref.shape[2]
    n_heads = width // HEAD_DIM
    nb = k_ref.shape[1] // blk
    scale = HEAD_DIM ** -0.5

    @pl.when(own == 0)
    def _():
        for n in range(nb):
            kb = k_ref[0, n * blk:(n + 1) * blk, :].astype(F32)
            kmean_sc[n:n + 1, :] = jnp.mean(kb, axis=0, keepdims=True)
            vt_sc[n] = _transposed_bf16(v_ref[0, n * blk:(n + 1) * blk, :])

    qt = q_ref[0].astype(F32).T
    row = lax.broadcasted_iota(I32, (LANES, blk), 0)
    blk_id = lax.broadcasted_iota(I32, (nb, blk), 0)
    valid = blk_id < own
    causal = (lax.broadcasted_iota(I32, (blk, blk), 0) <= lax.broadcasted_iota(I32, (blk, blk), 1))
    kmean = kmean_sc[...].astype(BF16)

    qs = [_head_queries(qt, row, hh, scale) for hh in range(n_heads)]
    gates = [_dot(kmean[:, _pair_lanes(hh)], _head_queries(qt, row, hh, 1.0)) for hh in range(n_heads)]

    def scores(n):
        offn = pl.multiple_of(n * blk, blk)
        return [_dot(k_ref[0, pl.ds(offn, blk), _pair_lanes(hh)], qs[hh]) for hh in range(n_heads)]

    own_scores = scores(own)

    for hh in range(n_heads):
        gate = jnp.where(valid, gates[hh], NEG)
        rank = jnp.zeros((nb, blk), I32)
        for m in range(nb):
            gm = gate[m:m + 1, :]
            beats = (gm > gate) | ((gm == gate) & (blk_id > m))
            rank = rank + beats.astype(I32)
        sel_sc[hh] = jnp.where(valid & (rank < MOBA_TOPK), 0.0, NEG)

    stats = []
    for hh, s in enumerate(own_scores):
        s = jnp.where(causal, s + tbl_ref[hh, blk:, :], NEG)
        m0 = _fold_keys(s, jnp.maximum, jnp.max)
        p = jnp.exp(s - m0)
        stats.append((m0, _fold_keys(p, jnp.add, jnp.sum), p.astype(BF16)))
    state = []
    for hh, (m0, l0, p) in enumerate(stats):
        state += [m0, l0, _dot(vt_sc[own, _head_dims(hh), :], p)]

    def softmax_step(hh, sn, m_run, l_run):
        m_new = jnp.maximum(m_run, _fold_keys(sn, jnp.maximum, jnp.max))
        a = jnp.exp(m_run - m_new)
        pn = jnp.exp(sn - m_new)
        return m_new, a * l_run + _fold_keys(pn, jnp.add, jnp.sum), a, pn.astype(BF16)

    def previous_block(n, state):
        stats = [softmax_step(hh, s + tbl_ref[hh, :blk, :] + sel_sc[hh, pl.ds(n, 1), :],
                              state[3 * hh], state[3 * hh + 1]) for hh, s in enumerate(scores(n))]
        new = []
        for hh, (m_new, l_new, a, pn) in enumerate(stats):
            new += [m_new, l_new, a * state[3 * hh + 2] + _dot(vt_sc[n, _head_dims(hh), :], pn)]
        return tuple(new)

    state = lax.fori_loop(jnp.maximum(own - 1, 0), own, previous_block, tuple(state))

    n_far = own - 1
    last_far = jnp.maximum(n_far - 1, 0)
    for hh, s in enumerate(scores(0)):
        s_sc[hh] = s
    p_sc[...] = jnp.zeros(p_sc.shape, BF16)

    def far_block(n, carry):
        state, a_prev = carry[:3 * n_heads], carry[3 * n_heads:]
        n_prev = jnp.maximum(n - 1, 0)
        pv = [_dot(vt_sc[n_prev, _head_dims(hh), :], p_sc[hh]) for hh in range(n_heads)]
        s_next = scores(jnp.minimum(n + 1, last_far))
        stats = [softmax_step(hh, s_sc[hh] + (rb_ref[N_BUCKETS - 1, hh] + sel_sc[hh, pl.ds(n, 1), :]),
                              state[3 * hh], state[3 * hh + 1]) for hh in range(n_heads)]
        new, a_new = [], []
        for hh, (m_new, l_new, a, pn) in enumerate(stats):
            p_sc[hh] = pn
            s_sc[hh] = s_next[hh]
            new += [m_new, l_new, a_prev[hh] * state[3 * hh + 2] + pv[hh]]
            a_new.append(a)
        return tuple(new) + tuple(a_new)

    ones = jnp.ones((1, blk), F32)
    carry = lax.fori_loop(0, n_far, far_block, tuple(state) + (ones,) * n_heads)
    state, a_prev = carry[:3 * n_heads], carry[3 * n_heads:]
    out_t = jnp.concatenate(
        [(a_prev[hh] * state[3 * hh + 2] + _dot(vt_sc[last_far, _head_dims(hh), :], p_sc[hh])) / state[3 * hh + 1]
         for hh in range(n_heads)], axis=0)
    o_ref[0] = out_t.T.astype(o_ref.dtype)


def _moba(qkv3, tbl, rel_bias, col0):
    bsz, seq, _ = qkv3.shape
    assert MAX_DISTANCE <= MOBA_BLOCK and seq % MOBA_BLOCK == 0
    width = N_MOBA_HEADS * HEAD_DIM
    blk = MOBA_BLOCK
    nb = seq // blk
    return pl.pallas_call(
        _moba_kernel,
        grid=(bsz, nb),
        in_specs=[pl.BlockSpec(memory_space=pltpu.SMEM),
                  pl.BlockSpec((1, blk, width), lambda b, i: (b, i, col0)),
                  pl.BlockSpec((1, seq, width), lambda b, i: (b, 0, col0 + 1)),
                  pl.BlockSpec((1, seq, width), lambda b, i: (b, 0, col0 + 2)),
                  pl.BlockSpec((N_MOBA_HEADS, 2 * blk, blk), lambda b, i: (0, 0, 0))],
        out_specs=pl.BlockSpec((1, blk, width), lambda b, i: (b, i, 0)),
        out_shape=jax.ShapeDtypeStruct((bsz, seq, width), BF16),
        scratch_shapes=[pltpu.VMEM((nb, width), F32),
                        pltpu.VMEM((nb, width, blk), BF16),
                        pltpu.VMEM((N_MOBA_HEADS, nb, blk), F32),
                        pltpu.VMEM((N_MOBA_HEADS, blk, blk), F32),
                        pltpu.VMEM((N_MOBA_HEADS, blk, blk), BF16)],
        compiler_params=_params(("arbitrary", "arbitrary"), 48),
        name="moba",
    )(rel_bias, qkv3, qkv3, qkv3, tbl)


def _sb_kernel(q_ref, k_ref, v_ref, o_ref, vt_sc):
    qi = pl.program_id(1)
    t, width = q_ref.shape[1], q_ref.shape[2]
    n_heads = width // HEAD_DIM
    nb = k_ref.shape[1] // t
    scale = HEAD_DIM ** -0.5

    @pl.when(qi == 0)
    def _():
        for n in range(nb):
            vt_sc[n] = _transposed_bf16(v_ref[0, n * t:(n + 1) * t, :])

    qt = q_ref[0].astype(F32).T
    row = lax.broadcasted_iota(I32, (LANES, t), 0)
    key = lax.broadcasted_iota(I32, (t, t), 0)
    qry = lax.broadcasted_iota(I32, (t, t), 1)
    strict = key < qry
    tri = jnp.where(qry >= key, 1.0, 0.0).astype(BF16)
    qs = [_head_queries(qt, row, hh, scale) for hh in range(n_heads)]

    def block(j, carries, diagonal):
        off = pl.multiple_of(j * t, t)
        zs = [_dot(k_ref[0, pl.ds(off, t), _pair_lanes(hh)], qs[hh]) for hh in range(n_heads)]
        csums = []
        for z in zs:
            sp = jnp.maximum(z, 0.0) + jnp.log(1.0 + jnp.exp(-jnp.abs(z)))
            if diagonal:
                sp = jnp.where(strict, sp, 0.0)
            hi = sp.astype(BF16)
            lo = (sp - hi.astype(F32)).astype(BF16)
            csums.append(_dot(tri, hi) + _dot(tri, lo))
        out = []
        for hh in range(n_heads):
            a = jnp.exp(zs[hh] - (csums[hh] + carries[hh]))
            if diagonal:
                a = jnp.where(strict, a, 0.0)
            pv = _dot(vt_sc[j, _head_dims(hh), :], a.astype(BF16))
            out.append((pv, csums[hh][0:1, :]))
        return out

    zero = jnp.zeros((1, t), F32)
    first = block(qi, [zero] * n_heads, True)
    accs = tuple(pv for pv, _ in first)
    carries = tuple(total for _, total in first)

    def lowest(carries):
        return jnp.min(functools.reduce(jnp.minimum, carries))

    def cond(state):
        j, cmin, _, _ = state
        return (j >= 0) & (cmin < SB_ZERO_LOG)

    def body(state):
        j, _, carries, accs = state
        res = block(j, carries, False)
        carries = tuple(c + total for c, (_, total) in zip(carries, res))
        accs = tuple(acc + pv for acc, (pv, _) in zip(accs, res))
        return j - 1, lowest(carries), carries, accs

    state = lax.while_loop(cond, body, (qi - 1, lowest(carries), carries, accs))
    out_t = jnp.concatenate(list(state[3]), axis=0)
    o_ref[0] = out_t.T.astype(o_ref.dtype)


def _stickbreak(qkv3, col0):
    bsz, seq, _ = qkv3.shape
    width = N_SB_HEADS * HEAD_DIM
    t = ATTN_ROWS
    return pl.pallas_call(
        _sb_kernel,
        grid=(bsz, seq // t),
        in_specs=[pl.BlockSpec((1, t, width), lambda b, i: (b, i, col0)),
                  pl.BlockSpec((1, seq, width), lambda b, i: (b, 0, col0 + 1)),
                  pl.BlockSpec((1, seq, width), lambda b, i: (b, 0, col0 + 2))],
        out_specs=pl.BlockSpec((1, t, width), lambda b, i: (b, i, 0)),
        out_shape=jax.ShapeDtypeStruct((bsz, seq, width), BF16),
        scratch_shapes=[pltpu.VMEM((seq // t, width, t), BF16)],
        compiler_params=_params(("arbitrary", "arbitrary"), 48),
        name="stickbreak",
    )(qkv3, qkv3, qkv3)


def _memkv_kernel(m_ref, w_ref, o_ref):
    o_ref[...] = _dot(m_ref[...].astype(BF16), w_ref[...]).astype(BF16)


def _memkv(mem2, w_bf16):
    n, d = mem2.shape
    width = w_bf16.shape[1]
    tm = min(n, 512)
    return pl.pallas_call(
        _memkv_kernel,
        grid=(n // tm,),
        in_specs=[pl.BlockSpec((tm, d), lambda i: (i, 0)),
                  pl.BlockSpec((d, width), lambda i: (0, 0))],
        out_specs=pl.BlockSpec((tm, width), lambda i: (i, 0)),
        out_shape=jax.ShapeDtypeStruct((n, width), BF16),
        compiler_params=_params(("arbitrary",), 32),
        name="memkv",
    )(mem2, w_bf16)


def _mem_kernel(q_ref, k_ref, v_ref, o_ref, vt_sc):
    t, width = q_ref.shape[1], q_ref.shape[2]
    n_heads = width // HEAD_DIM
    scale = HEAD_DIM ** -0.5

    @pl.when(pl.program_id(1) == 0)
    def _():
        vt_sc[...] = _transposed_bf16(v_ref[0])

    qt = q_ref[0].astype(F32).T
    row = lax.broadcasted_iota(I32, (LANES, t), 0)
    qs = [_head_queries(qt, row, hh, scale) for hh in range(n_heads)]
    scores = [_dot(k_ref[0, :, _pair_lanes(hh)], qs[hh]) for hh in range(n_heads)]
    probs = []
    for s in scores:
        e = jnp.exp(s - _fold_keys(s, jnp.maximum, jnp.max))
        probs.append((e * (1.0 / _fold_keys(e, jnp.add, jnp.sum))).astype(BF16))
    out_t = jnp.concatenate([_dot(vt_sc[_head_dims(hh), :], probs[hh]) for hh in range(n_heads)], axis=0)
    o_ref[0] = out_t.T.astype(o_ref.dtype)


def _mem_attention(qkv3, kv3, qcol):
    bsz, seq, _ = qkv3.shape
    n_mem = kv3.shape[1]
    width = N_MEM_HEADS * HEAD_DIM
    t = ATTN_ROWS
    return pl.pallas_call(
        _mem_kernel,
        grid=(bsz, seq // t),
        in_specs=[pl.BlockSpec((1, t, width), lambda b, i: (b, i, qcol)),
                  pl.BlockSpec((1, n_mem, width), lambda b, i: (b, 0, 0)),
                  pl.BlockSpec((1, n_mem, width), lambda b, i: (b, 0, 1))],
        out_specs=pl.BlockSpec((1, t, width), lambda b, i: (b, i, 0)),
        out_shape=jax.ShapeDtypeStruct((bsz, seq, width), BF16),
        scratch_shapes=[pltpu.VMEM((width, n_mem), BF16)],
        compiler_params=_params(("arbitrary", "arbitrary"), 32),
        name="mem_attention",
    )(qkv3, kv3, kv3)


def _merge_kernel(ya_ref, yb_ref, ym_ref, x_ref, wg_ref, wa_ref, wb_ref, wm_ref, wo_ref,
                  lng_ref, lnb_ref, wrh_ref, wrl_ref, br_ref, x1_ref, route_ref, cnt_ref,
                  *, alpha, n_experts):
    step = pl.program_id(0)
    tm, d = x_ref.shape

    @pl.when(step == 0)
    def _():
        cnt_ref[...] = jnp.zeros(cnt_ref.shape, F32)

    x = x_ref[...]
    xb = x.astype(BF16)
    merged = None
    for k, (y_ref, w_ref) in enumerate(((ya_ref, wa_ref), (yb_ref, wb_ref), (ym_ref, wm_ref))):
        gate_logits = _dot(xb, wg_ref[:, k * d:(k + 1) * d])
        term = _sigmoid(gate_logits) * _dot(y_ref[...], w_ref[...])
        merged = term if merged is None else merged + term
    mix = _dot(merged.astype(BF16), wo_ref[...])
    x1 = _layer_norm(alpha * x + mix, lng_ref[...], lnb_ref[...])
    x1_ref[...] = x1

    xh = x1.astype(BF16)
    xl = (x1 - xh.astype(F32)).astype(BF16)
    logits = (_dot(xh, wrh_ref[...]) + (_dot(xh, wrl_ref[...]) + _dot(xl, wrh_ref[...]))
              + br_ref[...])

    col = lax.broadcasted_iota(I32, (tm, LANES), 1)
    colf = col.astype(F32)
    big = float(LANES)
    gmask = (col >= n_experts) & (col < n_experts + N_GROUPS)
    lg = jnp.where(gmask, logits, -jnp.inf)
    gmax = jnp.max(lg, axis=1, keepdims=True)
    gidx = jnp.min(jnp.where(lg == gmax, colf, big), axis=1, keepdims=True) - n_experts
    g_p = 1.0 / jnp.sum(jnp.where(gmask, jnp.exp(logits - gmax), 0.0), axis=1, keepdims=True)
    lo_col = gidx * EXPERTS_PER_GROUP
    emask = (colf >= lo_col) & (colf < lo_col + EXPERTS_PER_GROUP)
    le = jnp.where(emask, logits, -jnp.inf)
    l1 = jnp.max(le, axis=1, keepdims=True)
    i1 = jnp.min(jnp.where(le == l1, colf, big), axis=1, keepdims=True)
    le2 = jnp.where(colf == i1, -jnp.inf, le)
    l2 = jnp.max(le2, axis=1, keepdims=True)
    i2 = jnp.min(jnp.where(le2 == l2, colf, big), axis=1, keepdims=True)
    e2 = jnp.exp(l2 - l1)
    gate1 = g_p * (1.0 / (1.0 + e2))
    gate2 = g_p * (e2 / (1.0 + e2))

    oh1 = colf == i1
    oh2 = colf == i2
    cnt = jnp.where(oh1 | oh2, 1.0, 0.0)
    rr = lax.broadcasted_iota(I32, (tm, tm), 0)
    cc = lax.broadcasted_iota(I32, (tm, tm), 1)
    before = jnp.where(cc < rr, 1.0, 0.0).astype(BF16)
    base = _dot(before, cnt.astype(BF16)) + cnt_ref[0:1, :]
    rank1 = jnp.sum(jnp.where(oh1, base, 0.0), axis=1, keepdims=True)
    rank2 = jnp.sum(jnp.where(oh2, base, 0.0), axis=1, keepdims=True)
    cnt_ref[...] = cnt_ref[...] + jnp.sum(cnt, axis=0, keepdims=True)

    route = jnp.zeros((tm, LANES), F32)
    for k, val in enumerate((i1, i2, rank1, rank2, gate1, gate2)):
        route = jnp.where(col == k, val, route)
    route_ref[...] = route


def _merge(ya, yb, ym, x2, w_gates, wa, wb, wm, wo, ln_g, ln_b, wr_hi, wr_lo, b_r, alpha, n_experts):
    n, d = x2.shape
    tm = MERGE_ROWS
    row = lambda w: pl.BlockSpec((tm, w), lambda i: (i, 0))
    full = lambda a: pl.BlockSpec(a.shape, lambda i: (0,) * a.ndim)
    return pl.pallas_call(
        functools.partial(_merge_kernel, alpha=alpha, n_experts=n_experts),
        grid=(n // tm,),
        in_specs=[row(ya.shape[1]), row(yb.shape[1]), row(ym.shape[1]), row(d),
                  full(w_gates), full(wa), full(wb), full(wm), full(wo), full(ln_g), full(ln_b),
                  full(wr_hi), full(wr_lo), full(b_r)],
        out_specs=[row(d), row(LANES), pl.BlockSpec((8, LANES), lambda i: (0, 0))],
        out_shape=[jax.ShapeDtypeStruct((n, d), F32),
                   jax.ShapeDtypeStruct((n, LANES), F32),
                   jax.ShapeDtypeStruct((8, LANES), F32)],
        compiler_params=_params(("arbitrary",), 52),
        name="merge",
    )(ya, yb, ym, x2, w_gates, wa, wb, wm, wo, ln_g, ln_b, wr_hi, wr_lo, b_r)


def _row_copy(src_hbm, row, buf, sem, r):
    return pltpu.make_async_copy(src_hbm.at[pl.ds(row, 1), :], buf.at[pl.ds(r, 1), :], sem)


def _gather_rows(src_hbm, idx_ref, buf, sem, n_rows, unrolled):
    if unrolled:
        for r in range(n_rows):
            _row_copy(src_hbm, idx_ref[0, 0, r], buf, sem, r).start()
    else:
        def issue(r, _):
            _row_copy(src_hbm, idx_ref[0, 0, r], buf, sem, r).start()
            return 0
        lax.fori_loop(0, n_rows, issue, 0)


def _wait_rows(src_hbm, buf, sem, n_rows):
    pltpu.make_async_copy(src_hbm.at[pl.ds(0, n_rows), :], buf, sem).wait()


DISPATCH_ROWS = 512
CHUNK = 8


def _chunk_copy(buf, xpad_hbm, sem, src_row, dst_row):
    return pltpu.make_async_copy(buf.at[pl.ds(src_row, CHUNK), :], xpad_hbm.at[pl.ds(dst_row, CHUNK), :], sem)


def _dispatch_kernel(nch_ref, off_ref, dst_ref, zrow_ref, zflag_ref, nu_ref, base_ref, route_ref, x_ref,
                     xpad_hbm, dest_ref, sorted_even, sorted_odd, zero_buf, sem, zsem):
    i = pl.program_id(0)
    n_steps = pl.num_programs(0)
    n_exp = base_ref.shape[1]
    td = x_ref.shape[0]
    rb = sorted_even.shape[0]
    zrows = zero_buf.shape[0]
    bufs = (sorted_even, sorted_odd)
    parity = lax.rem(i, 2)

    @pl.when(i == 0)
    def _():
        zero_buf[...] = jnp.zeros(zero_buf.shape, F32)

        def zero_block(row, start):
            cp = pltpu.make_async_copy(zero_buf, xpad_hbm.at[pl.ds(pl.multiple_of(row, zrows), zrows), :], zsem)
            cp.start() if start else cp.wait()

        for start in (True, False):
            for e in range(n_exp):
                pl.when(zflag_ref[e] == 1)(functools.partial(zero_block, zrow_ref[e], start))

            def unused(blk, _):
                zero_block(blk * zrows, start)
                return 0
            lax.fori_loop(nu_ref[0], xpad_hbm.shape[0] // zrows, unused, 0)

    rt = route_ref[...].T
    eid = lax.broadcasted_iota(I32, (n_exp, td), 0).astype(F32)
    rowid = lax.broadcasted_iota(I32, (rb, td), 0).astype(F32)
    hit = None
    dests = []
    for k in range(2):
        mine = eid == rt[k:k + 1, :]
        rank = rt[2 + k:3 + k, :]
        pos = jnp.sum(jnp.where(mine, base_ref[0, :, 0:1], 0.0), axis=0, keepdims=True) + rank
        dests.append(jnp.sum(jnp.where(mine, base_ref[0, :, 1:2], 0.0), axis=0, keepdims=True) + rank)
        hit = (rowid == pos) if hit is None else hit | (rowid == pos)
    dest_ref[0] = jnp.concatenate(dests + [jnp.zeros((8 - len(dests), td), F32)], axis=0).astype(I32)
    perm = jnp.where(hit, 1.0, 0.0).astype(BF16)
    rows_sorted = _dot(perm, x_ref[...].astype(BF16))

    def runs(tile, buf, s, start):
        for e in range(n_exp):
            k = tile * n_exp + e
            src0, dst0 = off_ref[k], dst_ref[k]

            def one(j, _):
                cp = _chunk_copy(buf, xpad_hbm, s, pl.multiple_of(src0 + j * CHUNK, CHUNK),
                                 pl.multiple_of(dst0 + j * CHUNK, CHUNK))
                cp.start() if start else cp.wait()
                return 0
            lax.fori_loop(0, nch_ref[k], one, 0)

    def step(cur):
        bufs[cur][...] = rows_sorted

        @pl.when(i > 0)
        def _():
            runs(i - 1, bufs[1 - cur], sem.at[1 - cur], False)

        runs(i, bufs[cur], sem.at[cur], True)

        @pl.when(i + 1 == n_steps)
        def _():
            runs(i, bufs[cur], sem.at[cur], False)

    for cur in range(2):
        pl.when(parity == cur)(functools.partial(step, cur))


def _dispatch(x1, route, nch, off8, dst, zrow, zflag, n_used, base, n_rows_out):
    n, d = x1.shape
    td = DISPATCH_ROWS
    n_tiles = n // td
    n_exp = base.shape[1]
    rb = 2 * td + n_exp * CHUNK
    grid_spec = pltpu.PrefetchScalarGridSpec(
        num_scalar_prefetch=6,
        grid=(n_tiles,),
        in_specs=[pl.BlockSpec((1, n_exp, LANES), lambda i, *_: (i, 0, 0)),
                  pl.BlockSpec((td, LANES), lambda i, *_: (i, 0)),
                  pl.BlockSpec((td, d), lambda i, *_: (i, 0))],
        out_specs=[pl.BlockSpec(memory_space=pl.ANY),
                   pl.BlockSpec((1, 8, td), lambda i, *_: (i, 0, 0))],
        scratch_shapes=[pltpu.VMEM((rb, d), F32), pltpu.VMEM((rb, d), F32),
                        pltpu.VMEM((MOE_BLOCK, d), F32),
                        pltpu.SemaphoreType.DMA((2,)), pltpu.SemaphoreType.DMA(())])
    return pl.pallas_call(
        _dispatch_kernel,
        grid_spec=grid_spec,
        out_shape=[jax.ShapeDtypeStruct((n_rows_out, d), F32),
                   jax.ShapeDtypeStruct((n_tiles, 8, td), I32)],
        compiler_params=_params(("arbitrary",), 48),
        name="dispatch",
    )(nch, off8, dst, zrow, zflag, n_used, base, route, x1)


def _experts_kernel(be_ref, nu_ref, x_ref, wg_ref, wu_ref, wd_ref, o_ref, wg_sc, wu_sc, wd_sc):
    b = pl.program_id(0)
    n_used = nu_ref[0]

    @pl.when(b < n_used)
    def _():
        changed = (b == 0) | (be_ref[b] != be_ref[jnp.maximum(b - 1, 0)])

        @pl.when(changed)
        def _():
            wg_sc[...] = wg_ref[0].astype(BF16)
            wu_sc[...] = wu_ref[0].astype(BF16)
            wd_sc[...] = wd_ref[0].astype(BF16)

        xb = x_ref[...].astype(BF16)
        g = _dot(xb, wg_sc[...])
        u = _dot(xb, wu_sc[...])
        h = (g * _sigmoid(g) * u).astype(BF16)
        o_ref[...] = _dot(h, wd_sc[...])

    @pl.when(b >= n_used)
    def _():
        o_ref[...] = jnp.zeros(o_ref.shape, F32)


def _experts(x_pad, block_expert, n_used, w_gate, w_up, w_down):
    d = x_pad.shape[1]
    n_blocks = block_expert.shape[0]
    rows = MOE_BLOCK
    d_exp = w_gate.shape[-1]
    grid_spec = pltpu.PrefetchScalarGridSpec(
        num_scalar_prefetch=2,
        grid=(n_blocks,),
        in_specs=[pl.BlockSpec((rows, d), lambda b, be, nu: (jnp.minimum(b, jnp.maximum(nu[0] - 1, 0)), 0)),
                  pl.BlockSpec((1, d, d_exp), lambda b, be, nu: (be[b], 0, 0)),
                  pl.BlockSpec((1, d, d_exp), lambda b, be, nu: (be[b], 0, 0)),
                  pl.BlockSpec((1, d_exp, d), lambda b, be, nu: (be[b], 0, 0))],
        out_specs=pl.BlockSpec((rows, d), lambda b, be, nu: (b, 0)),
        scratch_shapes=[pltpu.VMEM((d, d_exp), BF16),
                        pltpu.VMEM((d, d_exp), BF16),
                        pltpu.VMEM((d_exp, d), BF16)])
    return pl.pallas_call(
        _experts_kernel,
        grid_spec=grid_spec,
        out_shape=jax.ShapeDtypeStruct((n_blocks * rows, d), F32),
        compiler_params=_params(("arbitrary",), 44),
        name="experts",
    )(block_expert, n_used, x_pad, w_gate, w_up, w_down)


def _combine_kernel(idx_ref, idxn_ref, y_hbm, x1_ref, route_ref, lng_ref, lnb_ref, o_ref,
                    ybuf_even, ybuf_odd, sem, *, alpha):
    i = pl.program_id(0)
    n_steps = pl.num_programs(0)
    tm = x1_ref.shape[0]
    bufs = (ybuf_even, ybuf_odd)
    parity = lax.rem(i, 2)

    @pl.when(i == 0)
    def _():
        _gather_rows(y_hbm, idx_ref, bufs[0], sem.at[0], 2 * tm, False)

    def compute(cur):
        _gather_rows(y_hbm, idxn_ref, bufs[1 - cur], sem.at[1 - cur], 2 * tm, True)
        _wait_rows(y_hbm, bufs[cur], sem.at[cur], 2 * tm)
        ffn = (route_ref[:, 4:5] * bufs[cur][0:tm, :] + route_ref[:, 5:6] * bufs[cur][tm:2 * tm, :])
        o_ref[...] = _layer_norm(alpha * x1_ref[...] + ffn, lng_ref[...], lnb_ref[...])

        @pl.when(i + 1 == n_steps)
        def _():
            _wait_rows(y_hbm, bufs[1 - cur], sem.at[1 - cur], 2 * tm)

    for cur in range(2):
        pl.when(parity == cur)(functools.partial(compute, cur))


def _combine(y_pad, dest, x1, route, ln_g, ln_b, alpha):
    n, d = x1.shape
    tm = COMBINE_ROWS
    n_steps = n // tm
    idx3 = dest.reshape(n_steps, tm, 2).transpose(0, 2, 1).reshape(n_steps, 1, 2 * tm)
    smem_idx = lambda f: pl.BlockSpec((1, 1, 2 * tm), f, memory_space=pltpu.SMEM)
    return pl.pallas_call(
        functools.partial(_combine_kernel, alpha=alpha),
        grid=(n_steps,),
        in_specs=[smem_idx(lambda i: (i, 0, 0)),
                  smem_idx(lambda i: (jnp.minimum(i + 1, n_steps - 1), 0, 0)),
                  pl.BlockSpec(memory_space=pl.ANY),
                  pl.BlockSpec((tm, d), lambda i: (i, 0)),
                  pl.BlockSpec((tm, LANES), lambda i: (i, 0)),
                  pl.BlockSpec((1, d), lambda i: (0, 0)),
                  pl.BlockSpec((1, d), lambda i: (0, 0))],
        out_specs=pl.BlockSpec((tm, d), lambda i: (i, 0)),
        out_shape=jax.ShapeDtypeStruct((n, d), F32),
        scratch_shapes=[pltpu.VMEM((2 * tm, d), F32), pltpu.VMEM((2 * tm, d), F32),
                        pltpu.SemaphoreType.DMA((2,))],
        compiler_params=_params(("arbitrary",), 32),
        name="combine",
    )(idx3, idx3, y_pad, x1, route, ln_g, ln_b)


def _split_bf16(w):
    hi = w.astype(BF16)
    return hi, (w - hi.astype(F32)).astype(BF16)


def _layer(x, mem, tbl, rel_bias, w_in, w_mem_kv, w_br_moba, w_br_sb, w_br_mem, w_out, ln1_g, ln1_b,
           w_rg, b_rg, w_re, b_re, w_gate, w_up, w_down, ln2_g, ln2_b, alpha):
    bsz, seq, d = x.shape
    n = bsz * seq
    n_experts = w_re.shape[1]
    moba_w, sb_w, mem_w = N_MOBA_HEADS * HEAD_DIM, N_SB_HEADS * HEAD_DIM, N_MEM_HEADS * HEAD_DIM
    n_qkv = 3 * moba_w + 3 * sb_w + mem_w
    assert w_in.shape[1] == n_qkv + 3 * d and n_experts + N_GROUPS <= LANES and moba_w == sb_w

    x2 = x.reshape(n, d)
    qkv = _proj(x2, w_in[:, :n_qkv].astype(BF16))
    qkv3 = qkv.reshape(bsz, seq, n_qkv)
    y_a = _moba(qkv3, tbl, rel_bias, 0)
    y_b = _stickbreak(qkv3, 3 * moba_w // sb_w)
    kv = _memkv(mem.reshape(-1, d), w_mem_kv.astype(BF16)).reshape(bsz, mem.shape[1], 2 * mem_w)
    y_m = _mem_attention(qkv3, kv, (3 * moba_w + 3 * sb_w) // mem_w)

    w_r = jnp.zeros((d, LANES), F32).at[:, :n_experts].set(w_re).at[:, n_experts:n_experts + N_GROUPS].set(w_rg)
    b_r = jnp.zeros((1, LANES), F32).at[0, :n_experts].set(b_re).at[0, n_experts:n_experts + N_GROUPS].set(b_rg)
    wr_hi, wr_lo = _split_bf16(w_r)
    x1, route, counts = _merge(
        y_a.reshape(n, moba_w), y_b.reshape(n, sb_w), y_m.reshape(n, mem_w), x2, w_in[:, n_qkv:].astype(BF16),
        w_br_moba.astype(BF16), w_br_sb.astype(BF16), w_br_mem.astype(BF16), w_out.astype(BF16),
        ln1_g.reshape(1, d), ln1_b.reshape(1, d), wr_hi, wr_lo, b_r, alpha, n_experts)

    rows, td = MOE_BLOCK, DISPATCH_ROWS
    n_tiles = n // td
    expert = route[:, 0:2].astype(I32)
    tile_cnt = jnp.sum((expert.reshape(n_tiles, 2 * td, 1) == jnp.arange(n_experts, dtype=I32)).astype(I32), axis=1)
    n_chunks = (tile_cnt + CHUNK - 1) // CHUNK
    run_rows = n_chunks * CHUNK
    run_rank = jnp.cumsum(tile_cnt, axis=0) - tile_cnt
    run_off = jnp.cumsum(run_rows, axis=0) - run_rows
    sorted_off = jnp.cumsum(run_rows, axis=1) - run_rows
    expert_rows = jnp.sum(run_rows, axis=0)
    padded = (expert_rows + rows - 1) // rows * rows
    pend = jnp.cumsum(padded)
    pstart = pend - padded
    run_dst = pstart[None, :] + run_off
    n_blocks = (2 * n + n_tiles * n_experts * (CHUNK - 1)) // rows + n_experts
    block_start = jnp.arange(n_blocks, dtype=I32) * rows
    block_expert = jnp.minimum(jnp.sum((pend[None, :] <= block_start[:, None]).astype(I32), axis=1),
                               n_experts - 1)
    n_used = (pend[-1] // rows).astype(I32).reshape(1)
    base = jnp.zeros((n_tiles, n_experts, LANES), F32)
    base = base.at[:, :, 0].set((sorted_off - run_rank).astype(F32)).at[:, :, 1].set((run_dst - run_rank).astype(F32))
    x_pad, dest8 = _dispatch(x1, route, n_chunks.reshape(-1), sorted_off.reshape(-1), run_dst.reshape(-1),
                             jnp.maximum(pend - rows, 0), (padded > 0).astype(I32), n_used, base,
                             n_blocks * rows)
    dest = dest8[:, 0:2, :].transpose(0, 2, 1).reshape(n, 2)

    y_pad = _experts(x_pad, block_expert, n_used, w_gate, w_up, w_down)
    out = _combine(y_pad, dest, x1, route, ln2_g.reshape(1, d), ln2_b.reshape(1, d), alpha)
    return out.reshape(bsz, seq, d)


def kernel(x, mem, w_in, w_mem_kv, rel_bias, w_br_moba, w_br_sb, w_br_mem, w_out, ln1_g, ln1_b,
           w_router_group, b_router_group, w_router_expert, b_router_expert,
           w_gate, w_up, w_down, ln2_g, ln2_b):
    depth = w_in.shape[0]
    alpha = (2.0 * depth) ** 0.25
    tbl = _bias_table(rel_bias)
    for l in range(depth):
        x = _layer(x, mem, tbl, rel_bias, w_in[l], w_mem_kv[l], w_br_moba[l], w_br_sb[l], w_br_mem[l],
                   w_out[l], ln1_g[l], ln1_b[l], w_router_group[l], b_router_group[l],
                   w_router_expert[l], b_router_expert[l], w_gate[l], w_up[l], w_down[l],
                   ln2_g[l], ln2_b[l], alpha)
    return x
```

```python
import functools
import math

import jax
import jax.numpy as jnp
from jax import lax
from jax.experimental import pallas as pl
from jax.experimental.pallas import tpu as pltpu

F32, BF16, I32 = jnp.float32, jnp.bfloat16, jnp.int32

HEAD_DIM = 64
N_MOBA_HEADS = 6
N_SB_HEADS = 6
N_MEM_HEADS = 4
MOBA_BLOCK = 256
MOBA_TOPK = 3
N_BUCKETS = 32
MAX_DISTANCE = 128
N_GROUPS = 4
EXPERTS_PER_GROUP = 8
LN_EPS = 1e-5
NEG = -1e30

LANES = 128
VMEM_BYTES = 64 * 1024 * 1024
PAIR = LANES // HEAD_DIM

SB_ZERO_LOG = 110.0

PROJ_ROWS = 512
MERGE_ROWS = 512
ATTN_ROWS = 256
COMBINE_ROWS = 256
MOE_BLOCK = 512


def _params(semantics, vmem_mb):
    return pltpu.CompilerParams(dimension_semantics=semantics,
                                vmem_limit_bytes=min(vmem_mb * 1024 * 1024, VMEM_BYTES))


def _dot(a, b):
    return jnp.dot(a, b, preferred_element_type=F32)


def _layer_norm(h, g, b):
    mu = jnp.mean(h, axis=-1, keepdims=True)
    d = h - mu
    var = jnp.mean(d * d, axis=-1, keepdims=True)
    return d * lax.rsqrt(var + LN_EPS) * g + b


def _sigmoid(x):
    return 1.0 / (1.0 + jnp.exp(-x))


def _head_rows(qt, row, hh, scale):
    keep = (row >= hh * HEAD_DIM) & (row < (hh + 1) * HEAD_DIM)
    return jnp.where(keep, qt * scale, 0.0).astype(BF16)


def _transposed_bf16(x):
    return x.astype(F32).T.astype(BF16)


def _proj_kernel(x_ref, wq_ref, qkv_ref):
    qkv_ref[...] = _dot(x_ref[...].astype(BF16), wq_ref[...]).astype(BF16)


def _proj(x2, w_qkv):
    n, d = x2.shape
    n_qkv = w_qkv.shape[1]
    tm = PROJ_ROWS
    return pl.pallas_call(
        _proj_kernel,
        grid=(n // tm,),
        in_specs=[pl.BlockSpec((tm, d), lambda i: (i, 0)),
                  pl.BlockSpec((d, n_qkv), lambda i: (0, 0))],
        out_specs=pl.BlockSpec((tm, n_qkv), lambda i: (i, 0)),
        out_shape=jax.ShapeDtypeStruct((n, n_qkv), BF16),
        compiler_params=_params(("arbitrary",), 40),
        name="proj",
    )(x2, w_qkv)


def _t5_bucket(rel):
    rel = jnp.maximum(rel, 0)
    max_exact = N_BUCKETS // 2
    rel_f = jnp.maximum(rel, 1).astype(F32)
    large = max_exact + (jnp.log(rel_f / max_exact) / math.log(MAX_DISTANCE / max_exact)
                         * (N_BUCKETS - max_exact)).astype(I32)
    large = jnp.minimum(large, N_BUCKETS - 1)
    return jnp.where(rel < max_exact, rel, large)


def _bias_table_kernel(rb_ref, o_ref):
    h = pl.program_id(0)
    blk = o_ref.shape[2]
    j = lax.broadcasted_iota(I32, (2 * blk, blk), 0)
    i = lax.broadcasted_iota(I32, (2 * blk, blk), 1)
    bucket = _t5_bucket(blk + i - j)
    acc = jnp.zeros((2 * blk, blk), F32)
    for b in range(N_BUCKETS):
        acc = jnp.where(bucket == b, rb_ref[b, h], acc)
    o_ref[0] = acc


def _bias_table(rel_bias):
    n_heads = rel_bias.shape[1]
    blk = MOBA_BLOCK
    return pl.pallas_call(
        _bias_table_kernel,
        grid=(n_heads,),
        in_specs=[pl.BlockSpec(memory_space=pltpu.SMEM)],
        out_specs=pl.BlockSpec((1, 2 * blk, blk), lambda h: (h, 0, 0)),
        out_shape=jax.ShapeDtypeStruct((n_heads, 2 * blk, blk), F32),
        compiler_params=_params(("arbitrary",), 32),
        name="bias_table",
    )(rel_bias)


def _fold_keys(x, op, final):
    while x.shape[0] > 8 and x.shape[0] % 2 == 0:
        half = x.shape[0] // 2
        x = op(x[:half], x[half:])
    return final(x, axis=0, keepdims=True)


def _head_queries(qt, row, hh, scale):
    p, sub = divmod(hh, PAIR)
    return _head_rows(qt[p * LANES:(p + 1) * LANES, :], row, sub, scale)


def _pair_lanes(hh):
    p = hh // PAIR
    return slice(p * LANES, (p + 1) * LANES)


def _head_dims(hh):
    return slice(hh * HEAD_DIM, (hh + 1) * HEAD_DIM)


def _moba_kernel(rb_ref, q_ref, k_ref, v_ref, tbl_ref, o_ref, kmean_sc, vt_sc, sel_sc, s_sc, p_sc):
    own = pl.program_id(1)
    blk, width = q_ref.shape[1], q_ref.shape[2]
    n_heads = width // HEAD_DIM
    nb = k_ref.shape[1] // blk
    scale = HEAD_DIM ** -0.5

    @pl.when(own == 0)
    def _():
        for n in range(nb):
            kb = k_ref[0, n * blk:(n + 1) * blk, :].astype(F32)
            kmean_sc[n:n + 1, :] = jnp.mean(kb, axis=0, keepdims=True)
            vt_sc[n] = _transposed_bf16(v_ref[0, n * blk:(n + 1) * blk, :])

    qt = q_ref[0].astype(F32).T
    row = lax.broadcasted_iota(I32, (LANES, blk), 0)
    blk_id = lax.broadcasted_iota(I32, (nb, blk), 0)
    valid = blk_id < own
    causal = (lax.broadcasted_iota(I32, (blk, blk), 0) <= lax.broadcasted_iota(I32, (blk, blk), 1))
    kmean = kmean_sc[...].astype(BF16)

    qs = [_head_queries(qt, row, hh, scale) for hh in range(n_heads)]
    gates = [_dot(kmean[:, _pair_lanes(hh)], _head_queries(qt, row, hh, 1.0)) for hh in range(n_heads)]

    def scores(n):
        offn = pl.multiple_of(n * blk, blk)
        return [_dot(k_ref[0, pl.ds(offn, blk), _pair_lanes(hh)], qs[hh]) for hh in range(n_heads)]

    own_scores = scores(own)

    for hh in range(n_heads):
        gate = jnp.where(valid, gates[hh], NEG)
        rank = jnp.zeros((nb, blk), I32)
        for m in range(nb):
            gm = gate[m:m + 1, :]
            beats = (gm > gate) | ((gm == gate) & (blk_id > m))
            rank = rank + beats.astype(I32)
        sel_sc[hh] = jnp.where(valid & (rank < MOBA_TOPK), 0.0, NEG)

    stats = []
    for hh, s in enumerate(own_scores):
        s = jnp.where(causal, s + tbl_ref[hh, blk:, :], NEG)
        m0 = _fold_keys(s, jnp.maximum, jnp.max)
        p = jnp.exp(s - m0)
        stats.append((m0, _fold_keys(p, jnp.add, jnp.sum), p.astype(BF16)))
    state = []
    for hh, (m0, l0, p) in enumerate(stats):
        state += [m0, l0, _dot(vt_sc[own, _head_dims(hh), :], p)]

    def softmax_step(s, per_query, m_run, l_run):
        m_new = jnp.maximum(m_run, _fold_keys(s, jnp.maximum, jnp.max) + per_query)
        a = jnp.exp(m_run - m_new)
        pn = jnp.exp(s - (m_new - per_query))
        return m_new, a * l_run + _fold_keys(pn, jnp.add, jnp.sum), a, pn.astype(BF16)

    def previous_block(n, state):
        stats = [softmax_step(s + tbl_ref[hh, :blk, :], sel_sc[hh, pl.ds(n, 1), :],
                              state[3 * hh], state[3 * hh + 1]) for hh, s in enumerate(scores(n))]
        new = []
        for hh, (m_new, l_new, a, pn) in enumerate(stats):
            new += [m_new, l_new, a * state[3 * hh + 2] + _dot(vt_sc[n, _head_dims(hh), :], pn)]
        return tuple(new)

    state = lax.fori_loop(jnp.maximum(own - 1, 0), own, previous_block, tuple(state))

    n_far = own - 1
    last_far = jnp.maximum(n_far - 1, 0)
    for hh, s in enumerate(scores(0)):
        s_sc[hh] = s
    p_sc[...] = jnp.zeros(p_sc.shape, BF16)

    def far_block(n, carry):
        state, a_prev = carry[:3 * n_heads], carry[3 * n_heads:]
        n_prev = jnp.maximum(n - 1, 0)
        pv = [_dot(vt_sc[n_prev, _head_dims(hh), :], p_sc[hh]) for hh in range(n_heads)]
        s_next = scores(jnp.minimum(n + 1, last_far))
        stats = [softmax_step(s_sc[hh], rb_ref[N_BUCKETS - 1, hh] + sel_sc[hh, pl.ds(n, 1), :],
                              state[3 * hh], state[3 * hh + 1]) for hh in range(n_heads)]
        new, a_new = [], []
        for hh, (m_new, l_new, a, pn) in enumerate(stats):
            p_sc[hh] = pn
            s_sc[hh] = s_next[hh]
            new += [m_new, l_new, a_prev[hh] * state[3 * hh + 2] + pv[hh]]
            a_new.append(a)
        return tuple(new) + tuple(a_new)

    ones = jnp.ones((1, blk), F32)
    carry = lax.fori_loop(0, n_far, far_block, tuple(state) + (ones,) * n_heads)
    state, a_prev = carry[:3 * n_heads], carry[3 * n_heads:]
    out_t = jnp.concatenate(
        [(a_prev[hh] * state[3 * hh + 2] + _dot(vt_sc[last_far, _head_dims(hh), :], p_sc[hh])) / state[3 * hh + 1]
         for hh in range(n_heads)], axis=0)
    o_ref[0] = out_t.T.astype(o_ref.dtype)


def _moba(qkv3, tbl, rel_bias, col0):
    bsz, seq, _ = qkv3.shape
    assert MAX_DISTANCE <= MOBA_BLOCK and seq % MOBA_BLOCK == 0
    width = N_MOBA_HEADS * HEAD_DIM
    blk = MOBA_BLOCK
    nb = seq // blk
    return pl.pallas_call(
        _moba_kernel,
        grid=(bsz, nb),
        in_specs=[pl.BlockSpec(memory_space=pltpu.SMEM),
                  pl.BlockSpec((1, blk, width), lambda b, i: (b, i, col0)),
                  pl.BlockSpec((1, seq, width), lambda b, i: (b, 0, col0 + 1)),
                  pl.BlockSpec((1, seq, width), lambda b, i: (b, 0, col0 + 2)),
                  pl.BlockSpec((N_MOBA_HEADS, 2 * blk, blk), lambda b, i: (0, 0, 0))],
        out_specs=pl.BlockSpec((1, blk, width), lambda b, i: (b, i, 0)),
        out_shape=jax.ShapeDtypeStruct((bsz, seq, width), BF16),
        scratch_shapes=[pltpu.VMEM((nb, width), F32),
                        pltpu.VMEM((nb, width, blk), BF16),
                        pltpu.VMEM((N_MOBA_HEADS, nb, blk), F32),
                        pltpu.VMEM((N_MOBA_HEADS, blk, blk), F32),
                        pltpu.VMEM((N_MOBA_HEADS, blk, blk), BF16)],
        compiler_params=_params(("arbitrary", "arbitrary"), 48),
        name="moba",
    )(rel_bias, qkv3, qkv3, qkv3, tbl)


def _sb_kernel(q_ref, k_ref, v_ref, o_ref, vt_sc):
    qi = pl.program_id(1)
    t, width = q_ref.shape[1], q_ref.shape[2]
    n_heads = width // HEAD_DIM
    nb = k_ref.shape[1] // t
    scale = HEAD_DIM ** -0.5

    @pl.when(qi == 0)
    def _():
        for n in range(nb):
            vt_sc[n] = _transposed_bf16(v_ref[0, n * t:(n + 1) * t, :])

    qt = q_ref[0].astype(F32).T
    row = lax.broadcasted_iota(I32, (LANES, t), 0)
    key = lax.broadcasted_iota(I32, (t, t), 0)
    qry = lax.broadcasted_iota(I32, (t, t), 1)
    strict = key < qry
    tri = jnp.where(qry >= key, 1.0, 0.0).astype(BF16)
    qs = [_head_queries(qt, row, hh, scale) for hh in range(n_heads)]

    def block(j, carries, diagonal):
        off = pl.multiple_of(j * t, t)
        zs = [_dot(k_ref[0, pl.ds(off, t), _pair_lanes(hh)], qs[hh]) for hh in range(n_heads)]
        csums = []
        for z in zs:
            sp = jnp.maximum(z, 0.0) + jnp.log(1.0 + jnp.exp(-jnp.abs(z)))
            if diagonal:
                sp = jnp.where(strict, sp, 0.0)
            hi = sp.astype(BF16)
            lo = (sp - hi.astype(F32)).astype(BF16)
            csums.append(_dot(tri, hi) + _dot(tri, lo))
        out = []
        for hh in range(n_heads):
            a = jnp.exp(zs[hh] - csums[hh])
            if diagonal:
                a = jnp.where(strict, a, 0.0)
            pv = _dot(vt_sc[j, _head_dims(hh), :], a.astype(BF16))
            if carries is not None:
                pv = pv * jnp.exp(-carries[hh])
            out.append((pv, csums[hh][0:1, :]))
        return out

    first = block(qi, None, True)
    accs = tuple(pv for pv, _ in first)
    carries = tuple(total for _, total in first)

    def lowest(carries):
        return jnp.min(functools.reduce(jnp.minimum, carries))

    def cond(state):
        j, cmin, _, _ = state
        return (j >= 0) & (cmin < SB_ZERO_LOG)

    def body(state):
        j, _, carries, accs = state
        res = block(j, carries, False)
        carries = tuple(c + total for c, (_, total) in zip(carries, res))
        accs = tuple(acc + pv for acc, (pv, _) in zip(accs, res))
        return j - 1, lowest(carries), carries, accs

    state = lax.while_loop(cond, body, (qi - 1, lowest(carries), carries, accs))
    out_t = jnp.concatenate(list(state[3]), axis=0)
    o_ref[0] = out_t.T.astype(o_ref.dtype)


def _stickbreak(qkv3, col0):
    bsz, seq, _ = qkv3.shape
    width = N_SB_HEADS * HEAD_DIM
    t = ATTN_ROWS
    return pl.pallas_call(
        _sb_kernel,
        grid=(bsz, seq // t),
        in_specs=[pl.BlockSpec((1, t, width), lambda b, i: (b, i, col0)),
                  pl.BlockSpec((1, seq, width), lambda b, i: (b, 0, col0 + 1)),
                  pl.BlockSpec((1, seq, width), lambda b, i: (b, 0, col0 + 2))],
        out_specs=pl.BlockSpec((1, t, width), lambda b, i: (b, i, 0)),
        out_shape=jax.ShapeDtypeStruct((bsz, seq, width), BF16),
        scratch_shapes=[pltpu.VMEM((seq // t, width, t), BF16)],
        compiler_params=_params(("arbitrary", "arbitrary"), 48),
        name="stickbreak",
    )(qkv3, qkv3, qkv3)


def _memkv_kernel(m_ref, w_ref, o_ref):
    o_ref[...] = _dot(m_ref[...].astype(BF16), w_ref[...]).astype(BF16)


def _memkv(mem2, w_bf16):
    n, d = mem2.shape
    width = w_bf16.shape[1]
    tm = min(n, 512)
    return pl.pallas_call(
        _memkv_kernel,
        grid=(n // tm,),
        in_specs=[pl.BlockSpec((tm, d), lambda i: (i, 0)),
                  pl.BlockSpec((d, width), lambda i: (0, 0))],
        out_specs=pl.BlockSpec((tm, width), lambda i: (i, 0)),
        out_shape=jax.ShapeDtypeStruct((n, width), BF16),
        compiler_params=_params(("arbitrary",), 32),
        name="memkv",
    )(mem2, w_bf16)


def _mem_kernel(q_ref, k_ref, v_ref, o_ref, vt_sc):
    t, width = q_ref.shape[1], q_ref.shape[2]
    n_heads = width // HEAD_DIM
    scale = HEAD_DIM ** -0.5

    @pl.when(pl.program_id(1) == 0)
    def _():
        vt_sc[...] = _transposed_bf16(v_ref[0])

    qt = q_ref[0].astype(F32).T
    row = lax.broadcasted_iota(I32, (LANES, t), 0)
    qs = [_head_queries(qt, row, hh, scale) for hh in range(n_heads)]
    scores = [_dot(k_ref[0, :, _pair_lanes(hh)], qs[hh]) for hh in range(n_heads)]
    probs = []
    for s in scores:
        e = jnp.exp(s - _fold_keys(s, jnp.maximum, jnp.max))
        probs.append((e * (1.0 / _fold_keys(e, jnp.add, jnp.sum))).astype(BF16))
    out_t = jnp.concatenate([_dot(vt_sc[_head_dims(hh), :], probs[hh]) for hh in range(n_heads)], axis=0)
    o_ref[0] = out_t.T.astype(o_ref.dtype)


def _mem_attention(qkv3, kv3, qcol):
    bsz, seq, _ = qkv3.shape
    n_mem = kv3.shape[1]
    width = N_MEM_HEADS * HEAD_DIM
    t = ATTN_ROWS
    return pl.pallas_call(
        _mem_kernel,
        grid=(bsz, seq // t),
        in_specs=[pl.BlockSpec((1, t, width), lambda b, i: (b, i, qcol)),
                  pl.BlockSpec((1, n_mem, width), lambda b, i: (b, 0, 0)),
                  pl.BlockSpec((1, n_mem, width), lambda b, i: (b, 0, 1))],
        out_specs=pl.BlockSpec((1, t, width), lambda b, i: (b, i, 0)),
        out_shape=jax.ShapeDtypeStruct((bsz, seq, width), BF16),
        scratch_shapes=[pltpu.VMEM((width, n_mem), BF16)],
        compiler_params=_params(("arbitrary", "arbitrary"), 32),
        name="mem_attention",
    )(qkv3, kv3, kv3)


def _merge_kernel(ya_ref, yb_ref, ym_ref, x_ref, wg_ref, wa_ref, wb_ref, wm_ref, wo_ref,
                  lng_ref, lnb_ref, wrh_ref, br_ref, x1_ref, route_ref, cnt_ref,
                  *, alpha, n_experts):
    step = pl.program_id(0)
    tm, d = x_ref.shape

    @pl.when(step == 0)
    def _():
        cnt_ref[...] = jnp.zeros(cnt_ref.shape, F32)

    x = x_ref[...]
    xb = x.astype(BF16)
    merged = None
    for k, (y_ref, w_ref) in enumerate(((ya_ref, wa_ref), (yb_ref, wb_ref), (ym_ref, wm_ref))):
        gate_logits = _dot(xb, wg_ref[:, k * d:(k + 1) * d])
        term = _sigmoid(gate_logits) * _dot(y_ref[...], w_ref[...])
        merged = term if merged is None else merged + term
    mix = _dot(merged.astype(BF16), wo_ref[...])
    x1 = _layer_norm(alpha * x + mix, lng_ref[...], lnb_ref[...])
    x1_ref[...] = x1

    xh = x1.astype(BF16)
    xl = (x1 - xh.astype(F32)).astype(BF16)
    by_hi = _dot(xh, wrh_ref[...])
    logits = (by_hi[:, :LANES] + (by_hi[:, LANES:] + _dot(xl, wrh_ref[:, :LANES]))
              + br_ref[...])

    col = lax.broadcasted_iota(I32, (tm, LANES), 1)
    colf = col.astype(F32)
    big = float(LANES)
    gmask = (col >= n_experts) & (col < n_experts + N_GROUPS)
    lg = jnp.where(gmask, logits, -jnp.inf)
    gmax = jnp.max(lg, axis=1, keepdims=True)
    gidx = jnp.min(jnp.where(lg == gmax, colf, big), axis=1, keepdims=True) - n_experts
    g_p = 1.0 / jnp.sum(jnp.where(gmask, jnp.exp(logits - gmax), 0.0), axis=1, keepdims=True)
    lo_col = gidx * EXPERTS_PER_GROUP
    emask = (colf >= lo_col) & (colf < lo_col + EXPERTS_PER_GROUP)
    le = jnp.where(emask, logits, -jnp.inf)
    l1 = jnp.max(le, axis=1, keepdims=True)
    i1 = jnp.min(jnp.where(le == l1, colf, big), axis=1, keepdims=True)
    le2 = jnp.where(colf == i1, -jnp.inf, le)
    l2 = jnp.max(le2, axis=1, keepdims=True)
    i2 = jnp.min(jnp.where(le2 == l2, colf, big), axis=1, keepdims=True)
    e2 = jnp.exp(l2 - l1)
    gate1 = g_p * (1.0 / (1.0 + e2))
    gate2 = g_p * (e2 / (1.0 + e2))

    oh1 = colf == i1
    oh2 = colf == i2
    cnt = jnp.where(oh1 | oh2, 1.0, 0.0)
    rr = lax.broadcasted_iota(I32, (tm, tm), 0)
    cc = lax.broadcasted_iota(I32, (tm, tm), 1)
    before = jnp.where(cc < rr, 1.0, 0.0).astype(BF16)
    base = _dot(before, cnt.astype(BF16)) + cnt_ref[0:1, :]
    rank1 = jnp.sum(jnp.where(oh1, base, 0.0), axis=1, keepdims=True)
    rank2 = jnp.sum(jnp.where(oh2, base, 0.0), axis=1, keepdims=True)
    cnt_ref[...] = cnt_ref[...] + jnp.sum(cnt, axis=0, keepdims=True)

    route = jnp.zeros((tm, LANES), F32)
    for k, val in enumerate((i1, i2, rank1, rank2, gate1, gate2)):
        route = jnp.where(col == k, val, route)
    route_ref[...] = route


def _merge(ya, yb, ym, x2, w_gates, wa, wb, wm, wo, ln_g, ln_b, wr_split, b_r, alpha, n_experts):
    n, d = x2.shape
    tm = MERGE_ROWS
    row = lambda w: pl.BlockSpec((tm, w), lambda i: (i, 0))
    full = lambda a: pl.BlockSpec(a.shape, lambda i: (0,) * a.ndim)
    return pl.pallas_call(
        functools.partial(_merge_kernel, alpha=alpha, n_experts=n_experts),
        grid=(n // tm,),
        in_specs=[row(ya.shape[1]), row(yb.shape[1]), row(ym.shape[1]), row(d),
                  full(w_gates), full(wa), full(wb), full(wm), full(wo), full(ln_g), full(ln_b),
                  full(wr_split), full(b_r)],
        out_specs=[row(d), row(LANES), pl.BlockSpec((8, LANES), lambda i: (0, 0))],
        out_shape=[jax.ShapeDtypeStruct((n, d), F32),
                   jax.ShapeDtypeStruct((n, LANES), F32),
                   jax.ShapeDtypeStruct((8, LANES), F32)],
        compiler_params=_params(("arbitrary",), 52),
        name="merge",
    )(ya, yb, ym, x2, w_gates, wa, wb, wm, wo, ln_g, ln_b, wr_split, b_r)


def _row_copy(src_hbm, row, buf, sem, r):
    return pltpu.make_async_copy(src_hbm.at[pl.ds(row, 1), :], buf.at[pl.ds(r, 1), :], sem)


def _gather_rows(src_hbm, idx_ref, buf, sem, n_rows, unrolled):
    if unrolled:
        for r in range(n_rows):
            _row_copy(src_hbm, idx_ref[0, 0, r], buf, sem, r).start()
    else:
        def issue(r, _):
            _row_copy(src_hbm, idx_ref[0, 0, r], buf, sem, r).start()
            return 0
        lax.fori_loop(0, n_rows, issue, 0)


def _wait_rows(src_hbm, buf, sem, n_rows):
    pltpu.make_async_copy(src_hbm.at[pl.ds(0, n_rows), :], buf, sem).wait()


DISPATCH_ROWS = 512
CHUNK = 8


def _chunk_copy(buf, xpad_hbm, sem, src_row, dst_row):
    return pltpu.make_async_copy(buf.at[pl.ds(src_row, CHUNK), :], xpad_hbm.at[pl.ds(dst_row, CHUNK), :], sem)


def _dispatch_kernel(nch_ref, off_ref, dst_ref, zrow_ref, zflag_ref, nu_ref, base_ref, route_ref, x_ref,
                     xpad_hbm, dest_ref, sorted_even, sorted_odd, zero_buf, sem, zsem):
    i = pl.program_id(0)
    n_steps = pl.num_programs(0)
    n_exp = base_ref.shape[1]
    td = x_ref.shape[0]
    rb = sorted_even.shape[0]
    zrows = zero_buf.shape[0]
    bufs = (sorted_even, sorted_odd)
    parity = lax.rem(i, 2)

    @pl.when(i == 0)
    def _():
        zero_buf[...] = jnp.zeros(zero_buf.shape, F32)

        def zero_block(row, start):
            cp = pltpu.make_async_copy(zero_buf, xpad_hbm.at[pl.ds(pl.multiple_of(row, zrows), zrows), :], zsem)
            cp.start() if start else cp.wait()

        for start in (True, False):
            for e in range(n_exp):
                pl.when(zflag_ref[e] == 1)(functools.partial(zero_block, zrow_ref[e], start))

            def unused(blk, _):
                zero_block(blk * zrows, start)
                return 0
            lax.fori_loop(nu_ref[0], xpad_hbm.shape[0] // zrows, unused, 0)

    rt = route_ref[...].T
    eid = lax.broadcasted_iota(I32, (n_exp, td), 0).astype(F32)
    rowid = lax.broadcasted_iota(I32, (rb, td), 0).astype(F32)
    hit = None
    dests = []
    for k in range(2):
        mine = eid == rt[k:k + 1, :]
        rank = rt[2 + k:3 + k, :]
        pos = jnp.sum(jnp.where(mine, base_ref[0, :, 0:1], 0.0), axis=0, keepdims=True) + rank
        dests.append(jnp.sum(jnp.where(mine, base_ref[0, :, 1:2], 0.0), axis=0, keepdims=True) + rank)
        hit = (rowid == pos) if hit is None else hit | (rowid == pos)
    dest_ref[0] = jnp.concatenate(dests + [jnp.zeros((8 - len(dests), td), F32)], axis=0).astype(I32)
    perm = jnp.where(hit, 1.0, 0.0).astype(BF16)
    rows_sorted = _dot(perm, x_ref[...].astype(BF16))

    def runs(tile, buf, s, start):
        for e in range(n_exp):
            k = tile * n_exp + e
            src0, dst0 = off_ref[k], dst_ref[k]

            def one(j, _):
                cp = _chunk_copy(buf, xpad_hbm, s, pl.multiple_of(src0 + j * CHUNK, CHUNK),
                                 pl.multiple_of(dst0 + j * CHUNK, CHUNK))
                cp.start() if start else cp.wait()
                return 0
            lax.fori_loop(0, nch_ref[k], one, 0)

    def step(cur):
        bufs[cur][...] = rows_sorted

        @pl.when(i > 0)
        def _():
            runs(i - 1, bufs[1 - cur], sem.at[1 - cur], False)

        runs(i, bufs[cur], sem.at[cur], True)

        @pl.when(i + 1 == n_steps)
        def _():
            runs(i, bufs[cur], sem.at[cur], False)

    for cur in range(2):
        pl.when(parity == cur)(functools.partial(step, cur))


def _dispatch(x1, route, nch, off8, dst, zrow, zflag, n_used, base, n_rows_out):
    n, d = x1.shape
    td = DISPATCH_ROWS
    n_tiles = n // td
    n_exp = base.shape[1]
    rb = 2 * td + n_exp * CHUNK
    grid_spec = pltpu.PrefetchScalarGridSpec(
        num_scalar_prefetch=6,
        grid=(n_tiles,),
        in_specs=[pl.BlockSpec((1, n_exp, LANES), lambda i, *_: (i, 0, 0)),
                  pl.BlockSpec((td, LANES), lambda i, *_: (i, 0)),
                  pl.BlockSpec((td, d), lambda i, *_: (i, 0))],
        out_specs=[pl.BlockSpec(memory_space=pl.ANY),
                   pl.BlockSpec((1, 8, td), lambda i, *_: (i, 0, 0))],
        scratch_shapes=[pltpu.VMEM((rb, d), F32), pltpu.VMEM((rb, d), F32),
                        pltpu.VMEM((MOE_BLOCK, d), F32),
                        pltpu.SemaphoreType.DMA((2,)), pltpu.SemaphoreType.DMA(())])
    return pl.pallas_call(
        _dispatch_kernel,
        grid_spec=grid_spec,
        out_shape=[jax.ShapeDtypeStruct((n_rows_out, d), F32),
                   jax.ShapeDtypeStruct((n_tiles, 8, td), I32)],
        compiler_params=_params(("arbitrary",), 48),
        name="dispatch",
    )(nch, off8, dst, zrow, zflag, n_used, base, route, x1)


def _experts_kernel(be_ref, nu_ref, x_ref, wg_ref, wu_ref, wd_ref, o_ref, wg_sc, wu_sc, wd_sc):
    b = pl.program_id(0)
    n_used = nu_ref[0]

    @pl.when(b < n_used)
    def _():
        changed = (b == 0) | (be_ref[b] != be_ref[jnp.maximum(b - 1, 0)])

        @pl.when(changed)
        def _():
            wg_sc[...] = wg_ref[0].astype(BF16)
            wu_sc[...] = wu_ref[0].astype(BF16)
            wd_sc[...] = wd_ref[0].astype(BF16)

        xb = x_ref[...].astype(BF16)
        g = _dot(xb, wg_sc[...])
        u = _dot(xb, wu_sc[...])
        h = (g * _sigmoid(g) * u).astype(BF16)
        o_ref[...] = _dot(h, wd_sc[...])

    @pl.when(b >= n_used)
    def _():
        o_ref[...] = jnp.zeros(o_ref.shape, F32)


def _experts(x_pad, block_expert, n_used, w_gate, w_up, w_down):
    d = x_pad.shape[1]
    n_blocks = block_expert.shape[0]
    rows = MOE_BLOCK
    d_exp = w_gate.shape[-1]
    grid_spec = pltpu.PrefetchScalarGridSpec(
        num_scalar_prefetch=2,
        grid=(n_blocks,),
        in_specs=[pl.BlockSpec((rows, d), lambda b, be, nu: (jnp.minimum(b, jnp.maximum(nu[0] - 1, 0)), 0)),
                  pl.BlockSpec((1, d, d_exp), lambda b, be, nu: (be[b], 0, 0)),
                  pl.BlockSpec((1, d, d_exp), lambda b, be, nu: (be[b], 0, 0)),
                  pl.BlockSpec((1, d_exp, d), lambda b, be, nu: (be[b], 0, 0))],
        out_specs=pl.BlockSpec((rows, d), lambda b, be, nu: (b, 0)),
        scratch_shapes=[pltpu.VMEM((d, d_exp), BF16),
                        pltpu.VMEM((d, d_exp), BF16),
                        pltpu.VMEM((d_exp, d), BF16)])
    return pl.pallas_call(
        _experts_kernel,
        grid_spec=grid_spec,
        out_shape=jax.ShapeDtypeStruct((n_blocks * rows, d), F32),
        compiler_params=_params(("arbitrary",), 44),
        name="experts",
    )(block_expert, n_used, x_pad, w_gate, w_up, w_down)


def _combine_kernel(idx_ref, idxn_ref, y_hbm, x1_ref, route_ref, lng_ref, lnb_ref, o_ref,
                    ybuf_even, ybuf_odd, sem, *, alpha):
    i = pl.program_id(0)
    n_steps = pl.num_programs(0)
    tm = x1_ref.shape[0]
    bufs = (ybuf_even, ybuf_odd)
    parity = lax.rem(i, 2)

    @pl.when(i == 0)
    def _():
        _gather_rows(y_hbm, idx_ref, bufs[0], sem.at[0], 2 * tm, False)

    def compute(cur):
        _gather_rows(y_hbm, idxn_ref, bufs[1 - cur], sem.at[1 - cur], 2 * tm, True)
        _wait_rows(y_hbm, bufs[cur], sem.at[cur], 2 * tm)
        ffn = (route_ref[:, 4:5] * bufs[cur][0:tm, :] + route_ref[:, 5:6] * bufs[cur][tm:2 * tm, :])
        o_ref[...] = _layer_norm(alpha * x1_ref[...] + ffn, lng_ref[...], lnb_ref[...])

        @pl.when(i + 1 == n_steps)
        def _():
            _wait_rows(y_hbm, bufs[1 - cur], sem.at[1 - cur], 2 * tm)

    for cur in range(2):
        pl.when(parity == cur)(functools.partial(compute, cur))


def _combine(y_pad, dest, x1, route, ln_g, ln_b, alpha):
    n, d = x1.shape
    tm = COMBINE_ROWS
    n_steps = n // tm
    idx3 = dest.reshape(n_steps, tm, 2).transpose(0, 2, 1).reshape(n_steps, 1, 2 * tm)
    smem_idx = lambda f: pl.BlockSpec((1, 1, 2 * tm), f, memory_space=pltpu.SMEM)
    return pl.pallas_call(
        functools.partial(_combine_kernel, alpha=alpha),
        grid=(n_steps,),
        in_specs=[smem_idx(lambda i: (i, 0, 0)),
                  smem_idx(lambda i: (jnp.minimum(i + 1, n_steps - 1), 0, 0)),
                  pl.BlockSpec(memory_space=pl.ANY),
                  pl.BlockSpec((tm, d), lambda i: (i, 0)),
                  pl.BlockSpec((tm, LANES), lambda i: (i, 0)),
                  pl.BlockSpec((1, d), lambda i: (0, 0)),
                  pl.BlockSpec((1, d), lambda i: (0, 0))],
        out_specs=pl.BlockSpec((tm, d), lambda i: (i, 0)),
        out_shape=jax.ShapeDtypeStruct((n, d), F32),
        scratch_shapes=[pltpu.VMEM((2 * tm, d), F32), pltpu.VMEM((2 * tm, d), F32),
                        pltpu.SemaphoreType.DMA((2,))],
        compiler_params=_params(("arbitrary",), 32),
        name="combine",
    )(idx3, idx3, y_pad, x1, route, ln_g, ln_b)


def _split_bf16(w):
    hi = w.astype(BF16)
    return hi, (w - hi.astype(F32)).astype(BF16)


def _layer(x, mem, tbl, rel_bias, w_in, w_mem_kv, w_br_moba, w_br_sb, w_br_mem, w_out, ln1_g, ln1_b,
           w_rg, b_rg, w_re, b_re, w_gate, w_up, w_down, ln2_g, ln2_b, alpha):
    bsz, seq, d = x.shape
    n = bsz * seq
    n_experts = w_re.shape[1]
    moba_w, sb_w, mem_w = N_MOBA_HEADS * HEAD_DIM, N_SB_HEADS * HEAD_DIM, N_MEM_HEADS * HEAD_DIM
    n_qkv = 3 * moba_w + 3 * sb_w + mem_w
    assert w_in.shape[1] == n_qkv + 3 * d and n_experts + N_GROUPS <= LANES and moba_w == sb_w

    x2 = x.reshape(n, d)
    qkv = _proj(x2, w_in[:, :n_qkv].astype(BF16))
    qkv3 = qkv.reshape(bsz, seq, n_qkv)
    y_a = _moba(qkv3, tbl, rel_bias, 0)
    y_b = _stickbreak(qkv3, 3 * moba_w // sb_w)
    kv = _memkv(mem.reshape(-1, d), w_mem_kv.astype(BF16)).reshape(bsz, mem.shape[1], 2 * mem_w)
    y_m = _mem_attention(qkv3, kv, (3 * moba_w + 3 * sb_w) // mem_w)

    w_r = jnp.zeros((d, LANES), F32).at[:, :n_experts].set(w_re).at[:, n_experts:n_experts + N_GROUPS].set(w_rg)
    b_r = jnp.zeros((1, LANES), F32).at[0, :n_experts].set(b_re).at[0, n_experts:n_experts + N_GROUPS].set(b_rg)
    wr_split = jnp.concatenate(_split_bf16(w_r), axis=1)
    x1, route, counts = _merge(
        y_a.reshape(n, moba_w), y_b.reshape(n, sb_w), y_m.reshape(n, mem_w), x2, w_in[:, n_qkv:].astype(BF16),
        w_br_moba.astype(BF16), w_br_sb.astype(BF16), w_br_mem.astype(BF16), w_out.astype(BF16),
        ln1_g.reshape(1, d), ln1_b.reshape(1, d), wr_split, b_r, alpha, n_experts)

    rows, td = MOE_BLOCK, DISPATCH_ROWS
    n_tiles = n // td
    expert = route[:, 0:2].astype(I32)
    tile_cnt = jnp.sum((expert.reshape(n_tiles, 2 * td, 1) == jnp.arange(n_experts, dtype=I32)).astype(I32), axis=1)
    n_chunks = (tile_cnt + CHUNK - 1) // CHUNK
    run_rows = n_chunks * CHUNK
    run_rank = jnp.cumsum(tile_cnt, axis=0) - tile_cnt
    run_off = jnp.cumsum(run_rows, axis=0) - run_rows
    sorted_off = jnp.cumsum(run_rows, axis=1) - run_rows
    expert_rows = jnp.sum(run_rows, axis=0)
    padded = (expert_rows + rows - 1) // rows * rows
    pend = jnp.cumsum(padded)
    pstart = pend - padded
    run_dst = pstart[None, :] + run_off
    n_blocks = (2 * n + n_tiles * n_experts * (CHUNK - 1)) // rows + n_experts
    block_start = jnp.arange(n_blocks, dtype=I32) * rows
    block_expert = jnp.minimum(jnp.sum((pend[None, :] <= block_start[:, None]).astype(I32), axis=1),
                               n_experts - 1)
    n_used = (pend[-1] // rows).astype(I32).reshape(1)
    base = jnp.zeros((n_tiles, n_experts, LANES), F32)
    base = base.at[:, :, 0].set((sorted_off - run_rank).astype(F32)).at[:, :, 1].set((run_dst - run_rank).astype(F32))
    x_pad, dest8 = _dispatch(x1, route, n_chunks.reshape(-1), sorted_off.reshape(-1), run_dst.reshape(-1),
                             jnp.maximum(pend - rows, 0), (padded > 0).astype(I32), n_used, base,
                             n_blocks * rows)
    dest = dest8[:, 0:2, :].transpose(0, 2, 1).reshape(n, 2)

    y_pad = _experts(x_pad, block_expert, n_used, w_gate, w_up, w_down)
    out = _combine(y_pad, dest, x1, route, ln2_g.reshape(1, d), ln2_b.reshape(1, d), alpha)
    return out.reshape(bsz, seq, d)


def kernel(x, mem, w_in, w_mem_kv, rel_bias, w_br_moba, w_br_sb, w_br_mem, w_out, ln1_g, ln1_b,
           w_router_group, b_router_group, w_router_expert, b_router_expert,
           w_gate, w_up, w_down, ln2_g, ln2_b):
    depth = w_in.shape[0]
    alpha = (2.0 * depth) ** 0.25
    tbl = _bias_table(rel_bias)
    for l in range(depth):
        x = _layer(x, mem, tbl, rel_bias, w_in[l], w_mem_kv[l], w_br_moba[l], w_br_sb[l], w_br_mem[l],
                   w_out[l], ln1_g[l], ln1_b[l], w_router_group[l], b_router_group[l],
                   w_router_expert[l], b_router_expert[l], w_gate[l], w_up[l], w_down[l],
                   ln2_g[l], ln2_b[l], alpha)
    return x
```

```python
import functools
import math

import jax
import jax.numpy as jnp
from jax import lax
from jax.experimental import pallas as pl
from jax.experimental.pallas import tpu as pltpu

F32, BF16, I32 = jnp.float32, jnp.bfloat16, jnp.int32

HEAD_DIM = 64
N_MOBA_HEADS = 6
N_SB_HEADS = 6
N_MEM_HEADS = 4
MOBA_BLOCK = 256
MOBA_TOPK = 3
N_BUCKETS = 32
MAX_DISTANCE = 128
N_GROUPS = 4
EXPERTS_PER_GROUP = 8
LN_EPS = 1e-5
NEG = -1e30

LANES = 128
VMEM_BYTES = 64 * 1024 * 1024
PAIR = LANES // HEAD_DIM

SB_ZERO_LOG = 110.0

PROJ_ROWS = 512
MERGE_ROWS = 512
ATTN_ROWS = 256
MEM_ROWS = 512
COMBINE_ROWS = 256
MOE_BLOCK = 512


def _params(semantics, vmem_mb):
    return pltpu.CompilerParams(dimension_semantics=semantics,
                                vmem_limit_bytes=min(vmem_mb * 1024 * 1024, VMEM_BYTES))


def _dot(a, b):
    return jnp.dot(a, b, preferred_element_type=F32)


def _layer_norm(h, g, b):
    mu = jnp.mean(h, axis=-1, keepdims=True)
    d = h - mu
    var = jnp.mean(d * d, axis=-1, keepdims=True)
    return d * lax.rsqrt(var + LN_EPS) * g + b


def _sigmoid(x):
    return 1.0 / (1.0 + jnp.exp(-x))


def _head_rows(qt, row, hh, scale):
    keep = (row >= hh * HEAD_DIM) & (row < (hh + 1) * HEAD_DIM)
    return jnp.where(keep, qt * scale, 0.0).astype(BF16)


def _transposed_bf16(x):
    return x.astype(F32).T.astype(BF16)


def _proj_kernel(x_ref, wq_ref, qkv_ref):
    qkv_ref[...] = _dot(x_ref[...].astype(BF16), wq_ref[...]).astype(BF16)


def _proj(x2, w_qkv):
    n, d = x2.shape
    n_qkv = w_qkv.shape[1]
    tm = PROJ_ROWS
    return pl.pallas_call(
        _proj_kernel,
        grid=(n // tm,),
        in_specs=[pl.BlockSpec((tm, d), lambda i: (i, 0)),
                  pl.BlockSpec((d, n_qkv), lambda i: (0, 0))],
        out_specs=pl.BlockSpec((tm, n_qkv), lambda i: (i, 0)),
        out_shape=jax.ShapeDtypeStruct((n, n_qkv), BF16),
        compiler_params=_params(("arbitrary",), 40),
        name="proj",
    )(x2, w_qkv)


def _t5_bucket(rel):
    rel = jnp.maximum(rel, 0)
    max_exact = N_BUCKETS // 2
    rel_f = jnp.maximum(rel, 1).astype(F32)
    large = max_exact + (jnp.log(rel_f / max_exact) / math.log(MAX_DISTANCE / max_exact)
                         * (N_BUCKETS - max_exact)).astype(I32)
    large = jnp.minimum(large, N_BUCKETS - 1)
    return jnp.where(rel < max_exact, rel, large)


def _bias_table_kernel(rb_ref, o_ref):
    h = pl.program_id(0)
    blk = o_ref.shape[2]
    j = lax.broadcasted_iota(I32, (2 * blk, blk), 0)
    i = lax.broadcasted_iota(I32, (2 * blk, blk), 1)
    bucket = _t5_bucket(blk + i - j)
    acc = jnp.zeros((2 * blk, blk), F32)
    for b in range(N_BUCKETS):
        acc = jnp.where(bucket == b, rb_ref[b, h], acc)
    o_ref[0] = acc


def _bias_table(rel_bias):
    n_heads = rel_bias.shape[1]
    blk = MOBA_BLOCK
    return pl.pallas_call(
        _bias_table_kernel,
        grid=(n_heads,),
        in_specs=[pl.BlockSpec(memory_space=pltpu.SMEM)],
        out_specs=pl.BlockSpec((1, 2 * blk, blk), lambda h: (h, 0, 0)),
        out_shape=jax.ShapeDtypeStruct((n_heads, 2 * blk, blk), F32),
        compiler_params=_params(("arbitrary",), 32),
        name="bias_table",
    )(rel_bias)


def _fold_keys(x, op, final):
    while x.shape[0] > 8 and x.shape[0] % 2 == 0:
        half = x.shape[0] // 2
        x = op(x[:half], x[half:])
    return final(x, axis=0, keepdims=True)


def _head_queries(qt, row, hh, scale):
    p, sub = divmod(hh, PAIR)
    return _head_rows(qt[p * LANES:(p + 1) * LANES, :], row, sub, scale)


def _pair_lanes(hh):
    p = hh // PAIR
    return slice(p * LANES, (p + 1) * LANES)


def _head_dims(hh):
    return slice(hh * HEAD_DIM, (hh + 1) * HEAD_DIM)


def _moba_kernel(rb_ref, q_ref, k_ref, v_ref, tbl_ref, o_ref, kmean_sc, vt_sc, sel_sc, s_sc, p_sc):
    own = pl.program_id(1)
    blk, width = q_ref.shape[1], q_ref.shape[2]
    n_heads = width // HEAD_DIM
    nb = k_ref.shape[1] // blk
    scale = HEAD_DIM ** -0.5

    @pl.when(own == 0)
    def _():
        for n in range(nb):
            kb = k_ref[0, n * blk:(n + 1) * blk, :].astype(F32)
            kmean_sc[n:n + 1, :] = jnp.mean(kb, axis=0, keepdims=True)
            vt_sc[n] = _transposed_bf16(v_ref[0, n * blk:(n + 1) * blk, :])

    qt = q_ref[0].astype(F32).T
    row = lax.broadcasted_iota(I32, (LANES, blk), 0)
    blk_id = lax.broadcasted_iota(I32, (nb, blk), 0)
    valid = blk_id < own
    causal = (lax.broadcasted_iota(I32, (blk, blk), 0) <= lax.broadcasted_iota(I32, (blk, blk), 1))
    kmean = kmean_sc[...].astype(BF16)

    qs = [_head_queries(qt, row, hh, scale) for hh in range(n_heads)]
    gates = [_dot(kmean[:, _pair_lanes(hh)], _head_queries(qt, row, hh, 1.0)) for hh in range(n_heads)]

    def scores(n):
        offn = pl.multiple_of(n * blk, blk)
        return [_dot(k_ref[0, pl.ds(offn, blk), _pair_lanes(hh)], qs[hh]) for hh in range(n_heads)]

    own_scores = scores(own)

    blk_f = blk_id.astype(F32)
    for hh in range(n_heads):
        gate = jnp.where(valid, gates[hh], NEG)
        picked = jnp.zeros((nb, blk), jnp.bool_)
        for _ in range(MOBA_TOPK):
            top = _fold_keys(gate, jnp.maximum, jnp.max)
            first = _fold_keys(jnp.where(gate == top, blk_f, float(nb)), jnp.minimum, jnp.min)
            hit = blk_f == first
            picked = picked | hit
            gate = jnp.where(hit, -jnp.inf, gate)
        sel_sc[hh] = jnp.where(valid & picked, 0.0, NEG)

    stats = []
    for hh, s in enumerate(own_scores):
        s = jnp.where(causal, s + tbl_ref[hh, blk:, :], NEG)
        m0 = _fold_keys(s, jnp.maximum, jnp.max)
        p = jnp.exp(s - m0)
        stats.append((m0, _fold_keys(p, jnp.add, jnp.sum), p.astype(BF16)))
    state = []
    for hh, (m0, l0, p) in enumerate(stats):
        state += [m0, l0, _dot(vt_sc[own, _head_dims(hh), :], p)]

    def softmax_step(s, per_query, m_run, l_run):
        m_new = jnp.maximum(m_run, _fold_keys(s, jnp.maximum, jnp.max) + per_query)
        a = jnp.exp(m_run - m_new)
        pn = jnp.exp(s - (m_new - per_query))
        return m_new, a * l_run + _fold_keys(pn, jnp.add, jnp.sum), a, pn.astype(BF16)

    def previous_block(n, state):
        stats = [softmax_step(s + tbl_ref[hh, :blk, :], sel_sc[hh, pl.ds(n, 1), :],
                              state[3 * hh], state[3 * hh + 1]) for hh, s in enumerate(scores(n))]
        new = []
        for hh, (m_new, l_new, a, pn) in enumerate(stats):
            new += [m_new, l_new, a * state[3 * hh + 2] + _dot(vt_sc[n, _head_dims(hh), :], pn)]
        return tuple(new)

    state = lax.fori_loop(jnp.maximum(own - 1, 0), own, previous_block, tuple(state))

    n_far = own - 1
    last_far = jnp.maximum(n_far - 1, 0)
    for hh, s in enumerate(scores(0)):
        s_sc[hh] = s
    p_sc[...] = jnp.zeros(p_sc.shape, BF16)

    def far_block(n, carry):
        state, a_prev = carry[:3 * n_heads], carry[3 * n_heads:]
        n_prev = jnp.maximum(n - 1, 0)
        pv = [_dot(vt_sc[n_prev, _head_dims(hh), :], p_sc[hh]) for hh in range(n_heads)]
        s_next = scores(jnp.minimum(n + 1, last_far))
        stats = [softmax_step(s_sc[hh], rb_ref[N_BUCKETS - 1, hh] + sel_sc[hh, pl.ds(n, 1), :],
                              state[3 * hh], state[3 * hh + 1]) for hh in range(n_heads)]
        new, a_new = [], []
        for hh, (m_new, l_new, a, pn) in enumerate(stats):
            p_sc[hh] = pn
            s_sc[hh] = s_next[hh]
            new += [m_new, l_new, a_prev[hh] * state[3 * hh + 2] + pv[hh]]
            a_new.append(a)
        return tuple(new) + tuple(a_new)

    ones = jnp.ones((1, blk), F32)
    carry = lax.fori_loop(0, n_far, far_block, tuple(state) + (ones,) * n_heads)
    state, a_prev = carry[:3 * n_heads], carry[3 * n_heads:]
    out_t = jnp.concatenate(
        [(a_prev[hh] * state[3 * hh + 2] + _dot(vt_sc[last_far, _head_dims(hh), :], p_sc[hh])) / state[3 * hh + 1]
         for hh in range(n_heads)], axis=0)
    o_ref[0] = out_t.T.astype(o_ref.dtype)


def _moba(qkv3, tbl, rel_bias, col0):
    bsz, seq, _ = qkv3.shape
    assert MAX_DISTANCE <= MOBA_BLOCK and seq % MOBA_BLOCK == 0
    width = N_MOBA_HEADS * HEAD_DIM
    blk = MOBA_BLOCK
    nb = seq // blk
    return pl.pallas_call(
        _moba_kernel,
        grid=(bsz, nb),
        in_specs=[pl.BlockSpec(memory_space=pltpu.SMEM),
                  pl.BlockSpec((1, blk, width), lambda b, i: (b, i, col0)),
                  pl.BlockSpec((1, seq, width), lambda b, i: (b, 0, col0 + 1)),
                  pl.BlockSpec((1, seq, width), lambda b, i: (b, 0, col0 + 2)),
                  pl.BlockSpec((N_MOBA_HEADS, 2 * blk, blk), lambda b, i: (0, 0, 0))],
        out_specs=pl.BlockSpec((1, blk, width), lambda b, i: (b, i, 0)),
        out_shape=jax.ShapeDtypeStruct((bsz, seq, width), BF16),
        scratch_shapes=[pltpu.VMEM((nb, width), F32),
                        pltpu.VMEM((nb, width, blk), BF16),
                        pltpu.VMEM((N_MOBA_HEADS, nb, blk), F32),
                        pltpu.VMEM((N_MOBA_HEADS, blk, blk), F32),
                        pltpu.VMEM((N_MOBA_HEADS, blk, blk), BF16)],
        compiler_params=_params(("arbitrary", "arbitrary"), 48),
        name="moba",
    )(rel_bias, qkv3, qkv3, qkv3, tbl)


def _sb_kernel(q_ref, k_ref, v_ref, o_ref, vt_sc):
    qi = pl.program_id(1)
    t, width = q_ref.shape[1], q_ref.shape[2]
    n_heads = width // HEAD_DIM
    nb = k_ref.shape[1] // t
    scale = HEAD_DIM ** -0.5

    @pl.when(qi == 0)
    def _():
        for n in range(nb):
            vt_sc[n] = _transposed_bf16(v_ref[0, n * t:(n + 1) * t, :])

    qt = q_ref[0].astype(F32).T
    row = lax.broadcasted_iota(I32, (LANES, t), 0)
    key = lax.broadcasted_iota(I32, (t, t), 0)
    qry = lax.broadcasted_iota(I32, (t, t), 1)
    strict = key < qry
    tri = jnp.where(qry >= key, 1.0, 0.0).astype(BF16)
    qs = [_head_queries(qt, row, hh, scale) for hh in range(n_heads)]

    def block(j, carries, diagonal):
        off = pl.multiple_of(j * t, t)
        zs = [_dot(k_ref[0, pl.ds(off, t), _pair_lanes(hh)], qs[hh]) for hh in range(n_heads)]
        csums = []
        for z in zs:
            sp = jnp.maximum(z, 0.0) + jnp.log(1.0 + jnp.exp(-jnp.abs(z)))
            if diagonal:
                sp = jnp.where(strict, sp, 0.0)
            hi = sp.astype(BF16)
            lo = (sp - hi.astype(F32)).astype(BF16)
            csums.append(_dot(tri, hi) + _dot(tri, lo))
        out = []
        for hh in range(n_heads):
            a = jnp.exp(zs[hh] - csums[hh])
            if diagonal:
                a = jnp.where(strict, a, 0.0)
            pv = _dot(vt_sc[j, _head_dims(hh), :], a.astype(BF16))
            if carries is not None:
                pv = pv * jnp.exp(-carries[hh])
            out.append((pv, csums[hh][0:1, :]))
        return out

    first = block(qi, None, True)
    accs = tuple(pv for pv, _ in first)
    carries = tuple(total for _, total in first)

    def lowest(carries):
        return jnp.min(functools.reduce(jnp.minimum, carries))

    def cond(state):
        j, cmin, _, _ = state
        return (j >= 0) & (cmin < SB_ZERO_LOG)

    def body(state):
        j, _, carries, accs = state
        res = block(j, carries, False)
        carries = tuple(c + total for c, (_, total) in zip(carries, res))
        accs = tuple(acc + pv for acc, (pv, _) in zip(accs, res))
        return j - 1, lowest(carries), carries, accs

    state = lax.while_loop(cond, body, (qi - 1, lowest(carries), carries, accs))
    out_t = jnp.concatenate(list(state[3]), axis=0)
    o_ref[0] = out_t.T.astype(o_ref.dtype)


def _stickbreak(qkv3, col0):
    bsz, seq, _ = qkv3.shape
    width = N_SB_HEADS * HEAD_DIM
    t = ATTN_ROWS
    return pl.pallas_call(
        _sb_kernel,
        grid=(bsz, seq // t),
        in_specs=[pl.BlockSpec((1, t, width), lambda b, i: (b, i, col0)),
                  pl.BlockSpec((1, seq, width), lambda b, i: (b, 0, col0 + 1)),
                  pl.BlockSpec((1, seq, width), lambda b, i: (b, 0, col0 + 2))],
        out_specs=pl.BlockSpec((1, t, width), lambda b, i: (b, i, 0)),
        out_shape=jax.ShapeDtypeStruct((bsz, seq, width), BF16),
        scratch_shapes=[pltpu.VMEM((seq // t, width, t), BF16)],
        compiler_params=_params(("arbitrary", "arbitrary"), 48),
        name="stickbreak",
    )(qkv3, qkv3, qkv3)


def _memkv_kernel(m_ref, w_ref, o_ref):
    o_ref[...] = _dot(m_ref[...].astype(BF16), w_ref[...]).astype(BF16)


def _memkv(mem2, w_bf16):
    n, d = mem2.shape
    width = w_bf16.shape[1]
    tm = min(n, 512)
    return pl.pallas_call(
        _memkv_kernel,
        grid=(n // tm,),
        in_specs=[pl.BlockSpec((tm, d), lambda i: (i, 0)),
                  pl.BlockSpec((d, width), lambda i: (0, 0))],
        out_specs=pl.BlockSpec((tm, width), lambda i: (i, 0)),
        out_shape=jax.ShapeDtypeStruct((n, width), BF16),
        compiler_params=_params(("arbitrary",), 32),
        name="memkv",
    )(mem2, w_bf16)


def _mem_kernel(q_ref, k_ref, v_ref, o_ref, vt_sc):
    t, width = q_ref.shape[1], q_ref.shape[2]
    n_heads = width // HEAD_DIM
    scale = HEAD_DIM ** -0.5

    @pl.when(pl.program_id(1) == 0)
    def _():
        vt_sc[...] = _transposed_bf16(v_ref[0])

    qt = q_ref[0].astype(F32).T
    row = lax.broadcasted_iota(I32, (LANES, t), 0)
    qs = [_head_queries(qt, row, hh, scale) for hh in range(n_heads)]
    scores = [_dot(k_ref[0, :, _pair_lanes(hh)], qs[hh]) for hh in range(n_heads)]
    probs = []
    for s in scores:
        e = jnp.exp(s - _fold_keys(s, jnp.maximum, jnp.max))
        probs.append((e * (1.0 / _fold_keys(e, jnp.add, jnp.sum))).astype(BF16))
    out_t = jnp.concatenate([_dot(vt_sc[_head_dims(hh), :], probs[hh]) for hh in range(n_heads)], axis=0)
    o_ref[0] = out_t.T.astype(o_ref.dtype)


def _mem_attention(qkv3, kv3, qcol):
    bsz, seq, _ = qkv3.shape
    n_mem = kv3.shape[1]
    width = N_MEM_HEADS * HEAD_DIM
    t = MEM_ROWS
    return pl.pallas_call(
        _mem_kernel,
        grid=(bsz, seq // t),
        in_specs=[pl.BlockSpec((1, t, width), lambda b, i: (b, i, qcol)),
                  pl.BlockSpec((1, n_mem, width), lambda b, i: (b, 0, 0)),
                  pl.BlockSpec((1, n_mem, width), lambda b, i: (b, 0, 1))],
        out_specs=pl.BlockSpec((1, t, width), lambda b, i: (b, i, 0)),
        out_shape=jax.ShapeDtypeStruct((bsz, seq, width), BF16),
        scratch_shapes=[pltpu.VMEM((width, n_mem), BF16)],
        compiler_params=_params(("arbitrary", "arbitrary"), 32),
        name="mem_attention",
    )(qkv3, kv3, kv3)


def _merge_kernel(ya_ref, yb_ref, ym_ref, x_ref, wg_ref, wa_ref, wb_ref, wm_ref, wo_ref,
                  lng_ref, lnb_ref, wrh_ref, br_ref, x1_ref, route_ref, cnt_ref,
                  *, alpha, n_experts):
    step = pl.program_id(0)
    tm, d = x_ref.shape

    @pl.when(step == 0)
    def _():
        cnt_ref[...] = jnp.zeros(cnt_ref.shape, F32)

    x = x_ref[...]
    xb = x.astype(BF16)
    merged = None
    for k, (y_ref, w_ref) in enumerate(((ya_ref, wa_ref), (yb_ref, wb_ref), (ym_ref, wm_ref))):
        gate_logits = _dot(xb, wg_ref[:, k * d:(k + 1) * d])
        term = _sigmoid(gate_logits) * _dot(y_ref[...], w_ref[...])
        merged = term if merged is None else merged + term
    mix = _dot(merged.astype(BF16), wo_ref[...])
    x1 = _layer_norm(alpha * x + mix, lng_ref[...], lnb_ref[...])
    x1_ref[...] = x1

    xh = x1.astype(BF16)
    xl = (x1 - xh.astype(F32)).astype(BF16)
    by_hi = _dot(xh, wrh_ref[...])
    logits = (by_hi[:, :LANES] + (by_hi[:, LANES:] + _dot(xl, wrh_ref[:, :LANES]))
              + br_ref[...])

    col = lax.broadcasted_iota(I32, (tm, LANES), 1)
    colf = col.astype(F32)
    big = float(LANES)
    gmask = (col >= n_experts) & (col < n_experts + N_GROUPS)
    lg = jnp.where(gmask, logits, -jnp.inf)
    gmax = jnp.max(lg, axis=1, keepdims=True)
    gidx = jnp.min(jnp.where(lg == gmax, colf, big), axis=1, keepdims=True) - n_experts
    g_p = 1.0 / jnp.sum(jnp.where(gmask, jnp.exp(logits - gmax), 0.0), axis=1, keepdims=True)
    lo_col = gidx * EXPERTS_PER_GROUP
    emask = (colf >= lo_col) & (colf < lo_col + EXPERTS_PER_GROUP)
    le = jnp.where(emask, logits, -jnp.inf)
    l1 = jnp.max(le, axis=1, keepdims=True)
    i1 = jnp.min(jnp.where(le == l1, colf, big), axis=1, keepdims=True)
    le2 = jnp.where(colf == i1, -jnp.inf, le)
    l2 = jnp.max(le2, axis=1, keepdims=True)
    i2 = jnp.min(jnp.where(le2 == l2, colf, big), axis=1, keepdims=True)
    e2 = jnp.exp(l2 - l1)
    gate1 = g_p * (1.0 / (1.0 + e2))
    gate2 = g_p * (e2 / (1.0 + e2))

    oh1 = colf == i1
    oh2 = colf == i2
    cnt = jnp.where(oh1 | oh2, 1.0, 0.0)
    rr = lax.broadcasted_iota(I32, (tm, tm), 0)
    cc = lax.broadcasted_iota(I32, (tm, tm), 1)
    before = jnp.where(cc < rr, 1.0, 0.0).astype(BF16)
    base = _dot(before, cnt.astype(BF16)) + cnt_ref[0:1, :]
    rank1 = jnp.sum(jnp.where(oh1, base, 0.0), axis=1, keepdims=True)
    rank2 = jnp.sum(jnp.where(oh2, base, 0.0), axis=1, keepdims=True)
    cnt_ref[...] = cnt_ref[...] + jnp.sum(cnt, axis=0, keepdims=True)

    route = jnp.zeros((tm, LANES), F32)
    for k, val in enumerate((i1, i2, rank1, rank2, gate1, gate2)):
        route = jnp.where(col == k, val, route)
    route_ref[...] = route


def _merge(ya, yb, ym, x2, w_gates, wa, wb, wm, wo, ln_g, ln_b, wr_split, b_r, alpha, n_experts):
    n, d = x2.shape
    tm = MERGE_ROWS
    row = lambda w: pl.BlockSpec((tm, w), lambda i: (i, 0))
    full = lambda a: pl.BlockSpec(a.shape, lambda i: (0,) * a.ndim)
    return pl.pallas_call(
        functools.partial(_merge_kernel, alpha=alpha, n_experts=n_experts),
        grid=(n // tm,),
        in_specs=[row(ya.shape[1]), row(yb.shape[1]), row(ym.shape[1]), row(d),
                  full(w_gates), full(wa), full(wb), full(wm), full(wo), full(ln_g), full(ln_b),
                  full(wr_split), full(b_r)],
        out_specs=[row(d), row(LANES), pl.BlockSpec((8, LANES), lambda i: (0, 0))],
        out_shape=[jax.ShapeDtypeStruct((n, d), F32),
                   jax.ShapeDtypeStruct((n, LANES), F32),
                   jax.ShapeDtypeStruct((8, LANES), F32)],
        compiler_params=_params(("arbitrary",), 52),
        name="merge",
    )(ya, yb, ym, x2, w_gates, wa, wb, wm, wo, ln_g, ln_b, wr_split, b_r)


def _row_copy(src_hbm, row, buf, sem, r):
    return pltpu.make_async_copy(src_hbm.at[pl.ds(row, 1), :], buf.at[pl.ds(r, 1), :], sem)


def _gather_rows(src_hbm, idx_ref, buf, sem, n_rows, unrolled):
    if unrolled:
        for r in range(n_rows):
            _row_copy(src_hbm, idx_ref[0, 0, r], buf, sem, r).start()
    else:
        def issue(r, _):
            _row_copy(src_hbm, idx_ref[0, 0, r], buf, sem, r).start()
            return 0
        lax.fori_loop(0, n_rows, issue, 0)


def _wait_rows(src_hbm, buf, sem, n_rows):
    pltpu.make_async_copy(src_hbm.at[pl.ds(0, n_rows), :], buf, sem).wait()


DISPATCH_ROWS = 512
CHUNK = 8


def _chunk_copy(buf, xpad_hbm, sem, src_row, dst_row):
    return pltpu.make_async_copy(buf.at[pl.ds(src_row, CHUNK), :], xpad_hbm.at[pl.ds(dst_row, CHUNK), :], sem)


def _dispatch_kernel(nch_ref, off_ref, dst_ref, zrow_ref, zflag_ref, nu_ref, base_ref, route_ref, x_ref,
                     xpad_hbm, dest_ref, sorted_even, sorted_odd, zero_buf, sem, zsem):
    i = pl.program_id(0)
    n_steps = pl.num_programs(0)
    n_exp = base_ref.shape[1]
    td = x_ref.shape[0]
    rb = sorted_even.shape[0]
    zrows = zero_buf.shape[0]
    bufs = (sorted_even, sorted_odd)
    parity = lax.rem(i, 2)

    @pl.when(i == 0)
    def _():
        zero_buf[...] = jnp.zeros(zero_buf.shape, F32)

        def zero_block(row, start):
            cp = pltpu.make_async_copy(zero_buf, xpad_hbm.at[pl.ds(pl.multiple_of(row, zrows), zrows), :], zsem)
            cp.start() if start else cp.wait()

        for start in (True, False):
            for e in range(n_exp):
                pl.when(zflag_ref[e] == 1)(functools.partial(zero_block, zrow_ref[e], start))

            def unused(blk, _):
                zero_block(blk * zrows, start)
                return 0
            lax.fori_loop(nu_ref[0], xpad_hbm.shape[0] // zrows, unused, 0)

    rt = route_ref[...].T
    eid = lax.broadcasted_iota(I32, (n_exp, td), 0).astype(F32)
    rowid = lax.broadcasted_iota(I32, (rb, td), 0).astype(F32)
    hit = None
    dests = []
    for k in range(2):
        mine = eid == rt[k:k + 1, :]
        rank = rt[2 + k:3 + k, :]
        pos = jnp.sum(jnp.where(mine, base_ref[0, :, 0:1], 0.0), axis=0, keepdims=True) + rank
        dests.append(jnp.sum(jnp.where(mine, base_ref[0, :, 1:2], 0.0), axis=0, keepdims=True) + rank)
        hit = (rowid == pos) if hit is None else hit | (rowid == pos)
    dest_ref[0] = jnp.concatenate(dests + [jnp.zeros((8 - len(dests), td), F32)], axis=0).astype(I32)
    perm = jnp.where(hit, 1.0, 0.0).astype(BF16)
    rows_sorted = _dot(perm, x_ref[...].astype(BF16))

    def runs(tile, buf, s, start):
        for e in range(n_exp):
            k = tile * n_exp + e
            src0, dst0 = off_ref[k], dst_ref[k]

            def one(j, _):
                cp = _chunk_copy(buf, xpad_hbm, s, pl.multiple_of(src0 + j * CHUNK, CHUNK),
                                 pl.multiple_of(dst0 + j * CHUNK, CHUNK))
                cp.start() if start else cp.wait()
                return 0
            lax.fori_loop(0, nch_ref[k], one, 0)

    def wait_runs(tile, buf, s):
        k_last = tile * n_exp + n_exp - 1
        total = pl.multiple_of(off_ref[k_last] + nch_ref[k_last] * CHUNK, CHUNK)
        pltpu.make_async_copy(buf.at[pl.ds(0, total), :], xpad_hbm.at[pl.ds(0, total), :], s).wait()

    def step(cur):
        bufs[cur][...] = rows_sorted

        @pl.when(i > 0)
        def _():
            wait_runs(i - 1, bufs[1 - cur], sem.at[1 - cur])

        runs(i, bufs[cur], sem.at[cur], True)

        @pl.when(i + 1 == n_steps)
        def _():
            wait_runs(i, bufs[cur], sem.at[cur])

    for cur in range(2):
        pl.when(parity == cur)(functools.partial(step, cur))


def _dispatch(x1, route, nch, off8, dst, zrow, zflag, n_used, base, n_rows_out):
    n, d = x1.shape
    td = DISPATCH_ROWS
    n_tiles = n // td
    n_exp = base.shape[1]
    rb = 2 * td + n_exp * CHUNK
    grid_spec = pltpu.PrefetchScalarGridSpec(
        num_scalar_prefetch=6,
        grid=(n_tiles,),
        in_specs=[pl.BlockSpec((1, n_exp, LANES), lambda i, *_: (i, 0, 0)),
                  pl.BlockSpec((td, LANES), lambda i, *_: (i, 0)),
                  pl.BlockSpec((td, d), lambda i, *_: (i, 0))],
        out_specs=[pl.BlockSpec(memory_space=pl.ANY),
                   pl.BlockSpec((1, 8, td), lambda i, *_: (i, 0, 0))],
        scratch_shapes=[pltpu.VMEM((rb, d), F32), pltpu.VMEM((rb, d), F32),
                        pltpu.VMEM((MOE_BLOCK, d), F32),
                        pltpu.SemaphoreType.DMA((2,)), pltpu.SemaphoreType.DMA(())])
    return pl.pallas_call(
        _dispatch_kernel,
        grid_spec=grid_spec,
        out_shape=[jax.ShapeDtypeStruct((n_rows_out, d), F32),
                   jax.ShapeDtypeStruct((n_tiles, 8, td), I32)],
        compiler_params=_params(("arbitrary",), 48),
        name="dispatch",
    )(nch, off8, dst, zrow, zflag, n_used, base, route, x1)


def _experts_kernel(be_ref, nu_ref, x_ref, wg_ref, wu_ref, wd_ref, o_ref, wg_sc, wu_sc, wd_sc):
    b = pl.program_id(0)
    n_used = nu_ref[0]

    @pl.when(b < n_used)
    def _():
        changed = (b == 0) | (be_ref[b] != be_ref[jnp.maximum(b - 1, 0)])

        @pl.when(changed)
        def _():
            wg_sc[...] = wg_ref[0].astype(BF16)
            wu_sc[...] = wu_ref[0].astype(BF16)
            wd_sc[...] = wd_ref[0].astype(BF16)

        xb = x_ref[...].astype(BF16)
        g = _dot(xb, wg_sc[...])
        u = _dot(xb, wu_sc[...])
        h = (g * _sigmoid(g) * u).astype(BF16)
        o_ref[...] = _dot(h, wd_sc[...])

    @pl.when(b >= n_used)
    def _():
        o_ref[...] = jnp.zeros(o_ref.shape, F32)


def _experts(x_pad, block_expert, n_used, w_gate, w_up, w_down):
    d = x_pad.shape[1]
    n_blocks = block_expert.shape[0]
    rows = MOE_BLOCK
    d_exp = w_gate.shape[-1]
    grid_spec = pltpu.PrefetchScalarGridSpec(
        num_scalar_prefetch=2,
        grid=(n_blocks,),
        in_specs=[pl.BlockSpec((rows, d), lambda b, be, nu: (jnp.minimum(b, jnp.maximum(nu[0] - 1, 0)), 0)),
                  pl.BlockSpec((1, d, d_exp), lambda b, be, nu: (be[b], 0, 0)),
                  pl.BlockSpec((1, d, d_exp), lambda b, be, nu: (be[b], 0, 0)),
                  pl.BlockSpec((1, d_exp, d), lambda b, be, nu: (be[b], 0, 0))],
        out_specs=pl.BlockSpec((rows, d), lambda b, be, nu: (b, 0)),
        scratch_shapes=[pltpu.VMEM((d, d_exp), BF16),
                        pltpu.VMEM((d, d_exp), BF16),
                        pltpu.VMEM((d_exp, d), BF16)])
    return pl.pallas_call(
        _experts_kernel,
        grid_spec=grid_spec,
        out_shape=jax.ShapeDtypeStruct((n_blocks * rows, d), F32),
        compiler_params=_params(("arbitrary",), 44),
        name="experts",
    )(block_expert, n_used, x_pad, w_gate, w_up, w_down)


def _combine_kernel(idx_ref, idxn_ref, y_hbm, x1_ref, route_ref, lng_ref, lnb_ref, o_ref,
                    ybuf_even, ybuf_odd, sem, *, alpha):
    i = pl.program_id(0)
    n_steps = pl.num_programs(0)
    tm = x1_ref.shape[0]
    bufs = (ybuf_even, ybuf_odd)
    parity = lax.rem(i, 2)

    @pl.when(i == 0)
    def _():
        _gather_rows(y_hbm, idx_ref, bufs[0], sem.at[0], 2 * tm, False)

    def compute(cur):
        _gather_rows(y_hbm, idxn_ref, bufs[1 - cur], sem.at[1 - cur], 2 * tm, True)
        _wait_rows(y_hbm, bufs[cur], sem.at[cur], 2 * tm)
        ffn = (route_ref[:, 4:5] * bufs[cur][0:tm, :] + route_ref[:, 5:6] * bufs[cur][tm:2 * tm, :])
        o_ref[...] = _layer_norm(alpha * x1_ref[...] + ffn, lng_ref[...], lnb_ref[...])

        @pl.when(i + 1 == n_steps)
        def _():
            _wait_rows(y_hbm, bufs[1 - cur], sem.at[1 - cur], 2 * tm)

    for cur in range(2):
        pl.when(parity == cur)(functools.partial(compute, cur))


def _combine(y_pad, dest, x1, route, ln_g, ln_b, alpha):
    n, d = x1.shape
    tm = COMBINE_ROWS
    n_steps = n // tm
    idx3 = dest.reshape(n_steps, tm, 2).transpose(0, 2, 1).reshape(n_steps, 1, 2 * tm)
    smem_idx = lambda f: pl.BlockSpec((1, 1, 2 * tm), f, memory_space=pltpu.SMEM)
    return pl.pallas_call(
        functools.partial(_combine_kernel, alpha=alpha),
        grid=(n_steps,),
        in_specs=[smem_idx(lambda i: (i, 0, 0)),
                  smem_idx(lambda i: (jnp.minimum(i + 1, n_steps - 1), 0, 0)),
                  pl.BlockSpec(memory_space=pl.ANY),
                  pl.BlockSpec((tm, d), lambda i: (i, 0)),
                  pl.BlockSpec((tm, LANES), lambda i: (i, 0)),
                  pl.BlockSpec((1, d), lambda i: (0, 0)),
                  pl.BlockSpec((1, d), lambda i: (0, 0))],
        out_specs=pl.BlockSpec((tm, d), lambda i: (i, 0)),
        out_shape=jax.ShapeDtypeStruct((n, d), F32),
        scratch_shapes=[pltpu.VMEM((2 * tm, d), F32), pltpu.VMEM((2 * tm, d), F32),
                        pltpu.SemaphoreType.DMA((2,))],
        compiler_params=_params(("arbitrary",), 32),
        name="combine",
    )(idx3, idx3, y_pad, x1, route, ln_g, ln_b)


def _split_bf16(w):
    hi = w.astype(BF16)
    return hi, (w - hi.astype(F32)).astype(BF16)


def _layer(x, mem, tbl, rel_bias, w_in, w_mem_kv, w_br_moba, w_br_sb, w_br_mem, w_out, ln1_g, ln1_b,
           w_rg, b_rg, w_re, b_re, w_gate, w_up, w_down, ln2_g, ln2_b, alpha):
    bsz, seq, d = x.shape
    n = bsz * seq
    n_experts = w_re.shape[1]
    moba_w, sb_w, mem_w = N_MOBA_HEADS * HEAD_DIM, N_SB_HEADS * HEAD_DIM, N_MEM_HEADS * HEAD_DIM
    n_qkv = 3 * moba_w + 3 * sb_w + mem_w
    assert w_in.shape[1] == n_qkv + 3 * d and n_experts + N_GROUPS <= LANES and moba_w == sb_w

    x2 = x.reshape(n, d)
    qkv = _proj(x2, w_in[:, :n_qkv].astype(BF16))
    qkv3 = qkv.reshape(bsz, seq, n_qkv)
    y_a = _moba(qkv3, tbl, rel_bias, 0)
    y_b = _stickbreak(qkv3, 3 * moba_w // sb_w)
    kv = _memkv(mem.reshape(-1, d), w_mem_kv.astype(BF16)).reshape(bsz, mem.shape[1], 2 * mem_w)
    y_m = _mem_attention(qkv3, kv, (3 * moba_w + 3 * sb_w) // mem_w)

    w_r = jnp.zeros((d, LANES), F32).at[:, :n_experts].set(w_re).at[:, n_experts:n_experts + N_GROUPS].set(w_rg)
    b_r = jnp.zeros((1, LANES), F32).at[0, :n_experts].set(b_re).at[0, n_experts:n_experts + N_GROUPS].set(b_rg)
    wr_split = jnp.concatenate(_split_bf16(w_r), axis=1)
    x1, route, counts = _merge(
        y_a.reshape(n, moba_w), y_b.reshape(n, sb_w), y_m.reshape(n, mem_w), x2, w_in[:, n_qkv:].astype(BF16),
        w_br_moba.astype(BF16), w_br_sb.astype(BF16), w_br_mem.astype(BF16), w_out.astype(BF16),
        ln1_g.reshape(1, d), ln1_b.reshape(1, d), wr_split, b_r, alpha, n_experts)

    rows, td = MOE_BLOCK, DISPATCH_ROWS
    n_tiles = n // td
    expert = route[:, 0:2].astype(I32)
    tile_cnt = jnp.sum((expert.reshape(n_tiles, 2 * td, 1) == jnp.arange(n_experts, dtype=I32)).astype(I32), axis=1)
    n_chunks = (tile_cnt + CHUNK - 1) // CHUNK
    run_rows = n_chunks * CHUNK
    run_rank = jnp.cumsum(tile_cnt, axis=0) - tile_cnt
    run_off = jnp.cumsum(run_rows, axis=0) - run_rows
    sorted_off = jnp.cumsum(run_rows, axis=1) - run_rows
    expert_rows = jnp.sum(run_rows, axis=0)
    padded = (expert_rows + rows - 1) // rows * rows
    pend = jnp.cumsum(padded)
    pstart = pend - padded
    run_dst = pstart[None, :] + run_off
    n_blocks = (2 * n + n_tiles * n_experts * (CHUNK - 1)) // rows + n_experts
    block_start = jnp.arange(n_blocks, dtype=I32) * rows
    block_expert = jnp.minimum(jnp.sum((pend[None, :] <= block_start[:, None]).astype(I32), axis=1),
                               n_experts - 1)
    n_used = (pend[-1] // rows).astype(I32).reshape(1)
    base = jnp.zeros((n_tiles, n_experts, LANES), F32)
    base = base.at[:, :, 0].set((sorted_off - run_rank).astype(F32)).at[:, :, 1].set((run_dst - run_rank).astype(F32))
    x_pad, dest8 = _dispatch(x1, route, n_chunks.reshape(-1), sorted_off.reshape(-1), run_dst.reshape(-1),
                             jnp.maximum(pend - rows, 0), (padded > 0).astype(I32), n_used, base,
                             n_blocks * rows)
    dest = dest8[:, 0:2, :].transpose(0, 2, 1).reshape(n, 2)

    y_pad = _experts(x_pad, block_expert, n_used, w_gate, w_up, w_down)
    out = _combine(y_pad, dest, x1, route, ln2_g.reshape(1, d), ln2_b.reshape(1, d), alpha)
    return out.reshape(bsz, seq, d)


def kernel(x, mem, w_in, w_mem_kv, rel_bias, w_br_moba, w_br_sb, w_br_mem, w_out, ln1_g, ln1_b,
           w_router_group, b_router_group, w_router_expert, b_router_expert,
           w_gate, w_up, w_down, ln2_g, ln2_b):
    depth = w_in.shape[0]
    alpha = (2.0 * depth) ** 0.25
    tbl = _bias_table(rel_bias)
    for l in range(depth):
        x = _layer(x, mem, tbl, rel_bias, w_in[l], w_mem_kv[l], w_br_moba[l], w_br_sb[l], w_br_mem[l],
                   w_out[l], ln1_g[l], ln1_b[l], w_router_group[l], b_router_group[l],
                   w_router_expert[l], b_router_expert[l], w_gate[l], w_up[l], w_down[l],
                   ln2_g[l], ln2_b[l], alpha)
    return x
```

```python
import functools
import math

import jax
import jax.numpy as jnp
from jax import lax
from jax.experimental import pallas as pl
from jax.experimental.pallas import tpu as pltpu

F32, BF16, I32 = jnp.float32, jnp.bfloat16, jnp.int32

HEAD_DIM = 64
N_MOBA_HEADS = 6
N_SB_HEADS = 6
N_MEM_HEADS = 4
MOBA_BLOCK = 256
MOBA_TOPK = 3
N_BUCKETS = 32
MAX_DISTANCE = 128
N_GROUPS = 4
EXPERTS_PER_GROUP = 8
LN_EPS = 1e-5
NEG = -1e30

LANES = 128
VMEM_BYTES = 64 * 1024 * 1024
PAIR = LANES // HEAD_DIM

SB_ZERO_LOG = 110.0

PROJ_ROWS = 512
MERGE_ROWS = 512
ATTN_ROWS = 256
MEM_ROWS = 1024
COMBINE_ROWS = 256
MOE_BLOCK = 512


def _params(semantics, vmem_mb):
    return pltpu.CompilerParams(dimension_semantics=semantics,
                                vmem_limit_bytes=min(vmem_mb * 1024 * 1024, VMEM_BYTES))


def _dot(a, b):
    return jnp.dot(a, b, preferred_element_type=F32)


def _layer_norm(h, g, b):
    mu = jnp.mean(h, axis=-1, keepdims=True)
    d = h - mu
    var = jnp.mean(d * d, axis=-1, keepdims=True)
    return d * lax.rsqrt(var + LN_EPS) * g + b


def _sigmoid(x):
    return 1.0 / (1.0 + jnp.exp(-x))


def _head_rows(qt, row, hh, scale):
    keep = (row >= hh * HEAD_DIM) & (row < (hh + 1) * HEAD_DIM)
    return jnp.where(keep, qt * scale, 0.0).astype(BF16)


def _transposed_bf16(x):
    return x.astype(F32).T.astype(BF16)


def _proj_kernel(x_ref, wq_ref, qkv_ref):
    qkv_ref[...] = _dot(x_ref[...].astype(BF16), wq_ref[...]).astype(BF16)


def _proj(x2, w_qkv):
    n, d = x2.shape
    n_qkv = w_qkv.shape[1]
    tm = PROJ_ROWS
    return pl.pallas_call(
        _proj_kernel,
        grid=(n // tm,),
        in_specs=[pl.BlockSpec((tm, d), lambda i: (i, 0)),
                  pl.BlockSpec((d, n_qkv), lambda i: (0, 0))],
        out_specs=pl.BlockSpec((tm, n_qkv), lambda i: (i, 0)),
        out_shape=jax.ShapeDtypeStruct((n, n_qkv), BF16),
        compiler_params=_params(("arbitrary",), 40),
        name="proj",
    )(x2, w_qkv)


def _t5_bucket(rel):
    rel = jnp.maximum(rel, 0)
    max_exact = N_BUCKETS // 2
    rel_f = jnp.maximum(rel, 1).astype(F32)
    large = max_exact + (jnp.log(rel_f / max_exact) / math.log(MAX_DISTANCE / max_exact)
                         * (N_BUCKETS - max_exact)).astype(I32)
    large = jnp.minimum(large, N_BUCKETS - 1)
    return jnp.where(rel < max_exact, rel, large)


def _bias_table_kernel(rb_ref, o_ref):
    h = pl.program_id(0)
    blk = o_ref.shape[2]
    j = lax.broadcasted_iota(I32, (2 * blk, blk), 0)
    i = lax.broadcasted_iota(I32, (2 * blk, blk), 1)
    bucket = _t5_bucket(blk + i - j)
    acc = jnp.zeros((2 * blk, blk), F32)
    for b in range(N_BUCKETS):
        acc = jnp.where(bucket == b, rb_ref[b, h], acc)
    o_ref[0] = acc


def _bias_table(rel_bias):
    n_heads = rel_bias.shape[1]
    blk = MOBA_BLOCK
    return pl.pallas_call(
        _bias_table_kernel,
        grid=(n_heads,),
        in_specs=[pl.BlockSpec(memory_space=pltpu.SMEM)],
        out_specs=pl.BlockSpec((1, 2 * blk, blk), lambda h: (h, 0, 0)),
        out_shape=jax.ShapeDtypeStruct((n_heads, 2 * blk, blk), F32),
        compiler_params=_params(("arbitrary",), 32),
        name="bias_table",
    )(rel_bias)


def _fold_keys(x, op, final):
    while x.shape[0] > 8 and x.shape[0] % 2 == 0:
        half = x.shape[0] // 2
        x = op(x[:half], x[half:])
    return final(x, axis=0, keepdims=True)


def _head_queries(qt, row, hh, scale):
    p, sub = divmod(hh, PAIR)
    return _head_rows(qt[p * LANES:(p + 1) * LANES, :], row, sub, scale)


def _pair_lanes(hh):
    p = hh // PAIR
    return slice(p * LANES, (p + 1) * LANES)


def _head_dims(hh):
    return slice(hh * HEAD_DIM, (hh + 1) * HEAD_DIM)


def _moba_kernel(rb_ref, q_ref, k_ref, v_ref, tbl_ref, o_ref, kmean_sc, vt_sc, sel_sc, s_sc, p_sc):
    own = pl.program_id(1)
    blk, width = q_ref.shape[1], q_ref.shape[2]
    n_heads = width // HEAD_DIM
    nb = k_ref.shape[1] // blk
    scale = HEAD_DIM ** -0.5

    @pl.when(own == 0)
    def _():
        for n in range(nb):
            kb = k_ref[0, n * blk:(n + 1) * blk, :].astype(F32)
            kmean_sc[n:n + 1, :] = jnp.mean(kb, axis=0, keepdims=True)
            vt_sc[n] = _transposed_bf16(v_ref[0, n * blk:(n + 1) * blk, :])

    qt = q_ref[0].astype(F32).T
    row = lax.broadcasted_iota(I32, (LANES, blk), 0)
    blk_id = lax.broadcasted_iota(I32, (nb, blk), 0)
    valid = blk_id < own
    causal = (lax.broadcasted_iota(I32, (blk, blk), 0) <= lax.broadcasted_iota(I32, (blk, blk), 1))
    kmean = kmean_sc[...].astype(BF16)

    qs = [_head_queries(qt, row, hh, scale) for hh in range(n_heads)]
    gates = [_dot(kmean[:, _pair_lanes(hh)], _head_queries(qt, row, hh, 1.0)) for hh in range(n_heads)]

    def scores(n):
        offn = pl.multiple_of(n * blk, blk)
        return [_dot(k_ref[0, pl.ds(offn, blk), _pair_lanes(hh)], qs[hh]) for hh in range(n_heads)]

    own_scores = scores(own)

    blk_f = blk_id.astype(F32)
    for hh in range(n_heads):
        gate = jnp.where(valid, gates[hh], NEG)
        picked = jnp.zeros((nb, blk), jnp.bool_)
        for _ in range(MOBA_TOPK):
            top = _fold_keys(gate, jnp.maximum, jnp.max)
            first = _fold_keys(jnp.where(gate == top, blk_f, float(nb)), jnp.minimum, jnp.min)
            hit = blk_f == first
            picked = picked | hit
            gate = jnp.where(hit, -jnp.inf, gate)
        sel_sc[hh] = jnp.where(valid & picked, 0.0, NEG)

    stats = []
    for hh, s in enumerate(own_scores):
        s = jnp.where(causal, s + tbl_ref[hh, blk:, :], NEG)
        m0 = _fold_keys(s, jnp.maximum, jnp.max)
        p = jnp.exp(s - m0)
        stats.append((m0, _fold_keys(p, jnp.add, jnp.sum), p.astype(BF16)))
    state = []
    for hh, (m0, l0, p) in enumerate(stats):
        state += [m0, l0, _dot(vt_sc[own, _head_dims(hh), :], p)]

    def softmax_step(s, per_query, m_run, l_run):
        m_new = jnp.maximum(m_run, _fold_keys(s, jnp.maximum, jnp.max) + per_query)
        a = jnp.exp(m_run - m_new)
        pn = jnp.exp(s - (m_new - per_query))
        return m_new, a * l_run + _fold_keys(pn, jnp.add, jnp.sum), a, pn.astype(BF16)

    def previous_block(n, state):
        stats = [softmax_step(s + tbl_ref[hh, :blk, :], sel_sc[hh, pl.ds(n, 1), :],
                              state[3 * hh], state[3 * hh + 1]) for hh, s in enumerate(scores(n))]
        new = []
        for hh, (m_new, l_new, a, pn) in enumerate(stats):
            new += [m_new, l_new, a * state[3 * hh + 2] + _dot(vt_sc[n, _head_dims(hh), :], pn)]
        return tuple(new)

    state = lax.fori_loop(jnp.maximum(own - 1, 0), own, previous_block, tuple(state))

    n_far = own - 1
    last_far = jnp.maximum(n_far - 1, 0)
    for hh, s in enumerate(scores(0)):
        s_sc[hh] = s
    p_sc[...] = jnp.zeros(p_sc.shape, BF16)

    def far_block(n, carry):
        state, a_prev = carry[:3 * n_heads], carry[3 * n_heads:]
        n_prev = jnp.maximum(n - 1, 0)
        pv = [_dot(vt_sc[n_prev, _head_dims(hh), :], p_sc[hh]) for hh in range(n_heads)]
        s_next = scores(jnp.minimum(n + 1, last_far))
        stats = [softmax_step(s_sc[hh], rb_ref[N_BUCKETS - 1, hh] + sel_sc[hh, pl.ds(n, 1), :],
                              state[3 * hh], state[3 * hh + 1]) for hh in range(n_heads)]
        new, a_new = [], []
        for hh, (m_new, l_new, a, pn) in enumerate(stats):
            p_sc[hh] = pn
            s_sc[hh] = s_next[hh]
            new += [m_new, l_new, a_prev[hh] * state[3 * hh + 2] + pv[hh]]
            a_new.append(a)
        return tuple(new) + tuple(a_new)

    ones = jnp.ones((1, blk), F32)
    carry = lax.fori_loop(0, n_far, far_block, tuple(state) + (ones,) * n_heads)
    state, a_prev = carry[:3 * n_heads], carry[3 * n_heads:]
    out_t = jnp.concatenate(
        [(a_prev[hh] * state[3 * hh + 2] + _dot(vt_sc[last_far, _head_dims(hh), :], p_sc[hh])) / state[3 * hh + 1]
         for hh in range(n_heads)], axis=0)
    o_ref[0] = out_t.T.astype(o_ref.dtype)


def _moba(qkv3, tbl, rel_bias, col0):
    bsz, seq, _ = qkv3.shape
    assert MAX_DISTANCE <= MOBA_BLOCK and seq % MOBA_BLOCK == 0
    width = N_MOBA_HEADS * HEAD_DIM
    blk = MOBA_BLOCK
    nb = seq // blk
    return pl.pallas_call(
        _moba_kernel,
        grid=(bsz, nb),
        in_specs=[pl.BlockSpec(memory_space=pltpu.SMEM),
                  pl.BlockSpec((1, blk, width), lambda b, i: (b, i, col0)),
                  pl.BlockSpec((1, seq, width), lambda b, i: (b, 0, col0 + 1)),
                  pl.BlockSpec((1, seq, width), lambda b, i: (b, 0, col0 + 2)),
                  pl.BlockSpec((N_MOBA_HEADS, 2 * blk, blk), lambda b, i: (0, 0, 0))],
        out_specs=pl.BlockSpec((1, blk, width), lambda b, i: (b, i, 0)),
        out_shape=jax.ShapeDtypeStruct((bsz, seq, width), BF16),
        scratch_shapes=[pltpu.VMEM((nb, width), F32),
                        pltpu.VMEM((nb, width, blk), BF16),
                        pltpu.VMEM((N_MOBA_HEADS, nb, blk), F32),
                        pltpu.VMEM((N_MOBA_HEADS, blk, blk), F32),
                        pltpu.VMEM((N_MOBA_HEADS, blk, blk), BF16)],
        compiler_params=_params(("arbitrary", "arbitrary"), 48),
        name="moba",
    )(rel_bias, qkv3, qkv3, qkv3, tbl)


def _sb_kernel(q_ref, k_ref, v_ref, o_ref, vt_sc):
    qi = pl.program_id(1)
    t, width = q_ref.shape[1], q_ref.shape[2]
    n_heads = width // HEAD_DIM
    nb = k_ref.shape[1] // t
    scale = HEAD_DIM ** -0.5

    @pl.when(qi == 0)
    def _():
        for n in range(nb):
            vt_sc[n] = _transposed_bf16(v_ref[0, n * t:(n + 1) * t, :])

    qt = q_ref[0].astype(F32).T
    row = lax.broadcasted_iota(I32, (LANES, t), 0)
    key = lax.broadcasted_iota(I32, (t, t), 0)
    qry = lax.broadcasted_iota(I32, (t, t), 1)
    strict = key < qry
    tri = jnp.where(qry >= key, 1.0, 0.0).astype(BF16)
    qs = [_head_queries(qt, row, hh, scale) for hh in range(n_heads)]

    def block(j, carries, diagonal):
        off = pl.multiple_of(j * t, t)
        zs = [_dot(k_ref[0, pl.ds(off, t), _pair_lanes(hh)], qs[hh]) for hh in range(n_heads)]
        csums = []
        for z in zs:
            sp = jnp.maximum(z, 0.0) + jnp.log(1.0 + jnp.exp(-jnp.abs(z)))
            if diagonal:
                sp = jnp.where(strict, sp, 0.0)
            hi = sp.astype(BF16)
            lo = (sp - hi.astype(F32)).astype(BF16)
            csums.append(_dot(tri, hi) + _dot(tri, lo))
        out = []
        for hh in range(n_heads):
            a = jnp.exp(zs[hh] - csums[hh])
            if diagonal:
                a = jnp.where(strict, a, 0.0)
            pv = _dot(vt_sc[j, _head_dims(hh), :], a.astype(BF16))
            if carries is not None:
                pv = pv * jnp.exp(-carries[hh])
            out.append((pv, csums[hh][0:1, :]))
        return out

    first = block(qi, None, True)
    accs = tuple(pv for pv, _ in first)
    carries = tuple(total for _, total in first)

    def lowest(carries):
        return jnp.min(functools.reduce(jnp.minimum, carries))

    def cond(state):
        j, cmin, _, _ = state
        return (j >= 0) & (cmin < SB_ZERO_LOG)

    def body(state):
        j, _, carries, accs = state
        res = block(j, carries, False)
        carries = tuple(c + total for c, (_, total) in zip(carries, res))
        accs = tuple(acc + pv for acc, (pv, _) in zip(accs, res))
        return j - 1, lowest(carries), carries, accs

    state = lax.while_loop(cond, body, (qi - 1, lowest(carries), carries, accs))
    out_t = jnp.concatenate(list(state[3]), axis=0)
    o_ref[0] = out_t.T.astype(o_ref.dtype)


def _stickbreak(qkv3, col0):
    bsz, seq, _ = qkv3.shape
    width = N_SB_HEADS * HEAD_DIM
    t = ATTN_ROWS
    return pl.pallas_call(
        _sb_kernel,
        grid=(bsz, seq // t),
        in_specs=[pl.BlockSpec((1, t, width), lambda b, i: (b, i, col0)),
                  pl.BlockSpec((1, seq, width), lambda b, i: (b, 0, col0 + 1)),
                  pl.BlockSpec((1, seq, width), lambda b, i: (b, 0, col0 + 2))],
        out_specs=pl.BlockSpec((1, t, width), lambda b, i: (b, i, 0)),
        out_shape=jax.ShapeDtypeStruct((bsz, seq, width), BF16),
        scratch_shapes=[pltpu.VMEM((seq // t, width, t), BF16)],
        compiler_params=_params(("arbitrary", "arbitrary"), 48),
        name="stickbreak",
    )(qkv3, qkv3, qkv3)


def _memkv_kernel(m_ref, w_ref, o_ref):
    o_ref[...] = _dot(m_ref[...].astype(BF16), w_ref[...]).astype(BF16)


def _memkv(mem2, w_bf16):
    n, d = mem2.shape
    width = w_bf16.shape[1]
    tm = min(n, 512)
    return pl.pallas_call(
        _memkv_kernel,
        grid=(n // tm,),
        in_specs=[pl.BlockSpec((tm, d), lambda i: (i, 0)),
                  pl.BlockSpec((d, width), lambda i: (0, 0))],
        out_specs=pl.BlockSpec((tm, width), lambda i: (i, 0)),
        out_shape=jax.ShapeDtypeStruct((n, width), BF16),
        compiler_params=_params(("arbitrary",), 32),
        name="memkv",
    )(mem2, w_bf16)


def _mem_kernel(q_ref, k_ref, v_ref, o_ref, vt_sc):
    t, width = q_ref.shape[1], q_ref.shape[2]
    n_heads = width // HEAD_DIM
    scale = HEAD_DIM ** -0.5

    @pl.when(pl.program_id(1) == 0)
    def _():
        vt_sc[...] = _transposed_bf16(v_ref[0])

    qt = q_ref[0].astype(F32).T
    row = lax.broadcasted_iota(I32, (LANES, t), 0)
    qs = [_head_queries(qt, row, hh, scale) for hh in range(n_heads)]
    scores = [_dot(k_ref[0, :, _pair_lanes(hh)], qs[hh]) for hh in range(n_heads)]
    probs = []
    for s in scores:
        e = jnp.exp(s - _fold_keys(s, jnp.maximum, jnp.max))
        probs.append((e * (1.0 / _fold_keys(e, jnp.add, jnp.sum))).astype(BF16))
    out_t = jnp.concatenate([_dot(vt_sc[_head_dims(hh), :], probs[hh]) for hh in range(n_heads)], axis=0)
    o_ref[0] = out_t.T.astype(o_ref.dtype)


def _mem_attention(qkv3, kv3, qcol):
    bsz, seq, _ = qkv3.shape
    n_mem = kv3.shape[1]
    width = N_MEM_HEADS * HEAD_DIM
    t = MEM_ROWS
    return pl.pallas_call(
        _mem_kernel,
        grid=(bsz, seq // t),
        in_specs=[pl.BlockSpec((1, t, width), lambda b, i: (b, i, qcol)),
                  pl.BlockSpec((1, n_mem, width), lambda b, i: (b, 0, 0)),
                  pl.BlockSpec((1, n_mem, width), lambda b, i: (b, 0, 1))],
        out_specs=pl.BlockSpec((1, t, width), lambda b, i: (b, i, 0)),
        out_shape=jax.ShapeDtypeStruct((bsz, seq, width), BF16),
        scratch_shapes=[pltpu.VMEM((width, n_mem), BF16)],
        compiler_params=_params(("arbitrary", "arbitrary"), 32),
        name="mem_attention",
    )(qkv3, kv3, kv3)


def _merge_kernel(ya_ref, yb_ref, ym_ref, x_ref, wg_ref, wa_ref, wb_ref, wm_ref, wo_ref,
                  lng_ref, lnb_ref, wrh_ref, br_ref, x1_ref, route_ref, cnt_ref, h_sc,
                  *, alpha, n_experts):
    step = pl.program_id(0)
    tm, d = x_ref.shape
    slot = lax.rem(step, 2)

    @pl.when(step == 0)
    def _():
        cnt_ref[...] = jnp.zeros(cnt_ref.shape, F32)
        h_sc[...] = jnp.zeros(h_sc.shape, F32)

    x = x_ref[...]
    xb = x.astype(BF16)
    branches = ((ya_ref, wa_ref), (yb_ref, wb_ref), (ym_ref, wm_ref))

    def branch_term(k):
        gate_logits = _dot(xb, wg_ref[:, k * d:(k + 1) * d])
        return gate_logits, _dot(branches[k][0][...], branches[k][1][...])

    x1 = _layer_norm(h_sc[1 - slot], lng_ref[...], lnb_ref[...])
    x1_ref[...] = x1
    xh = x1.astype(BF16)
    xl = (x1 - xh.astype(F32)).astype(BF16)

    first_term = branch_term(0)

    by_hi = _dot(xh, wrh_ref[...])
    logits = (by_hi[:, :LANES] + (by_hi[:, LANES:] + _dot(xl, wrh_ref[:, :LANES]))
              + br_ref[...])

    merged = None
    for k in range(len(branches)):
        gate_logits, y = first_term if k == 0 else branch_term(k)
        term = _sigmoid(gate_logits) * y
        merged = term if merged is None else merged + term
    h_sc[slot] = alpha * x + _dot(merged.astype(BF16), wo_ref[...])

    col = lax.broadcasted_iota(I32, (tm, LANES), 1)
    colf = col.astype(F32)
    big = float(LANES)
    gmask = (col >= n_experts) & (col < n_experts + N_GROUPS)
    lg = jnp.where(gmask, logits, -jnp.inf)
    gmax = jnp.max(lg, axis=1, keepdims=True)
    gidx = jnp.min(jnp.where(lg == gmax, colf, big), axis=1, keepdims=True) - n_experts
    g_p = 1.0 / jnp.sum(jnp.where(gmask, jnp.exp(logits - gmax), 0.0), axis=1, keepdims=True)
    lo_col = gidx * EXPERTS_PER_GROUP
    emask = (colf >= lo_col) & (colf < lo_col + EXPERTS_PER_GROUP)
    le = jnp.where(emask, logits, -jnp.inf)
    l1 = jnp.max(le, axis=1, keepdims=True)
    i1 = jnp.min(jnp.where(le == l1, colf, big), axis=1, keepdims=True)
    le2 = jnp.where(colf == i1, -jnp.inf, le)
    l2 = jnp.max(le2, axis=1, keepdims=True)
    i2 = jnp.min(jnp.where(le2 == l2, colf, big), axis=1, keepdims=True)
    e2 = jnp.exp(l2 - l1)
    gate1 = g_p * (1.0 / (1.0 + e2))
    gate2 = g_p * (e2 / (1.0 + e2))

    oh1 = colf == i1
    oh2 = colf == i2
    cnt = jnp.where((oh1 | oh2) & (step > 0), 1.0, 0.0)
    rr = lax.broadcasted_iota(I32, (tm, tm), 0)
    cc = lax.broadcasted_iota(I32, (tm, tm), 1)
    before = jnp.where(cc < rr, 1.0, 0.0).astype(BF16)
    base = _dot(before, cnt.astype(BF16)) + cnt_ref[0:1, :]
    rank1 = jnp.sum(jnp.where(oh1, base, 0.0), axis=1, keepdims=True)
    rank2 = jnp.sum(jnp.where(oh2, base, 0.0), axis=1, keepdims=True)
    cnt_ref[...] = cnt_ref[...] + jnp.sum(cnt, axis=0, keepdims=True)

    route = jnp.zeros((tm, LANES), F32)
    for k, val in enumerate((i1, i2, rank1, rank2, gate1, gate2)):
        route = jnp.where(col == k, val, route)
    route_ref[...] = route


def _merge(ya, yb, ym, x2, w_gates, wa, wb, wm, wo, ln_g, ln_b, wr_split, b_r, alpha, n_experts):
    n, d = x2.shape
    tm = MERGE_ROWS
    n_tiles = n // tm
    row_in = lambda w: pl.BlockSpec((tm, w), lambda i: (jnp.minimum(i, n_tiles - 1), 0))
    row_out = lambda w: pl.BlockSpec((tm, w), lambda i: (jnp.maximum(i - 1, 0), 0))
    full = lambda a: pl.BlockSpec(a.shape, lambda i: (0,) * a.ndim)
    return pl.pallas_call(
        functools.partial(_merge_kernel, alpha=alpha, n_experts=n_experts),
        grid=(n_tiles + 1,),
        in_specs=[row_in(ya.shape[1]), row_in(yb.shape[1]), row_in(ym.shape[1]), row_in(d),
                  full(w_gates), full(wa), full(wb), full(wm), full(wo), full(ln_g), full(ln_b),
                  full(wr_split), full(b_r)],
        out_specs=[row_out(d), row_out(LANES), pl.BlockSpec((8, LANES), lambda i: (0, 0))],
        out_shape=[jax.ShapeDtypeStruct((n, d), F32),
                   jax.ShapeDtypeStruct((n, LANES), F32),
                   jax.ShapeDtypeStruct((8, LANES), F32)],
        scratch_shapes=[pltpu.VMEM((2, tm, d), F32)],
        compiler_params=_params(("arbitrary",), 56),
        name="merge",
    )(ya, yb, ym, x2, w_gates, wa, wb, wm, wo, ln_g, ln_b, wr_split, b_r)


def _row_copy(src_hbm, row, buf, sem, r):
    return pltpu.make_async_copy(src_hbm.at[pl.ds(row, 1), :], buf.at[pl.ds(r, 1), :], sem)


def _gather_rows(src_hbm, idx_ref, buf, sem, n_rows, unrolled):
    if unrolled:
        for r in range(n_rows):
            _row_copy(src_hbm, idx_ref[0, 0, r], buf, sem, r).start()
    else:
        def issue(r, _):
            _row_copy(src_hbm, idx_ref[0, 0, r], buf, sem, r).start()
            return 0
        lax.fori_loop(0, n_rows, issue, 0)


def _wait_rows(src_hbm, buf, sem, n_rows):
    pltpu.make_async_copy(src_hbm.at[pl.ds(0, n_rows), :], buf, sem).wait()


DISPATCH_ROWS = 512
CHUNK = 8


def _chunk_copy(buf, xpad_hbm, sem, src_row, dst_row):
    return pltpu.make_async_copy(buf.at[pl.ds(src_row, CHUNK), :], xpad_hbm.at[pl.ds(dst_row, CHUNK), :], sem)


def _dispatch_kernel(nch_ref, off_ref, dst_ref, zrow_ref, zflag_ref, nu_ref, base_ref, route_ref, x_ref,
                     xpad_hbm, dest_ref, sorted_even, sorted_odd, zero_buf, sem, zsem):
    i = pl.program_id(0)
    n_steps = pl.num_programs(0)
    n_exp = base_ref.shape[1]
    td = x_ref.shape[0]
    rb = sorted_even.shape[0]
    zrows = zero_buf.shape[0]
    bufs = (sorted_even, sorted_odd)
    parity = lax.rem(i, 2)

    @pl.when(i == 0)
    def _():
        zero_buf[...] = jnp.zeros(zero_buf.shape, F32)

        def zero_block(row, start):
            cp = pltpu.make_async_copy(zero_buf, xpad_hbm.at[pl.ds(pl.multiple_of(row, zrows), zrows), :], zsem)
            cp.start() if start else cp.wait()

        for start in (True, False):
            for e in range(n_exp):
                pl.when(zflag_ref[e] == 1)(functools.partial(zero_block, zrow_ref[e], start))

            def unused(blk, _):
                zero_block(blk * zrows, start)
                return 0
            lax.fori_loop(nu_ref[0], xpad_hbm.shape[0] // zrows, unused, 0)

    rt = route_ref[...].T
    eid = lax.broadcasted_iota(I32, (n_exp, td), 0).astype(F32)
    rowid = lax.broadcasted_iota(I32, (rb, td), 0).astype(F32)
    hit = None
    dests = []
    for k in range(2):
        mine = eid == rt[k:k + 1, :]
        rank = rt[2 + k:3 + k, :]
        pos = jnp.sum(jnp.where(mine, base_ref[0, :, 0:1], 0.0), axis=0, keepdims=True) + rank
        dests.append(jnp.sum(jnp.where(mine, base_ref[0, :, 1:2], 0.0), axis=0, keepdims=True) + rank)
        hit = (rowid == pos) if hit is None else hit | (rowid == pos)
    dest_ref[0] = jnp.concatenate(dests + [jnp.zeros((8 - len(dests), td), F32)], axis=0).astype(I32)
    perm = jnp.where(hit, 1.0, 0.0).astype(BF16)
    rows_sorted = _dot(perm, x_ref[...].astype(BF16))

    def runs(tile, buf, s, start):
        for e in range(n_exp):
            k = tile * n_exp + e
            src0, dst0 = off_ref[k], dst_ref[k]

            def one(j, _):
                cp = _chunk_copy(buf, xpad_hbm, s, pl.multiple_of(src0 + j * CHUNK, CHUNK),
                                 pl.multiple_of(dst0 + j * CHUNK, CHUNK))
                cp.start() if start else cp.wait()
                return 0
            lax.fori_loop(0, nch_ref[k], one, 0)

    def wait_runs(tile, buf, s):
        k_last = tile * n_exp + n_exp - 1
        total = pl.multiple_of(off_ref[k_last] + nch_ref[k_last] * CHUNK, CHUNK)
        pltpu.make_async_copy(buf.at[pl.ds(0, total), :], xpad_hbm.at[pl.ds(0, total), :], s).wait()

    def step(cur):
        bufs[cur][...] = rows_sorted

        @pl.when(i > 0)
        def _():
            wait_runs(i - 1, bufs[1 - cur], sem.at[1 - cur])

        runs(i, bufs[cur], sem.at[cur], True)

        @pl.when(i + 1 == n_steps)
        def _():
            wait_runs(i, bufs[cur], sem.at[cur])

    for cur in range(2):
        pl.when(parity == cur)(functools.partial(step, cur))


def _dispatch(x1, route, nch, off8, dst, zrow, zflag, n_used, base, n_rows_out):
    n, d = x1.shape
    td = DISPATCH_ROWS
    n_tiles = n // td
    n_exp = base.shape[1]
    rb = 2 * td + n_exp * CHUNK
    grid_spec = pltpu.PrefetchScalarGridSpec(
        num_scalar_prefetch=6,
        grid=(n_tiles,),
        in_specs=[pl.BlockSpec((1, n_exp, LANES), lambda i, *_: (i, 0, 0)),
                  pl.BlockSpec((td, LANES), lambda i, *_: (i, 0)),
                  pl.BlockSpec((td, d), lambda i, *_: (i, 0))],
        out_specs=[pl.BlockSpec(memory_space=pl.ANY),
                   pl.BlockSpec((1, 8, td), lambda i, *_: (i, 0, 0))],
        scratch_shapes=[pltpu.VMEM((rb, d), F32), pltpu.VMEM((rb, d), F32),
                        pltpu.VMEM((MOE_BLOCK, d), F32),
                        pltpu.SemaphoreType.DMA((2,)), pltpu.SemaphoreType.DMA(())])
    return pl.pallas_call(
        _dispatch_kernel,
        grid_spec=grid_spec,
        out_shape=[jax.ShapeDtypeStruct((n_rows_out, d), F32),
                   jax.ShapeDtypeStruct((n_tiles, 8, td), I32)],
        compiler_params=_params(("arbitrary",), 48),
        name="dispatch",
    )(nch, off8, dst, zrow, zflag, n_used, base, route, x1)


def _experts_kernel(be_ref, nu_ref, x_ref, wg_ref, wu_ref, wd_ref, o_ref, wg_sc, wu_sc, wd_sc):
    b = pl.program_id(0)
    n_used = nu_ref[0]

    @pl.when(b < n_used)
    def _():
        changed = (b == 0) | (be_ref[b] != be_ref[jnp.maximum(b - 1, 0)])

        @pl.when(changed)
        def _():
            wg_sc[...] = wg_ref[0].astype(BF16)
            wu_sc[...] = wu_ref[0].astype(BF16)
            wd_sc[...] = wd_ref[0].astype(BF16)

        xb = x_ref[...].astype(BF16)
        g = _dot(xb, wg_sc[...])
        u = _dot(xb, wu_sc[...])
        h = (g * _sigmoid(g) * u).astype(BF16)
        o_ref[...] = _dot(h, wd_sc[...])

    @pl.when(b >= n_used)
    def _():
        o_ref[...] = jnp.zeros(o_ref.shape, F32)


def _experts(x_pad, block_expert, n_used, w_gate, w_up, w_down):
    d = x_pad.shape[1]
    n_blocks = block_expert.shape[0]
    rows = MOE_BLOCK
    d_exp = w_gate.shape[-1]
    grid_spec = pltpu.PrefetchScalarGridSpec(
        num_scalar_prefetch=2,
        grid=(n_blocks,),
        in_specs=[pl.BlockSpec((rows, d), lambda b, be, nu: (jnp.minimum(b, jnp.maximum(nu[0] - 1, 0)), 0)),
                  pl.BlockSpec((1, d, d_exp), lambda b, be, nu: (be[b], 0, 0)),
                  pl.BlockSpec((1, d, d_exp), lambda b, be, nu: (be[b], 0, 0)),
                  pl.BlockSpec((1, d_exp, d), lambda b, be, nu: (be[b], 0, 0))],
        out_specs=pl.BlockSpec((rows, d), lambda b, be, nu: (b, 0)),
        scratch_shapes=[pltpu.VMEM((d, d_exp), BF16),
                        pltpu.VMEM((d, d_exp), BF16),
                        pltpu.VMEM((d_exp, d), BF16)])
    return pl.pallas_call(
        _experts_kernel,
        grid_spec=grid_spec,
        out_shape=jax.ShapeDtypeStruct((n_blocks * rows, d), F32),
        compiler_params=_params(("arbitrary",), 44),
        name="experts",
    )(block_expert, n_used, x_pad, w_gate, w_up, w_down)


def _combine_kernel(idx_ref, idxn_ref, y_hbm, x1_ref, route_ref, lng_ref, lnb_ref, o_ref,
                    ybuf_even, ybuf_odd, sem, *, alpha):
    i = pl.program_id(0)
    n_steps = pl.num_programs(0)
    tm = x1_ref.shape[0]
    bufs = (ybuf_even, ybuf_odd)
    parity = lax.rem(i, 2)

    @pl.when(i == 0)
    def _():
        _gather_rows(y_hbm, idx_ref, bufs[0], sem.at[0], 2 * tm, False)

    def compute(cur):
        _gather_rows(y_hbm, idxn_ref, bufs[1 - cur], sem.at[1 - cur], 2 * tm, True)
        _wait_rows(y_hbm, bufs[cur], sem.at[cur], 2 * tm)
        ffn = (route_ref[:, 4:5] * bufs[cur][0:tm, :] + route_ref[:, 5:6] * bufs[cur][tm:2 * tm, :])
        o_ref[...] = _layer_norm(alpha * x1_ref[...] + ffn, lng_ref[...], lnb_ref[...])

        @pl.when(i + 1 == n_steps)
        def _():
            _wait_rows(y_hbm, bufs[1 - cur], sem.at[1 - cur], 2 * tm)

    for cur in range(2):
        pl.when(parity == cur)(functools.partial(compute, cur))


def _combine(y_pad, dest, x1, route, ln_g, ln_b, alpha):
    n, d = x1.shape
    tm = COMBINE_ROWS
    n_steps = n // tm
    idx3 = dest.reshape(n_steps, tm, 2).transpose(0, 2, 1).reshape(n_steps, 1, 2 * tm)
    smem_idx = lambda f: pl.BlockSpec((1, 1, 2 * tm), f, memory_space=pltpu.SMEM)
    return pl.pallas_call(
        functools.partial(_combine_kernel, alpha=alpha),
        grid=(n_steps,),
        in_specs=[smem_idx(lambda i: (i, 0, 0)),
                  smem_idx(lambda i: (jnp.minimum(i + 1, n_steps - 1), 0, 0)),
                  pl.BlockSpec(memory_space=pl.ANY),
                  pl.BlockSpec((tm, d), lambda i: (i, 0)),
                  pl.BlockSpec((tm, LANES), lambda i: (i, 0)),
                  pl.BlockSpec((1, d), lambda i: (0, 0)),
                  pl.BlockSpec((1, d), lambda i: (0, 0))],
        out_specs=pl.BlockSpec((tm, d), lambda i: (i, 0)),
        out_shape=jax.ShapeDtypeStruct((n, d), F32),
        scratch_shapes=[pltpu.VMEM((2 * tm, d), F32), pltpu.VMEM((2 * tm, d), F32),
                        pltpu.SemaphoreType.DMA((2,))],
        compiler_params=_params(("arbitrary",), 32),
        name="combine",
    )(idx3, idx3, y_pad, x1, route, ln_g, ln_b)


def _split_bf16(w):
    hi = w.astype(BF16)
    return hi, (w - hi.astype(F32)).astype(BF16)


def _layer(x, mem, tbl, rel_bias, w_in, w_mem_kv, w_br_moba, w_br_sb, w_br_mem, w_out, ln1_g, ln1_b,
           w_rg, b_rg, w_re, b_re, w_gate, w_up, w_down, ln2_g, ln2_b, alpha):
    bsz, seq, d = x.shape
    n = bsz * seq
    n_experts = w_re.shape[1]
    moba_w, sb_w, mem_w = N_MOBA_HEADS * HEAD_DIM, N_SB_HEADS * HEAD_DIM, N_MEM_HEADS * HEAD_DIM
    n_qkv = 3 * moba_w + 3 * sb_w + mem_w
    assert w_in.shape[1] == n_qkv + 3 * d and n_experts + N_GROUPS <= LANES and moba_w == sb_w

    x2 = x.reshape(n, d)
    qkv = _proj(x2, w_in[:, :n_qkv].astype(BF16))
    qkv3 = qkv.reshape(bsz, seq, n_qkv)
    y_a = _moba(qkv3, tbl, rel_bias, 0)
    y_b = _stickbreak(qkv3, 3 * moba_w // sb_w)
    kv = _memkv(mem.reshape(-1, d), w_mem_kv.astype(BF16)).reshape(bsz, mem.shape[1], 2 * mem_w)
    y_m = _mem_attention(qkv3, kv, (3 * moba_w + 3 * sb_w) // mem_w)

    w_r = jnp.zeros((d, LANES), F32).at[:, :n_experts].set(w_re).at[:, n_experts:n_experts + N_GROUPS].set(w_rg)
    b_r = jnp.zeros((1, LANES), F32).at[0, :n_experts].set(b_re).at[0, n_experts:n_experts + N_GROUPS].set(b_rg)
    wr_split = jnp.concatenate(_split_bf16(w_r), axis=1)
    x1, route, counts = _merge(
        y_a.reshape(n, moba_w), y_b.reshape(n, sb_w), y_m.reshape(n, mem_w), x2, w_in[:, n_qkv:].astype(BF16),
        w_br_moba.astype(BF16), w_br_sb.astype(BF16), w_br_mem.astype(BF16), w_out.astype(BF16),
        ln1_g.reshape(1, d), ln1_b.reshape(1, d), wr_split, b_r, alpha, n_experts)

    rows, td = MOE_BLOCK, DISPATCH_ROWS
    n_tiles = n // td
    expert = route[:, 0:2].astype(I32)
    tile_cnt = jnp.sum((expert.reshape(n_tiles, 2 * td, 1) == jnp.arange(n_experts, dtype=I32)).astype(I32), axis=1)
    n_chunks = (tile_cnt + CHUNK - 1) // CHUNK
    run_rows = n_chunks * CHUNK
    run_rank = jnp.cumsum(tile_cnt, axis=0) - tile_cnt
    run_off = jnp.cumsum(run_rows, axis=0) - run_rows
    sorted_off = jnp.cumsum(run_rows, axis=1) - run_rows
    expert_rows = jnp.sum(run_rows, axis=0)
    padded = (expert_rows + rows - 1) // rows * rows
    pend = jnp.cumsum(padded)
    pstart = pend - padded
    run_dst = pstart[None, :] + run_off
    n_blocks = (2 * n + n_tiles * n_experts * (CHUNK - 1)) // rows + n_experts
    block_start = jnp.arange(n_blocks, dtype=I32) * rows
    block_expert = jnp.minimum(jnp.sum((pend[None, :] <= block_start[:, None]).astype(I32), axis=1),
                               n_experts - 1)
    n_used = (pend[-1] // rows).astype(I32).reshape(1)
    base = jnp.zeros((n_tiles, n_experts, LANES), F32)
    base = base.at[:, :, 0].set((sorted_off - run_rank).astype(F32)).at[:, :, 1].set((run_dst - run_rank).astype(F32))
    x_pad, dest8 = _dispatch(x1, route, n_chunks.reshape(-1), sorted_off.reshape(-1), run_dst.reshape(-1),
                             jnp.maximum(pend - rows, 0), (padded > 0).astype(I32), n_used, base,
                             n_blocks * rows)
    dest = dest8[:, 0:2, :].transpose(0, 2, 1).reshape(n, 2)

    y_pad = _experts(x_pad, block_expert, n_used, w_gate, w_up, w_down)
    out = _combine(y_pad, dest, x1, route, ln2_g.reshape(1, d), ln2_b.reshape(1, d), alpha)
    return out.reshape(bsz, seq, d)


def kernel(x, mem, w_in, w_mem_kv, rel_bias, w_br_moba, w_br_sb, w_br_mem, w_out, ln1_g, ln1_b,
           w_router_group, b_router_group, w_router_expert, b_router_expert,
           w_gate, w_up, w_down, ln2_g, ln2_b):
    depth = w_in.shape[0]
    alpha = (2.0 * depth) ** 0.25
    tbl = _bias_table(rel_bias)
    for l in range(depth):
        x = _layer(x, mem, tbl, rel_bias, w_in[l], w_mem_kv[l], w_br_moba[l], w_br_sb[l], w_br_mem[l],
                   w_out[l], ln1_g[l], ln1_b[l], w_router_group[l], b_router_group[l],
                   w_router_expert[l], b_router_expert[l], w_gate[l], w_up[l], w_down[l],
                   ln2_g[l], ln2_b[l], alpha)
    return x
```

```python
import functools
import math

import jax
import jax.numpy as jnp
from jax import lax
from jax.experimental import pallas as pl
from jax.experimental.pallas import tpu as pltpu

F32, BF16, I32 = jnp.float32, jnp.bfloat16, jnp.int32

HEAD_DIM = 64
N_MOBA_HEADS = 6
N_SB_HEADS = 6
N_MEM_HEADS = 4
MOBA_BLOCK = 256
MOBA_TOPK = 3
N_BUCKETS = 32
MAX_DISTANCE = 128
N_GROUPS = 4
EXPERTS_PER_GROUP = 8
LN_EPS = 1e-5
NEG = -1e30

LANES = 128
VMEM_BYTES = 64 * 1024 * 1024
PAIR = LANES // HEAD_DIM

SB_ZERO_LOG = 110.0

PROJ_ROWS = 512
MERGE_ROWS = 512
ATTN_ROWS = 256
MEM_ROWS = 1024
COMBINE_ROWS = 256
MOE_BLOCK = 512


def _params(semantics, vmem_mb):
    return pltpu.CompilerParams(dimension_semantics=semantics,
                                vmem_limit_bytes=min(vmem_mb * 1024 * 1024, VMEM_BYTES))


def _dot(a, b):
    return jnp.dot(a, b, preferred_element_type=F32)


def _layer_norm(h, g, b):
    mu = jnp.mean(h, axis=-1, keepdims=True)
    d = h - mu
    var = jnp.mean(d * d, axis=-1, keepdims=True)
    return d * lax.rsqrt(var + LN_EPS) * g + b


def _sigmoid(x):
    return 1.0 / (1.0 + jnp.exp(-x))


def _head_rows(qt, row, hh, scale):
    keep = (row >= hh * HEAD_DIM) & (row < (hh + 1) * HEAD_DIM)
    return jnp.where(keep, qt * scale, 0.0).astype(BF16)


def _transposed_bf16(x):
    return x.astype(F32).T.astype(BF16)


def _proj_kernel(x_ref, wq_ref, qkv_ref):
    qkv_ref[...] = _dot(x_ref[...].astype(BF16), wq_ref[...]).astype(BF16)


def _proj(x2, w_qkv):
    n, d = x2.shape
    n_qkv = w_qkv.shape[1]
    tm = PROJ_ROWS
    return pl.pallas_call(
        _proj_kernel,
        grid=(n // tm,),
        in_specs=[pl.BlockSpec((tm, d), lambda i: (i, 0)),
                  pl.BlockSpec((d, n_qkv), lambda i: (0, 0))],
        out_specs=pl.BlockSpec((tm, n_qkv), lambda i: (i, 0)),
        out_shape=jax.ShapeDtypeStruct((n, n_qkv), BF16),
        compiler_params=_params(("arbitrary",), 40),
        name="proj",
    )(x2, w_qkv)


def _t5_bucket(rel):
    rel = jnp.maximum(rel, 0)
    max_exact = N_BUCKETS // 2
    rel_f = jnp.maximum(rel, 1).astype(F32)
    large = max_exact + (jnp.log(rel_f / max_exact) / math.log(MAX_DISTANCE / max_exact)
                         * (N_BUCKETS - max_exact)).astype(I32)
    large = jnp.minimum(large, N_BUCKETS - 1)
    return jnp.where(rel < max_exact, rel, large)


def _bias_table_kernel(rb_ref, o_ref):
    h = pl.program_id(0)
    blk = o_ref.shape[2]
    j = lax.broadcasted_iota(I32, (2 * blk, blk), 0)
    i = lax.broadcasted_iota(I32, (2 * blk, blk), 1)
    bucket = _t5_bucket(blk + i - j)
    acc = jnp.zeros((2 * blk, blk), F32)
    for b in range(N_BUCKETS):
        acc = jnp.where(bucket == b, rb_ref[b, h], acc)
    o_ref[0] = acc


def _bias_table(rel_bias):
    n_heads = rel_bias.shape[1]
    blk = MOBA_BLOCK
    return pl.pallas_call(
        _bias_table_kernel,
        grid=(n_heads,),
        in_specs=[pl.BlockSpec(memory_space=pltpu.SMEM)],
        out_specs=pl.BlockSpec((1, 2 * blk, blk), lambda h: (h, 0, 0)),
        out_shape=jax.ShapeDtypeStruct((n_heads, 2 * blk, blk), F32),
        compiler_params=_params(("arbitrary",), 32),
        name="bias_table",
    )(rel_bias)


def _fold_keys(x, op, final):
    while x.shape[0] > 8 and x.shape[0] % 2 == 0:
        half = x.shape[0] // 2
        x = op(x[:half], x[half:])
    return final(x, axis=0, keepdims=True)


def _head_queries(qt, row, hh, scale):
    p, sub = divmod(hh, PAIR)
    return _head_rows(qt[p * LANES:(p + 1) * LANES, :], row, sub, scale)


def _pair_lanes(hh):
    p = hh // PAIR
    return slice(p * LANES, (p + 1) * LANES)


def _head_dims(hh):
    return slice(hh * HEAD_DIM, (hh + 1) * HEAD_DIM)


def _moba_kernel(rb_ref, q_ref, k_ref, v_ref, tbl_ref, o_ref, kmean_sc, vt_sc, sel_sc, s_sc, p_sc):
    own = pl.program_id(1)
    blk, width = q_ref.shape[1], q_ref.shape[2]
    n_heads = width // HEAD_DIM
    nb = k_ref.shape[1] // blk
    scale = HEAD_DIM ** -0.5

    @pl.when(own == 0)
    def _():
        for n in range(nb):
            kb = k_ref[0, n * blk:(n + 1) * blk, :].astype(F32)
            kmean_sc[n:n + 1, :] = jnp.mean(kb, axis=0, keepdims=True)
            vt_sc[n] = _transposed_bf16(v_ref[0, n * blk:(n + 1) * blk, :])

    qt = q_ref[0].astype(F32).T
    row = lax.broadcasted_iota(I32, (LANES, blk), 0)
    blk_id = lax.broadcasted_iota(I32, (nb, blk), 0)
    valid = blk_id < own
    causal = (lax.broadcasted_iota(I32, (blk, blk), 0) <= lax.broadcasted_iota(I32, (blk, blk), 1))
    kmean = kmean_sc[...].astype(BF16)

    qs = [_head_queries(qt, row, hh, scale) for hh in range(n_heads)]
    gates = [_dot(kmean[:, _pair_lanes(hh)], _head_queries(qt, row, hh, 1.0)) for hh in range(n_heads)]

    def scores(n):
        offn = pl.multiple_of(n * blk, blk)
        return [_dot(k_ref[0, pl.ds(offn, blk), _pair_lanes(hh)], qs[hh]) for hh in range(n_heads)]

    own_scores = scores(own)
    prev_scores = scores(jnp.maximum(own - 1, 0))

    blk_f = blk_id.astype(F32)
    for hh in range(n_heads):
        gate = jnp.where(valid, gates[hh], NEG)
        picked = jnp.zeros((nb, blk), jnp.bool_)
        for _ in range(MOBA_TOPK):
            top = _fold_keys(gate, jnp.maximum, jnp.max)
            first = _fold_keys(jnp.where(gate == top, blk_f, float(nb)), jnp.minimum, jnp.min)
            hit = blk_f == first
            picked = picked | hit
            gate = jnp.where(hit, -jnp.inf, gate)
        sel_sc[hh] = jnp.where(valid & picked, 0.0, NEG)

    prev = jnp.maximum(own - 1, 0)
    no_prev = jnp.where(own > 0, 0.0, NEG)
    stats = []
    for hh, (s_own, s_prev) in enumerate(zip(own_scores, prev_scores)):
        s_own = jnp.where(causal, s_own + tbl_ref[hh, blk:, :], NEG)
        s_prev = s_prev + tbl_ref[hh, :blk, :]
        term = sel_sc[hh, pl.ds(prev, 1), :] + no_prev
        m0 = jnp.maximum(_fold_keys(s_own, jnp.maximum, jnp.max),
                         _fold_keys(s_prev, jnp.maximum, jnp.max) + term)
        p_own = jnp.exp(s_own - m0)
        p_prev = jnp.exp(s_prev - (m0 - term))
        l0 = _fold_keys(p_own, jnp.add, jnp.sum) + _fold_keys(p_prev, jnp.add, jnp.sum)
        stats.append((m0, l0, p_own.astype(BF16), p_prev.astype(BF16)))
    state = []
    for hh, (m0, l0, p_own, p_prev) in enumerate(stats):
        state += [m0, l0, _dot(vt_sc[own, _head_dims(hh), :], p_own) + _dot(vt_sc[prev, _head_dims(hh), :], p_prev)]
    state = tuple(state)

    def softmax_step(s, per_query, m_run, l_run):
        m_new = jnp.maximum(m_run, _fold_keys(s, jnp.maximum, jnp.max) + per_query)
        a = jnp.exp(m_run - m_new)
        pn = jnp.exp(s - (m_new - per_query))
        return m_new, a * l_run + _fold_keys(pn, jnp.add, jnp.sum), a, pn.astype(BF16)

    n_far = own - 1
    last_far = jnp.maximum(n_far - 1, 0)
    for hh, s in enumerate(scores(0)):
        s_sc[hh] = s
    p_sc[...] = jnp.zeros(p_sc.shape, BF16)

    def far_block(n, carry):
        state, a_prev = carry[:3 * n_heads], carry[3 * n_heads:]
        n_prev = jnp.maximum(n - 1, 0)
        pv = [_dot(vt_sc[n_prev, _head_dims(hh), :], p_sc[hh]) for hh in range(n_heads)]
        s_next = scores(jnp.minimum(n + 1, last_far))
        stats = [softmax_step(s_sc[hh], rb_ref[N_BUCKETS - 1, hh] + sel_sc[hh, pl.ds(n, 1), :],
                              state[3 * hh], state[3 * hh + 1]) for hh in range(n_heads)]
        new, a_new = [], []
        for hh, (m_new, l_new, a, pn) in enumerate(stats):
            p_sc[hh] = pn
            s_sc[hh] = s_next[hh]
            new += [m_new, l_new, a_prev[hh] * state[3 * hh + 2] + pv[hh]]
            a_new.append(a)
        return tuple(new) + tuple(a_new)

    ones = jnp.ones((1, blk), F32)
    carry = lax.fori_loop(0, n_far, far_block, tuple(state) + (ones,) * n_heads)
    state, a_prev = carry[:3 * n_heads], carry[3 * n_heads:]
    out_t = jnp.concatenate(
        [(a_prev[hh] * state[3 * hh + 2] + _dot(vt_sc[last_far, _head_dims(hh), :], p_sc[hh])) / state[3 * hh + 1]
         for hh in range(n_heads)], axis=0)
    o_ref[0] = out_t.T.astype(o_ref.dtype)


def _moba(qkv3, tbl, rel_bias, col0):
    bsz, seq, _ = qkv3.shape
    assert MAX_DISTANCE <= MOBA_BLOCK and seq % MOBA_BLOCK == 0
    width = N_MOBA_HEADS * HEAD_DIM
    blk = MOBA_BLOCK
    nb = seq // blk
    return pl.pallas_call(
        _moba_kernel,
        grid=(bsz, nb),
        in_specs=[pl.BlockSpec(memory_space=pltpu.SMEM),
                  pl.BlockSpec((1, blk, width), lambda b, i: (b, i, col0)),
                  pl.BlockSpec((1, seq, width), lambda b, i: (b, 0, col0 + 1)),
                  pl.BlockSpec((1, seq, width), lambda b, i: (b, 0, col0 + 2)),
                  pl.BlockSpec((N_MOBA_HEADS, 2 * blk, blk), lambda b, i: (0, 0, 0))],
        out_specs=pl.BlockSpec((1, blk, width), lambda b, i: (b, i, 0)),
        out_shape=jax.ShapeDtypeStruct((bsz, seq, width), BF16),
        scratch_shapes=[pltpu.VMEM((nb, width), F32),
                        pltpu.VMEM((nb, width, blk), BF16),
                        pltpu.VMEM((N_MOBA_HEADS, nb, blk), F32),
                        pltpu.VMEM((N_MOBA_HEADS, blk, blk), F32),
                        pltpu.VMEM((N_MOBA_HEADS, blk, blk), BF16)],
        compiler_params=_params(("arbitrary", "arbitrary"), 48),
        name="moba",
    )(rel_bias, qkv3, qkv3, qkv3, tbl)


def _sb_kernel(q_ref, k_ref, v_ref, o_ref, vt_sc):
    qi = pl.program_id(1)
    t, width = q_ref.shape[1], q_ref.shape[2]
    n_heads = width // HEAD_DIM
    nb = k_ref.shape[1] // t
    scale = HEAD_DIM ** -0.5

    @pl.when(qi == 0)
    def _():
        for n in range(nb):
            vt_sc[n] = _transposed_bf16(v_ref[0, n * t:(n + 1) * t, :])

    qt = q_ref[0].astype(F32).T
    row = lax.broadcasted_iota(I32, (LANES, t), 0)
    key = lax.broadcasted_iota(I32, (t, t), 0)
    qry = lax.broadcasted_iota(I32, (t, t), 1)
    strict = key < qry
    tri = jnp.where(qry >= key, 1.0, 0.0).astype(BF16)
    qs = [_head_queries(qt, row, hh, scale) for hh in range(n_heads)]

    def blocks(tiles):
        zs = [[_dot(k_ref[0, pl.ds(pl.multiple_of(j * t, t), t), _pair_lanes(hh)], qs[hh])
               for hh in range(n_heads)] for j, _ in tiles]
        csums = []
        for (_, diagonal), zt in zip(tiles, zs):
            row_sums = []
            for z in zt:
                sp = jnp.maximum(z, 0.0) + jnp.log(1.0 + jnp.exp(-jnp.abs(z)))
                if diagonal:
                    sp = jnp.where(strict, sp, 0.0)
                hi = sp.astype(BF16)
                lo = (sp - hi.astype(F32)).astype(BF16)
                row_sums.append(_dot(tri, hi) + _dot(tri, lo))
            csums.append(row_sums)
        out = []
        for (j, diagonal), zt, ct in zip(tiles, zs, csums):
            pvs = []
            for hh in range(n_heads):
                a = jnp.exp(zt[hh] - ct[hh])
                if diagonal:
                    a = jnp.where(strict, a, 0.0)
                pvs.append(_dot(vt_sc[j, _head_dims(hh), :], a.astype(BF16)))
            out.append((pvs, [c[0:1, :] for c in ct]))
        return out

    has_prev = qi > 0
    (pv_d, tot_d), (pv_p, tot_p) = blocks([(qi, True), (jnp.maximum(qi - 1, 0), False)])
    accs = tuple(d + jnp.where(has_prev, p * jnp.exp(-c), 0.0) for d, p, c in zip(pv_d, pv_p, tot_d))
    carries = tuple(c + jnp.where(has_prev, p, 0.0) for c, p in zip(tot_d, tot_p))

    def lowest(carries):
        return jnp.min(functools.reduce(jnp.minimum, carries))

    def cond(state):
        j, cmin, _, _ = state
        return (j >= 0) & (cmin < SB_ZERO_LOG)

    def body(state):
        j, _, carries, accs = state
        (pvs, totals), = blocks([(j, False)])
        accs = tuple(acc + pv * jnp.exp(-c) for acc, pv, c in zip(accs, pvs, carries))
        carries = tuple(c + total for c, total in zip(carries, totals))
        return j - 1, lowest(carries), carries, accs

    state = lax.while_loop(cond, body, (qi - 2, lowest(carries), carries, accs))
    out_t = jnp.concatenate(list(state[3]), axis=0)
    o_ref[0] = out_t.T.astype(o_ref.dtype)


def _stickbreak(qkv3, col0):
    bsz, seq, _ = qkv3.shape
    width = N_SB_HEADS * HEAD_DIM
    t = ATTN_ROWS
    return pl.pallas_call(
        _sb_kernel,
        grid=(bsz, seq // t),
        in_specs=[pl.BlockSpec((1, t, width), lambda b, i: (b, i, col0)),
                  pl.BlockSpec((1, seq, width), lambda b, i: (b, 0, col0 + 1)),
                  pl.BlockSpec((1, seq, width), lambda b, i: (b, 0, col0 + 2))],
        out_specs=pl.BlockSpec((1, t, width), lambda b, i: (b, i, 0)),
        out_shape=jax.ShapeDtypeStruct((bsz, seq, width), BF16),
        scratch_shapes=[pltpu.VMEM((seq // t, width, t), BF16)],
        compiler_params=_params(("arbitrary", "arbitrary"), 48),
        name="stickbreak",
    )(qkv3, qkv3, qkv3)


def _memkv_kernel(m_ref, w_ref, o_ref):
    o_ref[...] = _dot(m_ref[...].astype(BF16), w_ref[...]).astype(BF16)


def _memkv(mem2, w_bf16):
    n, d = mem2.shape
    width = w_bf16.shape[1]
    tm = min(n, 512)
    return pl.pallas_call(
        _memkv_kernel,
        grid=(n // tm,),
        in_specs=[pl.BlockSpec((tm, d), lambda i: (i, 0)),
                  pl.BlockSpec((d, width), lambda i: (0, 0))],
        out_specs=pl.BlockSpec((tm, width), lambda i: (i, 0)),
        out_shape=jax.ShapeDtypeStruct((n, width), BF16),
        compiler_params=_params(("arbitrary",), 32),
        name="memkv",
    )(mem2, w_bf16)


def _mem_kernel(q_ref, k_ref, v_ref, o_ref, vt_sc):
    t, width = q_ref.shape[1], q_ref.shape[2]
    n_heads = width // HEAD_DIM
    scale = HEAD_DIM ** -0.5

    @pl.when(pl.program_id(1) == 0)
    def _():
        vt_sc[...] = _transposed_bf16(v_ref[0])

    qt = q_ref[0].astype(F32).T
    row = lax.broadcasted_iota(I32, (LANES, t), 0)
    qs = [_head_queries(qt, row, hh, scale) for hh in range(n_heads)]
    scores = [_dot(k_ref[0, :, _pair_lanes(hh)], qs[hh]) for hh in range(n_heads)]
    probs = []
    for s in scores:
        e = jnp.exp(s - _fold_keys(s, jnp.maximum, jnp.max))
        probs.append((e * (1.0 / _fold_keys(e, jnp.add, jnp.sum))).astype(BF16))
    out_t = jnp.concatenate([_dot(vt_sc[_head_dims(hh), :], probs[hh]) for hh in range(n_heads)], axis=0)
    o_ref[0] = out_t.T.astype(o_ref.dtype)


def _mem_attention(qkv3, kv3, qcol):
    bsz, seq, _ = qkv3.shape
    n_mem = kv3.shape[1]
    width = N_MEM_HEADS * HEAD_DIM
    t = MEM_ROWS
    return pl.pallas_call(
        _mem_kernel,
        grid=(bsz, seq // t),
        in_specs=[pl.BlockSpec((1, t, width), lambda b, i: (b, i, qcol)),
                  pl.BlockSpec((1, n_mem, width), lambda b, i: (b, 0, 0)),
                  pl.BlockSpec((1, n_mem, width), lambda b, i: (b, 0, 1))],
        out_specs=pl.BlockSpec((1, t, width), lambda b, i: (b, i, 0)),
        out_shape=jax.ShapeDtypeStruct((bsz, seq, width), BF16),
        scratch_shapes=[pltpu.VMEM((width, n_mem), BF16)],
        compiler_params=_params(("arbitrary", "arbitrary"), 32),
        name="mem_attention",
    )(qkv3, kv3, kv3)


def _merge_kernel(ya_ref, yb_ref, ym_ref, x_ref, wg_ref, wa_ref, wb_ref, wm_ref, wo_ref,
                  lng_ref, lnb_ref, wrh_ref, br_ref, x1_ref, route_ref, cnt_ref, h_sc,
                  *, alpha, n_experts):
    step = pl.program_id(0)
    tm, d = x_ref.shape
    slot = lax.rem(step, 2)

    @pl.when(step == 0)
    def _():
        cnt_ref[...] = jnp.zeros(cnt_ref.shape, F32)
        h_sc[...] = jnp.zeros(h_sc.shape, F32)

    x = x_ref[...]
    xb = x.astype(BF16)
    branches = ((ya_ref, wa_ref), (yb_ref, wb_ref), (ym_ref, wm_ref))

    def branch_term(k):
        gate_logits = _dot(xb, wg_ref[:, k * d:(k + 1) * d])
        return gate_logits, _dot(branches[k][0][...], branches[k][1][...])

    x1 = _layer_norm(h_sc[1 - slot], lng_ref[...], lnb_ref[...])
    x1_ref[...] = x1
    xh = x1.astype(BF16)
    xl = (x1 - xh.astype(F32)).astype(BF16)

    first_term = branch_term(0)

    by_hi = _dot(xh, wrh_ref[...])
    logits = (by_hi[:, :LANES] + (by_hi[:, LANES:] + _dot(xl, wrh_ref[:, :LANES]))
              + br_ref[...])

    merged = None
    for k in range(len(branches)):
        gate_logits, y = first_term if k == 0 else branch_term(k)
        term = _sigmoid(gate_logits) * y
        merged = term if merged is None else merged + term
    h_sc[slot] = alpha * x + _dot(merged.astype(BF16), wo_ref[...])

    col = lax.broadcasted_iota(I32, (tm, LANES), 1)
    colf = col.astype(F32)
    big = float(LANES)
    gmask = (col >= n_experts) & (col < n_experts + N_GROUPS)
    lg = jnp.where(gmask, logits, -jnp.inf)
    gmax = jnp.max(lg, axis=1, keepdims=True)
    gidx = jnp.min(jnp.where(lg == gmax, colf, big), axis=1, keepdims=True) - n_experts
    g_p = 1.0 / jnp.sum(jnp.where(gmask, jnp.exp(logits - gmax), 0.0), axis=1, keepdims=True)
    lo_col = gidx * EXPERTS_PER_GROUP
    emask = (colf >= lo_col) & (colf < lo_col + EXPERTS_PER_GROUP)
    le = jnp.where(emask, logits, -jnp.inf)
    l1 = jnp.max(le, axis=1, keepdims=True)
    i1 = jnp.min(jnp.where(le == l1, colf, big), axis=1, keepdims=True)
    le2 = jnp.where(colf == i1, -jnp.inf, le)
    l2 = jnp.max(le2, axis=1, keepdims=True)
    i2 = jnp.min(jnp.where(le2 == l2, colf, big), axis=1, keepdims=True)
    e2 = jnp.exp(l2 - l1)
    gate1 = g_p * (1.0 / (1.0 + e2))
    gate2 = g_p * (e2 / (1.0 + e2))

    oh1 = colf == i1
    oh2 = colf == i2
    cnt = jnp.where((oh1 | oh2) & (step > 0), 1.0, 0.0)
    rr = lax.broadcasted_iota(I32, (tm, tm), 0)
    cc = lax.broadcasted_iota(I32, (tm, tm), 1)
    before = jnp.where(cc < rr, 1.0, 0.0).astype(BF16)
    base = _dot(before, cnt.astype(BF16)) + cnt_ref[0:1, :]
    rank1 = jnp.sum(jnp.where(oh1, base, 0.0), axis=1, keepdims=True)
    rank2 = jnp.sum(jnp.where(oh2, base, 0.0), axis=1, keepdims=True)
    cnt_ref[...] = cnt_ref[...] + jnp.sum(cnt, axis=0, keepdims=True)

    route = jnp.zeros((tm, LANES), F32)
    for k, val in enumerate((i1, i2, rank1, rank2, gate1, gate2)):
        route = jnp.where(col == k, val, route)
    route_ref[...] = route


def _merge(ya, yb, ym, x2, w_gates, wa, wb, wm, wo, ln_g, ln_b, wr_split, b_r, alpha, n_experts):
    n, d = x2.shape
    tm = MERGE_ROWS
    n_tiles = n // tm
    row_in = lambda w: pl.BlockSpec((tm, w), lambda i: (jnp.minimum(i, n_tiles - 1), 0))
    row_out = lambda w: pl.BlockSpec((tm, w), lambda i: (jnp.maximum(i - 1, 0), 0))
    full = lambda a: pl.BlockSpec(a.shape, lambda i: (0,) * a.ndim)
    return pl.pallas_call(
        functools.partial(_merge_kernel, alpha=alpha, n_experts=n_experts),
        grid=(n_tiles + 1,),
        in_specs=[row_in(ya.shape[1]), row_in(yb.shape[1]), row_in(ym.shape[1]), row_in(d),
                  full(w_gates), full(wa), full(wb), full(wm), full(wo), full(ln_g), full(ln_b),
                  full(wr_split), full(b_r)],
        out_specs=[row_out(d), row_out(LANES), pl.BlockSpec((8, LANES), lambda i: (0, 0))],
        out_shape=[jax.ShapeDtypeStruct((n, d), F32),
                   jax.ShapeDtypeStruct((n, LANES), F32),
                   jax.ShapeDtypeStruct((8, LANES), F32)],
        scratch_shapes=[pltpu.VMEM((2, tm, d), F32)],
        compiler_params=_params(("arbitrary",), 56),
        name="merge",
    )(ya, yb, ym, x2, w_gates, wa, wb, wm, wo, ln_g, ln_b, wr_split, b_r)


def _row_copy(src_hbm, row, buf, sem, r):
    return pltpu.make_async_copy(src_hbm.at[pl.ds(row, 1), :], buf.at[pl.ds(r, 1), :], sem)


def _gather_rows(src_hbm, idx_ref, buf, sem, n_rows, unrolled):
    if unrolled:
        for r in range(n_rows):
            _row_copy(src_hbm, idx_ref[0, 0, r], buf, sem, r).start()
    else:
        def issue(r, _):
            _row_copy(src_hbm, idx_ref[0, 0, r], buf, sem, r).start()
            return 0
        lax.fori_loop(0, n_rows, issue, 0)


def _wait_rows(src_hbm, buf, sem, n_rows):
    pltpu.make_async_copy(src_hbm.at[pl.ds(0, n_rows), :], buf, sem).wait()


DISPATCH_ROWS = 512
CHUNK = 8


def _chunk_copy(buf, xpad_hbm, sem, src_row, dst_row):
    return pltpu.make_async_copy(buf.at[pl.ds(src_row, CHUNK), :], xpad_hbm.at[pl.ds(dst_row, CHUNK), :], sem)


def _dispatch_kernel(nch_ref, off_ref, dst_ref, zrow_ref, zflag_ref, nu_ref, base_ref, route_ref, x_ref,
                     xpad_hbm, dest_ref, sorted_even, sorted_odd, zero_buf, sem, zsem):
    i = pl.program_id(0)
    n_steps = pl.num_programs(0)
    n_exp = base_ref.shape[1]
    td = x_ref.shape[0]
    rb = sorted_even.shape[0]
    zrows = zero_buf.shape[0]
    bufs = (sorted_even, sorted_odd)
    parity = lax.rem(i, 2)

    @pl.when(i == 0)
    def _():
        zero_buf[...] = jnp.zeros(zero_buf.shape, F32)

        def zero_block(row, start):
            cp = pltpu.make_async_copy(zero_buf, xpad_hbm.at[pl.ds(pl.multiple_of(row, zrows), zrows), :], zsem)
            cp.start() if start else cp.wait()

        for start in (True, False):
            for e in range(n_exp):
                pl.when(zflag_ref[e] == 1)(functools.partial(zero_block, zrow_ref[e], start))

            def unused(blk, _):
                zero_block(blk * zrows, start)
                return 0
            lax.fori_loop(nu_ref[0], xpad_hbm.shape[0] // zrows, unused, 0)

    rt = route_ref[...].T
    eid = lax.broadcasted_iota(I32, (n_exp, td), 0).astype(F32)
    rowid = lax.broadcasted_iota(I32, (rb, td), 0).astype(F32)
    hit = None
    dests = []
    for k in range(2):
        mine = eid == rt[k:k + 1, :]
        rank = rt[2 + k:3 + k, :]
        pos = jnp.sum(jnp.where(mine, base_ref[0, :, 0:1], 0.0), axis=0, keepdims=True) + rank
        dests.append(jnp.sum(jnp.where(mine, base_ref[0, :, 1:2], 0.0), axis=0, keepdims=True) + rank)
        hit = (rowid == pos) if hit is None else hit | (rowid == pos)
    dest_ref[0] = jnp.concatenate(dests + [jnp.zeros((8 - len(dests), td), F32)], axis=0).astype(I32)
    perm = jnp.where(hit, 1.0, 0.0).astype(BF16)
    rows_sorted = _dot(perm, x_ref[...].astype(BF16))

    def runs(tile, buf, s, start):
        for e in range(n_exp):
            k = tile * n_exp + e
            src0, dst0 = off_ref[k], dst_ref[k]

            def one(j, _):
                cp = _chunk_copy(buf, xpad_hbm, s, pl.multiple_of(src0 + j * CHUNK, CHUNK),
                                 pl.multiple_of(dst0 + j * CHUNK, CHUNK))
                cp.start() if start else cp.wait()
                return 0
            lax.fori_loop(0, nch_ref[k], one, 0)

    def wait_runs(tile, buf, s):
        k_last = tile * n_exp + n_exp - 1
        total = pl.multiple_of(off_ref[k_last] + nch_ref[k_last] * CHUNK, CHUNK)
        pltpu.make_async_copy(buf.at[pl.ds(0, total), :], xpad_hbm.at[pl.ds(0, total), :], s).wait()

    def step(cur):
        bufs[cur][...] = rows_sorted

        @pl.when(i > 0)
        def _():
            wait_runs(i - 1, bufs[1 - cur], sem.at[1 - cur])

        runs(i, bufs[cur], sem.at[cur], True)

        @pl.when(i + 1 == n_steps)
        def _():
            wait_runs(i, bufs[cur], sem.at[cur])

    for cur in range(2):
        pl.when(parity == cur)(functools.partial(step, cur))


def _dispatch(x1, route, nch, off8, dst, zrow, zflag, n_used, base, n_rows_out):
    n, d = x1.shape
    td = DISPATCH_ROWS
    n_tiles = n // td
    n_exp = base.shape[1]
    rb = 2 * td + n_exp * CHUNK
    grid_spec = pltpu.PrefetchScalarGridSpec(
        num_scalar_prefetch=6,
        grid=(n_tiles,),
        in_specs=[pl.BlockSpec((1, n_exp, LANES), lambda i, *_: (i, 0, 0)),
                  pl.BlockSpec((td, LANES), lambda i, *_: (i, 0)),
                  pl.BlockSpec((td, d), lambda i, *_: (i, 0))],
        out_specs=[pl.BlockSpec(memory_space=pl.ANY),
                   pl.BlockSpec((1, 8, td), lambda i, *_: (i, 0, 0))],
        scratch_shapes=[pltpu.VMEM((rb, d), F32), pltpu.VMEM((rb, d), F32),
                        pltpu.VMEM((MOE_BLOCK, d), F32),
                        pltpu.SemaphoreType.DMA((2,)), pltpu.SemaphoreType.DMA(())])
    return pl.pallas_call(
        _dispatch_kernel,
        grid_spec=grid_spec,
        out_shape=[jax.ShapeDtypeStruct((n_rows_out, d), F32),
                   jax.ShapeDtypeStruct((n_tiles, 8, td), I32)],
        compiler_params=_params(("arbitrary",), 48),
        name="dispatch",
    )(nch, off8, dst, zrow, zflag, n_used, base, route, x1)


def _experts_kernel(be_ref, nu_ref, x_ref, wg_ref, wu_ref, wd_ref, o_ref, wg_sc, wu_sc, wd_sc):
    b = pl.program_id(0)
    n_used = nu_ref[0]

    @pl.when(b < n_used)
    def _():
        changed = (b == 0) | (be_ref[b] != be_ref[jnp.maximum(b - 1, 0)])

        @pl.when(changed)
        def _():
            wg_sc[...] = wg_ref[0].astype(BF16)
            wu_sc[...] = wu_ref[0].astype(BF16)
            wd_sc[...] = wd_ref[0].astype(BF16)

        xb = x_ref[...].astype(BF16)
        g = _dot(xb, wg_sc[...])
        u = _dot(xb, wu_sc[...])
        h = (g * _sigmoid(g) * u).astype(BF16)
        o_ref[...] = _dot(h, wd_sc[...])

    @pl.when(b >= n_used)
    def _():
        o_ref[...] = jnp.zeros(o_ref.shape, F32)


def _experts(x_pad, block_expert, n_used, w_gate, w_up, w_down):
    d = x_pad.shape[1]
    n_blocks = block_expert.shape[0]
    rows = MOE_BLOCK
    d_exp = w_gate.shape[-1]
    grid_spec = pltpu.PrefetchScalarGridSpec(
        num_scalar_prefetch=2,
        grid=(n_blocks,),
        in_specs=[pl.BlockSpec((rows, d), lambda b, be, nu: (jnp.minimum(b, jnp.maximum(nu[0] - 1, 0)), 0)),
                  pl.BlockSpec((1, d, d_exp), lambda b, be, nu: (be[b], 0, 0)),
                  pl.BlockSpec((1, d, d_exp), lambda b, be, nu: (be[b], 0, 0)),
                  pl.BlockSpec((1, d_exp, d), lambda b, be, nu: (be[b], 0, 0))],
        out_specs=pl.BlockSpec((rows, d), lambda b, be, nu: (b, 0)),
        scratch_shapes=[pltpu.VMEM((d, d_exp), BF16),
                        pltpu.VMEM((d, d_exp), BF16),
                        pltpu.VMEM((d_exp, d), BF16)])
    return pl.pallas_call(
        _experts_kernel,
        grid_spec=grid_spec,
        out_shape=jax.ShapeDtypeStruct((n_blocks * rows, d), F32),
        compiler_params=_params(("arbitrary",), 44),
        name="experts",
    )(block_expert, n_used, x_pad, w_gate, w_up, w_down)


def _combine_kernel(idx_ref, idxn_ref, y_hbm, x1_ref, route_ref, lng_ref, lnb_ref, o_ref,
                    ybuf_even, ybuf_odd, sem, *, alpha):
    i = pl.program_id(0)
    n_steps = pl.num_programs(0)
    tm = x1_ref.shape[0]
    bufs = (ybuf_even, ybuf_odd)
    parity = lax.rem(i, 2)

    @pl.when(i == 0)
    def _():
        _gather_rows(y_hbm, idx_ref, bufs[0], sem.at[0], 2 * tm, False)

    def compute(cur):
        _gather_rows(y_hbm, idxn_ref, bufs[1 - cur], sem.at[1 - cur], 2 * tm, True)
        _wait_rows(y_hbm, bufs[cur], sem.at[cur], 2 * tm)
        ffn = (route_ref[:, 4:5] * bufs[cur][0:tm, :] + route_ref[:, 5:6] * bufs[cur][tm:2 * tm, :])
        o_ref[...] = _layer_norm(alpha * x1_ref[...] + ffn, lng_ref[...], lnb_ref[...])

        @pl.when(i + 1 == n_steps)
        def _():
            _wait_rows(y_hbm, bufs[1 - cur], sem.at[1 - cur], 2 * tm)

    for cur in range(2):
        pl.when(parity == cur)(functools.partial(compute, cur))


def _combine(y_pad, dest, x1, route, ln_g, ln_b, alpha):
    n, d = x1.shape
    tm = COMBINE_ROWS
    n_steps = n // tm
    idx3 = dest.reshape(n_steps, tm, 2).transpose(0, 2, 1).reshape(n_steps, 1, 2 * tm)
    smem_idx = lambda f: pl.BlockSpec((1, 1, 2 * tm), f, memory_space=pltpu.SMEM)
    return pl.pallas_call(
        functools.partial(_combine_kernel, alpha=alpha),
        grid=(n_steps,),
        in_specs=[smem_idx(lambda i: (i, 0, 0)),
                  smem_idx(lambda i: (jnp.minimum(i + 1, n_steps - 1), 0, 0)),
                  pl.BlockSpec(memory_space=pl.ANY),
                  pl.BlockSpec((tm, d), lambda i: (i, 0)),
                  pl.BlockSpec((tm, LANES), lambda i: (i, 0)),
                  pl.BlockSpec((1, d), lambda i: (0, 0)),
                  pl.BlockSpec((1, d), lambda i: (0, 0))],
        out_specs=pl.BlockSpec((tm, d), lambda i: (i, 0)),
        out_shape=jax.ShapeDtypeStruct((n, d), F32),
        scratch_shapes=[pltpu.VMEM((2 * tm, d), F32), pltpu.VMEM((2 * tm, d), F32),
                        pltpu.SemaphoreType.DMA((2,))],
        compiler_params=_params(("arbitrary",), 32),
        name="combine",
    )(idx3, idx3, y_pad, x1, route, ln_g, ln_b)


def _split_bf16(w):
    hi = w.astype(BF16)
    return hi, (w - hi.astype(F32)).astype(BF16)


def _layer(x, mem, tbl, rel_bias, w_in, w_mem_kv, w_br_moba, w_br_sb, w_br_mem, w_out, ln1_g, ln1_b,
           w_rg, b_rg, w_re, b_re, w_gate, w_up, w_down, ln2_g, ln2_b, alpha):
    bsz, seq, d = x.shape
    n = bsz * seq
    n_experts = w_re.shape[1]
    moba_w, sb_w, mem_w = N_MOBA_HEADS * HEAD_DIM, N_SB_HEADS * HEAD_DIM, N_MEM_HEADS * HEAD_DIM
    n_qkv = 3 * moba_w + 3 * sb_w + mem_w
    assert w_in.shape[1] == n_qkv + 3 * d and n_experts + N_GROUPS <= LANES and moba_w == sb_w

    x2 = x.reshape(n, d)
    qkv = _proj(x2, w_in[:, :n_qkv].astype(BF16))
    qkv3 = qkv.reshape(bsz, seq, n_qkv)
    y_a = _moba(qkv3, tbl, rel_bias, 0)
    y_b = _stickbreak(qkv3, 3 * moba_w // sb_w)
    kv = _memkv(mem.reshape(-1, d), w_mem_kv.astype(BF16)).reshape(bsz, mem.shape[1], 2 * mem_w)
    y_m = _mem_attention(qkv3, kv, (3 * moba_w + 3 * sb_w) // mem_w)

    w_r = jnp.zeros((d, LANES), F32).at[:, :n_experts].set(w_re).at[:, n_experts:n_experts + N_GROUPS].set(w_rg)
    b_r = jnp.zeros((1, LANES), F32).at[0, :n_experts].set(b_re).at[0, n_experts:n_experts + N_GROUPS].set(b_rg)
    wr_split = jnp.concatenate(_split_bf16(w_r), axis=1)
    x1, route, counts = _merge(
        y_a.reshape(n, moba_w), y_b.reshape(n, sb_w), y_m.reshape(n, mem_w), x2, w_in[:, n_qkv:].astype(BF16),
        w_br_moba.astype(BF16), w_br_sb.astype(BF16), w_br_mem.astype(BF16), w_out.astype(BF16),
        ln1_g.reshape(1, d), ln1_b.reshape(1, d), wr_split, b_r, alpha, n_experts)

    rows, td = MOE_BLOCK, DISPATCH_ROWS
    n_tiles = n // td
    expert = route[:, 0:2].astype(I32)
    tile_cnt = jnp.sum((expert.reshape(n_tiles, 2 * td, 1) == jnp.arange(n_experts, dtype=I32)).astype(I32), axis=1)
    n_chunks = (tile_cnt + CHUNK - 1) // CHUNK
    run_rows = n_chunks * CHUNK
    run_rank = jnp.cumsum(tile_cnt, axis=0) - tile_cnt
    run_off = jnp.cumsum(run_rows, axis=0) - run_rows
    sorted_off = jnp.cumsum(run_rows, axis=1) - run_rows
    expert_rows = jnp.sum(run_rows, axis=0)
    padded = (expert_rows + rows - 1) // rows * rows
    pend = jnp.cumsum(padded)
    pstart = pend - padded
    run_dst = pstart[None, :] + run_off
    n_blocks = (2 * n + n_tiles * n_experts * (CHUNK - 1)) // rows + n_experts
    block_start = jnp.arange(n_blocks, dtype=I32) * rows
    block_expert = jnp.minimum(jnp.sum((pend[None, :] <= block_start[:, None]).astype(I32), axis=1),
                               n_experts - 1)
    n_used = (pend[-1] // rows).astype(I32).reshape(1)
    base = jnp.zeros((n_tiles, n_experts, LANES), F32)
    base = base.at[:, :, 0].set((sorted_off - run_rank).astype(F32)).at[:, :, 1].set((run_dst - run_rank).astype(F32))
    x_pad, dest8 = _dispatch(x1, route, n_chunks.reshape(-1), sorted_off.reshape(-1), run_dst.reshape(-1),
                             jnp.maximum(pend - rows, 0), (padded > 0).astype(I32), n_used, base,
                             n_blocks * rows)
    dest = dest8[:, 0:2, :].transpose(0, 2, 1).reshape(n, 2)

    y_pad = _experts(x_pad, block_expert, n_used, w_gate, w_up, w_down)
    out = _combine(y_pad, dest, x1, route, ln2_g.reshape(1, d), ln2_b.reshape(1, d), alpha)
    return out.reshape(bsz, seq, d)


def kernel(x, mem, w_in, w_mem_kv, rel_bias, w_br_moba, w_br_sb, w_br_mem, w_out, ln1_g, ln1_b,
           w_router_group, b_router_group, w_router_expert, b_router_expert,
           w_gate, w_up, w_down, ln2_g, ln2_b):
    depth = w_in.shape[0]
    alpha = (2.0 * depth) ** 0.25
    tbl = _bias_table(rel_bias)
    for l in range(depth):
        x = _layer(x, mem, tbl, rel_bias, w_in[l], w_mem_kv[l], w_br_moba[l], w_br_sb[l], w_br_mem[l],
                   w_out[l], ln1_g[l], ln1_b[l], w_router_group[l], b_router_group[l],
                   w_router_expert[l], b_router_expert[l], w_gate[l], w_up[l], w_down[l],
                   ln2_g[l], ln2_b[l], alpha)
    return x
```

```python
import functools
import math

import jax
import jax.numpy as jnp
from jax import lax
from jax.experimental import pallas as pl
from jax.experimental.pallas import tpu as pltpu

F32, BF16, I32 = jnp.float32, jnp.bfloat16, jnp.int32

HEAD_DIM = 64
N_MOBA_HEADS = 6
N_SB_HEADS = 6
N_MEM_HEADS = 4
MOBA_BLOCK = 256
MOBA_TOPK = 3
N_BUCKETS = 32
MAX_DISTANCE = 128
N_GROUPS = 4
EXPERTS_PER_GROUP = 8
LN_EPS = 1e-5
NEG = -1e30

LANES = 128
VMEM_BYTES = 64 * 1024 * 1024
PAIR = LANES // HEAD_DIM

SB_ZERO_LOG = 110.0

PROJ_ROWS = 1024
MERGE_ROWS = 512
ATTN_ROWS = 256
MEM_ROWS = 1024
COMBINE_ROWS = 256
MOE_BLOCK = 512


def _params(semantics, vmem_mb):
    return pltpu.CompilerParams(dimension_semantics=semantics,
                                vmem_limit_bytes=min(vmem_mb * 1024 * 1024, VMEM_BYTES))


def _dot(a, b):
    return jnp.dot(a, b, preferred_element_type=F32)


def _layer_norm(h, g, b):
    mu = jnp.mean(h, axis=-1, keepdims=True)
    d = h - mu
    var = jnp.mean(d * d, axis=-1, keepdims=True)
    return d * lax.rsqrt(var + LN_EPS) * g + b


def _sigmoid(x):
    return 1.0 / (1.0 + jnp.exp(-x))


def _head_rows(qt, row, hh, scale):
    keep = (row >= hh * HEAD_DIM) & (row < (hh + 1) * HEAD_DIM)
    return jnp.where(keep, qt * scale, 0.0).astype(BF16)


def _transposed_bf16(x):
    return x.astype(F32).T.astype(BF16)


def _proj_kernel(x_ref, wq_ref, qkv_ref):
    qkv_ref[...] = _dot(x_ref[...].astype(BF16), wq_ref[...]).astype(BF16)


def _proj(x2, w_qkv):
    n, d = x2.shape
    n_qkv = w_qkv.shape[1]
    tm = PROJ_ROWS
    return pl.pallas_call(
        _proj_kernel,
        grid=(n // tm,),
        in_specs=[pl.BlockSpec((tm, d), lambda i: (i, 0)),
                  pl.BlockSpec((d, n_qkv), lambda i: (0, 0))],
        out_specs=pl.BlockSpec((tm, n_qkv), lambda i: (i, 0)),
        out_shape=jax.ShapeDtypeStruct((n, n_qkv), BF16),
        compiler_params=_params(("arbitrary",), 40),
        name="proj",
    )(x2, w_qkv)


def _t5_bucket(rel):
    rel = jnp.maximum(rel, 0)
    max_exact = N_BUCKETS // 2
    rel_f = jnp.maximum(rel, 1).astype(F32)
    large = max_exact + (jnp.log(rel_f / max_exact) / math.log(MAX_DISTANCE / max_exact)
                         * (N_BUCKETS - max_exact)).astype(I32)
    large = jnp.minimum(large, N_BUCKETS - 1)
    return jnp.where(rel < max_exact, rel, large)


def _bias_table_kernel(rb_ref, o_ref):
    h = pl.program_id(0)
    blk = o_ref.shape[2]
    j = lax.broadcasted_iota(I32, (2 * blk, blk), 0)
    i = lax.broadcasted_iota(I32, (2 * blk, blk), 1)
    bucket = _t5_bucket(blk + i - j)
    acc = jnp.zeros((2 * blk, blk), F32)
    for b in range(N_BUCKETS):
        acc = jnp.where(bucket == b, rb_ref[b, h], acc)
    o_ref[0] = acc


def _bias_table(rel_bias):
    n_heads = rel_bias.shape[1]
    blk = MOBA_BLOCK
    return pl.pallas_call(
        _bias_table_kernel,
        grid=(n_heads,),
        in_specs=[pl.BlockSpec(memory_space=pltpu.SMEM)],
        out_specs=pl.BlockSpec((1, 2 * blk, blk), lambda h: (h, 0, 0)),
        out_shape=jax.ShapeDtypeStruct((n_heads, 2 * blk, blk), F32),
        compiler_params=_params(("arbitrary",), 32),
        name="bias_table",
    )(rel_bias)


def _fold_keys(x, op, final):
    while x.shape[0] > 8 and x.shape[0] % 2 == 0:
        half = x.shape[0] // 2
        x = op(x[:half], x[half:])
    return final(x, axis=0, keepdims=True)


def _head_queries(qt, row, hh, scale):
    p, sub = divmod(hh, PAIR)
    return _head_rows(qt[p * LANES:(p + 1) * LANES, :], row, sub, scale)


def _pair_lanes(hh):
    p = hh // PAIR
    return slice(p * LANES, (p + 1) * LANES)


def _head_dims(hh):
    return slice(hh * HEAD_DIM, (hh + 1) * HEAD_DIM)


def _moba_kernel(rb_ref, q_ref, k_ref, v_ref, tbl_ref, o_ref, kmean_sc, vt_sc, sel_sc, s_sc, p_sc):
    own = pl.program_id(1)
    blk, width = q_ref.shape[1], q_ref.shape[2]
    n_heads = width // HEAD_DIM
    nb = k_ref.shape[1] // blk
    scale = HEAD_DIM ** -0.5

    @pl.when(own == 0)
    def _():
        for n in range(nb):
            kb = k_ref[0, n * blk:(n + 1) * blk, :].astype(F32)
            kmean_sc[n:n + 1, :] = jnp.mean(kb, axis=0, keepdims=True)
            vt_sc[n] = _transposed_bf16(v_ref[0, n * blk:(n + 1) * blk, :])

    qt = q_ref[0].astype(F32).T
    row = lax.broadcasted_iota(I32, (LANES, blk), 0)
    blk_id = lax.broadcasted_iota(I32, (nb, blk), 0)
    valid = blk_id < own
    causal = (lax.broadcasted_iota(I32, (blk, blk), 0) <= lax.broadcasted_iota(I32, (blk, blk), 1))
    kmean = kmean_sc[...].astype(BF16)

    qs = [_head_queries(qt, row, hh, scale) for hh in range(n_heads)]
    gates = [_dot(kmean[:, _pair_lanes(hh)], _head_queries(qt, row, hh, 1.0)) for hh in range(n_heads)]

    def scores(n):
        offn = pl.multiple_of(n * blk, blk)
        return [_dot(k_ref[0, pl.ds(offn, blk), _pair_lanes(hh)], qs[hh]) for hh in range(n_heads)]

    own_scores = scores(own)
    prev_scores = scores(jnp.maximum(own - 1, 0))

    blk_f = blk_id.astype(F32)
    for hh in range(n_heads):
        gate = jnp.where(valid, gates[hh], NEG)
        picked = jnp.zeros((nb, blk), jnp.bool_)
        for _ in range(MOBA_TOPK):
            top = _fold_keys(gate, jnp.maximum, jnp.max)
            first = _fold_keys(jnp.where(gate == top, blk_f, float(nb)), jnp.minimum, jnp.min)
            hit = blk_f == first
            picked = picked | hit
            gate = jnp.where(hit, -jnp.inf, gate)
        sel_sc[hh] = jnp.where(valid & picked, 0.0, NEG)

    prev = jnp.maximum(own - 1, 0)
    no_prev = jnp.where(own > 0, 0.0, NEG)
    stats = []
    for hh, (s_own, s_prev) in enumerate(zip(own_scores, prev_scores)):
        s_own = jnp.where(causal, s_own + tbl_ref[hh, blk:, :], NEG)
        s_prev = s_prev + tbl_ref[hh, :blk, :]
        term = sel_sc[hh, pl.ds(prev, 1), :] + no_prev
        m0 = jnp.maximum(_fold_keys(s_own, jnp.maximum, jnp.max),
                         _fold_keys(s_prev, jnp.maximum, jnp.max) + term)
        p_own = jnp.exp(s_own - m0)
        p_prev = jnp.exp(s_prev - (m0 - term))
        l0 = _fold_keys(p_own, jnp.add, jnp.sum) + _fold_keys(p_prev, jnp.add, jnp.sum)
        stats.append((m0, l0, p_own.astype(BF16), p_prev.astype(BF16)))
    state = []
    for hh, (m0, l0, p_own, p_prev) in enumerate(stats):
        state += [m0, l0, _dot(vt_sc[own, _head_dims(hh), :], p_own) + _dot(vt_sc[prev, _head_dims(hh), :], p_prev)]
    state = tuple(state)

    def softmax_step(s, per_query, m_run, l_run):
        m_new = jnp.maximum(m_run, _fold_keys(s, jnp.maximum, jnp.max) + per_query)
        a = jnp.exp(m_run - m_new)
        pn = jnp.exp(s - (m_new - per_query))
        return m_new, a * l_run + _fold_keys(pn, jnp.add, jnp.sum), a, pn.astype(BF16)

    n_far = own - 1
    last_far = jnp.maximum(n_far - 1, 0)
    for hh, s in enumerate(scores(0)):
        s_sc[hh] = s
    p_sc[...] = jnp.zeros(p_sc.shape, BF16)

    def far_block(n, carry):
        state, a_prev = carry[:3 * n_heads], carry[3 * n_heads:]
        n_prev = jnp.maximum(n - 1, 0)
        pv = [_dot(vt_sc[n_prev, _head_dims(hh), :], p_sc[hh]) for hh in range(n_heads)]
        s_next = scores(jnp.minimum(n + 1, last_far))
        stats = [softmax_step(s_sc[hh], rb_ref[N_BUCKETS - 1, hh] + sel_sc[hh, pl.ds(n, 1), :],
                              state[3 * hh], state[3 * hh + 1]) for hh in range(n_heads)]
        new, a_new = [], []
        for hh, (m_new, l_new, a, pn) in enumerate(stats):
            p_sc[hh] = pn
            s_sc[hh] = s_next[hh]
            new += [m_new, l_new, a_prev[hh] * state[3 * hh + 2] + pv[hh]]
            a_new.append(a)
        return tuple(new) + tuple(a_new)

    ones = jnp.ones((1, blk), F32)
    carry = lax.fori_loop(0, n_far, far_block, tuple(state) + (ones,) * n_heads)
    state, a_prev = carry[:3 * n_heads], carry[3 * n_heads:]
    out_t = jnp.concatenate(
        [(a_prev[hh] * state[3 * hh + 2] + _dot(vt_sc[last_far, _head_dims(hh), :], p_sc[hh])) / state[3 * hh + 1]
         for hh in range(n_heads)], axis=0)
    o_ref[0] = out_t.T.astype(o_ref.dtype)


def _moba(qkv3, tbl, rel_bias, col0):
    bsz, seq, _ = qkv3.shape
    assert MAX_DISTANCE <= MOBA_BLOCK and seq % MOBA_BLOCK == 0
    width = N_MOBA_HEADS * HEAD_DIM
    blk = MOBA_BLOCK
    nb = seq // blk
    return pl.pallas_call(
        _moba_kernel,
        grid=(bsz, nb),
        in_specs=[pl.BlockSpec(memory_space=pltpu.SMEM),
                  pl.BlockSpec((1, blk, width), lambda b, i: (b, i, col0)),
                  pl.BlockSpec((1, seq, width), lambda b, i: (b, 0, col0 + 1)),
                  pl.BlockSpec((1, seq, width), lambda b, i: (b, 0, col0 + 2)),
                  pl.BlockSpec((N_MOBA_HEADS, 2 * blk, blk), lambda b, i: (0, 0, 0))],
        out_specs=pl.BlockSpec((1, blk, width), lambda b, i: (b, i, 0)),
        out_shape=jax.ShapeDtypeStruct((bsz, seq, width), BF16),
        scratch_shapes=[pltpu.VMEM((nb, width), F32),
                        pltpu.VMEM((nb, width, blk), BF16),
                        pltpu.VMEM((N_MOBA_HEADS, nb, blk), F32),
                        pltpu.VMEM((N_MOBA_HEADS, blk, blk), F32),
                        pltpu.VMEM((N_MOBA_HEADS, blk, blk), BF16)],
        compiler_params=_params(("arbitrary", "arbitrary"), 48),
        name="moba",
    )(rel_bias, qkv3, qkv3, qkv3, tbl)


def _sb_kernel(q_ref, k_ref, v_ref, o_ref, vt_sc):
    qi = pl.program_id(1)
    t, width = q_ref.shape[1], q_ref.shape[2]
    n_heads = width // HEAD_DIM
    nb = k_ref.shape[1] // t
    scale = HEAD_DIM ** -0.5

    @pl.when(qi == 0)
    def _():
        for n in range(nb):
            vt_sc[n] = _transposed_bf16(v_ref[0, n * t:(n + 1) * t, :])

    qt = q_ref[0].astype(F32).T
    row = lax.broadcasted_iota(I32, (LANES, t), 0)
    key = lax.broadcasted_iota(I32, (t, t), 0)
    qry = lax.broadcasted_iota(I32, (t, t), 1)
    strict = key < qry
    tri = jnp.where(qry >= key, 1.0, 0.0).astype(BF16)
    qs = [_head_queries(qt, row, hh, scale) for hh in range(n_heads)]

    def blocks(tiles):
        zs = [[_dot(k_ref[0, pl.ds(pl.multiple_of(j * t, t), t), _pair_lanes(hh)], qs[hh])
               for hh in range(n_heads)] for j, _ in tiles]
        csums = []
        for (_, diagonal), zt in zip(tiles, zs):
            row_sums = []
            for z in zt:
                sp = jnp.maximum(z, 0.0) + jnp.log(1.0 + jnp.exp(-jnp.abs(z)))
                if diagonal:
                    sp = jnp.where(strict, sp, 0.0)
                hi = sp.astype(BF16)
                lo = (sp - hi.astype(F32)).astype(BF16)
                row_sums.append(_dot(tri, hi) + _dot(tri, lo))
            csums.append(row_sums)
        out = []
        for (j, diagonal), zt, ct in zip(tiles, zs, csums):
            pvs = []
            for hh in range(n_heads):
                a = jnp.exp(zt[hh] - ct[hh])
                if diagonal:
                    a = jnp.where(strict, a, 0.0)
                pvs.append(_dot(vt_sc[j, _head_dims(hh), :], a.astype(BF16)))
            out.append((pvs, [c[0:1, :] for c in ct]))
        return out

    has_prev = qi > 0
    (pv_d, tot_d), (pv_p, tot_p) = blocks([(qi, True), (jnp.maximum(qi - 1, 0), False)])
    accs = tuple(d + jnp.where(has_prev, p * jnp.exp(-c), 0.0) for d, p, c in zip(pv_d, pv_p, tot_d))
    carries = tuple(c + jnp.where(has_prev, p, 0.0) for c, p in zip(tot_d, tot_p))

    def lowest(carries):
        return jnp.min(functools.reduce(jnp.minimum, carries))

    def cond(state):
        j, cmin, _, _ = state
        return (j >= 0) & (cmin < SB_ZERO_LOG)

    def body(state):
        j, _, carries, accs = state
        (pvs, totals), = blocks([(j, False)])
        accs = tuple(acc + pv * jnp.exp(-c) for acc, pv, c in zip(accs, pvs, carries))
        carries = tuple(c + total for c, total in zip(carries, totals))
        return j - 1, lowest(carries), carries, accs

    state = lax.while_loop(cond, body, (qi - 2, lowest(carries), carries, accs))
    out_t = jnp.concatenate(list(state[3]), axis=0)
    o_ref[0] = out_t.T.astype(o_ref.dtype)


def _stickbreak(qkv3, col0):
    bsz, seq, _ = qkv3.shape
    width = N_SB_HEADS * HEAD_DIM
    t = ATTN_ROWS
    return pl.pallas_call(
        _sb_kernel,
        grid=(bsz, seq // t),
        in_specs=[pl.BlockSpec((1, t, width), lambda b, i: (b, i, col0)),
                  pl.BlockSpec((1, seq, width), lambda b, i: (b, 0, col0 + 1)),
                  pl.BlockSpec((1, seq, width), lambda b, i: (b, 0, col0 + 2))],
        out_specs=pl.BlockSpec((1, t, width), lambda b, i: (b, i, 0)),
        out_shape=jax.ShapeDtypeStruct((bsz, seq, width), BF16),
        scratch_shapes=[pltpu.VMEM((seq // t, width, t), BF16)],
        compiler_params=_params(("arbitrary", "arbitrary"), 48),
        name="stickbreak",
    )(qkv3, qkv3, qkv3)


def _memkv_kernel(m_ref, w_ref, o_ref):
    o_ref[...] = _dot(m_ref[...].astype(BF16), w_ref[...]).astype(BF16)


def _memkv(mem2, w_bf16):
    n, d = mem2.shape
    width = w_bf16.shape[1]
    tm = min(n, 512)
    return pl.pallas_call(
        _memkv_kernel,
        grid=(n // tm,),
        in_specs=[pl.BlockSpec((tm, d), lambda i: (i, 0)),
                  pl.BlockSpec((d, width), lambda i: (0, 0))],
        out_specs=pl.BlockSpec((tm, width), lambda i: (i, 0)),
        out_shape=jax.ShapeDtypeStruct((n, width), BF16),
        compiler_params=_params(("arbitrary",), 32),
        name="memkv",
    )(mem2, w_bf16)


def _mem_kernel(q_ref, k_ref, v_ref, o_ref, vt_sc):
    t, width = q_ref.shape[1], q_ref.shape[2]
    n_heads = width // HEAD_DIM
    scale = HEAD_DIM ** -0.5

    @pl.when(pl.program_id(1) == 0)
    def _():
        vt_sc[...] = _transposed_bf16(v_ref[0])

    qt = q_ref[0].astype(F32).T
    row = lax.broadcasted_iota(I32, (LANES, t), 0)
    qs = [_head_queries(qt, row, hh, scale) for hh in range(n_heads)]
    scores = [_dot(k_ref[0, :, _pair_lanes(hh)], qs[hh]) for hh in range(n_heads)]
    probs = []
    for s in scores:
        e = jnp.exp(s - _fold_keys(s, jnp.maximum, jnp.max))
        probs.append((e * (1.0 / _fold_keys(e, jnp.add, jnp.sum))).astype(BF16))
    out_t = jnp.concatenate([_dot(vt_sc[_head_dims(hh), :], probs[hh]) for hh in range(n_heads)], axis=0)
    o_ref[0] = out_t.T.astype(o_ref.dtype)


def _mem_attention(qkv3, kv3, qcol):
    bsz, seq, _ = qkv3.shape
    n_mem = kv3.shape[1]
    width = N_MEM_HEADS * HEAD_DIM
    t = MEM_ROWS
    return pl.pallas_call(
        _mem_kernel,
        grid=(bsz, seq // t),
        in_specs=[pl.BlockSpec((1, t, width), lambda b, i: (b, i, qcol)),
                  pl.BlockSpec((1, n_mem, width), lambda b, i: (b, 0, 0)),
                  pl.BlockSpec((1, n_mem, width), lambda b, i: (b, 0, 1))],
        out_specs=pl.BlockSpec((1, t, width), lambda b, i: (b, i, 0)),
        out_shape=jax.ShapeDtypeStruct((bsz, seq, width), BF16),
        scratch_shapes=[pltpu.VMEM((width, n_mem), BF16)],
        compiler_params=_params(("arbitrary", "arbitrary"), 32),
        name="mem_attention",
    )(qkv3, kv3, kv3)


def _merge_kernel(ya_ref, yb_ref, ym_ref, x_ref, wg_ref, wa_ref, wb_ref, wm_ref, wo_ref,
                  lng_ref, lnb_ref, wrh_ref, br_ref, x1_ref, route_ref, tcnt_ref, h_sc, cnt_ref,
                  *, alpha, n_experts):
    step = pl.program_id(0)
    tm, d = x_ref.shape
    slot = lax.rem(step, 2)

    @pl.when(step == 0)
    def _():
        cnt_ref[...] = jnp.zeros(cnt_ref.shape, F32)
        h_sc[...] = jnp.zeros(h_sc.shape, F32)

    x = x_ref[...]
    xb = x.astype(BF16)
    branches = ((ya_ref, wa_ref), (yb_ref, wb_ref), (ym_ref, wm_ref))

    def branch_term(k):
        gate_logits = _dot(xb, wg_ref[:, k * d:(k + 1) * d])
        return gate_logits, _dot(branches[k][0][...], branches[k][1][...])

    x1 = _layer_norm(h_sc[1 - slot], lng_ref[...], lnb_ref[...])
    x1_ref[...] = x1
    xh = x1.astype(BF16)
    xl = (x1 - xh.astype(F32)).astype(BF16)

    first_term = branch_term(0)

    by_hi = _dot(xh, wrh_ref[...])
    logits = (by_hi[:, :LANES] + (by_hi[:, LANES:] + _dot(xl, wrh_ref[:, :LANES]))
              + br_ref[...])

    merged = None
    for k in range(len(branches)):
        gate_logits, y = first_term if k == 0 else branch_term(k)
        term = _sigmoid(gate_logits) * y
        merged = term if merged is None else merged + term
    h_sc[slot] = alpha * x + _dot(merged.astype(BF16), wo_ref[...])

    col = lax.broadcasted_iota(I32, (tm, LANES), 1)
    colf = col.astype(F32)
    big = float(LANES)
    gmask = (col >= n_experts) & (col < n_experts + N_GROUPS)
    lg = jnp.where(gmask, logits, -jnp.inf)
    gmax = jnp.max(lg, axis=1, keepdims=True)
    gidx = jnp.min(jnp.where(lg == gmax, colf, big), axis=1, keepdims=True) - n_experts
    g_p = 1.0 / jnp.sum(jnp.where(gmask, jnp.exp(logits - gmax), 0.0), axis=1, keepdims=True)
    lo_col = gidx * EXPERTS_PER_GROUP
    emask = (colf >= lo_col) & (colf < lo_col + EXPERTS_PER_GROUP)
    le = jnp.where(emask, logits, -jnp.inf)
    l1 = jnp.max(le, axis=1, keepdims=True)
    i1 = jnp.min(jnp.where(le == l1, colf, big), axis=1, keepdims=True)
    le2 = jnp.where(colf == i1, -jnp.inf, le)
    l2 = jnp.max(le2, axis=1, keepdims=True)
    i2 = jnp.min(jnp.where(le2 == l2, colf, big), axis=1, keepdims=True)
    e2 = jnp.exp(l2 - l1)
    gate1 = g_p * (1.0 / (1.0 + e2))
    gate2 = g_p * (e2 / (1.0 + e2))

    oh1 = colf == i1
    oh2 = colf == i2
    cnt = jnp.where((oh1 | oh2) & (step > 0), 1.0, 0.0)
    rr = lax.broadcasted_iota(I32, (tm, tm), 0)
    cc = lax.broadcasted_iota(I32, (tm, tm), 1)
    before = jnp.where(cc < rr, 1.0, 0.0).astype(BF16)
    base = _dot(before, cnt.astype(BF16)) + cnt_ref[0:1, :]
    rank1 = jnp.sum(jnp.where(oh1, base, 0.0), axis=1, keepdims=True)
    rank2 = jnp.sum(jnp.where(oh2, base, 0.0), axis=1, keepdims=True)
    tile_cnt = jnp.sum(cnt, axis=0, keepdims=True)
    cnt_ref[...] = cnt_ref[...] + tile_cnt
    tcnt_ref[0] = jnp.broadcast_to(tile_cnt, tcnt_ref.shape[1:])

    route = jnp.zeros((tm, LANES), F32)
    for k, val in enumerate((i1, i2, rank1, rank2, gate1, gate2)):
        route = jnp.where(col == k, val, route)
    route_ref[...] = route


def _merge(ya, yb, ym, x2, w_gates, wa, wb, wm, wo, ln_g, ln_b, wr_split, b_r, alpha, n_experts):
    n, d = x2.shape
    tm = MERGE_ROWS
    n_tiles = n // tm
    row_in = lambda w: pl.BlockSpec((tm, w), lambda i: (jnp.minimum(i, n_tiles - 1), 0))
    row_out = lambda w: pl.BlockSpec((tm, w), lambda i: (jnp.maximum(i - 1, 0), 0))
    full = lambda a: pl.BlockSpec(a.shape, lambda i: (0,) * a.ndim)
    return pl.pallas_call(
        functools.partial(_merge_kernel, alpha=alpha, n_experts=n_experts),
        grid=(n_tiles + 1,),
        in_specs=[row_in(ya.shape[1]), row_in(yb.shape[1]), row_in(ym.shape[1]), row_in(d),
                  full(w_gates), full(wa), full(wb), full(wm), full(wo), full(ln_g), full(ln_b),
                  full(wr_split), full(b_r)],
        out_specs=[row_out(d), row_out(LANES),
                   pl.BlockSpec((1, 8, LANES), lambda i: (jnp.maximum(i - 1, 0), 0, 0))],
        out_shape=[jax.ShapeDtypeStruct((n, d), F32),
                   jax.ShapeDtypeStruct((n, LANES), F32),
                   jax.ShapeDtypeStruct((n_tiles, 8, LANES), F32)],
        scratch_shapes=[pltpu.VMEM((2, tm, d), F32), pltpu.VMEM((8, LANES), F32)],
        compiler_params=_params(("arbitrary",), 56),
        name="merge",
    )(ya, yb, ym, x2, w_gates, wa, wb, wm, wo, ln_g, ln_b, wr_split, b_r)


def _row_copy(src_hbm, row, buf, sem, r):
    return pltpu.make_async_copy(src_hbm.at[pl.ds(row, 1), :], buf.at[pl.ds(r, 1), :], sem)


def _gather_rows(src_hbm, idx_ref, buf, sem, n_rows, unrolled):
    if unrolled:
        for r in range(n_rows):
            _row_copy(src_hbm, idx_ref[0, 0, r], buf, sem, r).start()
    else:
        def issue(r, _):
            _row_copy(src_hbm, idx_ref[0, 0, r], buf, sem, r).start()
            return 0
        lax.fori_loop(0, n_rows, issue, 0)


def _wait_rows(src_hbm, buf, sem, n_rows):
    pltpu.make_async_copy(src_hbm.at[pl.ds(0, n_rows), :], buf, sem).wait()


DISPATCH_ROWS = 512
CHUNK = 8


def _chunk_copy(buf, xpad_hbm, sem, src_row, dst_row):
    return pltpu.make_async_copy(buf.at[pl.ds(src_row, CHUNK), :], xpad_hbm.at[pl.ds(dst_row, CHUNK), :], sem)


def _dispatch_kernel(nch_ref, off_ref, dst_ref, zrow_ref, zflag_ref, nu_ref, base_ref, route_ref, x_ref,
                     xpad_hbm, dest_ref, sorted_even, sorted_odd, zero_buf, sem, zsem):
    i = pl.program_id(0)
    n_steps = pl.num_programs(0)
    n_exp = base_ref.shape[1]
    td = x_ref.shape[0]
    rb = sorted_even.shape[0]
    zrows = zero_buf.shape[0]
    bufs = (sorted_even, sorted_odd)
    parity = lax.rem(i, 2)

    @pl.when(i == 0)
    def _():
        zero_buf[...] = jnp.zeros(zero_buf.shape, F32)

        def zero_block(row, start):
            cp = pltpu.make_async_copy(zero_buf, xpad_hbm.at[pl.ds(pl.multiple_of(row, zrows), zrows), :], zsem)
            cp.start() if start else cp.wait()

        for start in (True, False):
            for e in range(n_exp):
                pl.when(zflag_ref[e] == 1)(functools.partial(zero_block, zrow_ref[e], start))

            def unused(blk, _):
                zero_block(blk * zrows, start)
                return 0
            lax.fori_loop(nu_ref[0], xpad_hbm.shape[0] // zrows, unused, 0)

    rt = route_ref[...].T
    eid = lax.broadcasted_iota(I32, (n_exp, td), 0).astype(F32)
    rowid = lax.broadcasted_iota(I32, (rb, td), 0).astype(F32)
    hit = None
    dests = []
    for k in range(2):
        mine = eid == rt[k:k + 1, :]
        rank = rt[2 + k:3 + k, :]
        pos = jnp.sum(jnp.where(mine, base_ref[0, :, 0:1], 0.0), axis=0, keepdims=True) + rank
        dests.append(jnp.sum(jnp.where(mine, base_ref[0, :, 1:2], 0.0), axis=0, keepdims=True) + rank)
        hit = (rowid == pos) if hit is None else hit | (rowid == pos)
    dest_ref[0] = jnp.concatenate(dests + [jnp.zeros((8 - len(dests), td), F32)], axis=0).astype(I32)
    perm = jnp.where(hit, 1.0, 0.0).astype(BF16)
    rows_sorted = _dot(perm, x_ref[...].astype(BF16))

    def runs(tile, buf, s, start):
        for e in range(n_exp):
            k = tile * n_exp + e
            src0, dst0 = off_ref[k], dst_ref[k]

            def one(j, _):
                cp = _chunk_copy(buf, xpad_hbm, s, pl.multiple_of(src0 + j * CHUNK, CHUNK),
                                 pl.multiple_of(dst0 + j * CHUNK, CHUNK))
                cp.start() if start else cp.wait()
                return 0
            lax.fori_loop(0, nch_ref[k], one, 0)

    def wait_runs(tile, buf, s):
        k_last = tile * n_exp + n_exp - 1
        total = pl.multiple_of(off_ref[k_last] + nch_ref[k_last] * CHUNK, CHUNK)
        pltpu.make_async_copy(buf.at[pl.ds(0, total), :], xpad_hbm.at[pl.ds(0, total), :], s).wait()

    def step(cur):
        bufs[cur][...] = rows_sorted

        @pl.when(i > 0)
        def _():
            wait_runs(i - 1, bufs[1 - cur], sem.at[1 - cur])

        runs(i, bufs[cur], sem.at[cur], True)

        @pl.when(i + 1 == n_steps)
        def _():
            wait_runs(i, bufs[cur], sem.at[cur])

    for cur in range(2):
        pl.when(parity == cur)(functools.partial(step, cur))


def _dispatch(x1, route, nch, off8, dst, zrow, zflag, n_used, base, n_rows_out):
    n, d = x1.shape
    td = DISPATCH_ROWS
    n_tiles = n // td
    n_exp = base.shape[1]
    rb = 2 * td + n_exp * CHUNK
    grid_spec = pltpu.PrefetchScalarGridSpec(
        num_scalar_prefetch=6,
        grid=(n_tiles,),
        in_specs=[pl.BlockSpec((1, n_exp, 2), lambda i, *_: (i, 0, 0)),
                  pl.BlockSpec((td, LANES), lambda i, *_: (i, 0)),
                  pl.BlockSpec((td, d), lambda i, *_: (i, 0))],
        out_specs=[pl.BlockSpec(memory_space=pl.ANY),
                   pl.BlockSpec((1, 8, td), lambda i, *_: (i, 0, 0))],
        scratch_shapes=[pltpu.VMEM((rb, d), F32), pltpu.VMEM((rb, d), F32),
                        pltpu.VMEM((MOE_BLOCK, d), F32),
                        pltpu.SemaphoreType.DMA((2,)), pltpu.SemaphoreType.DMA(())])
    return pl.pallas_call(
        _dispatch_kernel,
        grid_spec=grid_spec,
        out_shape=[jax.ShapeDtypeStruct((n_rows_out, d), F32),
                   jax.ShapeDtypeStruct((n_tiles, 8, td), I32)],
        compiler_params=_params(("arbitrary",), 48),
        name="dispatch",
    )(nch, off8, dst, zrow, zflag, n_used, base, route, x1)


def _experts_kernel(be_ref, nu_ref, x_ref, wg_ref, wu_ref, wd_ref, o_ref, wg_sc, wu_sc, wd_sc):
    b = pl.program_id(0)
    n_used = nu_ref[0]

    @pl.when(b < n_used)
    def _():
        changed = (b == 0) | (be_ref[b] != be_ref[jnp.maximum(b - 1, 0)])

        @pl.when(changed)
        def _():
            wg_sc[...] = wg_ref[0].astype(BF16)
            wu_sc[...] = wu_ref[0].astype(BF16)
            wd_sc[...] = wd_ref[0].astype(BF16)

        xb = x_ref[...].astype(BF16)
        g = _dot(xb, wg_sc[...])
        u = _dot(xb, wu_sc[...])
        h = (g * _sigmoid(g) * u).astype(BF16)
        o_ref[...] = _dot(h, wd_sc[...])

    @pl.when(b >= n_used)
    def _():
        o_ref[...] = jnp.zeros(o_ref.shape, F32)


def _experts(x_pad, block_expert, n_used, w_gate, w_up, w_down):
    d = x_pad.shape[1]
    n_blocks = block_expert.shape[0]
    rows = MOE_BLOCK
    d_exp = w_gate.shape[-1]
    grid_spec = pltpu.PrefetchScalarGridSpec(
        num_scalar_prefetch=2,
        grid=(n_blocks,),
        in_specs=[pl.BlockSpec((rows, d), lambda b, be, nu: (jnp.minimum(b, jnp.maximum(nu[0] - 1, 0)), 0)),
                  pl.BlockSpec((1, d, d_exp), lambda b, be, nu: (be[b], 0, 0)),
                  pl.BlockSpec((1, d, d_exp), lambda b, be, nu: (be[b], 0, 0)),
                  pl.BlockSpec((1, d_exp, d), lambda b, be, nu: (be[b], 0, 0))],
        out_specs=pl.BlockSpec((rows, d), lambda b, be, nu: (b, 0)),
        scratch_shapes=[pltpu.VMEM((d, d_exp), BF16),
                        pltpu.VMEM((d, d_exp), BF16),
                        pltpu.VMEM((d_exp, d), BF16)])
    return pl.pallas_call(
        _experts_kernel,
        grid_spec=grid_spec,
        out_shape=jax.ShapeDtypeStruct((n_blocks * rows, d), F32),
        compiler_params=_params(("arbitrary",), 44),
        name="experts",
    )(block_expert, n_used, x_pad, w_gate, w_up, w_down)


def _combine_kernel(idx_ref, idxn_ref, y_hbm, x1_ref, route_ref, lng_ref, lnb_ref, o_ref,
                    ybuf_even, ybuf_odd, sem, *, alpha):
    i = pl.program_id(0)
    n_steps = pl.num_programs(0)
    tm = x1_ref.shape[0]
    bufs = (ybuf_even, ybuf_odd)
    parity = lax.rem(i, 2)

    @pl.when(i == 0)
    def _():
        _gather_rows(y_hbm, idx_ref, bufs[0], sem.at[0], 2 * tm, False)

    def compute(cur):
        _gather_rows(y_hbm, idxn_ref, bufs[1 - cur], sem.at[1 - cur], 2 * tm, True)
        _wait_rows(y_hbm, bufs[cur], sem.at[cur], 2 * tm)
        ffn = (route_ref[:, 4:5] * bufs[cur][0:tm, :] + route_ref[:, 5:6] * bufs[cur][tm:2 * tm, :])
        o_ref[...] = _layer_norm(alpha * x1_ref[...] + ffn, lng_ref[...], lnb_ref[...])

        @pl.when(i + 1 == n_steps)
        def _():
            _wait_rows(y_hbm, bufs[1 - cur], sem.at[1 - cur], 2 * tm)

    for cur in range(2):
        pl.when(parity == cur)(functools.partial(compute, cur))


def _combine(y_pad, dest, x1, route, ln_g, ln_b, alpha):
    n, d = x1.shape
    tm = COMBINE_ROWS
    n_steps = n // tm
    idx3 = dest.reshape(n_steps, tm, 2).transpose(0, 2, 1).reshape(n_steps, 1, 2 * tm)
    smem_idx = lambda f: pl.BlockSpec((1, 1, 2 * tm), f, memory_space=pltpu.SMEM)
    return pl.pallas_call(
        functools.partial(_combine_kernel, alpha=alpha),
        grid=(n_steps,),
        in_specs=[smem_idx(lambda i: (i, 0, 0)),
                  smem_idx(lambda i: (jnp.minimum(i + 1, n_steps - 1), 0, 0)),
                  pl.BlockSpec(memory_space=pl.ANY),
                  pl.BlockSpec((tm, d), lambda i: (i, 0)),
                  pl.BlockSpec((tm, LANES), lambda i: (i, 0)),
                  pl.BlockSpec((1, d), lambda i: (0, 0)),
                  pl.BlockSpec((1, d), lambda i: (0, 0))],
        out_specs=pl.BlockSpec((tm, d), lambda i: (i, 0)),
        out_shape=jax.ShapeDtypeStruct((n, d), F32),
        scratch_shapes=[pltpu.VMEM((2 * tm, d), F32), pltpu.VMEM((2 * tm, d), F32),
                        pltpu.SemaphoreType.DMA((2,))],
        compiler_params=_params(("arbitrary",), 32),
        name="combine",
    )(idx3, idx3, y_pad, x1, route, ln_g, ln_b)


def _split_bf16(w):
    hi = w.astype(BF16)
    return hi, (w - hi.astype(F32)).astype(BF16)


def _layer(x, mem, tbl, rel_bias, w_in, w_mem_kv, w_br_moba, w_br_sb, w_br_mem, w_out, ln1_g, ln1_b,
           w_rg, b_rg, w_re, b_re, w_gate, w_up, w_down, ln2_g, ln2_b, alpha):
    bsz, seq, d = x.shape
    n = bsz * seq
    n_experts = w_re.shape[1]
    moba_w, sb_w, mem_w = N_MOBA_HEADS * HEAD_DIM, N_SB_HEADS * HEAD_DIM, N_MEM_HEADS * HEAD_DIM
    n_qkv = 3 * moba_w + 3 * sb_w + mem_w
    assert w_in.shape[1] == n_qkv + 3 * d and n_experts + N_GROUPS <= LANES and moba_w == sb_w

    x2 = x.reshape(n, d)
    qkv = _proj(x2, w_in[:, :n_qkv].astype(BF16))
    qkv3 = qkv.reshape(bsz, seq, n_qkv)
    y_a = _moba(qkv3, tbl, rel_bias, 0)
    y_b = _stickbreak(qkv3, 3 * moba_w // sb_w)
    kv = _memkv(mem.reshape(-1, d), w_mem_kv.astype(BF16)).reshape(bsz, mem.shape[1], 2 * mem_w)
    y_m = _mem_attention(qkv3, kv, (3 * moba_w + 3 * sb_w) // mem_w)

    w_r = jnp.zeros((d, LANES), F32).at[:, :n_experts].set(w_re).at[:, n_experts:n_experts + N_GROUPS].set(w_rg)
    b_r = jnp.zeros((1, LANES), F32).at[0, :n_experts].set(b_re).at[0, n_experts:n_experts + N_GROUPS].set(b_rg)
    wr_split = jnp.concatenate(_split_bf16(w_r), axis=1)
    x1, route, tile_counts = _merge(
        y_a.reshape(n, moba_w), y_b.reshape(n, sb_w), y_m.reshape(n, mem_w), x2, w_in[:, n_qkv:].astype(BF16),
        w_br_moba.astype(BF16), w_br_sb.astype(BF16), w_br_mem.astype(BF16), w_out.astype(BF16),
        ln1_g.reshape(1, d), ln1_b.reshape(1, d), wr_split, b_r, alpha, n_experts)

    rows, td = MOE_BLOCK, DISPATCH_ROWS
    assert td == MERGE_ROWS
    n_tiles = n // td
    tile_cnt = tile_counts[:, 0, :n_experts].astype(I32)
    n_chunks = (tile_cnt + CHUNK - 1) // CHUNK
    run_rows = n_chunks * CHUNK
    run_rank = jnp.cumsum(tile_cnt, axis=0) - tile_cnt
    run_off = jnp.cumsum(run_rows, axis=0) - run_rows
    sorted_off = jnp.cumsum(run_rows, axis=1) - run_rows
    expert_rows = jnp.sum(run_rows, axis=0)
    padded = (expert_rows + rows - 1) // rows * rows
    pend = jnp.cumsum(padded)
    pstart = pend - padded
    run_dst = pstart[None, :] + run_off
    n_blocks = (2 * n + n_tiles * n_experts * (CHUNK - 1)) // rows + n_experts
    block_start = jnp.arange(n_blocks, dtype=I32) * rows
    block_expert = jnp.minimum(jnp.sum((pend[None, :] <= block_start[:, None]).astype(I32), axis=1),
                               n_experts - 1)
    n_used = (pend[-1] // rows).astype(I32).reshape(1)
    base = jnp.stack([sorted_off - run_rank, run_dst - run_rank], axis=-1).astype(F32)
    x_pad, dest8 = _dispatch(x1, route, n_chunks.reshape(-1), sorted_off.reshape(-1), run_dst.reshape(-1),
                             jnp.maximum(pend - rows, 0), (padded > 0).astype(I32), n_used, base,
                             n_blocks * rows)
    dest = dest8[:, 0:2, :].transpose(0, 2, 1).reshape(n, 2)

    y_pad = _experts(x_pad, block_expert, n_used, w_gate, w_up, w_down)
    out = _combine(y_pad, dest, x1, route, ln2_g.reshape(1, d), ln2_b.reshape(1, d), alpha)
    return out.reshape(bsz, seq, d)


def kernel(x, mem, w_in, w_mem_kv, rel_bias, w_br_moba, w_br_sb, w_br_mem, w_out, ln1_g, ln1_b,
           w_router_group, b_router_group, w_router_expert, b_router_expert,
           w_gate, w_up, w_down, ln2_g, ln2_b):
    depth = w_in.shape[0]
    alpha = (2.0 * depth) ** 0.25
    tbl = _bias_table(rel_bias)
    for l in range(depth):
        x = _layer(x, mem, tbl, rel_bias, w_in[l], w_mem_kv[l], w_br_moba[l], w_br_sb[l], w_br_mem[l],
                   w_out[l], ln1_g[l], ln1_b[l], w_router_group[l], b_router_group[l],
                   w_router_expert[l], b_router_expert[l], w_gate[l], w_up[l], w_down[l],
                   ln2_g[l], ln2_b[l], alpha)
    return x
```

```python
import functools
import math

import jax
import jax.numpy as jnp
from jax import lax
from jax.experimental import pallas as pl
from jax.experimental.pallas import tpu as pltpu

F32, BF16, I32 = jnp.float32, jnp.bfloat16, jnp.int32

HEAD_DIM = 64
N_MOBA_HEADS = 6
N_SB_HEADS = 6
N_MEM_HEADS = 4
MOBA_BLOCK = 256
MOBA_TOPK = 3
N_BUCKETS = 32
MAX_DISTANCE = 128
N_GROUPS = 4
EXPERTS_PER_GROUP = 8
LN_EPS = 1e-5
NEG = -1e30

LANES = 128
VMEM_BYTES = 64 * 1024 * 1024
PAIR = LANES // HEAD_DIM

SB_ZERO_LOG = 110.0

PROJ_ROWS = 1024
MERGE_ROWS = 512
ATTN_ROWS = 256
MEM_ROWS = 1024
COMBINE_ROWS = 512
MOE_BLOCK = 512


def _params(semantics, vmem_mb):
    return pltpu.CompilerParams(dimension_semantics=semantics,
                                vmem_limit_bytes=min(vmem_mb * 1024 * 1024, VMEM_BYTES))


def _dot(a, b):
    return jnp.dot(a, b, preferred_element_type=F32)


def _layer_norm(h, g, b):
    mu = jnp.mean(h, axis=-1, keepdims=True)
    d = h - mu
    var = jnp.mean(d * d, axis=-1, keepdims=True)
    return d * lax.rsqrt(var + LN_EPS) * g + b


def _sigmoid(x):
    return 1.0 / (1.0 + jnp.exp(-x))


def _head_rows(qt, row, hh, scale):
    keep = (row >= hh * HEAD_DIM) & (row < (hh + 1) * HEAD_DIM)
    return jnp.where(keep, qt * scale, 0.0).astype(BF16)


def _transposed_bf16(x):
    return x.astype(F32).T.astype(BF16)


def _proj_kernel(x_ref, wq_ref, qkv_ref):
    qkv_ref[...] = _dot(x_ref[...].astype(BF16), wq_ref[...]).astype(BF16)


def _proj(x2, w_qkv):
    n, d = x2.shape
    n_qkv = w_qkv.shape[1]
    tm = PROJ_ROWS
    return pl.pallas_call(
        _proj_kernel,
        grid=(n // tm,),
        in_specs=[pl.BlockSpec((tm, d), lambda i: (i, 0)),
                  pl.BlockSpec((d, n_qkv), lambda i: (0, 0))],
        out_specs=pl.BlockSpec((tm, n_qkv), lambda i: (i, 0)),
        out_shape=jax.ShapeDtypeStruct((n, n_qkv), BF16),
        compiler_params=_params(("arbitrary",), 40),
        name="proj",
    )(x2, w_qkv)


def _t5_bucket(rel):
    rel = jnp.maximum(rel, 0)
    max_exact = N_BUCKETS // 2
    rel_f = jnp.maximum(rel, 1).astype(F32)
    large = max_exact + (jnp.log(rel_f / max_exact) / math.log(MAX_DISTANCE / max_exact)
                         * (N_BUCKETS - max_exact)).astype(I32)
    large = jnp.minimum(large, N_BUCKETS - 1)
    return jnp.where(rel < max_exact, rel, large)


def _bias_table_kernel(rb_ref, o_ref):
    h = pl.program_id(0)
    blk = o_ref.shape[2]
    j = lax.broadcasted_iota(I32, (2 * blk, blk), 0)
    i = lax.broadcasted_iota(I32, (2 * blk, blk), 1)
    bucket = _t5_bucket(blk + i - j)
    acc = jnp.zeros((2 * blk, blk), F32)
    for b in range(N_BUCKETS):
        acc = jnp.where(bucket == b, rb_ref[b, h], acc)
    o_ref[0] = acc


def _bias_table(rel_bias):
    n_heads = rel_bias.shape[1]
    blk = MOBA_BLOCK
    return pl.pallas_call(
        _bias_table_kernel,
        grid=(n_heads,),
        in_specs=[pl.BlockSpec(memory_space=pltpu.SMEM)],
        out_specs=pl.BlockSpec((1, 2 * blk, blk), lambda h: (h, 0, 0)),
        out_shape=jax.ShapeDtypeStruct((n_heads, 2 * blk, blk), F32),
        compiler_params=_params(("arbitrary",), 32),
        name="bias_table",
    )(rel_bias)


def _fold_keys(x, op, final):
    while x.shape[0] > 8 and x.shape[0] % 2 == 0:
        half = x.shape[0] // 2
        x = op(x[:half], x[half:])
    return final(x, axis=0, keepdims=True)


def _head_queries(qt, row, hh, scale):
    p, sub = divmod(hh, PAIR)
    return _head_rows(qt[p * LANES:(p + 1) * LANES, :], row, sub, scale)


def _pair_lanes(hh):
    p = hh // PAIR
    return slice(p * LANES, (p + 1) * LANES)


def _head_dims(hh):
    return slice(hh * HEAD_DIM, (hh + 1) * HEAD_DIM)


def _moba_kernel(rb_ref, q_ref, k_ref, v_ref, tbl_ref, o_ref, kmean_sc, vt_sc, sel_sc, s_sc, p_sc):
    own = pl.program_id(1)
    blk, width = q_ref.shape[1], q_ref.shape[2]
    n_heads = width // HEAD_DIM
    nb = k_ref.shape[1] // blk
    scale = HEAD_DIM ** -0.5

    @pl.when(own == 0)
    def _():
        for n in range(nb):
            kb = k_ref[0, n * blk:(n + 1) * blk, :].astype(F32)
            kmean_sc[n:n + 1, :] = jnp.mean(kb, axis=0, keepdims=True)
            vt_sc[n] = _transposed_bf16(v_ref[0, n * blk:(n + 1) * blk, :])

    qt = q_ref[0].astype(F32).T
    row = lax.broadcasted_iota(I32, (LANES, blk), 0)
    blk_id = lax.broadcasted_iota(I32, (nb, blk), 0)
    valid = blk_id < own
    causal = (lax.broadcasted_iota(I32, (blk, blk), 0) <= lax.broadcasted_iota(I32, (blk, blk), 1))
    kmean = kmean_sc[...].astype(BF16)

    qs = [_head_queries(qt, row, hh, scale) for hh in range(n_heads)]
    gates = [_dot(kmean[:, _pair_lanes(hh)], _head_queries(qt, row, hh, 1.0)) for hh in range(n_heads)]

    def scores(n):
        offn = pl.multiple_of(n * blk, blk)
        return [_dot(k_ref[0, pl.ds(offn, blk), _pair_lanes(hh)], qs[hh]) for hh in range(n_heads)]

    own_scores = scores(own)
    prev_scores = scores(jnp.maximum(own - 1, 0))

    blk_f = blk_id.astype(F32)
    for hh in range(n_heads):
        gate = jnp.where(valid, gates[hh], NEG)
        picked = jnp.zeros((nb, blk), jnp.bool_)
        for _ in range(MOBA_TOPK):
            top = _fold_keys(gate, jnp.maximum, jnp.max)
            first = _fold_keys(jnp.where(gate == top, blk_f, float(nb)), jnp.minimum, jnp.min)
            hit = blk_f == first
            picked = picked | hit
            gate = jnp.where(hit, -jnp.inf, gate)
        sel_sc[hh] = jnp.where(valid & picked, 0.0, NEG)

    prev = jnp.maximum(own - 1, 0)
    no_prev = jnp.where(own > 0, 0.0, NEG)
    stats = []
    for hh, (s_own, s_prev) in enumerate(zip(own_scores, prev_scores)):
        s_own = jnp.where(causal, s_own + tbl_ref[hh, blk:, :], NEG)
        s_prev = s_prev + tbl_ref[hh, :blk, :]
        term = sel_sc[hh, pl.ds(prev, 1), :] + no_prev
        m0 = jnp.maximum(_fold_keys(s_own, jnp.maximum, jnp.max),
                         _fold_keys(s_prev, jnp.maximum, jnp.max) + term)
        p_own = jnp.exp(s_own - m0)
        p_prev = jnp.exp(s_prev - (m0 - term))
        l0 = _fold_keys(p_own, jnp.add, jnp.sum) + _fold_keys(p_prev, jnp.add, jnp.sum)
        stats.append((m0, l0, p_own.astype(BF16), p_prev.astype(BF16)))
    state = []
    for hh, (m0, l0, p_own, p_prev) in enumerate(stats):
        state += [m0, l0, _dot(vt_sc[own, _head_dims(hh), :], p_own) + _dot(vt_sc[prev, _head_dims(hh), :], p_prev)]
    state = tuple(state)

    def softmax_step(s, per_query, m_run, l_run):
        m_new = jnp.maximum(m_run, _fold_keys(s, jnp.maximum, jnp.max) + per_query)
        a = jnp.exp(m_run - m_new)
        pn = jnp.exp(s - (m_new - per_query))
        return m_new, a * l_run + _fold_keys(pn, jnp.add, jnp.sum), a, pn.astype(BF16)

    n_far = own - 1
    last_far = jnp.maximum(n_far - 1, 0)
    for hh, s in enumerate(scores(0)):
        s_sc[hh] = s
    p_sc[...] = jnp.zeros(p_sc.shape, BF16)

    def far_block(n, carry):
        state, a_prev = carry[:3 * n_heads], carry[3 * n_heads:]
        n_prev = jnp.maximum(n - 1, 0)
        pv = [_dot(vt_sc[n_prev, _head_dims(hh), :], p_sc[hh]) for hh in range(n_heads)]
        s_next = scores(jnp.minimum(n + 1, last_far))
        stats = [softmax_step(s_sc[hh], rb_ref[N_BUCKETS - 1, hh] + sel_sc[hh, pl.ds(n, 1), :],
                              state[3 * hh], state[3 * hh + 1]) for hh in range(n_heads)]
        new, a_new = [], []
        for hh, (m_new, l_new, a, pn) in enumerate(stats):
            p_sc[hh] = pn
            s_sc[hh] = s_next[hh]
            new += [m_new, l_new, a_prev[hh] * state[3 * hh + 2] + pv[hh]]
            a_new.append(a)
        return tuple(new) + tuple(a_new)

    ones = jnp.ones((1, blk), F32)
    carry = lax.fori_loop(0, n_far, far_block, tuple(state) + (ones,) * n_heads)
    state, a_prev = carry[:3 * n_heads], carry[3 * n_heads:]
    out_t = jnp.concatenate(
        [(a_prev[hh] * state[3 * hh + 2] + _dot(vt_sc[last_far, _head_dims(hh), :], p_sc[hh])) / state[3 * hh + 1]
         for hh in range(n_heads)], axis=0)
    o_ref[0] = out_t.T.astype(o_ref.dtype)


def _moba(qkv3, tbl, rel_bias, col0):
    bsz, seq, _ = qkv3.shape
    assert MAX_DISTANCE <= MOBA_BLOCK and seq % MOBA_BLOCK == 0
    width = N_MOBA_HEADS * HEAD_DIM
    blk = MOBA_BLOCK
    nb = seq // blk
    return pl.pallas_call(
        _moba_kernel,
        grid=(bsz, nb),
        in_specs=[pl.BlockSpec(memory_space=pltpu.SMEM),
                  pl.BlockSpec((1, blk, width), lambda b, i: (b, i, col0)),
                  pl.BlockSpec((1, seq, width), lambda b, i: (b, 0, col0 + 1)),
                  pl.BlockSpec((1, seq, width), lambda b, i: (b, 0, col0 + 2)),
                  pl.BlockSpec((N_MOBA_HEADS, 2 * blk, blk), lambda b, i: (0, 0, 0))],
        out_specs=pl.BlockSpec((1, blk, width), lambda b, i: (b, i, 0)),
        out_shape=jax.ShapeDtypeStruct((bsz, seq, width), BF16),
        scratch_shapes=[pltpu.VMEM((nb, width), F32),
                        pltpu.VMEM((nb, width, blk), BF16),
                        pltpu.VMEM((N_MOBA_HEADS, nb, blk), F32),
                        pltpu.VMEM((N_MOBA_HEADS, blk, blk), F32),
                        pltpu.VMEM((N_MOBA_HEADS, blk, blk), BF16)],
        compiler_params=_params(("arbitrary", "arbitrary"), 48),
        name="moba",
    )(rel_bias, qkv3, qkv3, qkv3, tbl)


def _sb_kernel(q_ref, k_ref, v_ref, o_ref, vt_sc):
    qi = pl.program_id(1)
    t, width = q_ref.shape[1], q_ref.shape[2]
    n_heads = width // HEAD_DIM
    nb = k_ref.shape[1] // t
    scale = HEAD_DIM ** -0.5

    @pl.when(qi == 0)
    def _():
        for n in range(nb):
            vt_sc[n] = _transposed_bf16(v_ref[0, n * t:(n + 1) * t, :])

    qt = q_ref[0].astype(F32).T
    row = lax.broadcasted_iota(I32, (LANES, t), 0)
    key = lax.broadcasted_iota(I32, (t, t), 0)
    qry = lax.broadcasted_iota(I32, (t, t), 1)
    strict = key < qry
    tri = jnp.where(qry >= key, 1.0, 0.0).astype(BF16)
    qs = [_head_queries(qt, row, hh, scale) for hh in range(n_heads)]

    def blocks(tiles):
        zs = [[_dot(k_ref[0, pl.ds(pl.multiple_of(j * t, t), t), _pair_lanes(hh)], qs[hh])
               for hh in range(n_heads)] for j, _ in tiles]
        csums = []
        for (_, diagonal), zt in zip(tiles, zs):
            row_sums = []
            for z in zt:
                sp = jnp.maximum(z, 0.0) + jnp.log(1.0 + jnp.exp(-jnp.abs(z)))
                if diagonal:
                    sp = jnp.where(strict, sp, 0.0)
                hi = sp.astype(BF16)
                lo = (sp - hi.astype(F32)).astype(BF16)
                row_sums.append(_dot(tri, hi) + _dot(tri, lo))
            csums.append(row_sums)
        out = []
        for (j, diagonal), zt, ct in zip(tiles, zs, csums):
            pvs = []
            for hh in range(n_heads):
                a = jnp.exp(zt[hh] - ct[hh])
                if diagonal:
                    a = jnp.where(strict, a, 0.0)
                pvs.append(_dot(vt_sc[j, _head_dims(hh), :], a.astype(BF16)))
            out.append((pvs, [c[0:1, :] for c in ct]))
        return out

    has_prev = qi > 0
    (pv_d, tot_d), (pv_p, tot_p) = blocks([(qi, True), (jnp.maximum(qi - 1, 0), False)])
    accs = tuple(d + jnp.where(has_prev, p * jnp.exp(-c), 0.0) for d, p, c in zip(pv_d, pv_p, tot_d))
    carries = tuple(c + jnp.where(has_prev, p, 0.0) for c, p in zip(tot_d, tot_p))

    def lowest(carries):
        return jnp.min(functools.reduce(jnp.minimum, carries))

    def cond(state):
        j, cmin, _, _ = state
        return (j >= 0) & (cmin < SB_ZERO_LOG)

    def body(state):
        j, _, carries, accs = state
        (pvs, totals), = blocks([(j, False)])
        accs = tuple(acc + pv * jnp.exp(-c) for acc, pv, c in zip(accs, pvs, carries))
        carries = tuple(c + total for c, total in zip(carries, totals))
        return j - 1, lowest(carries), carries, accs

    state = lax.while_loop(cond, body, (qi - 2, lowest(carries), carries, accs))
    out_t = jnp.concatenate(list(state[3]), axis=0)
    o_ref[0] = out_t.T.astype(o_ref.dtype)


def _stickbreak(qkv3, col0):
    bsz, seq, _ = qkv3.shape
    width = N_SB_HEADS * HEAD_DIM
    t = ATTN_ROWS
    return pl.pallas_call(
        _sb_kernel,
        grid=(bsz, seq // t),
        in_specs=[pl.BlockSpec((1, t, width), lambda b, i: (b, i, col0)),
                  pl.BlockSpec((1, seq, width), lambda b, i: (b, 0, col0 + 1)),
                  pl.BlockSpec((1, seq, width), lambda b, i: (b, 0, col0 + 2))],
        out_specs=pl.BlockSpec((1, t, width), lambda b, i: (b, i, 0)),
        out_shape=jax.ShapeDtypeStruct((bsz, seq, width), BF16),
        scratch_shapes=[pltpu.VMEM((seq // t, width, t), BF16)],
        compiler_params=_params(("arbitrary", "arbitrary"), 48),
        name="stickbreak",
    )(qkv3, qkv3, qkv3)


def _memkv_kernel(m_ref, w_ref, o_ref):
    o_ref[...] = _dot(m_ref[...].astype(BF16), w_ref[...]).astype(BF16)


def _memkv(mem2, w_bf16):
    n, d = mem2.shape
    width = w_bf16.shape[1]
    tm = min(n, 512)
    return pl.pallas_call(
        _memkv_kernel,
        grid=(n // tm,),
        in_specs=[pl.BlockSpec((tm, d), lambda i: (i, 0)),
                  pl.BlockSpec((d, width), lambda i: (0, 0))],
        out_specs=pl.BlockSpec((tm, width), lambda i: (i, 0)),
        out_shape=jax.ShapeDtypeStruct((n, width), BF16),
        compiler_params=_params(("arbitrary",), 32),
        name="memkv",
    )(mem2, w_bf16)


def _mem_kernel(q_ref, k_ref, v_ref, o_ref, vt_sc):
    t, width = q_ref.shape[1], q_ref.shape[2]
    n_heads = width // HEAD_DIM
    scale = HEAD_DIM ** -0.5

    @pl.when(pl.program_id(1) == 0)
    def _():
        vt_sc[...] = _transposed_bf16(v_ref[0])

    qt = q_ref[0].astype(F32).T
    row = lax.broadcasted_iota(I32, (LANES, t), 0)
    qs = [_head_queries(qt, row, hh, scale) for hh in range(n_heads)]
    scores = [_dot(k_ref[0, :, _pair_lanes(hh)], qs[hh]) for hh in range(n_heads)]
    probs = []
    for s in scores:
        e = jnp.exp(s - _fold_keys(s, jnp.maximum, jnp.max))
        probs.append((e * (1.0 / _fold_keys(e, jnp.add, jnp.sum))).astype(BF16))
    out_t = jnp.concatenate([_dot(vt_sc[_head_dims(hh), :], probs[hh]) for hh in range(n_heads)], axis=0)
    o_ref[0] = out_t.T.astype(o_ref.dtype)


def _mem_attention(qkv3, kv3, qcol):
    bsz, seq, _ = qkv3.shape
    n_mem = kv3.shape[1]
    width = N_MEM_HEADS * HEAD_DIM
    t = MEM_ROWS
    return pl.pallas_call(
        _mem_kernel,
        grid=(bsz, seq // t),
        in_specs=[pl.BlockSpec((1, t, width), lambda b, i: (b, i, qcol)),
                  pl.BlockSpec((1, n_mem, width), lambda b, i: (b, 0, 0)),
                  pl.BlockSpec((1, n_mem, width), lambda b, i: (b, 0, 1))],
        out_specs=pl.BlockSpec((1, t, width), lambda b, i: (b, i, 0)),
        out_shape=jax.ShapeDtypeStruct((bsz, seq, width), BF16),
        scratch_shapes=[pltpu.VMEM((width, n_mem), BF16)],
        compiler_params=_params(("arbitrary", "arbitrary"), 32),
        name="mem_attention",
    )(qkv3, kv3, kv3)


def _merge_kernel(ya_ref, yb_ref, ym_ref, x_ref, wg_ref, wa_ref, wb_ref, wm_ref, wo_ref,
                  lng_ref, lnb_ref, wrh_ref, br_ref, x1_ref, route_ref, tcnt_ref, h_sc, cnt_ref,
                  *, alpha, n_experts):
    step = pl.program_id(0)
    tm, d = x_ref.shape
    slot = lax.rem(step, 2)

    @pl.when(step == 0)
    def _():
        cnt_ref[...] = jnp.zeros(cnt_ref.shape, F32)
        h_sc[...] = jnp.zeros(h_sc.shape, F32)

    x = x_ref[...]
    xb = x.astype(BF16)
    branches = ((ya_ref, wa_ref), (yb_ref, wb_ref), (ym_ref, wm_ref))

    def branch_term(k):
        gate_logits = _dot(xb, wg_ref[:, k * d:(k + 1) * d])
        return gate_logits, _dot(branches[k][0][...], branches[k][1][...])

    x1 = _layer_norm(h_sc[1 - slot], lng_ref[...], lnb_ref[...])
    x1_ref[...] = x1
    xh = x1.astype(BF16)
    xl = (x1 - xh.astype(F32)).astype(BF16)

    first_term = branch_term(0)

    by_hi = _dot(xh, wrh_ref[...])
    logits = (by_hi[:, :LANES] + (by_hi[:, LANES:] + _dot(xl, wrh_ref[:, :LANES]))
              + br_ref[...])

    merged = None
    for k in range(len(branches)):
        gate_logits, y = first_term if k == 0 else branch_term(k)
        term = _sigmoid(gate_logits) * y
        merged = term if merged is None else merged + term
    h_sc[slot] = alpha * x + _dot(merged.astype(BF16), wo_ref[...])

    col = lax.broadcasted_iota(I32, (tm, LANES), 1)
    colf = col.astype(F32)
    big = float(LANES)
    gmask = (col >= n_experts) & (col < n_experts + N_GROUPS)
    lg = jnp.where(gmask, logits, -jnp.inf)
    gmax = jnp.max(lg, axis=1, keepdims=True)
    gidx = jnp.min(jnp.where(lg == gmax, colf, big), axis=1, keepdims=True) - n_experts
    g_p = 1.0 / jnp.sum(jnp.where(gmask, jnp.exp(logits - gmax), 0.0), axis=1, keepdims=True)
    lo_col = gidx * EXPERTS_PER_GROUP
    emask = (colf >= lo_col) & (colf < lo_col + EXPERTS_PER_GROUP)
    le = jnp.where(emask, logits, -jnp.inf)
    l1 = jnp.max(le, axis=1, keepdims=True)
    i1 = jnp.min(jnp.where(le == l1, colf, big), axis=1, keepdims=True)
    le2 = jnp.where(colf == i1, -jnp.inf, le)
    l2 = jnp.max(le2, axis=1, keepdims=True)
    i2 = jnp.min(jnp.where(le2 == l2, colf, big), axis=1, keepdims=True)
    e2 = jnp.exp(l2 - l1)
    gate1 = g_p * (1.0 / (1.0 + e2))
    gate2 = g_p * (e2 / (1.0 + e2))

    oh1 = colf == i1
    oh2 = colf == i2
    cnt = jnp.where((oh1 | oh2) & (step > 0), 1.0, 0.0)
    rr = lax.broadcasted_iota(I32, (tm, tm), 0)
    cc = lax.broadcasted_iota(I32, (tm, tm), 1)
    before = jnp.where(cc < rr, 1.0, 0.0).astype(BF16)
    base = _dot(before, cnt.astype(BF16)) + cnt_ref[0:1, :]
    rank1 = jnp.sum(jnp.where(oh1, base, 0.0), axis=1, keepdims=True)
    rank2 = jnp.sum(jnp.where(oh2, base, 0.0), axis=1, keepdims=True)
    tile_cnt = jnp.sum(cnt, axis=0, keepdims=True)
    cnt_ref[...] = cnt_ref[...] + tile_cnt
    tcnt_ref[0] = jnp.broadcast_to(tile_cnt, tcnt_ref.shape[1:])

    route = jnp.zeros((tm, LANES), F32)
    for k, val in enumerate((i1, i2, rank1, rank2, gate1, gate2)):
        route = jnp.where(col == k, val, route)
    route_ref[...] = route


def _merge(ya, yb, ym, x2, w_gates, wa, wb, wm, wo, ln_g, ln_b, wr_split, b_r, alpha, n_experts):
    n, d = x2.shape
    tm = MERGE_ROWS
    n_tiles = n // tm
    row_in = lambda w: pl.BlockSpec((tm, w), lambda i: (jnp.minimum(i, n_tiles - 1), 0))
    row_out = lambda w: pl.BlockSpec((tm, w), lambda i: (jnp.maximum(i - 1, 0), 0))
    full = lambda a: pl.BlockSpec(a.shape, lambda i: (0,) * a.ndim)
    return pl.pallas_call(
        functools.partial(_merge_kernel, alpha=alpha, n_experts=n_experts),
        grid=(n_tiles + 1,),
        in_specs=[row_in(ya.shape[1]), row_in(yb.shape[1]), row_in(ym.shape[1]), row_in(d),
                  full(w_gates), full(wa), full(wb), full(wm), full(wo), full(ln_g), full(ln_b),
                  full(wr_split), full(b_r)],
        out_specs=[row_out(d), row_out(LANES),
                   pl.BlockSpec((1, 8, LANES), lambda i: (jnp.maximum(i - 1, 0), 0, 0))],
        out_shape=[jax.ShapeDtypeStruct((n, d), F32),
                   jax.ShapeDtypeStruct((n, LANES), F32),
                   jax.ShapeDtypeStruct((n_tiles, 8, LANES), F32)],
        scratch_shapes=[pltpu.VMEM((2, tm, d), F32), pltpu.VMEM((8, LANES), F32)],
        compiler_params=_params(("arbitrary",), 56),
        name="merge",
    )(ya, yb, ym, x2, w_gates, wa, wb, wm, wo, ln_g, ln_b, wr_split, b_r)


def _row_copy(src_hbm, row, buf, sem, r):
    return pltpu.make_async_copy(src_hbm.at[pl.ds(row, 1), :], buf.at[pl.ds(r, 1), :], sem)


def _gather_rows(src_hbm, idx_ref, buf, sem, n_rows, unrolled):
    if unrolled:
        for r in range(n_rows):
            _row_copy(src_hbm, idx_ref[0, 0, r], buf, sem, r).start()
    else:
        def issue(r, _):
            _row_copy(src_hbm, idx_ref[0, 0, r], buf, sem, r).start()
            return 0
        lax.fori_loop(0, n_rows, issue, 0)


def _wait_rows(src_hbm, buf, sem, n_rows):
    pltpu.make_async_copy(src_hbm.at[pl.ds(0, n_rows), :], buf, sem).wait()


DISPATCH_ROWS = 512
CHUNK = 8


def _chunk_copy(buf, xpad_hbm, sem, src_row, dst_row):
    return pltpu.make_async_copy(buf.at[pl.ds(src_row, CHUNK), :], xpad_hbm.at[pl.ds(dst_row, CHUNK), :], sem)


def _dispatch_kernel(nch_ref, off_ref, dst_ref, zrow_ref, zflag_ref, nu_ref, base_ref, route_ref, x_ref,
                     xpad_hbm, dest_ref, sorted_even, sorted_odd, zero_buf, sem, zsem):
    i = pl.program_id(0)
    n_steps = pl.num_programs(0)
    n_exp = base_ref.shape[1]
    td = x_ref.shape[0]
    rb = sorted_even.shape[0]
    zrows = zero_buf.shape[0]
    bufs = (sorted_even, sorted_odd)
    parity = lax.rem(i, 2)

    @pl.when(i == 0)
    def _():
        zero_buf[...] = jnp.zeros(zero_buf.shape, F32)

        def zero_block(row, start):
            cp = pltpu.make_async_copy(zero_buf, xpad_hbm.at[pl.ds(pl.multiple_of(row, zrows), zrows), :], zsem)
            cp.start() if start else cp.wait()

        for start in (True, False):
            for e in range(n_exp):
                pl.when(zflag_ref[e] == 1)(functools.partial(zero_block, zrow_ref[e], start))

            def unused(blk, _):
                zero_block(blk * zrows, start)
                return 0
            lax.fori_loop(nu_ref[0], xpad_hbm.shape[0] // zrows, unused, 0)

    rt = route_ref[...].T
    eid = lax.broadcasted_iota(I32, (n_exp, td), 0).astype(F32)
    rowid = lax.broadcasted_iota(I32, (rb, td), 0).astype(F32)
    hit = None
    dests = []
    for k in range(2):
        mine = eid == rt[k:k + 1, :]
        rank = rt[2 + k:3 + k, :]
        pos = jnp.sum(jnp.where(mine, base_ref[0, :, 0:1], 0.0), axis=0, keepdims=True) + rank
        dests.append(jnp.sum(jnp.where(mine, base_ref[0, :, 1:2], 0.0), axis=0, keepdims=True) + rank)
        hit = (rowid == pos) if hit is None else hit | (rowid == pos)
    dest_ref[0] = jnp.concatenate(dests + [jnp.zeros((8 - len(dests), td), F32)], axis=0).astype(I32)
    perm = jnp.where(hit, 1.0, 0.0).astype(BF16)
    rows_sorted = _dot(perm, x_ref[...].astype(BF16))

    def runs(tile, buf, s, start):
        for e in range(n_exp):
            k = tile * n_exp + e
            src0, dst0 = off_ref[k], dst_ref[k]

            def one(j, _):
                cp = _chunk_copy(buf, xpad_hbm, s, pl.multiple_of(src0 + j * CHUNK, CHUNK),
                                 pl.multiple_of(dst0 + j * CHUNK, CHUNK))
                cp.start() if start else cp.wait()
                return 0
            lax.fori_loop(0, nch_ref[k], one, 0)

    def wait_runs(tile, buf, s):
        k_last = tile * n_exp + n_exp - 1
        total = pl.multiple_of(off_ref[k_last] + nch_ref[k_last] * CHUNK, CHUNK)
        pltpu.make_async_copy(buf.at[pl.ds(0, total), :], xpad_hbm.at[pl.ds(0, total), :], s).wait()

    def step(cur):
        bufs[cur][...] = rows_sorted

        @pl.when(i > 0)
        def _():
            wait_runs(i - 1, bufs[1 - cur], sem.at[1 - cur])

        runs(i, bufs[cur], sem.at[cur], True)

        @pl.when(i + 1 == n_steps)
        def _():
            wait_runs(i, bufs[cur], sem.at[cur])

    for cur in range(2):
        pl.when(parity == cur)(functools.partial(step, cur))


def _dispatch(x1, route, nch, off8, dst, zrow, zflag, n_used, base, n_rows_out):
    n, d = x1.shape
    td = DISPATCH_ROWS
    n_tiles = n // td
    n_exp = base.shape[1]
    rb = 2 * td + n_exp * CHUNK
    grid_spec = pltpu.PrefetchScalarGridSpec(
        num_scalar_prefetch=6,
        grid=(n_tiles,),
        in_specs=[pl.BlockSpec((1, n_exp, 2), lambda i, *_: (i, 0, 0)),
                  pl.BlockSpec((td, LANES), lambda i, *_: (i, 0)),
                  pl.BlockSpec((td, d), lambda i, *_: (i, 0))],
        out_specs=[pl.BlockSpec(memory_space=pl.ANY),
                   pl.BlockSpec((1, 8, td), lambda i, *_: (i, 0, 0))],
        scratch_shapes=[pltpu.VMEM((rb, d), F32), pltpu.VMEM((rb, d), F32),
                        pltpu.VMEM((MOE_BLOCK, d), F32),
                        pltpu.SemaphoreType.DMA((2,)), pltpu.SemaphoreType.DMA(())])
    return pl.pallas_call(
        _dispatch_kernel,
        grid_spec=grid_spec,
        out_shape=[jax.ShapeDtypeStruct((n_rows_out, d), F32),
                   jax.ShapeDtypeStruct((n_tiles, 8, td), I32)],
        compiler_params=_params(("arbitrary",), 48),
        name="dispatch",
    )(nch, off8, dst, zrow, zflag, n_used, base, route, x1)


def _experts_kernel(be_ref, nu_ref, side_ref, e0_ref, e1_ref, x_ref, wg0_ref, wu0_ref, wd0_ref,
                    wg1_ref, wu1_ref, wd1_ref, o_ref, wg_sc, wu_sc, wd_sc):
    b = pl.program_id(0)
    n_used = nu_ref[0]

    @pl.when(b < n_used)
    def _():
        changed = (b == 0) | (be_ref[b] != be_ref[jnp.maximum(b - 1, 0)])

        for side, (wg_ref, wu_ref, wd_ref) in enumerate(((wg0_ref, wu0_ref, wd0_ref), (wg1_ref, wu1_ref, wd1_ref))):
            @pl.when(changed & (side_ref[b] == side))
            def _():
                wg_sc[...] = wg_ref[0].astype(BF16)
                wu_sc[...] = wu_ref[0].astype(BF16)
                wd_sc[...] = wd_ref[0].astype(BF16)

        xb = x_ref[...].astype(BF16)
        g = _dot(xb, wg_sc[...])
        u = _dot(xb, wu_sc[...])
        h = (g * _sigmoid(g) * u).astype(BF16)
        o_ref[...] = _dot(h, wd_sc[...])

    @pl.when(b >= n_used)
    def _():
        o_ref[...] = jnp.zeros(o_ref.shape, F32)


def _experts(x_pad, block_expert, n_used, w_gate, w_up, w_down):
    d = x_pad.shape[1]
    n_blocks = block_expert.shape[0]
    rows = MOE_BLOCK
    d_exp = w_gate.shape[-1]
    blk = jnp.arange(n_blocks, dtype=I32)
    new_expert = jnp.concatenate([jnp.ones((1,), I32), (block_expert[1:] != block_expert[:-1]).astype(I32)])
    side = (jnp.cumsum(new_expert) - 1) % 2
    held = []
    for s in range(2):
        nxt = jnp.flip(lax.cummin(jnp.flip(jnp.where(side == s, blk, n_blocks - 1))))
        held.append(block_expert[nxt])
    x_map = lambda b, be, nu, sd, e0, e1: (jnp.minimum(b, jnp.maximum(nu[0] - 1, 0)), 0)
    w_map = lambda s: (lambda b, be, nu, sd, e0, e1: ((e0, e1)[s][b], 0, 0))
    w_specs = [pl.BlockSpec(shape, w_map(s)) for s in range(2)
               for shape in ((1, d, d_exp), (1, d, d_exp), (1, d_exp, d))]
    grid_spec = pltpu.PrefetchScalarGridSpec(
        num_scalar_prefetch=5,
        grid=(n_blocks,),
        in_specs=[pl.BlockSpec((rows, d), x_map)] + w_specs,
        out_specs=pl.BlockSpec((rows, d), lambda b, *_: (b, 0)),
        scratch_shapes=[pltpu.VMEM((d, d_exp), BF16),
                        pltpu.VMEM((d, d_exp), BF16),
                        pltpu.VMEM((d_exp, d), BF16)])
    return pl.pallas_call(
        _experts_kernel,
        grid_spec=grid_spec,
        out_shape=jax.ShapeDtypeStruct((n_blocks * rows, d), F32),
        compiler_params=_params(("arbitrary",), 52),
        name="experts",
    )(block_expert, n_used, side.astype(I32), held[0], held[1], x_pad,
      w_gate, w_up, w_down, w_gate, w_up, w_down)


def _combine_kernel(idx_ref, idxn_ref, y_hbm, x1_ref, route_ref, lng_ref, lnb_ref, o_ref,
                    ybuf_even, ybuf_odd, sem, *, alpha):
    i = pl.program_id(0)
    n_steps = pl.num_programs(0)
    tm = x1_ref.shape[0]
    bufs = (ybuf_even, ybuf_odd)
    parity = lax.rem(i, 2)

    @pl.when(i == 0)
    def _():
        _gather_rows(y_hbm, idx_ref, bufs[0], sem.at[0], 2 * tm, False)

    def compute(cur):
        _gather_rows(y_hbm, idxn_ref, bufs[1 - cur], sem.at[1 - cur], 2 * tm, True)
        _wait_rows(y_hbm, bufs[cur], sem.at[cur], 2 * tm)
        ffn = (route_ref[:, 4:5] * bufs[cur][0:tm, :] + route_ref[:, 5:6] * bufs[cur][tm:2 * tm, :])
        o_ref[...] = _layer_norm(alpha * x1_ref[...] + ffn, lng_ref[...], lnb_ref[...])

        @pl.when(i + 1 == n_steps)
        def _():
            _wait_rows(y_hbm, bufs[1 - cur], sem.at[1 - cur], 2 * tm)

    for cur in range(2):
        pl.when(parity == cur)(functools.partial(compute, cur))


def _combine(y_pad, dest, x1, route, ln_g, ln_b, alpha):
    n, d = x1.shape
    tm = COMBINE_ROWS
    n_steps = n // tm
    idx3 = dest.reshape(n_steps, tm, 2).transpose(0, 2, 1).reshape(n_steps, 1, 2 * tm)
    smem_idx = lambda f: pl.BlockSpec((1, 1, 2 * tm), f, memory_space=pltpu.SMEM)
    return pl.pallas_call(
        functools.partial(_combine_kernel, alpha=alpha),
        grid=(n_steps,),
        in_specs=[smem_idx(lambda i: (i, 0, 0)),
                  smem_idx(lambda i: (jnp.minimum(i + 1, n_steps - 1), 0, 0)),
                  pl.BlockSpec(memory_space=pl.ANY),
                  pl.BlockSpec((tm, d), lambda i: (i, 0)),
                  pl.BlockSpec((tm, LANES), lambda i: (i, 0)),
                  pl.BlockSpec((1, d), lambda i: (0, 0)),
                  pl.BlockSpec((1, d), lambda i: (0, 0))],
        out_specs=pl.BlockSpec((tm, d), lambda i: (i, 0)),
        out_shape=jax.ShapeDtypeStruct((n, d), F32),
        scratch_shapes=[pltpu.VMEM((2 * tm, d), F32), pltpu.VMEM((2 * tm, d), F32),
                        pltpu.SemaphoreType.DMA((2,))],
        compiler_params=_params(("arbitrary",), 32),
        name="combine",
    )(idx3, idx3, y_pad, x1, route, ln_g, ln_b)


def _split_bf16(w):
    hi = w.astype(BF16)
    return hi, (w - hi.astype(F32)).astype(BF16)


def _layer(x, mem, tbl, rel_bias, w_in, w_mem_kv, w_br_moba, w_br_sb, w_br_mem, w_out, ln1_g, ln1_b,
           w_rg, b_rg, w_re, b_re, w_gate, w_up, w_down, ln2_g, ln2_b, alpha):
    bsz, seq, d = x.shape
    n = bsz * seq
    n_experts = w_re.shape[1]
    moba_w, sb_w, mem_w = N_MOBA_HEADS * HEAD_DIM, N_SB_HEADS * HEAD_DIM, N_MEM_HEADS * HEAD_DIM
    n_qkv = 3 * moba_w + 3 * sb_w + mem_w
    assert w_in.shape[1] == n_qkv + 3 * d and n_experts + N_GROUPS <= LANES and moba_w == sb_w

    x2 = x.reshape(n, d)
    qkv = _proj(x2, w_in[:, :n_qkv].astype(BF16))
    qkv3 = qkv.reshape(bsz, seq, n_qkv)
    y_a = _moba(qkv3, tbl, rel_bias, 0)
    y_b = _stickbreak(qkv3, 3 * moba_w // sb_w)
    kv = _memkv(mem.reshape(-1, d), w_mem_kv.astype(BF16)).reshape(bsz, mem.shape[1], 2 * mem_w)
    y_m = _mem_attention(qkv3, kv, (3 * moba_w + 3 * sb_w) // mem_w)

    w_r = jnp.zeros((d, LANES), F32).at[:, :n_experts].set(w_re).at[:, n_experts:n_experts + N_GROUPS].set(w_rg)
    b_r = jnp.zeros((1, LANES), F32).at[0, :n_experts].set(b_re).at[0, n_experts:n_experts + N_GROUPS].set(b_rg)
    wr_split = jnp.concatenate(_split_bf16(w_r), axis=1)
    x1, route, tile_counts = _merge(
        y_a.reshape(n, moba_w), y_b.reshape(n, sb_w), y_m.reshape(n, mem_w), x2, w_in[:, n_qkv:].astype(BF16),
        w_br_moba.astype(BF16), w_br_sb.astype(BF16), w_br_mem.astype(BF16), w_out.astype(BF16),
        ln1_g.reshape(1, d), ln1_b.reshape(1, d), wr_split, b_r, alpha, n_experts)

    rows, td = MOE_BLOCK, DISPATCH_ROWS
    assert td == MERGE_ROWS
    n_tiles = n // td
    tile_cnt = tile_counts[:, 0, :n_experts].astype(I32)
    n_chunks = (tile_cnt + CHUNK - 1) // CHUNK
    run_rows = n_chunks * CHUNK
    run_rank = jnp.cumsum(tile_cnt, axis=0) - tile_cnt
    run_off = jnp.cumsum(run_rows, axis=0) - run_rows
    sorted_off = jnp.cumsum(run_rows, axis=1) - run_rows
    expert_rows = jnp.sum(run_rows, axis=0)
    padded = (expert_rows + rows - 1) // rows * rows
    pend = jnp.cumsum(padded)
    pstart = pend - padded
    run_dst = pstart[None, :] + run_off
    n_blocks = (2 * n + n_tiles * n_experts * (CHUNK - 1)) // rows + n_experts
    block_start = jnp.arange(n_blocks, dtype=I32) * rows
    block_expert = jnp.minimum(jnp.sum((pend[None, :] <= block_start[:, None]).astype(I32), axis=1),
                               n_experts - 1)
    n_used = (pend[-1] // rows).astype(I32).reshape(1)
    base = jnp.stack([sorted_off - run_rank, run_dst - run_rank], axis=-1).astype(F32)
    x_pad, dest8 = _dispatch(x1, route, n_chunks.reshape(-1), sorted_off.reshape(-1), run_dst.reshape(-1),
                             jnp.maximum(pend - rows, 0), (padded > 0).astype(I32), n_used, base,
                             n_blocks * rows)
    dest = dest8[:, 0:2, :].transpose(0, 2, 1).reshape(n, 2)

    y_pad = _experts(x_pad, block_expert, n_used, w_gate, w_up, w_down)
    out = _combine(y_pad, dest, x1, route, ln2_g.reshape(1, d), ln2_b.reshape(1, d), alpha)
    return out.reshape(bsz, seq, d)


def kernel(x, mem, w_in, w_mem_kv, rel_bias, w_br_moba, w_br_sb, w_br_mem, w_out, ln1_g, ln1_b,
           w_router_group, b_router_group, w_router_expert, b_router_expert,
           w_gate, w_up, w_down, ln2_g, ln2_b):
    depth = w_in.shape[0]
    alpha = (2.0 * depth) ** 0.25
    tbl = _bias_table(rel_bias)
    for l in range(depth):
        x = _layer(x, mem, tbl, rel_bias, w_in[l], w_mem_kv[l], w_br_moba[l], w_br_sb[l], w_br_mem[l],
                   w_out[l], ln1_g[l], ln1_b[l], w_router_group[l], b_router_group[l],
                   w_router_expert[l], b_router_expert[l], w_gate[l], w_up[l], w_down[l],
                   ln2_g[l], ln2_b[l], alpha)
    return x
```

```python
import functools
import math

import jax
import jax.numpy as jnp
from jax import lax
from jax.experimental import pallas as pl
from jax.experimental.pallas import tpu as pltpu

F32, BF16, I32 = jnp.float32, jnp.bfloat16, jnp.int32

HEAD_DIM = 64
N_MOBA_HEADS = 6
N_SB_HEADS = 6
N_MEM_HEADS = 4
MOBA_BLOCK = 256
MOBA_TOPK = 3
N_BUCKETS = 32
MAX_DISTANCE = 128
N_GROUPS = 4
EXPERTS_PER_GROUP = 8
LN_EPS = 1e-5
NEG = -1e30

LANES = 128
MXU_WIDTH = 256
VMEM_BYTES = 64 * 1024 * 1024
PAIR = LANES // HEAD_DIM

SB_ZERO_LOG = 110.0

PROJ_ROWS = 1024
MERGE_ROWS = 512
ATTN_ROWS = 256
MEM_ROWS = 1024
COMBINE_ROWS = 256
MOE_BLOCK = 512


def _params(semantics, vmem_mb):
    return pltpu.CompilerParams(dimension_semantics=semantics,
                                vmem_limit_bytes=min(vmem_mb * 1024 * 1024, VMEM_BYTES))


def _dot(a, b):
    return jnp.dot(a, b, preferred_element_type=F32)


def _layer_norm(h, g, b):
    mu = jnp.mean(h, axis=-1, keepdims=True)
    d = h - mu
    var = jnp.mean(d * d, axis=-1, keepdims=True)
    return d * lax.rsqrt(var + LN_EPS) * g + b


def _sigmoid(x):
    return 1.0 / (1.0 + jnp.exp(-x))


def _head_rows(qt, row, hh, scale):
    keep = (row >= hh * HEAD_DIM) & (row < (hh + 1) * HEAD_DIM)
    return jnp.where(keep, qt * scale, 0.0).astype(BF16)


def _transposed_bf16(x):
    return x.astype(F32).T.astype(BF16)


def _proj_kernel(x_ref, wq_ref, qkv_ref):
    qkv_ref[...] = _dot(x_ref[...].astype(BF16), wq_ref[...]).astype(BF16)


def _proj(x2, w_qkv):
    n, d = x2.shape
    n_qkv = w_qkv.shape[1]
    tm = PROJ_ROWS
    return pl.pallas_call(
        _proj_kernel,
        grid=(n // tm,),
        in_specs=[pl.BlockSpec((tm, d), lambda i: (i, 0)),
                  pl.BlockSpec((d, n_qkv), lambda i: (0, 0))],
        out_specs=pl.BlockSpec((tm, n_qkv), lambda i: (i, 0)),
        out_shape=jax.ShapeDtypeStruct((n, n_qkv), BF16),
        compiler_params=_params(("arbitrary",), 40),
        name="proj",
    )(x2, w_qkv)


def _t5_bucket(rel):
    rel = jnp.maximum(rel, 0)
    max_exact = N_BUCKETS // 2
    rel_f = jnp.maximum(rel, 1).astype(F32)
    large = max_exact + (jnp.log(rel_f / max_exact) / math.log(MAX_DISTANCE / max_exact)
                         * (N_BUCKETS - max_exact)).astype(I32)
    large = jnp.minimum(large, N_BUCKETS - 1)
    return jnp.where(rel < max_exact, rel, large)


def _bias_table_kernel(rb_ref, o_ref):
    h = pl.program_id(0)
    blk = o_ref.shape[2]
    j = lax.broadcasted_iota(I32, (2 * blk, blk), 0)
    i = lax.broadcasted_iota(I32, (2 * blk, blk), 1)
    bucket = _t5_bucket(blk + i - j)
    acc = jnp.zeros((2 * blk, blk), F32)
    for b in range(N_BUCKETS):
        acc = jnp.where(bucket == b, rb_ref[b, h], acc)
    o_ref[0] = acc


def _bias_table(rel_bias):
    n_heads = rel_bias.shape[1]
    blk = MOBA_BLOCK
    return pl.pallas_call(
        _bias_table_kernel,
        grid=(n_heads,),
        in_specs=[pl.BlockSpec(memory_space=pltpu.SMEM)],
        out_specs=pl.BlockSpec((1, 2 * blk, blk), lambda h: (h, 0, 0)),
        out_shape=jax.ShapeDtypeStruct((n_heads, 2 * blk, blk), F32),
        compiler_params=_params(("arbitrary",), 32),
        name="bias_table",
    )(rel_bias)


def _fold_keys(x, op, final):
    while x.shape[0] > 8 and x.shape[0] % 2 == 0:
        half = x.shape[0] // 2
        x = op(x[:half], x[half:])
    return final(x, axis=0, keepdims=True)


def _head_queries(qt, row, hh, scale):
    p, sub = divmod(hh, PAIR)
    return _head_rows(qt[p * LANES:(p + 1) * LANES, :], row, sub, scale)


def _pair_lanes(hh):
    p = hh // PAIR
    return slice(p * LANES, (p + 1) * LANES)


def _head_dims(hh):
    return slice(hh * HEAD_DIM, (hh + 1) * HEAD_DIM)


def _moba_kernel(rb_ref, q_ref, k_ref, v_ref, tbl_ref, o_ref, kmean_sc, vt_sc, sel_sc, s_sc, p_sc):
    own = pl.program_id(1)
    blk, width = q_ref.shape[1], q_ref.shape[2]
    n_heads = width // HEAD_DIM
    nb = k_ref.shape[1] // blk
    scale = HEAD_DIM ** -0.5

    @pl.when(own == 0)
    def _():
        for n in range(nb):
            kb = k_ref[0, n * blk:(n + 1) * blk, :].astype(F32)
            kmean_sc[n:n + 1, :] = jnp.mean(kb, axis=0, keepdims=True)
            vt_sc[n] = _transposed_bf16(v_ref[0, n * blk:(n + 1) * blk, :])

    qt = q_ref[0].astype(F32).T
    row = lax.broadcasted_iota(I32, (LANES, blk), 0)
    blk_id = lax.broadcasted_iota(I32, (nb, blk), 0)
    valid = blk_id < own
    causal = (lax.broadcasted_iota(I32, (blk, blk), 0) <= lax.broadcasted_iota(I32, (blk, blk), 1))
    kmean = kmean_sc[...].astype(BF16)

    qs = [_head_queries(qt, row, hh, scale) for hh in range(n_heads)]
    gates = [_dot(kmean[:, _pair_lanes(hh)], _head_queries(qt, row, hh, 1.0)) for hh in range(n_heads)]

    def scores(n):
        offn = pl.multiple_of(n * blk, blk)
        return [_dot(k_ref[0, pl.ds(offn, blk), _pair_lanes(hh)], qs[hh]) for hh in range(n_heads)]

    own_scores = scores(own)
    prev_scores = scores(jnp.maximum(own - 1, 0))

    blk_f = blk_id.astype(F32)
    for hh in range(n_heads):
        gate = jnp.where(valid, gates[hh], NEG)
        picked = jnp.zeros((nb, blk), jnp.bool_)
        for _ in range(MOBA_TOPK):
            top = _fold_keys(gate, jnp.maximum, jnp.max)
            first = _fold_keys(jnp.where(gate == top, blk_f, float(nb)), jnp.minimum, jnp.min)
            hit = blk_f == first
            picked = picked | hit
            gate = jnp.where(hit, -jnp.inf, gate)
        sel_sc[hh] = jnp.where(valid & picked, 0.0, NEG)

    prev = jnp.maximum(own - 1, 0)
    no_prev = jnp.where(own > 0, 0.0, NEG)
    stats = []
    for hh, (s_own, s_prev) in enumerate(zip(own_scores, prev_scores)):
        s_own = jnp.where(causal, s_own + tbl_ref[hh, blk:, :], NEG)
        s_prev = s_prev + tbl_ref[hh, :blk, :]
        term = sel_sc[hh, pl.ds(prev, 1), :] + no_prev
        m0 = jnp.maximum(_fold_keys(s_own, jnp.maximum, jnp.max),
                         _fold_keys(s_prev, jnp.maximum, jnp.max) + term)
        p_own = jnp.exp(s_own - m0)
        p_prev = jnp.exp(s_prev - (m0 - term))
        l0 = _fold_keys(p_own, jnp.add, jnp.sum) + _fold_keys(p_prev, jnp.add, jnp.sum)
        stats.append((m0, l0, p_own.astype(BF16), p_prev.astype(BF16)))
    state = []
    for hh, (m0, l0, p_own, p_prev) in enumerate(stats):
        state += [m0, l0, _dot(vt_sc[own, _head_dims(hh), :], p_own) + _dot(vt_sc[prev, _head_dims(hh), :], p_prev)]
    state = tuple(state)

    def softmax_step(s, per_query, m_run, l_run):
        m_new = jnp.maximum(m_run, _fold_keys(s, jnp.maximum, jnp.max) + per_query)
        a = jnp.exp(m_run - m_new)
        pn = jnp.exp(s - (m_new - per_query))
        return m_new, a * l_run + _fold_keys(pn, jnp.add, jnp.sum), a, pn.astype(BF16)

    n_far = own - 1
    last_far = jnp.maximum(n_far - 1, 0)
    for hh, s in enumerate(scores(0)):
        s_sc[hh] = s
    p_sc[...] = jnp.zeros(p_sc.shape, BF16)

    def far_block(n, carry):
        state, a_prev = carry[:3 * n_heads], carry[3 * n_heads:]
        n_prev = jnp.maximum(n - 1, 0)
        pv = [_dot(vt_sc[n_prev, _head_dims(hh), :], p_sc[hh]) for hh in range(n_heads)]
        s_next = scores(jnp.minimum(n + 1, last_far))
        stats = [softmax_step(s_sc[hh], rb_ref[N_BUCKETS - 1, hh] + sel_sc[hh, pl.ds(n, 1), :],
                              state[3 * hh], state[3 * hh + 1]) for hh in range(n_heads)]
        new, a_new = [], []
        for hh, (m_new, l_new, a, pn) in enumerate(stats):
            p_sc[hh] = pn
            s_sc[hh] = s_next[hh]
            new += [m_new, l_new, a_prev[hh] * state[3 * hh + 2] + pv[hh]]
            a_new.append(a)
        return tuple(new) + tuple(a_new)

    ones = jnp.ones((1, blk), F32)
    carry = lax.fori_loop(0, n_far, far_block, tuple(state) + (ones,) * n_heads)
    state, a_prev = carry[:3 * n_heads], carry[3 * n_heads:]
    out_t = jnp.concatenate(
        [(a_prev[hh] * state[3 * hh + 2] + _dot(vt_sc[last_far, _head_dims(hh), :], p_sc[hh])) / state[3 * hh + 1]
         for hh in range(n_heads)], axis=0)
    o_ref[0] = out_t.T.astype(o_ref.dtype)


def _moba(qkv3, tbl, rel_bias, col0):
    bsz, seq, _ = qkv3.shape
    assert MAX_DISTANCE <= MOBA_BLOCK and seq % MOBA_BLOCK == 0
    width = N_MOBA_HEADS * HEAD_DIM
    blk = MOBA_BLOCK
    nb = seq // blk
    return pl.pallas_call(
        _moba_kernel,
        grid=(bsz, nb),
        in_specs=[pl.BlockSpec(memory_space=pltpu.SMEM),
                  pl.BlockSpec((1, blk, width), lambda b, i: (b, i, col0)),
                  pl.BlockSpec((1, seq, width), lambda b, i: (b, 0, col0 + 1)),
                  pl.BlockSpec((1, seq, width), lambda b, i: (b, 0, col0 + 2)),
                  pl.BlockSpec((N_MOBA_HEADS, 2 * blk, blk), lambda b, i: (0, 0, 0))],
        out_specs=pl.BlockSpec((1, blk, width), lambda b, i: (b, i, 0)),
        out_shape=jax.ShapeDtypeStruct((bsz, seq, width), BF16),
        scratch_shapes=[pltpu.VMEM((nb, width), F32),
                        pltpu.VMEM((nb, width, blk), BF16),
                        pltpu.VMEM((N_MOBA_HEADS, nb, blk), F32),
                        pltpu.VMEM((N_MOBA_HEADS, blk, blk), F32),
                        pltpu.VMEM((N_MOBA_HEADS, blk, blk), BF16)],
        compiler_params=_params(("arbitrary", "arbitrary"), 48),
        name="moba",
    )(rel_bias, qkv3, qkv3, qkv3, tbl)


def _sb_kernel(q_ref, k_ref, v_ref, o_ref, vt_sc):
    qi = pl.program_id(1)
    t, width = q_ref.shape[1], q_ref.shape[2]
    n_heads = width // HEAD_DIM
    nb = k_ref.shape[1] // t
    scale = HEAD_DIM ** -0.5

    @pl.when(qi == 0)
    def _():
        for n in range(nb):
            vt_sc[n] = _transposed_bf16(v_ref[0, n * t:(n + 1) * t, :])

    qt = q_ref[0].astype(F32).T
    row = lax.broadcasted_iota(I32, (LANES, t), 0)
    key = lax.broadcasted_iota(I32, (t, t), 0)
    qry = lax.broadcasted_iota(I32, (t, t), 1)
    strict = key < qry
    tri = jnp.where(qry >= key, 1.0, 0.0).astype(BF16)
    qs = [_head_queries(qt, row, hh, scale) for hh in range(n_heads)]

    def blocks(tiles):
        zs = [[_dot(k_ref[0, pl.ds(pl.multiple_of(j * t, t), t), _pair_lanes(hh)], qs[hh])
               for hh in range(n_heads)] for j, _ in tiles]
        csums = []
        for (_, diagonal), zt in zip(tiles, zs):
            row_sums = []
            for z in zt:
                sp = jnp.maximum(z, 0.0) + jnp.log(1.0 + jnp.exp(-jnp.abs(z)))
                if diagonal:
                    sp = jnp.where(strict, sp, 0.0)
                hi = sp.astype(BF16)
                lo = (sp - hi.astype(F32)).astype(BF16)
                row_sums.append(_dot(tri, hi) + _dot(tri, lo))
            csums.append(row_sums)
        out = []
        for (j, diagonal), zt, ct in zip(tiles, zs, csums):
            pvs = []
            for hh in range(n_heads):
                a = jnp.exp(zt[hh] - ct[hh])
                if diagonal:
                    a = jnp.where(strict, a, 0.0)
                pvs.append(_dot(vt_sc[j, _head_dims(hh), :], a.astype(BF16)))
            out.append((pvs, [c[0:1, :] for c in ct]))
        return out

    has_prev = qi > 0
    (pv_d, tot_d), (pv_p, tot_p) = blocks([(qi, True), (jnp.maximum(qi - 1, 0), False)])
    accs = tuple(d + jnp.where(has_prev, p * jnp.exp(-c), 0.0) for d, p, c in zip(pv_d, pv_p, tot_d))
    carries = tuple(c + jnp.where(has_prev, p, 0.0) for c, p in zip(tot_d, tot_p))

    def lowest(carries):
        return jnp.min(functools.reduce(jnp.minimum, carries))

    def cond(state):
        j, cmin, _, _ = state
        return (j >= 0) & (cmin < SB_ZERO_LOG)

    def body(state):
        j, _, carries, accs = state
        (pvs, totals), = blocks([(j, False)])
        accs = tuple(acc + pv * jnp.exp(-c) for acc, pv, c in zip(accs, pvs, carries))
        carries = tuple(c + total for c, total in zip(carries, totals))
        return j - 1, lowest(carries), carries, accs

    state = lax.while_loop(cond, body, (qi - 2, lowest(carries), carries, accs))
    out_t = jnp.concatenate(list(state[3]), axis=0)
    o_ref[0] = out_t.T.astype(o_ref.dtype)


def _stickbreak(qkv3, col0):
    bsz, seq, _ = qkv3.shape
    width = N_SB_HEADS * HEAD_DIM
    t = ATTN_ROWS
    return pl.pallas_call(
        _sb_kernel,
        grid=(bsz, seq // t),
        in_specs=[pl.BlockSpec((1, t, width), lambda b, i: (b, i, col0)),
                  pl.BlockSpec((1, seq, width), lambda b, i: (b, 0, col0 + 1)),
                  pl.BlockSpec((1, seq, width), lambda b, i: (b, 0, col0 + 2))],
        out_specs=pl.BlockSpec((1, t, width), lambda b, i: (b, i, 0)),
        out_shape=jax.ShapeDtypeStruct((bsz, seq, width), BF16),
        scratch_shapes=[pltpu.VMEM((seq // t, width, t), BF16)],
        compiler_params=_params(("arbitrary", "arbitrary"), 48),
        name="stickbreak",
    )(qkv3, qkv3, qkv3)


def _memkv_kernel(m_ref, w_ref, o_ref):
    o_ref[...] = _dot(m_ref[...].astype(BF16), w_ref[...]).astype(BF16)


def _memkv(mem2, w_bf16):
    n, d = mem2.shape
    width = w_bf16.shape[1]
    tm = min(n, 512)
    return pl.pallas_call(
        _memkv_kernel,
        grid=(n // tm,),
        in_specs=[pl.BlockSpec((tm, d), lambda i: (i, 0)),
                  pl.BlockSpec((d, width), lambda i: (0, 0))],
        out_specs=pl.BlockSpec((tm, width), lambda i: (i, 0)),
        out_shape=jax.ShapeDtypeStruct((n, width), BF16),
        compiler_params=_params(("arbitrary",), 32),
        name="memkv",
    )(mem2, w_bf16)


def _mem_kernel(q_ref, k_ref, v_ref, o_ref, vt_sc):
    t, width = q_ref.shape[1], q_ref.shape[2]
    n_heads = width // HEAD_DIM
    scale = HEAD_DIM ** -0.5

    @pl.when(pl.program_id(1) == 0)
    def _():
        vt_sc[...] = _transposed_bf16(v_ref[0])

    qt = q_ref[0].astype(F32).T
    row = lax.broadcasted_iota(I32, (LANES, t), 0)
    qs = [_head_queries(qt, row, hh, scale) for hh in range(n_heads)]
    scores = [_dot(k_ref[0, :, _pair_lanes(hh)], qs[hh]) for hh in range(n_heads)]
    probs = []
    for s in scores:
        e = jnp.exp(s - _fold_keys(s, jnp.maximum, jnp.max))
        probs.append((e * (1.0 / _fold_keys(e, jnp.add, jnp.sum))).astype(BF16))
    out_t = jnp.concatenate([_dot(vt_sc[_head_dims(hh), :], probs[hh]) for hh in range(n_heads)], axis=0)
    o_ref[0] = out_t.T.astype(o_ref.dtype)


def _mem_attention(qkv3, kv3, qcol):
    bsz, seq, _ = qkv3.shape
    n_mem = kv3.shape[1]
    width = N_MEM_HEADS * HEAD_DIM
    t = MEM_ROWS
    return pl.pallas_call(
        _mem_kernel,
        grid=(bsz, seq // t),
        in_specs=[pl.BlockSpec((1, t, width), lambda b, i: (b, i, qcol)),
                  pl.BlockSpec((1, n_mem, width), lambda b, i: (b, 0, 0)),
                  pl.BlockSpec((1, n_mem, width), lambda b, i: (b, 0, 1))],
        out_specs=pl.BlockSpec((1, t, width), lambda b, i: (b, i, 0)),
        out_shape=jax.ShapeDtypeStruct((bsz, seq, width), BF16),
        scratch_shapes=[pltpu.VMEM((width, n_mem), BF16)],
        compiler_params=_params(("arbitrary", "arbitrary"), 32),
        name="mem_attention",
    )(qkv3, kv3, kv3)


def _merge_kernel(ya_ref, yb_ref, ym_ref, x_ref, wg_ref, wa_ref, wb_ref, wm_ref, wo_ref,
                  lng_ref, lnb_ref, wrh_ref, br_ref, x1_ref, route_ref, tcnt_ref, h_sc, cnt_ref,
                  *, alpha, n_experts):
    step = pl.program_id(0)
    tm, d = x_ref.shape
    slot = lax.rem(step, 2)

    @pl.when(step == 0)
    def _():
        cnt_ref[...] = jnp.zeros(cnt_ref.shape, F32)
        h_sc[...] = jnp.zeros(h_sc.shape, F32)

    x = x_ref[...]
    xb = x.astype(BF16)
    branches = ((ya_ref, wa_ref), (yb_ref, wb_ref), (ym_ref, wm_ref))

    def branch_term(k):
        gate_logits = _dot(xb, wg_ref[:, k * d:(k + 1) * d])
        return gate_logits, _dot(branches[k][0][...], branches[k][1][...])

    x1 = _layer_norm(h_sc[1 - slot], lng_ref[...], lnb_ref[...])
    x1_ref[...] = x1
    xh = x1.astype(BF16)
    xl = (x1 - xh.astype(F32)).astype(BF16)

    first_term = branch_term(0)

    by_hi = _dot(xh, wrh_ref[...])
    logits = (by_hi[:, :LANES] + (by_hi[:, LANES:] + _dot(xl, wrh_ref[:, :LANES]))
              + br_ref[...])

    merged = None
    for k in range(len(branches)):
        gate_logits, y = first_term if k == 0 else branch_term(k)
        term = _sigmoid(gate_logits) * y
        merged = term if merged is None else merged + term
    h_sc[slot] = alpha * x + _dot(merged.astype(BF16), wo_ref[...])

    col = lax.broadcasted_iota(I32, (tm, LANES), 1)
    colf = col.astype(F32)
    big = float(LANES)
    gmask = (col >= n_experts) & (col < n_experts + N_GROUPS)
    lg = jnp.where(gmask, logits, -jnp.inf)
    gmax = jnp.max(lg, axis=1, keepdims=True)
    gidx = jnp.min(jnp.where(lg == gmax, colf, big), axis=1, keepdims=True) - n_experts
    g_p = 1.0 / jnp.sum(jnp.where(gmask, jnp.exp(logits - gmax), 0.0), axis=1, keepdims=True)
    lo_col = gidx * EXPERTS_PER_GROUP
    emask = (colf >= lo_col) & (colf < lo_col + EXPERTS_PER_GROUP)
    le = jnp.where(emask, logits, -jnp.inf)
    l1 = jnp.max(le, axis=1, keepdims=True)
    i1 = jnp.min(jnp.where(le == l1, colf, big), axis=1, keepdims=True)
    le2 = jnp.where(colf == i1, -jnp.inf, le)
    l2 = jnp.max(le2, axis=1, keepdims=True)
    i2 = jnp.min(jnp.where(le2 == l2, colf, big), axis=1, keepdims=True)
    e2 = jnp.exp(l2 - l1)
    gate1 = g_p * (1.0 / (1.0 + e2))
    gate2 = g_p * (e2 / (1.0 + e2))

    oh1 = colf == i1
    oh2 = colf == i2
    cnt = jnp.where((oh1 | oh2) & (step > 0), 1.0, 0.0)
    rr = lax.broadcasted_iota(I32, (tm, tm), 0)
    cc = lax.broadcasted_iota(I32, (tm, tm), 1)
    before = jnp.where(cc < rr, 1.0, 0.0).astype(BF16)
    base = _dot(before, cnt.astype(BF16)) + cnt_ref[0:1, :]
    rank1 = jnp.sum(jnp.where(oh1, base, 0.0), axis=1, keepdims=True)
    rank2 = jnp.sum(jnp.where(oh2, base, 0.0), axis=1, keepdims=True)
    tile_cnt = jnp.sum(cnt, axis=0, keepdims=True)
    cnt_ref[...] = cnt_ref[...] + tile_cnt
    tcnt_ref[0] = jnp.broadcast_to(tile_cnt, tcnt_ref.shape[1:])

    route = jnp.zeros((tm, LANES), F32)
    for k, val in enumerate((i1, i2, rank1, rank2, gate1, gate2)):
        route = jnp.where(col == k, val, route)
    route_ref[...] = route


def _merge(ya, yb, ym, x2, w_gates, wa, wb, wm, wo, ln_g, ln_b, wr_split, b_r, alpha, n_experts):
    n, d = x2.shape
    tm = MERGE_ROWS
    n_tiles = n // tm
    row_in = lambda w: pl.BlockSpec((tm, w), lambda i: (jnp.minimum(i, n_tiles - 1), 0))
    row_out = lambda w: pl.BlockSpec((tm, w), lambda i: (jnp.maximum(i - 1, 0), 0))
    full = lambda a: pl.BlockSpec(a.shape, lambda i: (0,) * a.ndim)
    return pl.pallas_call(
        functools.partial(_merge_kernel, alpha=alpha, n_experts=n_experts),
        grid=(n_tiles + 1,),
        in_specs=[row_in(ya.shape[1]), row_in(yb.shape[1]), row_in(ym.shape[1]), row_in(d),
                  full(w_gates), full(wa), full(wb), full(wm), full(wo), full(ln_g), full(ln_b),
                  full(wr_split), full(b_r)],
        out_specs=[row_out(d), row_out(LANES),
                   pl.BlockSpec((1, 8, LANES), lambda i: (jnp.maximum(i - 1, 0), 0, 0))],
        out_shape=[jax.ShapeDtypeStruct((n, d), F32),
                   jax.ShapeDtypeStruct((n, LANES), F32),
                   jax.ShapeDtypeStruct((n_tiles, 8, LANES), F32)],
        scratch_shapes=[pltpu.VMEM((2, tm, d), F32), pltpu.VMEM((8, LANES), F32)],
        compiler_params=_params(("arbitrary",), 56),
        name="merge",
    )(ya, yb, ym, x2, w_gates, wa, wb, wm, wo, ln_g, ln_b, wr_split, b_r)


def _row_copy(src_hbm, row, buf, sem, r):
    return pltpu.make_async_copy(src_hbm.at[pl.ds(row, 1), :], buf.at[pl.ds(r, 1), :], sem)


def _gather_rows(src_hbm, idx_ref, buf, sem, n_rows, unrolled):
    if unrolled:
        for r in range(n_rows):
            _row_copy(src_hbm, idx_ref[0, 0, r], buf, sem, r).start()
    else:
        def issue(r, _):
            _row_copy(src_hbm, idx_ref[0, 0, r], buf, sem, r).start()
            return 0
        lax.fori_loop(0, n_rows, issue, 0)


def _wait_rows(src_hbm, buf, sem, n_rows):
    pltpu.make_async_copy(src_hbm.at[pl.ds(0, n_rows), :], buf, sem).wait()


DISPATCH_ROWS = 512
CHUNK = 8


def _chunk_copy(buf, xpad_hbm, sem, src_row, dst_row):
    return pltpu.make_async_copy(buf.at[pl.ds(src_row, CHUNK), :], xpad_hbm.at[pl.ds(dst_row, CHUNK), :], sem)


def _dispatch_kernel(nch_ref, off_ref, dst_ref, zrow_ref, zflag_ref, nu_ref, base_ref, route_ref, x_ref,
                     xpad_hbm, dest_ref, sorted_even, sorted_odd, zero_buf, sem, zsem):
    i = pl.program_id(0)
    n_steps = pl.num_programs(0)
    n_exp = base_ref.shape[1]
    td = x_ref.shape[0]
    rb = sorted_even.shape[0]
    zrows = zero_buf.shape[0]
    bufs = (sorted_even, sorted_odd)
    parity = lax.rem(i, 2)

    @pl.when(i == 0)
    def _():
        zero_buf[...] = jnp.zeros(zero_buf.shape, F32)

        def zero_block(row, start):
            cp = pltpu.make_async_copy(zero_buf, xpad_hbm.at[pl.ds(pl.multiple_of(row, zrows), zrows), :], zsem)
            cp.start() if start else cp.wait()

        for start in (True, False):
            for e in range(n_exp):
                pl.when(zflag_ref[e] == 1)(functools.partial(zero_block, zrow_ref[e], start))

            def unused(blk, _):
                zero_block(blk * zrows, start)
                return 0
            lax.fori_loop(nu_ref[0], xpad_hbm.shape[0] // zrows, unused, 0)

    rt = route_ref[...].T
    eid = lax.broadcasted_iota(I32, (n_exp, td), 0).astype(F32)
    rowid = lax.broadcasted_iota(I32, (rb, td), 0).astype(F32)
    hit = None
    dests = []
    for k in range(2):
        mine = eid == rt[k:k + 1, :]
        rank = rt[2 + k:3 + k, :]
        pos = jnp.sum(jnp.where(mine, base_ref[0, :, 0:1], 0.0), axis=0, keepdims=True) + rank
        dests.append(jnp.sum(jnp.where(mine, base_ref[0, :, 1:2], 0.0), axis=0, keepdims=True) + rank)
        hit = (rowid == pos) if hit is None else hit | (rowid == pos)
    dest_ref[0] = jnp.concatenate(dests + [jnp.zeros((8 - len(dests), td), F32)], axis=0).astype(I32)
    perm = jnp.where(hit, 1.0, 0.0).astype(BF16)
    rows_sorted = _dot(perm, x_ref[...].astype(BF16))

    def runs(tile, buf, s, start):
        for e in range(n_exp):
            k = tile * n_exp + e
            src0, dst0 = off_ref[k], dst_ref[k]

            def one(j, _):
                cp = _chunk_copy(buf, xpad_hbm, s, pl.multiple_of(src0 + j * CHUNK, CHUNK),
                                 pl.multiple_of(dst0 + j * CHUNK, CHUNK))
                cp.start() if start else cp.wait()
                return 0
            lax.fori_loop(0, nch_ref[k], one, 0)

    def wait_runs(tile, buf, s):
        k_last = tile * n_exp + n_exp - 1
        total = pl.multiple_of(off_ref[k_last] + nch_ref[k_last] * CHUNK, CHUNK)
        pltpu.make_async_copy(buf.at[pl.ds(0, total), :], xpad_hbm.at[pl.ds(0, total), :], s).wait()

    def step(cur):
        bufs[cur][...] = rows_sorted

        @pl.when(i > 0)
        def _():
            wait_runs(i - 1, bufs[1 - cur], sem.at[1 - cur])

        runs(i, bufs[cur], sem.at[cur], True)

        @pl.when(i + 1 == n_steps)
        def _():
            wait_runs(i, bufs[cur], sem.at[cur])

    for cur in range(2):
        pl.when(parity == cur)(functools.partial(step, cur))


def _dispatch(x1, route, nch, off8, dst, zrow, zflag, n_used, base, n_rows_out):
    n, d = x1.shape
    td = DISPATCH_ROWS
    n_tiles = n // td
    n_exp = base.shape[1]
    rb = 2 * td + n_exp * CHUNK
    grid_spec = pltpu.PrefetchScalarGridSpec(
        num_scalar_prefetch=6,
        grid=(n_tiles,),
        in_specs=[pl.BlockSpec((1, n_exp, 2), lambda i, *_: (i, 0, 0)),
                  pl.BlockSpec((td, LANES), lambda i, *_: (i, 0)),
                  pl.BlockSpec((td, d), lambda i, *_: (i, 0))],
        out_specs=[pl.BlockSpec(memory_space=pl.ANY),
                   pl.BlockSpec((1, 8, td), lambda i, *_: (i, 0, 0))],
        scratch_shapes=[pltpu.VMEM((rb, d), F32), pltpu.VMEM((rb, d), F32),
                        pltpu.VMEM((MOE_BLOCK, d), F32),
                        pltpu.SemaphoreType.DMA((2,)), pltpu.SemaphoreType.DMA(())])
    return pl.pallas_call(
        _dispatch_kernel,
        grid_spec=grid_spec,
        out_shape=[jax.ShapeDtypeStruct((n_rows_out, d), F32),
                   jax.ShapeDtypeStruct((n_tiles, 8, td), I32)],
        compiler_params=_params(("arbitrary",), 48),
        name="dispatch",
    )(nch, off8, dst, zrow, zflag, n_used, base, route, x1)


def _experts_kernel(be_ref, nu_ref, x_ref, wg_ref, wu_ref, wd_ref, o_ref, wg_sc, wu_sc, wd_sc):
    b = pl.program_id(0)
    n_used = nu_ref[0]

    @pl.when(b < n_used)
    def _():
        changed = (b == 0) | (be_ref[b] != be_ref[jnp.maximum(b - 1, 0)])

        @pl.when(changed)
        def _():
            wg_sc[...] = wg_ref[0].astype(BF16)
            wu_sc[...] = wu_ref[0].astype(BF16)
            wd_sc[...] = wd_ref[0].astype(BF16)

        xb = x_ref[...].astype(BF16)
        y = None
        for c in range(0, wg_sc.shape[1], MXU_WIDTH):
            g = _dot(xb, wg_sc[:, c:c + MXU_WIDTH])
            u = _dot(xb, wu_sc[:, c:c + MXU_WIDTH])
            h = (g * _sigmoid(g) * u).astype(BF16)
            part = _dot(h, wd_sc[c:c + MXU_WIDTH, :])
            y = part if y is None else y + part
        o_ref[...] = y

    @pl.when(b >= n_used)
    def _():
        o_ref[...] = jnp.zeros(o_ref.shape, F32)


def _experts(x_pad, block_expert, n_used, w_gate, w_up, w_down):
    d = x_pad.shape[1]
    n_blocks = block_expert.shape[0]
    rows = MOE_BLOCK
    d_exp = w_gate.shape[-1]
    grid_spec = pltpu.PrefetchScalarGridSpec(
        num_scalar_prefetch=2,
        grid=(n_blocks,),
        in_specs=[pl.BlockSpec((rows, d), lambda b, be, nu: (jnp.minimum(b, jnp.maximum(nu[0] - 1, 0)), 0)),
                  pl.BlockSpec((1, d, d_exp), lambda b, be, nu: (be[b], 0, 0)),
                  pl.BlockSpec((1, d, d_exp), lambda b, be, nu: (be[b], 0, 0)),
                  pl.BlockSpec((1, d_exp, d), lambda b, be, nu: (be[b], 0, 0))],
        out_specs=pl.BlockSpec((rows, d), lambda b, be, nu: (b, 0)),
        scratch_shapes=[pltpu.VMEM((d, d_exp), BF16),
                        pltpu.VMEM((d, d_exp), BF16),
                        pltpu.VMEM((d_exp, d), BF16)])
    return pl.pallas_call(
        _experts_kernel,
        grid_spec=grid_spec,
        out_shape=jax.ShapeDtypeStruct((n_blocks * rows, d), F32),
        compiler_params=_params(("arbitrary",), 44),
        name="experts",
    )(block_expert, n_used, x_pad, w_gate, w_up, w_down)


def _combine_kernel(idx_ref, idxn_ref, y_hbm, x1_ref, route_ref, lng_ref, lnb_ref, o_ref,
                    ybuf_even, ybuf_odd, sem, *, alpha):
    i = pl.program_id(0)
    n_steps = pl.num_programs(0)
    tm = x1_ref.shape[0]
    bufs = (ybuf_even, ybuf_odd)
    parity = lax.rem(i, 2)

    @pl.when(i == 0)
    def _():
        _gather_rows(y_hbm, idx_ref, bufs[0], sem.at[0], 2 * tm, False)

    def compute(cur):
        _gather_rows(y_hbm, idxn_ref, bufs[1 - cur], sem.at[1 - cur], 2 * tm, True)
        _wait_rows(y_hbm, bufs[cur], sem.at[cur], 2 * tm)
        ffn = (route_ref[:, 4:5] * bufs[cur][0:tm, :] + route_ref[:, 5:6] * bufs[cur][tm:2 * tm, :])
        o_ref[...] = _layer_norm(alpha * x1_ref[...] + ffn, lng_ref[...], lnb_ref[...])

        @pl.when(i + 1 == n_steps)
        def _():
            _wait_rows(y_hbm, bufs[1 - cur], sem.at[1 - cur], 2 * tm)

    for cur in range(2):
        pl.when(parity == cur)(functools.partial(compute, cur))


def _combine(y_pad, dest, x1, route, ln_g, ln_b, alpha):
    n, d = x1.shape
    tm = COMBINE_ROWS
    n_steps = n // tm
    idx3 = dest.reshape(n_steps, tm, 2).transpose(0, 2, 1).reshape(n_steps, 1, 2 * tm)
    smem_idx = lambda f: pl.BlockSpec((1, 1, 2 * tm), f, memory_space=pltpu.SMEM)
    return pl.pallas_call(
        functools.partial(_combine_kernel, alpha=alpha),
        grid=(n_steps,),
        in_specs=[smem_idx(lambda i: (i, 0, 0)),
                  smem_idx(lambda i: (jnp.minimum(i + 1, n_steps - 1), 0, 0)),
                  pl.BlockSpec(memory_space=pl.ANY),
                  pl.BlockSpec((tm, d), lambda i: (i, 0)),
                  pl.BlockSpec((tm, LANES), lambda i: (i, 0)),
                  pl.BlockSpec((1, d), lambda i: (0, 0)),
                  pl.BlockSpec((1, d), lambda i: (0, 0))],
        out_specs=pl.BlockSpec((tm, d), lambda i: (i, 0)),
        out_shape=jax.ShapeDtypeStruct((n, d), F32),
        scratch_shapes=[pltpu.VMEM((2 * tm, d), F32), pltpu.VMEM((2 * tm, d), F32),
                        pltpu.SemaphoreType.DMA((2,))],
        compiler_params=_params(("arbitrary",), 32),
        name="combine",
    )(idx3, idx3, y_pad, x1, route, ln_g, ln_b)


def _split_bf16(w):
    hi = w.astype(BF16)
    return hi, (w - hi.astype(F32)).astype(BF16)


def _layer(x, mem, tbl, rel_bias, w_in, w_mem_kv, w_br_moba, w_br_sb, w_br_mem, w_out, ln1_g, ln1_b,
           w_rg, b_rg, w_re, b_re, w_gate, w_up, w_down, ln2_g, ln2_b, alpha):
    bsz, seq, d = x.shape
    n = bsz * seq
    n_experts = w_re.shape[1]
    moba_w, sb_w, mem_w = N_MOBA_HEADS * HEAD_DIM, N_SB_HEADS * HEAD_DIM, N_MEM_HEADS * HEAD_DIM
    n_qkv = 3 * moba_w + 3 * sb_w + mem_w
    assert w_in.shape[1] == n_qkv + 3 * d and n_experts + N_GROUPS <= LANES and moba_w == sb_w

    x2 = x.reshape(n, d)
    qkv = _proj(x2, w_in[:, :n_qkv].astype(BF16))
    qkv3 = qkv.reshape(bsz, seq, n_qkv)
    y_a = _moba(qkv3, tbl, rel_bias, 0)
    y_b = _stickbreak(qkv3, 3 * moba_w // sb_w)
    kv = _memkv(mem.reshape(-1, d), w_mem_kv.astype(BF16)).reshape(bsz, mem.shape[1], 2 * mem_w)
    y_m = _mem_attention(qkv3, kv, (3 * moba_w + 3 * sb_w) // mem_w)

    w_r = jnp.zeros((d, LANES), F32).at[:, :n_experts].set(w_re).at[:, n_experts:n_experts + N_GROUPS].set(w_rg)
    b_r = jnp.zeros((1, LANES), F32).at[0, :n_experts].set(b_re).at[0, n_experts:n_experts + N_GROUPS].set(b_rg)
    wr_split = jnp.concatenate(_split_bf16(w_r), axis=1)
    x1, route, tile_counts = _merge(
        y_a.reshape(n, moba_w), y_b.reshape(n, sb_w), y_m.reshape(n, mem_w), x2, w_in[:, n_qkv:].astype(BF16),
        w_br_moba.astype(BF16), w_br_sb.astype(BF16), w_br_mem.astype(BF16), w_out.astype(BF16),
        ln1_g.reshape(1, d), ln1_b.reshape(1, d), wr_split, b_r, alpha, n_experts)

    rows, td = MOE_BLOCK, DISPATCH_ROWS
    assert td == MERGE_ROWS
    n_tiles = n // td
    tile_cnt = tile_counts[:, 0, :n_experts].astype(I32)
    n_chunks = (tile_cnt + CHUNK - 1) // CHUNK
    run_rows = n_chunks * CHUNK
    run_rank = jnp.cumsum(tile_cnt, axis=0) - tile_cnt
    run_off = jnp.cumsum(run_rows, axis=0) - run_rows
    sorted_off = jnp.cumsum(run_rows, axis=1) - run_rows
    expert_rows = jnp.sum(run_rows, axis=0)
    padded = (expert_rows + rows - 1) // rows * rows
    pend = jnp.cumsum(padded)
    pstart = pend - padded
    run_dst = pstart[None, :] + run_off
    n_blocks = (2 * n + n_tiles * n_experts * (CHUNK - 1)) // rows + n_experts
    block_start = jnp.arange(n_blocks, dtype=I32) * rows
    block_expert = jnp.minimum(jnp.sum((pend[None, :] <= block_start[:, None]).astype(I32), axis=1),
                               n_experts - 1)
    n_used = (pend[-1] // rows).astype(I32).reshape(1)
    base = jnp.stack([sorted_off - run_rank, run_dst - run_rank], axis=-1).astype(F32)
    x_pad, dest8 = _dispatch(x1, route, n_chunks.reshape(-1), sorted_off.reshape(-1), run_dst.reshape(-1),
                             jnp.maximum(pend - rows, 0), (padded > 0).astype(I32), n_used, base,
                             n_blocks * rows)
    dest = dest8[:, 0:2, :].transpose(0, 2, 1).reshape(n, 2)

    y_pad = _experts(x_pad, block_expert, n_used, w_gate, w_up, w_down)
    out = _combine(y_pad, dest, x1, route, ln2_g.reshape(1, d), ln2_b.reshape(1, d), alpha)
    return out.reshape(bsz, seq, d)


def kernel(x, mem, w_in, w_mem_kv, rel_bias, w_br_moba, w_br_sb, w_br_mem, w_out, ln1_g, ln1_b,
           w_router_group, b_router_group, w_router_expert, b_router_expert,
           w_gate, w_up, w_down, ln2_g, ln2_b):
    depth = w_in.shape[0]
    alpha = (2.0 * depth) ** 0.25
    tbl = _bias_table(rel_bias)
    for l in range(depth):
        x = _layer(x, mem, tbl, rel_bias, w_in[l], w_mem_kv[l], w_br_moba[l], w_br_sb[l], w_br_mem[l],
                   w_out[l], ln1_g[l], ln1_b[l], w_router_group[l], b_router_group[l],
                   w_router_expert[l], b_router_expert[l], w_gate[l], w_up[l], w_down[l],
                   ln2_g[l], ln2_b[l], alpha)
    return x
```

```python
import functools
import math

import jax
import jax.numpy as jnp
from jax import lax
from jax.experimental import pallas as pl
from jax.experimental.pallas import tpu as pltpu

F32, BF16, I32 = jnp.float32, jnp.bfloat16, jnp.int32

HEAD_DIM = 64
N_MOBA_HEADS = 6
N_SB_HEADS = 6
N_MEM_HEADS = 4
MOBA_BLOCK = 256
MOBA_TOPK = 3
N_BUCKETS = 32
MAX_DISTANCE = 128
N_GROUPS = 4
EXPERTS_PER_GROUP = 8
LN_EPS = 1e-5
NEG = -1e30

LANES = 128
VMEM_BYTES = 64 * 1024 * 1024
PAIR = LANES // HEAD_DIM

SB_ZERO_LOG = 110.0

PROJ_ROWS = 1024
MERGE_ROWS = 512
ATTN_ROWS = 256
MEM_ROWS = 1024
COMBINE_ROWS = 256
MOE_BLOCK = 512
EXPERT_INPUT_SLOTS = 3


def _params(semantics, vmem_mb):
    return pltpu.CompilerParams(dimension_semantics=semantics,
                                vmem_limit_bytes=min(vmem_mb * 1024 * 1024, VMEM_BYTES))


def _dot(a, b):
    return jnp.dot(a, b, preferred_element_type=F32)


def _layer_norm(h, g, b):
    mu = jnp.mean(h, axis=-1, keepdims=True)
    d = h - mu
    var = jnp.mean(d * d, axis=-1, keepdims=True)
    return d * lax.rsqrt(var + LN_EPS) * g + b


def _sigmoid(x):
    return 1.0 / (1.0 + jnp.exp(-x))


def _head_rows(qt, row, hh, scale):
    keep = (row >= hh * HEAD_DIM) & (row < (hh + 1) * HEAD_DIM)
    return jnp.where(keep, qt * scale, 0.0).astype(BF16)


def _transposed_bf16(x):
    return x.astype(F32).T.astype(BF16)


def _proj_kernel(x_ref, wq_ref, qkv_ref):
    qkv_ref[...] = _dot(x_ref[...].astype(BF16), wq_ref[...]).astype(BF16)


def _proj(x2, w_qkv):
    n, d = x2.shape
    n_qkv = w_qkv.shape[1]
    tm = PROJ_ROWS
    return pl.pallas_call(
        _proj_kernel,
        grid=(n // tm,),
        in_specs=[pl.BlockSpec((tm, d), lambda i: (i, 0)),
                  pl.BlockSpec((d, n_qkv), lambda i: (0, 0))],
        out_specs=pl.BlockSpec((tm, n_qkv), lambda i: (i, 0)),
        out_shape=jax.ShapeDtypeStruct((n, n_qkv), BF16),
        compiler_params=_params(("arbitrary",), 40),
        name="proj",
    )(x2, w_qkv)


def _t5_bucket(rel):
    rel = jnp.maximum(rel, 0)
    max_exact = N_BUCKETS // 2
    rel_f = jnp.maximum(rel, 1).astype(F32)
    large = max_exact + (jnp.log(rel_f / max_exact) / math.log(MAX_DISTANCE / max_exact)
                         * (N_BUCKETS - max_exact)).astype(I32)
    large = jnp.minimum(large, N_BUCKETS - 1)
    return jnp.where(rel < max_exact, rel, large)


def _bias_table_kernel(rb_ref, o_ref):
    h = pl.program_id(0)
    blk = o_ref.shape[2]
    j = lax.broadcasted_iota(I32, (2 * blk, blk), 0)
    i = lax.broadcasted_iota(I32, (2 * blk, blk), 1)
    bucket = _t5_bucket(blk + i - j)
    acc = jnp.zeros((2 * blk, blk), F32)
    for b in range(N_BUCKETS):
        acc = jnp.where(bucket == b, rb_ref[b, h], acc)
    o_ref[0] = acc


def _bias_table(rel_bias):
    n_heads = rel_bias.shape[1]
    blk = MOBA_BLOCK
    return pl.pallas_call(
        _bias_table_kernel,
        grid=(n_heads,),
        in_specs=[pl.BlockSpec(memory_space=pltpu.SMEM)],
        out_specs=pl.BlockSpec((1, 2 * blk, blk), lambda h: (h, 0, 0)),
        out_shape=jax.ShapeDtypeStruct((n_heads, 2 * blk, blk), F32),
        compiler_params=_params(("arbitrary",), 32),
        name="bias_table",
    )(rel_bias)


def _fold_keys(x, op, final):
    while x.shape[0] > 8 and x.shape[0] % 2 == 0:
        half = x.shape[0] // 2
        x = op(x[:half], x[half:])
    return final(x, axis=0, keepdims=True)


def _head_queries(qt, row, hh, scale):
    p, sub = divmod(hh, PAIR)
    return _head_rows(qt[p * LANES:(p + 1) * LANES, :], row, sub, scale)


def _pair_lanes(hh):
    p = hh // PAIR
    return slice(p * LANES, (p + 1) * LANES)


def _head_dims(hh):
    return slice(hh * HEAD_DIM, (hh + 1) * HEAD_DIM)


def _moba_kernel(rb_ref, q_ref, k_ref, v_ref, tbl_ref, o_ref, kmean_sc, vt_sc, sel_sc, s_sc, p_sc):
    own = pl.program_id(1)
    blk, width = q_ref.shape[1], q_ref.shape[2]
    n_heads = width // HEAD_DIM
    nb = k_ref.shape[1] // blk
    scale = HEAD_DIM ** -0.5

    @pl.when(own == 0)
    def _():
        for n in range(nb):
            kb = k_ref[0, n * blk:(n + 1) * blk, :].astype(F32)
            kmean_sc[n:n + 1, :] = jnp.mean(kb, axis=0, keepdims=True)
            vt_sc[n] = _transposed_bf16(v_ref[0, n * blk:(n + 1) * blk, :])

    qt = q_ref[0].astype(F32).T
    row = lax.broadcasted_iota(I32, (LANES, blk), 0)
    blk_id = lax.broadcasted_iota(I32, (nb, blk), 0)
    valid = blk_id < own
    causal = (lax.broadcasted_iota(I32, (blk, blk), 0) <= lax.broadcasted_iota(I32, (blk, blk), 1))
    kmean = kmean_sc[...].astype(BF16)

    qs = [_head_queries(qt, row, hh, scale) for hh in range(n_heads)]
    gates = [_dot(kmean[:, _pair_lanes(hh)], _head_queries(qt, row, hh, 1.0)) for hh in range(n_heads)]

    def scores(n):
        offn = pl.multiple_of(n * blk, blk)
        return [_dot(k_ref[0, pl.ds(offn, blk), _pair_lanes(hh)], qs[hh]) for hh in range(n_heads)]

    own_scores = scores(own)
    prev_scores = scores(jnp.maximum(own - 1, 0))

    blk_f = blk_id.astype(F32)
    for hh in range(n_heads):
        gate = jnp.where(valid, gates[hh], NEG)
        picked = jnp.zeros((nb, blk), jnp.bool_)
        for _ in range(MOBA_TOPK):
            top = _fold_keys(gate, jnp.maximum, jnp.max)
            first = _fold_keys(jnp.where(gate == top, blk_f, float(nb)), jnp.minimum, jnp.min)
            hit = blk_f == first
            picked = picked | hit
            gate = jnp.where(hit, -jnp.inf, gate)
        sel_sc[hh] = jnp.where(valid & picked, 0.0, NEG)

    prev = jnp.maximum(own - 1, 0)
    no_prev = jnp.where(own > 0, 0.0, NEG)
    stats = []
    for hh, (s_own, s_prev) in enumerate(zip(own_scores, prev_scores)):
        s_own = jnp.where(causal, s_own + tbl_ref[hh, blk:, :], NEG)
        s_prev = s_prev + tbl_ref[hh, :blk, :]
        term = sel_sc[hh, pl.ds(prev, 1), :] + no_prev
        m0 = jnp.maximum(_fold_keys(s_own, jnp.maximum, jnp.max),
                         _fold_keys(s_prev, jnp.maximum, jnp.max) + term)
        p_own = jnp.exp(s_own - m0)
        p_prev = jnp.exp(s_prev - (m0 - term))
        l0 = _fold_keys(p_own, jnp.add, jnp.sum) + _fold_keys(p_prev, jnp.add, jnp.sum)
        stats.append((m0, l0, p_own.astype(BF16), p_prev.astype(BF16)))
    state = []
    for hh, (m0, l0, p_own, p_prev) in enumerate(stats):
        state += [m0, l0, _dot(vt_sc[own, _head_dims(hh), :], p_own) + _dot(vt_sc[prev, _head_dims(hh), :], p_prev)]
    state = tuple(state)

    def softmax_step(s, per_query, m_run, l_run):
        m_new = jnp.maximum(m_run, _fold_keys(s, jnp.maximum, jnp.max) + per_query)
        a = jnp.exp(m_run - m_new)
        pn = jnp.exp(s - (m_new - per_query))
        return m_new, a * l_run + _fold_keys(pn, jnp.add, jnp.sum), a, pn.astype(BF16)

    n_far = own - 1
    last_far = jnp.maximum(n_far - 1, 0)
    for hh, s in enumerate(scores(0)):
        s_sc[hh] = s
    p_sc[...] = jnp.zeros(p_sc.shape, BF16)

    def far_block(n, carry):
        state, a_prev = carry[:3 * n_heads], carry[3 * n_heads:]
        n_prev = jnp.maximum(n - 1, 0)
        pv = [_dot(vt_sc[n_prev, _head_dims(hh), :], p_sc[hh]) for hh in range(n_heads)]
        s_next = scores(jnp.minimum(n + 1, last_far))
        stats = [softmax_step(s_sc[hh], rb_ref[N_BUCKETS - 1, hh] + sel_sc[hh, pl.ds(n, 1), :],
                              state[3 * hh], state[3 * hh + 1]) for hh in range(n_heads)]
        new, a_new = [], []
        for hh, (m_new, l_new, a, pn) in enumerate(stats):
            p_sc[hh] = pn
            s_sc[hh] = s_next[hh]
            new += [m_new, l_new, a_prev[hh] * state[3 * hh + 2] + pv[hh]]
            a_new.append(a)
        return tuple(new) + tuple(a_new)

    ones = jnp.ones((1, blk), F32)
    carry = lax.fori_loop(0, n_far, far_block, tuple(state) + (ones,) * n_heads)
    state, a_prev = carry[:3 * n_heads], carry[3 * n_heads:]
    out_t = jnp.concatenate(
        [(a_prev[hh] * state[3 * hh + 2] + _dot(vt_sc[last_far, _head_dims(hh), :], p_sc[hh])) / state[3 * hh + 1]
         for hh in range(n_heads)], axis=0)
    o_ref[0] = out_t.T.astype(o_ref.dtype)


def _moba(qkv3, tbl, rel_bias, col0):
    bsz, seq, _ = qkv3.shape
    assert MAX_DISTANCE <= MOBA_BLOCK and seq % MOBA_BLOCK == 0
    width = N_MOBA_HEADS * HEAD_DIM
    blk = MOBA_BLOCK
    nb = seq // blk
    return pl.pallas_call(
        _moba_kernel,
        grid=(bsz, nb),
        in_specs=[pl.BlockSpec(memory_space=pltpu.SMEM),
                  pl.BlockSpec((1, blk, width), lambda b, i: (b, i, col0)),
                  pl.BlockSpec((1, seq, width), lambda b, i: (b, 0, col0 + 1)),
                  pl.BlockSpec((1, seq, width), lambda b, i: (b, 0, col0 + 2)),
                  pl.BlockSpec((N_MOBA_HEADS, 2 * blk, blk), lambda b, i: (0, 0, 0))],
        out_specs=pl.BlockSpec((1, blk, width), lambda b, i: (b, i, 0)),
        out_shape=jax.ShapeDtypeStruct((bsz, seq, width), BF16),
        scratch_shapes=[pltpu.VMEM((nb, width), F32),
                        pltpu.VMEM((nb, width, blk), BF16),
                        pltpu.VMEM((N_MOBA_HEADS, nb, blk), F32),
                        pltpu.VMEM((N_MOBA_HEADS, blk, blk), F32),
                        pltpu.VMEM((N_MOBA_HEADS, blk, blk), BF16)],
        compiler_params=_params(("arbitrary", "arbitrary"), 48),
        name="moba",
    )(rel_bias, qkv3, qkv3, qkv3, tbl)


def _sb_kernel(q_ref, k_ref, v_ref, o_ref, vt_sc):
    qi = pl.program_id(1)
    t, width = q_ref.shape[1], q_ref.shape[2]
    n_heads = width // HEAD_DIM
    nb = k_ref.shape[1] // t
    scale = HEAD_DIM ** -0.5

    @pl.when(qi == 0)
    def _():
        for n in range(nb):
            vt_sc[n] = _transposed_bf16(v_ref[0, n * t:(n + 1) * t, :])

    qt = q_ref[0].astype(F32).T
    row = lax.broadcasted_iota(I32, (LANES, t), 0)
    key = lax.broadcasted_iota(I32, (t, t), 0)
    qry = lax.broadcasted_iota(I32, (t, t), 1)
    strict = key < qry
    tri = jnp.where(qry >= key, 1.0, 0.0).astype(BF16)
    qs = [_head_queries(qt, row, hh, scale) for hh in range(n_heads)]

    def blocks(tiles):
        zs = [[_dot(k_ref[0, pl.ds(pl.multiple_of(j * t, t), t), _pair_lanes(hh)], qs[hh])
               for hh in range(n_heads)] for j, _ in tiles]
        csums = []
        for (_, diagonal), zt in zip(tiles, zs):
            row_sums = []
            for z in zt:
                sp = jnp.maximum(z, 0.0) + jnp.log(1.0 + jnp.exp(-jnp.abs(z)))
                if diagonal:
                    sp = jnp.where(strict, sp, 0.0)
                hi = sp.astype(BF16)
                lo = (sp - hi.astype(F32)).astype(BF16)
                row_sums.append(_dot(tri, hi) + _dot(tri, lo))
            csums.append(row_sums)
        out = []
        for (j, diagonal), zt, ct in zip(tiles, zs, csums):
            pvs = []
            for hh in range(n_heads):
                a = jnp.exp(zt[hh] - ct[hh])
                if diagonal:
                    a = jnp.where(strict, a, 0.0)
                pvs.append(_dot(vt_sc[j, _head_dims(hh), :], a.astype(BF16)))
            out.append((pvs, [c[0:1, :] for c in ct]))
        return out

    has_prev = qi > 0
    (pv_d, tot_d), (pv_p, tot_p) = blocks([(qi, True), (jnp.maximum(qi - 1, 0), False)])
    accs = tuple(d + jnp.where(has_prev, p * jnp.exp(-c), 0.0) for d, p, c in zip(pv_d, pv_p, tot_d))
    carries = tuple(c + jnp.where(has_prev, p, 0.0) for c, p in zip(tot_d, tot_p))

    def lowest(carries):
        return jnp.min(functools.reduce(jnp.minimum, carries))

    def cond(state):
        j, cmin, _, _ = state
        return (j >= 0) & (cmin < SB_ZERO_LOG)

    def body(state):
        j, _, carries, accs = state
        (pvs, totals), = blocks([(j, False)])
        accs = tuple(acc + pv * jnp.exp(-c) for acc, pv, c in zip(accs, pvs, carries))
        carries = tuple(c + total for c, total in zip(carries, totals))
        return j - 1, lowest(carries), carries, accs

    state = lax.while_loop(cond, body, (qi - 2, lowest(carries), carries, accs))
    out_t = jnp.concatenate(list(state[3]), axis=0)
    o_ref[0] = out_t.T.astype(o_ref.dtype)


def _stickbreak(qkv3, col0):
    bsz, seq, _ = qkv3.shape
    width = N_SB_HEADS * HEAD_DIM
    t = ATTN_ROWS
    return pl.pallas_call(
        _sb_kernel,
        grid=(bsz, seq // t),
        in_specs=[pl.BlockSpec((1, t, width), lambda b, i: (b, i, col0)),
                  pl.BlockSpec((1, seq, width), lambda b, i: (b, 0, col0 + 1)),
                  pl.BlockSpec((1, seq, width), lambda b, i: (b, 0, col0 + 2))],
        out_specs=pl.BlockSpec((1, t, width), lambda b, i: (b, i, 0)),
        out_shape=jax.ShapeDtypeStruct((bsz, seq, width), BF16),
        scratch_shapes=[pltpu.VMEM((seq // t, width, t), BF16)],
        compiler_params=_params(("arbitrary", "arbitrary"), 48),
        name="stickbreak",
    )(qkv3, qkv3, qkv3)


def _memkv_kernel(m_ref, w_ref, o_ref):
    o_ref[...] = _dot(m_ref[...].astype(BF16), w_ref[...]).astype(BF16)


def _memkv(mem2, w_bf16):
    n, d = mem2.shape
    width = w_bf16.shape[1]
    tm = min(n, 512)
    return pl.pallas_call(
        _memkv_kernel,
        grid=(n // tm,),
        in_specs=[pl.BlockSpec((tm, d), lambda i: (i, 0)),
                  pl.BlockSpec((d, width), lambda i: (0, 0))],
        out_specs=pl.BlockSpec((tm, width), lambda i: (i, 0)),
        out_shape=jax.ShapeDtypeStruct((n, width), BF16),
        compiler_params=_params(("arbitrary",), 32),
        name="memkv",
    )(mem2, w_bf16)


def _mem_kernel(q_ref, k_ref, v_ref, o_ref, vt_sc):
    t, width = q_ref.shape[1], q_ref.shape[2]
    n_heads = width // HEAD_DIM
    scale = HEAD_DIM ** -0.5

    @pl.when(pl.program_id(1) == 0)
    def _():
        vt_sc[...] = _transposed_bf16(v_ref[0])

    qt = q_ref[0].astype(F32).T
    row = lax.broadcasted_iota(I32, (LANES, t), 0)
    qs = [_head_queries(qt, row, hh, scale) for hh in range(n_heads)]
    scores = [_dot(k_ref[0, :, _pair_lanes(hh)], qs[hh]) for hh in range(n_heads)]
    probs = []
    for s in scores:
        e = jnp.exp(s - _fold_keys(s, jnp.maximum, jnp.max))
        probs.append((e * (1.0 / _fold_keys(e, jnp.add, jnp.sum))).astype(BF16))
    out_t = jnp.concatenate([_dot(vt_sc[_head_dims(hh), :], probs[hh]) for hh in range(n_heads)], axis=0)
    o_ref[0] = out_t.T.astype(o_ref.dtype)


def _mem_attention(qkv3, kv3, qcol):
    bsz, seq, _ = qkv3.shape
    n_mem = kv3.shape[1]
    width = N_MEM_HEADS * HEAD_DIM
    t = MEM_ROWS
    return pl.pallas_call(
        _mem_kernel,
        grid=(bsz, seq // t),
        in_specs=[pl.BlockSpec((1, t, width), lambda b, i: (b, i, qcol)),
                  pl.BlockSpec((1, n_mem, width), lambda b, i: (b, 0, 0)),
                  pl.BlockSpec((1, n_mem, width), lambda b, i: (b, 0, 1))],
        out_specs=pl.BlockSpec((1, t, width), lambda b, i: (b, i, 0)),
        out_shape=jax.ShapeDtypeStruct((bsz, seq, width), BF16),
        scratch_shapes=[pltpu.VMEM((width, n_mem), BF16)],
        compiler_params=_params(("arbitrary", "arbitrary"), 32),
        name="mem_attention",
    )(qkv3, kv3, kv3)


def _merge_kernel(ya_ref, yb_ref, ym_ref, x_ref, wg_ref, wa_ref, wb_ref, wm_ref, wo_ref,
                  lng_ref, lnb_ref, wrh_ref, br_ref, x1_ref, route_ref, tcnt_ref, h_sc, cnt_ref,
                  *, alpha, n_experts):
    step = pl.program_id(0)
    tm, d = x_ref.shape
    slot = lax.rem(step, 2)

    @pl.when(step == 0)
    def _():
        cnt_ref[...] = jnp.zeros(cnt_ref.shape, F32)
        h_sc[...] = jnp.zeros(h_sc.shape, F32)

    x = x_ref[...]
    xb = x.astype(BF16)
    branches = ((ya_ref, wa_ref), (yb_ref, wb_ref), (ym_ref, wm_ref))

    def branch_term(k):
        gate_logits = _dot(xb, wg_ref[:, k * d:(k + 1) * d])
        return gate_logits, _dot(branches[k][0][...], branches[k][1][...])

    x1 = _layer_norm(h_sc[1 - slot], lng_ref[...], lnb_ref[...])
    x1_ref[...] = x1
    xh = x1.astype(BF16)
    xl = (x1 - xh.astype(F32)).astype(BF16)

    first_term = branch_term(0)

    by_hi = _dot(xh, wrh_ref[...])
    logits = (by_hi[:, :LANES] + (by_hi[:, LANES:] + _dot(xl, wrh_ref[:, :LANES]))
              + br_ref[...])

    merged = None
    for k in range(len(branches)):
        gate_logits, y = first_term if k == 0 else branch_term(k)
        term = _sigmoid(gate_logits) * y
        merged = term if merged is None else merged + term
    h_sc[slot] = alpha * x + _dot(merged.astype(BF16), wo_ref[...])

    col = lax.broadcasted_iota(I32, (tm, LANES), 1)
    colf = col.astype(F32)
    big = float(LANES)
    gmask = (col >= n_experts) & (col < n_experts + N_GROUPS)
    lg = jnp.where(gmask, logits, -jnp.inf)
    gmax = jnp.max(lg, axis=1, keepdims=True)
    gidx = jnp.min(jnp.where(lg == gmax, colf, big), axis=1, keepdims=True) - n_experts
    g_p = 1.0 / jnp.sum(jnp.where(gmask, jnp.exp(logits - gmax), 0.0), axis=1, keepdims=True)
    lo_col = gidx * EXPERTS_PER_GROUP
    emask = (colf >= lo_col) & (colf < lo_col + EXPERTS_PER_GROUP)
    le = jnp.where(emask, logits, -jnp.inf)
    l1 = jnp.max(le, axis=1, keepdims=True)
    i1 = jnp.min(jnp.where(le == l1, colf, big), axis=1, keepdims=True)
    le2 = jnp.where(colf == i1, -jnp.inf, le)
    l2 = jnp.max(le2, axis=1, keepdims=True)
    i2 = jnp.min(jnp.where(le2 == l2, colf, big), axis=1, keepdims=True)
    e2 = jnp.exp(l2 - l1)
    gate1 = g_p * (1.0 / (1.0 + e2))
    gate2 = g_p * (e2 / (1.0 + e2))

    oh1 = colf == i1
    oh2 = colf == i2
    cnt = jnp.where((oh1 | oh2) & (step > 0), 1.0, 0.0)
    rr = lax.broadcasted_iota(I32, (tm, tm), 0)
    cc = lax.broadcasted_iota(I32, (tm, tm), 1)
    before = jnp.where(cc < rr, 1.0, 0.0).astype(BF16)
    base = _dot(before, cnt.astype(BF16)) + cnt_ref[0:1, :]
    rank1 = jnp.sum(jnp.where(oh1, base, 0.0), axis=1, keepdims=True)
    rank2 = jnp.sum(jnp.where(oh2, base, 0.0), axis=1, keepdims=True)
    tile_cnt = jnp.sum(cnt, axis=0, keepdims=True)
    cnt_ref[...] = cnt_ref[...] + tile_cnt
    tcnt_ref[0] = jnp.broadcast_to(tile_cnt, tcnt_ref.shape[1:])

    route = jnp.zeros((tm, LANES), F32)
    for k, val in enumerate((i1, i2, rank1, rank2, gate1, gate2)):
        route = jnp.where(col == k, val, route)
    route_ref[...] = route


def _merge(ya, yb, ym, x2, w_gates, wa, wb, wm, wo, ln_g, ln_b, wr_split, b_r, alpha, n_experts):
    n, d = x2.shape
    tm = MERGE_ROWS
    n_tiles = n // tm
    row_in = lambda w: pl.BlockSpec((tm, w), lambda i: (jnp.minimum(i, n_tiles - 1), 0))
    row_out = lambda w: pl.BlockSpec((tm, w), lambda i: (jnp.maximum(i - 1, 0), 0))
    full = lambda a: pl.BlockSpec(a.shape, lambda i: (0,) * a.ndim)
    return pl.pallas_call(
        functools.partial(_merge_kernel, alpha=alpha, n_experts=n_experts),
        grid=(n_tiles + 1,),
        in_specs=[row_in(ya.shape[1]), row_in(yb.shape[1]), row_in(ym.shape[1]), row_in(d),
                  full(w_gates), full(wa), full(wb), full(wm), full(wo), full(ln_g), full(ln_b),
                  full(wr_split), full(b_r)],
        out_specs=[row_out(d), row_out(LANES),
                   pl.BlockSpec((1, 8, LANES), lambda i: (jnp.maximum(i - 1, 0), 0, 0))],
        out_shape=[jax.ShapeDtypeStruct((n, d), F32),
                   jax.ShapeDtypeStruct((n, LANES), F32),
                   jax.ShapeDtypeStruct((n_tiles, 8, LANES), F32)],
        scratch_shapes=[pltpu.VMEM((2, tm, d), F32), pltpu.VMEM((8, LANES), F32)],
        compiler_params=_params(("arbitrary",), 56),
        name="merge",
    )(ya, yb, ym, x2, w_gates, wa, wb, wm, wo, ln_g, ln_b, wr_split, b_r)


def _row_copy(src_hbm, row, buf, sem, r):
    return pltpu.make_async_copy(src_hbm.at[pl.ds(row, 1), :], buf.at[pl.ds(r, 1), :], sem)


def _gather_rows(src_hbm, idx_ref, buf, sem, n_rows, unrolled):
    if unrolled:
        for r in range(n_rows):
            _row_copy(src_hbm, idx_ref[0, 0, r], buf, sem, r).start()
    else:
        def issue(r, _):
            _row_copy(src_hbm, idx_ref[0, 0, r], buf, sem, r).start()
            return 0
        lax.fori_loop(0, n_rows, issue, 0)


def _wait_rows(src_hbm, buf, sem, n_rows):
    pltpu.make_async_copy(src_hbm.at[pl.ds(0, n_rows), :], buf, sem).wait()


DISPATCH_ROWS = 512
CHUNK = 8


def _chunk_copy(buf, xpad_hbm, sem, src_row, dst_row):
    return pltpu.make_async_copy(buf.at[pl.ds(src_row, CHUNK), :], xpad_hbm.at[pl.ds(dst_row, CHUNK), :], sem)


def _dispatch_kernel(nch_ref, off_ref, dst_ref, zrow_ref, zflag_ref, nu_ref, base_ref, route_ref, x_ref,
                     xpad_hbm, dest_ref, sorted_even, sorted_odd, zero_buf, sem, zsem):
    i = pl.program_id(0)
    n_steps = pl.num_programs(0)
    n_exp = base_ref.shape[1]
    td = x_ref.shape[0]
    rb = sorted_even.shape[0]
    zrows = zero_buf.shape[0]
    bufs = (sorted_even, sorted_odd)
    parity = lax.rem(i, 2)

    @pl.when(i == 0)
    def _():
        zero_buf[...] = jnp.zeros(zero_buf.shape, F32)

        def zero_block(row, start):
            cp = pltpu.make_async_copy(zero_buf, xpad_hbm.at[pl.ds(pl.multiple_of(row, zrows), zrows), :], zsem)
            cp.start() if start else cp.wait()

        for start in (True, False):
            for e in range(n_exp):
                pl.when(zflag_ref[e] == 1)(functools.partial(zero_block, zrow_ref[e], start))

            def unused(blk, _):
                zero_block(blk * zrows, start)
                return 0
            lax.fori_loop(nu_ref[0], xpad_hbm.shape[0] // zrows, unused, 0)

    rt = route_ref[...].T
    eid = lax.broadcasted_iota(I32, (n_exp, td), 0).astype(F32)
    rowid = lax.broadcasted_iota(I32, (rb, td), 0).astype(F32)
    hit = None
    dests = []
    for k in range(2):
        mine = eid == rt[k:k + 1, :]
        rank = rt[2 + k:3 + k, :]
        pos = jnp.sum(jnp.where(mine, base_ref[0, :, 0:1], 0.0), axis=0, keepdims=True) + rank
        dests.append(jnp.sum(jnp.where(mine, base_ref[0, :, 1:2], 0.0), axis=0, keepdims=True) + rank)
        hit = (rowid == pos) if hit is None else hit | (rowid == pos)
    dest_ref[0] = jnp.concatenate(dests + [jnp.zeros((8 - len(dests), td), F32)], axis=0).astype(I32)
    perm = jnp.where(hit, 1.0, 0.0).astype(BF16)
    rows_sorted = _dot(perm, x_ref[...].astype(BF16))

    def runs(tile, buf, s, start):
        for e in range(n_exp):
            k = tile * n_exp + e
            src0, dst0 = off_ref[k], dst_ref[k]

            def one(j, _):
                cp = _chunk_copy(buf, xpad_hbm, s, pl.multiple_of(src0 + j * CHUNK, CHUNK),
                                 pl.multiple_of(dst0 + j * CHUNK, CHUNK))
                cp.start() if start else cp.wait()
                return 0
            lax.fori_loop(0, nch_ref[k], one, 0)

    def wait_runs(tile, buf, s):
        k_last = tile * n_exp + n_exp - 1
        total = pl.multiple_of(off_ref[k_last] + nch_ref[k_last] * CHUNK, CHUNK)
        pltpu.make_async_copy(buf.at[pl.ds(0, total), :], xpad_hbm.at[pl.ds(0, total), :], s).wait()

    def step(cur):
        bufs[cur][...] = rows_sorted

        @pl.when(i > 0)
        def _():
            wait_runs(i - 1, bufs[1 - cur], sem.at[1 - cur])

        runs(i, bufs[cur], sem.at[cur], True)

        @pl.when(i + 1 == n_steps)
        def _():
            wait_runs(i, bufs[cur], sem.at[cur])

    for cur in range(2):
        pl.when(parity == cur)(functools.partial(step, cur))


def _dispatch(x1, route, nch, off8, dst, zrow, zflag, n_used, base, n_rows_out):
    n, d = x1.shape
    td = DISPATCH_ROWS
    n_tiles = n // td
    n_exp = base.shape[1]
    rb = 2 * td + n_exp * CHUNK
    grid_spec = pltpu.PrefetchScalarGridSpec(
        num_scalar_prefetch=6,
        grid=(n_tiles,),
        in_specs=[pl.BlockSpec((1, n_exp, 2), lambda i, *_: (i, 0, 0)),
                  pl.BlockSpec((td, LANES), lambda i, *_: (i, 0)),
                  pl.BlockSpec((td, d), lambda i, *_: (i, 0))],
        out_specs=[pl.BlockSpec(memory_space=pl.ANY),
                   pl.BlockSpec((1, 8, td), lambda i, *_: (i, 0, 0))],
        scratch_shapes=[pltpu.VMEM((rb, d), F32), pltpu.VMEM((rb, d), F32),
                        pltpu.VMEM((MOE_BLOCK, d), F32),
                        pltpu.SemaphoreType.DMA((2,)), pltpu.SemaphoreType.DMA(())])
    return pl.pallas_call(
        _dispatch_kernel,
        grid_spec=grid_spec,
        out_shape=[jax.ShapeDtypeStruct((n_rows_out, d), F32),
                   jax.ShapeDtypeStruct((n_tiles, 8, td), I32)],
        compiler_params=_params(("arbitrary",), 48),
        name="dispatch",
    )(nch, off8, dst, zrow, zflag, n_used, base, route, x1)


def _experts_kernel(be_ref, nu_ref, x_hbm, wg_ref, wu_ref, wd_ref, o_ref, xbuf, sem, wg_sc, wu_sc, wd_sc):
    b = pl.program_id(0)
    n_used = nu_ref[0]
    rows = o_ref.shape[0]
    n_slots = xbuf.shape[0]

    def fetch(blk):
        slot = lax.rem(blk, n_slots)
        return pltpu.make_async_copy(x_hbm.at[pl.ds(pl.multiple_of(blk * rows, rows), rows), :],
                                     xbuf.at[slot], sem.at[slot])

    @pl.when(b == 0)
    def _():
        for blk in range(n_slots - 1):
            pl.when(blk < n_used)(lambda blk=blk: fetch(blk).start())

    @pl.when(b < n_used)
    def _():
        @pl.when(b + n_slots - 1 < n_used)
        def _():
            fetch(b + n_slots - 1).start()

        changed = (b == 0) | (be_ref[b] != be_ref[jnp.maximum(b - 1, 0)])

        @pl.when(changed)
        def _():
            wg_sc[...] = wg_ref[0].astype(BF16)
            wu_sc[...] = wu_ref[0].astype(BF16)
            wd_sc[...] = wd_ref[0].astype(BF16)

        fetch(b).wait()
        xb = xbuf[lax.rem(b, n_slots)].astype(BF16)
        g = _dot(xb, wg_sc[...])
        u = _dot(xb, wu_sc[...])
        h = (g * _sigmoid(g) * u).astype(BF16)
        o_ref[...] = _dot(h, wd_sc[...])

    @pl.when(b >= n_used)
    def _():
        o_ref[...] = jnp.zeros(o_ref.shape, F32)


def _experts(x_pad, block_expert, n_used, w_gate, w_up, w_down):
    d = x_pad.shape[1]
    n_blocks = block_expert.shape[0]
    rows = MOE_BLOCK
    d_exp = w_gate.shape[-1]
    grid_spec = pltpu.PrefetchScalarGridSpec(
        num_scalar_prefetch=2,
        grid=(n_blocks,),
        in_specs=[pl.BlockSpec(memory_space=pl.ANY),
                  pl.BlockSpec((1, d, d_exp), lambda b, be, nu: (be[b], 0, 0)),
                  pl.BlockSpec((1, d, d_exp), lambda b, be, nu: (be[b], 0, 0)),
                  pl.BlockSpec((1, d_exp, d), lambda b, be, nu: (be[b], 0, 0))],
        out_specs=pl.BlockSpec((rows, d), lambda b, be, nu: (b, 0)),
        scratch_shapes=[pltpu.VMEM((EXPERT_INPUT_SLOTS, rows, d), F32),
                        pltpu.SemaphoreType.DMA((EXPERT_INPUT_SLOTS,)),
                        pltpu.VMEM((d, d_exp), BF16),
                        pltpu.VMEM((d, d_exp), BF16),
                        pltpu.VMEM((d_exp, d), BF16)])
    return pl.pallas_call(
        _experts_kernel,
        grid_spec=grid_spec,
        out_shape=jax.ShapeDtypeStruct((n_blocks * rows, d), F32),
        compiler_params=_params(("arbitrary",), 44),
        name="experts",
    )(block_expert, n_used, x_pad, w_gate, w_up, w_down)


def _combine_kernel(idx_ref, idxn_ref, y_hbm, x1_ref, route_ref, lng_ref, lnb_ref, o_ref,
                    ybuf_even, ybuf_odd, sem, *, alpha):
    i = pl.program_id(0)
    n_steps = pl.num_programs(0)
    tm = x1_ref.shape[0]
    bufs = (ybuf_even, ybuf_odd)
    parity = lax.rem(i, 2)

    @pl.when(i == 0)
    def _():
        _gather_rows(y_hbm, idx_ref, bufs[0], sem.at[0], 2 * tm, False)

    def compute(cur):
        _gather_rows(y_hbm, idxn_ref, bufs[1 - cur], sem.at[1 - cur], 2 * tm, True)
        _wait_rows(y_hbm, bufs[cur], sem.at[cur], 2 * tm)
        ffn = (route_ref[:, 4:5] * bufs[cur][0:tm, :] + route_ref[:, 5:6] * bufs[cur][tm:2 * tm, :])
        o_ref[...] = _layer_norm(alpha * x1_ref[...] + ffn, lng_ref[...], lnb_ref[...])

        @pl.when(i + 1 == n_steps)
        def _():
            _wait_rows(y_hbm, bufs[1 - cur], sem.at[1 - cur], 2 * tm)

    for cur in range(2):
        pl.when(parity == cur)(functools.partial(compute, cur))


def _combine(y_pad, dest, x1, route, ln_g, ln_b, alpha):
    n, d = x1.shape
    tm = COMBINE_ROWS
    n_steps = n // tm
    idx3 = dest.reshape(n_steps, tm, 2).transpose(0, 2, 1).reshape(n_steps, 1, 2 * tm)
    smem_idx = lambda f: pl.BlockSpec((1, 1, 2 * tm), f, memory_space=pltpu.SMEM)
    return pl.pallas_call(
        functools.partial(_combine_kernel, alpha=alpha),
        grid=(n_steps,),
        in_specs=[smem_idx(lambda i: (i, 0, 0)),
                  smem_idx(lambda i: (jnp.minimum(i + 1, n_steps - 1), 0, 0)),
                  pl.BlockSpec(memory_space=pl.ANY),
                  pl.BlockSpec((tm, d), lambda i: (i, 0)),
                  pl.BlockSpec((tm, LANES), lambda i: (i, 0)),
                  pl.BlockSpec((1, d), lambda i: (0, 0)),
                  pl.BlockSpec((1, d), lambda i: (0, 0))],
        out_specs=pl.BlockSpec((tm, d), lambda i: (i, 0)),
        out_shape=jax.ShapeDtypeStruct((n, d), F32),
        scratch_shapes=[pltpu.VMEM((2 * tm, d), F32), pltpu.VMEM((2 * tm, d), F32),
                        pltpu.SemaphoreType.DMA((2,))],
        compiler_params=_params(("arbitrary",), 32),
        name="combine",
    )(idx3, idx3, y_pad, x1, route, ln_g, ln_b)


def _split_bf16(w):
    hi = w.astype(BF16)
    return hi, (w - hi.astype(F32)).astype(BF16)


def _layer(x, mem, tbl, rel_bias, w_in, w_mem_kv, w_br_moba, w_br_sb, w_br_mem, w_out, ln1_g, ln1_b,
           w_rg, b_rg, w_re, b_re, w_gate, w_up, w_down, ln2_g, ln2_b, alpha):
    bsz, seq, d = x.shape
    n = bsz * seq
    n_experts = w_re.shape[1]
    moba_w, sb_w, mem_w = N_MOBA_HEADS * HEAD_DIM, N_SB_HEADS * HEAD_DIM, N_MEM_HEADS * HEAD_DIM
    n_qkv = 3 * moba_w + 3 * sb_w + mem_w
    assert w_in.shape[1] == n_qkv + 3 * d and n_experts + N_GROUPS <= LANES and moba_w == sb_w

    x2 = x.reshape(n, d)
    qkv = _proj(x2, w_in[:, :n_qkv].astype(BF16))
    qkv3 = qkv.reshape(bsz, seq, n_qkv)
    y_a = _moba(qkv3, tbl, rel_bias, 0)
    y_b = _stickbreak(qkv3, 3 * moba_w // sb_w)
    kv = _memkv(mem.reshape(-1, d), w_mem_kv.astype(BF16)).reshape(bsz, mem.shape[1], 2 * mem_w)
    y_m = _mem_attention(qkv3, kv, (3 * moba_w + 3 * sb_w) // mem_w)

    w_r = jnp.zeros((d, LANES), F32).at[:, :n_experts].set(w_re).at[:, n_experts:n_experts + N_GROUPS].set(w_rg)
    b_r = jnp.zeros((1, LANES), F32).at[0, :n_experts].set(b_re).at[0, n_experts:n_experts + N_GROUPS].set(b_rg)
    wr_split = jnp.concatenate(_split_bf16(w_r), axis=1)
    x1, route, tile_counts = _merge(
        y_a.reshape(n, moba_w), y_b.reshape(n, sb_w), y_m.reshape(n, mem_w), x2, w_in[:, n_qkv:].astype(BF16),
        w_br_moba.astype(BF16), w_br_sb.astype(BF16), w_br_mem.astype(BF16), w_out.astype(BF16),
        ln1_g.reshape(1, d), ln1_b.reshape(1, d), wr_split, b_r, alpha, n_experts)

    rows, td = MOE_BLOCK, DISPATCH_ROWS
    assert td == MERGE_ROWS
    n_tiles = n // td
    tile_cnt = tile_counts[:, 0, :n_experts].astype(I32)
    n_chunks = (tile_cnt + CHUNK - 1) // CHUNK
    run_rows = n_chunks * CHUNK
    run_rank = jnp.cumsum(tile_cnt, axis=0) - tile_cnt
    run_off = jnp.cumsum(run_rows, axis=0) - run_rows
    sorted_off = jnp.cumsum(run_rows, axis=1) - run_rows
    expert_rows = jnp.sum(run_rows, axis=0)
    padded = (expert_rows + rows - 1) // rows * rows
    pend = jnp.cumsum(padded)
    pstart = pend - padded
    run_dst = pstart[None, :] + run_off
    n_blocks = (2 * n + n_tiles * n_experts * (CHUNK - 1)) // rows + n_experts
    block_start = jnp.arange(n_blocks, dtype=I32) * rows
    block_expert = jnp.minimum(jnp.sum((pend[None, :] <= block_start[:, None]).astype(I32), axis=1),
                               n_experts - 1)
    n_used = (pend[-1] // rows).astype(I32).reshape(1)
    base = jnp.stack([sorted_off - run_rank, run_dst - run_rank], axis=-1).astype(F32)
    x_pad, dest8 = _dispatch(x1, route, n_chunks.reshape(-1), sorted_off.reshape(-1), run_dst.reshape(-1),
                             jnp.maximum(pend - rows, 0), (padded > 0).astype(I32), n_used, base,
                             n_blocks * rows)
    dest = dest8[:, 0:2, :].transpose(0, 2, 1).reshape(n, 2)

    y_pad = _experts(x_pad, block_expert, n_used, w_gate, w_up, w_down)
    out = _combine(y_pad, dest, x1, route, ln2_g.reshape(1, d), ln2_b.reshape(1, d), alpha)
    return out.reshape(bsz, seq, d)


def kernel(x, mem, w_in, w_mem_kv, rel_bias, w_br_moba, w_br_sb, w_br_mem, w_out, ln1_g, ln1_b,
           w_router_group, b_router_group, w_router_expert, b_router_expert,
           w_gate, w_up, w_down, ln2_g, ln2_b):
    depth = w_in.shape[0]
    alpha = (2.0 * depth) ** 0.25
    tbl = _bias_table(rel_bias)
    for l in range(depth):
        x = _layer(x, mem, tbl, rel_bias, w_in[l], w_mem_kv[l], w_br_moba[l], w_br_sb[l], w_br_mem[l],
                   w_out[l], ln1_g[l], ln1_b[l], w_router_group[l], b_router_group[l],
                   w_router_expert[l], b_router_expert[l], w_gate[l], w_up[l], w_down[l],
                   ln2_g[l], ln2_b[l], alpha)
    return x
```

```python
import functools
import math

import jax
import jax.numpy as jnp
from jax import lax
from jax.experimental import pallas as pl
from jax.experimental.pallas import tpu as pltpu

F32, BF16, I32 = jnp.float32, jnp.bfloat16, jnp.int32

HEAD_DIM = 64
N_MOBA_HEADS = 6
N_SB_HEADS = 6
N_MEM_HEADS = 4
MOBA_BLOCK = 256
MOBA_TOPK = 3
N_BUCKETS = 32
MAX_DISTANCE = 128
N_GROUPS = 4
EXPERTS_PER_GROUP = 8
LN_EPS = 1e-5
NEG = -1e30

LANES = 128
VMEM_BYTES = 64 * 1024 * 1024
PAIR = LANES // HEAD_DIM

SB_ZERO_LOG = 110.0

PROJ_ROWS = 1024
MERGE_ROWS = 512
ATTN_ROWS = 256
MEM_ROWS = 1024
COMBINE_ROWS = 256
MOE_BLOCK = 512
EXPERT_INPUT_SLOTS = 3


def _params(semantics, vmem_mb):
    return pltpu.CompilerParams(dimension_semantics=semantics,
                                vmem_limit_bytes=min(vmem_mb * 1024 * 1024, VMEM_BYTES))


def _dot(a, b):
    return jnp.dot(a, b, preferred_element_type=F32)


def _layer_norm(h, g, b):
    mu = jnp.mean(h, axis=-1, keepdims=True)
    d = h - mu
    var = jnp.mean(d * d, axis=-1, keepdims=True)
    return d * lax.rsqrt(var + LN_EPS) * g + b


def _sigmoid(x):
    return 1.0 / (1.0 + jnp.exp(-x))


def _head_rows(qt, row, hh, scale):
    keep = (row >= hh * HEAD_DIM) & (row < (hh + 1) * HEAD_DIM)
    return jnp.where(keep, qt * scale, 0.0).astype(BF16)


def _transposed_bf16(x):
    return x.astype(F32).T.astype(BF16)


def _proj_kernel(x_ref, wq_ref, qkv_ref):
    qkv_ref[...] = _dot(x_ref[...].astype(BF16), wq_ref[...]).astype(BF16)


def _proj(x2, w_qkv):
    n, d = x2.shape
    n_qkv = w_qkv.shape[1]
    tm = PROJ_ROWS
    return pl.pallas_call(
        _proj_kernel,
        grid=(n // tm,),
        in_specs=[pl.BlockSpec((tm, d), lambda i: (i, 0)),
                  pl.BlockSpec((d, n_qkv), lambda i: (0, 0))],
        out_specs=pl.BlockSpec((tm, n_qkv), lambda i: (i, 0)),
        out_shape=jax.ShapeDtypeStruct((n, n_qkv), BF16),
        compiler_params=_params(("arbitrary",), 40),
        name="proj",
    )(x2, w_qkv)


def _t5_bucket(rel):
    rel = jnp.maximum(rel, 0)
    max_exact = N_BUCKETS // 2
    rel_f = jnp.maximum(rel, 1).astype(F32)
    large = max_exact + (jnp.log(rel_f / max_exact) / math.log(MAX_DISTANCE / max_exact)
                         * (N_BUCKETS - max_exact)).astype(I32)
    large = jnp.minimum(large, N_BUCKETS - 1)
    return jnp.where(rel < max_exact, rel, large)


def _bias_table_kernel(rb_ref, o_ref):
    h = pl.program_id(0)
    blk = o_ref.shape[2]
    j = lax.broadcasted_iota(I32, (2 * blk, blk), 0)
    i = lax.broadcasted_iota(I32, (2 * blk, blk), 1)
    bucket = _t5_bucket(blk + i - j)
    acc = jnp.zeros((2 * blk, blk), F32)
    for b in range(N_BUCKETS):
        acc = jnp.where(bucket == b, rb_ref[b, h], acc)
    o_ref[0] = acc


def _bias_table(rel_bias):
    n_heads = rel_bias.shape[1]
    blk = MOBA_BLOCK
    return pl.pallas_call(
        _bias_table_kernel,
        grid=(n_heads,),
        in_specs=[pl.BlockSpec(memory_space=pltpu.SMEM)],
        out_specs=pl.BlockSpec((1, 2 * blk, blk), lambda h: (h, 0, 0)),
        out_shape=jax.ShapeDtypeStruct((n_heads, 2 * blk, blk), F32),
        compiler_params=_params(("arbitrary",), 32),
        name="bias_table",
    )(rel_bias)


def _fold_keys(x, op, final):
    while x.shape[0] > 8 and x.shape[0] % 2 == 0:
        half = x.shape[0] // 2
        x = op(x[:half], x[half:])
    return final(x, axis=0, keepdims=True)


def _head_queries(qt, row, hh, scale):
    p, sub = divmod(hh, PAIR)
    return _head_rows(qt[p * LANES:(p + 1) * LANES, :], row, sub, scale)


def _pair_lanes(hh):
    p = hh // PAIR
    return slice(p * LANES, (p + 1) * LANES)


def _head_dims(hh):
    return slice(hh * HEAD_DIM, (hh + 1) * HEAD_DIM)


def _moba_kernel(rb_ref, q_ref, k_ref, v_ref, tbl_ref, o_ref, kmean_sc, vt_sc, sel_sc, s_sc, p_sc, acc_sc):
    own = pl.program_id(1)
    blk, width = q_ref.shape[1], q_ref.shape[2]
    n_heads = width // HEAD_DIM
    nb = k_ref.shape[1] // blk
    scale = HEAD_DIM ** -0.5

    @pl.when(own == 0)
    def _():
        for n in range(nb):
            kb = k_ref[0, n * blk:(n + 1) * blk, :].astype(F32)
            kmean_sc[n:n + 1, :] = jnp.mean(kb, axis=0, keepdims=True)
            vt_sc[n] = _transposed_bf16(v_ref[0, n * blk:(n + 1) * blk, :])

    qt = q_ref[0].astype(F32).T
    row = lax.broadcasted_iota(I32, (LANES, blk), 0)
    blk_id = lax.broadcasted_iota(I32, (nb, blk), 0)
    valid = blk_id < own
    causal = (lax.broadcasted_iota(I32, (blk, blk), 0) <= lax.broadcasted_iota(I32, (blk, blk), 1))
    kmean = kmean_sc[...].astype(BF16)

    qs = [_head_queries(qt, row, hh, scale) for hh in range(n_heads)]
    gates = [_dot(kmean[:, _pair_lanes(hh)], _head_queries(qt, row, hh, 1.0)) for hh in range(n_heads)]

    def scores(n):
        offn = pl.multiple_of(n * blk, blk)
        return [_dot(k_ref[0, pl.ds(offn, blk), _pair_lanes(hh)], qs[hh]) for hh in range(n_heads)]

    own_scores = scores(own)
    prev_scores = scores(jnp.maximum(own - 1, 0))

    blk_f = blk_id.astype(F32)
    for hh in range(n_heads):
        gate = jnp.where(valid, gates[hh], NEG)
        picked = jnp.zeros((nb, blk), jnp.bool_)
        for _ in range(MOBA_TOPK):
            top = _fold_keys(gate, jnp.maximum, jnp.max)
            first = _fold_keys(jnp.where(gate == top, blk_f, float(nb)), jnp.minimum, jnp.min)
            hit = blk_f == first
            picked = picked | hit
            gate = jnp.where(hit, -jnp.inf, gate)
        sel_sc[hh] = jnp.where(valid & picked, 0.0, NEG)

    prev = jnp.maximum(own - 1, 0)
    no_prev = jnp.where(own > 0, 0.0, NEG)
    stats = []
    for hh, (s_own, s_prev) in enumerate(zip(own_scores, prev_scores)):
        s_own = jnp.where(causal, s_own + tbl_ref[hh, blk:, :], NEG)
        s_prev = s_prev + tbl_ref[hh, :blk, :]
        term = sel_sc[hh, pl.ds(prev, 1), :] + no_prev
        m0 = jnp.maximum(_fold_keys(s_own, jnp.maximum, jnp.max),
                         _fold_keys(s_prev, jnp.maximum, jnp.max) + term)
        p_own = jnp.exp(s_own - m0)
        p_prev = jnp.exp(s_prev - (m0 - term))
        l0 = _fold_keys(p_own, jnp.add, jnp.sum) + _fold_keys(p_prev, jnp.add, jnp.sum)
        stats.append((m0, l0, p_own.astype(BF16), p_prev.astype(BF16)))
    state = []
    for hh, (m0, l0, p_own, p_prev) in enumerate(stats):
        state += [m0, l0, _dot(vt_sc[own, _head_dims(hh), :], p_own) + _dot(vt_sc[prev, _head_dims(hh), :], p_prev)]
    state = tuple(state)

    def softmax_step(s, per_query, m_run, l_run):
        m_new = jnp.maximum(m_run, _fold_keys(s, jnp.maximum, jnp.max) + per_query)
        a = jnp.exp(m_run - m_new)
        pn = jnp.exp(s - (m_new - per_query))
        return m_new, a * l_run + _fold_keys(pn, jnp.add, jnp.sum), a, pn.astype(BF16)

    n_far = own - 1
    last_far = jnp.maximum(n_far - 1, 0)
    for hh, s in enumerate(scores(0)):
        s_sc[hh] = s
    p_sc[...] = jnp.zeros(p_sc.shape, BF16)

    for hh in range(n_heads):
        acc_sc[hh] = state[3 * hh + 2]

    def far_block(n, carry):
        ml, a_prev = carry[:2 * n_heads], carry[2 * n_heads:]
        n_prev = jnp.maximum(n - 1, 0)
        pv = [_dot(vt_sc[n_prev, _head_dims(hh), :], p_sc[hh]) for hh in range(n_heads)]
        s_next = scores(jnp.minimum(n + 1, last_far))
        stats = [softmax_step(s_sc[hh], rb_ref[N_BUCKETS - 1, hh] + sel_sc[hh, pl.ds(n, 1), :],
                              ml[2 * hh], ml[2 * hh + 1]) for hh in range(n_heads)]
        new, a_new = [], []
        for hh, (m_new, l_new, a, pn) in enumerate(stats):
            p_sc[hh] = pn
            s_sc[hh] = s_next[hh]
            acc_sc[hh] = a_prev[hh] * acc_sc[hh] + pv[hh]
            new += [m_new, l_new]
            a_new.append(a)
        return tuple(new) + tuple(a_new)

    ones = jnp.ones((1, blk), F32)
    ml0 = tuple(v for hh in range(n_heads) for v in state[3 * hh:3 * hh + 2])
    carry = lax.fori_loop(0, n_far, far_block, ml0 + (ones,) * n_heads)
    ml, a_prev = carry[:2 * n_heads], carry[2 * n_heads:]
    out_t = jnp.concatenate(
        [(a_prev[hh] * acc_sc[hh] + _dot(vt_sc[last_far, _head_dims(hh), :], p_sc[hh])) / ml[2 * hh + 1]
         for hh in range(n_heads)], axis=0)
    o_ref[0] = out_t.T.astype(o_ref.dtype)


def _moba(qkv3, tbl, rel_bias, col0):
    bsz, seq, _ = qkv3.shape
    assert MAX_DISTANCE <= MOBA_BLOCK and seq % MOBA_BLOCK == 0
    width = N_MOBA_HEADS * HEAD_DIM
    blk = MOBA_BLOCK
    nb = seq // blk
    return pl.pallas_call(
        _moba_kernel,
        grid=(bsz, nb),
        in_specs=[pl.BlockSpec(memory_space=pltpu.SMEM),
                  pl.BlockSpec((1, blk, width), lambda b, i: (b, i, col0)),
                  pl.BlockSpec((1, seq, width), lambda b, i: (b, 0, col0 + 1)),
                  pl.BlockSpec((1, seq, width), lambda b, i: (b, 0, col0 + 2)),
                  pl.BlockSpec((N_MOBA_HEADS, 2 * blk, blk), lambda b, i: (0, 0, 0))],
        out_specs=pl.BlockSpec((1, blk, width), lambda b, i: (b, i, 0)),
        out_shape=jax.ShapeDtypeStruct((bsz, seq, width), BF16),
        scratch_shapes=[pltpu.VMEM((nb, width), F32),
                        pltpu.VMEM((nb, width, blk), BF16),
                        pltpu.VMEM((N_MOBA_HEADS, nb, blk), F32),
                        pltpu.VMEM((N_MOBA_HEADS, blk, blk), F32),
                        pltpu.VMEM((N_MOBA_HEADS, blk, blk), BF16),
                        pltpu.VMEM((N_MOBA_HEADS, HEAD_DIM, blk), F32)],
        compiler_params=_params(("arbitrary", "arbitrary"), 48),
        name="moba",
    )(rel_bias, qkv3, qkv3, qkv3, tbl)


def _sb_kernel(q_ref, k_ref, v_ref, o_ref, vt_sc):
    qi = pl.program_id(1)
    t, width = q_ref.shape[1], q_ref.shape[2]
    n_heads = width // HEAD_DIM
    nb = k_ref.shape[1] // t
    scale = HEAD_DIM ** -0.5

    @pl.when(qi == 0)
    def _():
        for n in range(nb):
            vt_sc[n] = _transposed_bf16(v_ref[0, n * t:(n + 1) * t, :])

    qt = q_ref[0].astype(F32).T
    row = lax.broadcasted_iota(I32, (LANES, t), 0)
    key = lax.broadcasted_iota(I32, (t, t), 0)
    qry = lax.broadcasted_iota(I32, (t, t), 1)
    strict = key < qry
    tri = jnp.where(qry >= key, 1.0, 0.0).astype(BF16)
    qs = [_head_queries(qt, row, hh, scale) for hh in range(n_heads)]

    def blocks(tiles):
        zs = [[_dot(k_ref[0, pl.ds(pl.multiple_of(j * t, t), t), _pair_lanes(hh)], qs[hh])
               for hh in range(n_heads)] for j, _ in tiles]
        csums = []
        for (_, diagonal), zt in zip(tiles, zs):
            row_sums = []
            for z in zt:
                sp = jnp.maximum(z, 0.0) + jnp.log(1.0 + jnp.exp(-jnp.abs(z)))
                if diagonal:
                    sp = jnp.where(strict, sp, 0.0)
                hi = sp.astype(BF16)
                lo = (sp - hi.astype(F32)).astype(BF16)
                row_sums.append(_dot(tri, hi) + _dot(tri, lo))
            csums.append(row_sums)
        out = []
        for (j, diagonal), zt, ct in zip(tiles, zs, csums):
            pvs = []
            for hh in range(n_heads):
                a = jnp.exp(zt[hh] - ct[hh])
                if diagonal:
                    a = jnp.where(strict, a, 0.0)
                pvs.append(_dot(vt_sc[j, _head_dims(hh), :], a.astype(BF16)))
            out.append((pvs, [c[0:1, :] for c in ct]))
        return out

    has_prev = qi > 0
    (pv_d, tot_d), (pv_p, tot_p) = blocks([(qi, True), (jnp.maximum(qi - 1, 0), False)])
    accs = tuple(d + jnp.where(has_prev, p * jnp.exp(-c), 0.0) for d, p, c in zip(pv_d, pv_p, tot_d))
    carries = tuple(c + jnp.where(has_prev, p, 0.0) for c, p in zip(tot_d, tot_p))

    def lowest(carries):
        return jnp.min(functools.reduce(jnp.minimum, carries))

    def cond(state):
        j, cmin, _, _ = state
        return (j >= 0) & (cmin < SB_ZERO_LOG)

    def body(state):
        j, _, carries, accs = state
        (pvs, totals), = blocks([(j, False)])
        accs = tuple(acc + pv * jnp.exp(-c) for acc, pv, c in zip(accs, pvs, carries))
        carries = tuple(c + total for c, total in zip(carries, totals))
        return j - 1, lowest(carries), carries, accs

    state = lax.while_loop(cond, body, (qi - 2, lowest(carries), carries, accs))
    out_t = jnp.concatenate(list(state[3]), axis=0)
    o_ref[0] = out_t.T.astype(o_ref.dtype)


def _stickbreak(qkv3, col0):
    bsz, seq, _ = qkv3.shape
    width = N_SB_HEADS * HEAD_DIM
    t = ATTN_ROWS
    return pl.pallas_call(
        _sb_kernel,
        grid=(bsz, seq // t),
        in_specs=[pl.BlockSpec((1, t, width), lambda b, i: (b, i, col0)),
                  pl.BlockSpec((1, seq, width), lambda b, i: (b, 0, col0 + 1)),
                  pl.BlockSpec((1, seq, width), lambda b, i: (b, 0, col0 + 2))],
        out_specs=pl.BlockSpec((1, t, width), lambda b, i: (b, i, 0)),
        out_shape=jax.ShapeDtypeStruct((bsz, seq, width), BF16),
        scratch_shapes=[pltpu.VMEM((seq // t, width, t), BF16)],
        compiler_params=_params(("arbitrary", "arbitrary"), 48),
        name="stickbreak",
    )(qkv3, qkv3, qkv3)


def _memkv_kernel(m_ref, w_ref, o_ref):
    o_ref[...] = _dot(m_ref[...].astype(BF16), w_ref[...]).astype(BF16)


def _memkv(mem2, w_bf16):
    n, d = mem2.shape
    width = w_bf16.shape[1]
    tm = min(n, 512)
    return pl.pallas_call(
        _memkv_kernel,
        grid=(n // tm,),
        in_specs=[pl.BlockSpec((tm, d), lambda i: (i, 0)),
                  pl.BlockSpec((d, width), lambda i: (0, 0))],
        out_specs=pl.BlockSpec((tm, width), lambda i: (i, 0)),
        out_shape=jax.ShapeDtypeStruct((n, width), BF16),
        compiler_params=_params(("arbitrary",), 32),
        name="memkv",
    )(mem2, w_bf16)


def _mem_kernel(q_ref, k_ref, v_ref, o_ref, vt_sc):
    t, width = q_ref.shape[1], q_ref.shape[2]
    n_heads = width // HEAD_DIM
    scale = HEAD_DIM ** -0.5

    @pl.when(pl.program_id(1) == 0)
    def _():
        vt_sc[...] = _transposed_bf16(v_ref[0])

    qt = q_ref[0].astype(F32).T
    row = lax.broadcasted_iota(I32, (LANES, t), 0)
    qs = [_head_queries(qt, row, hh, scale) for hh in range(n_heads)]
    scores = [_dot(k_ref[0, :, _pair_lanes(hh)], qs[hh]) for hh in range(n_heads)]
    probs = []
    for s in scores:
        e = jnp.exp(s - _fold_keys(s, jnp.maximum, jnp.max))
        probs.append((e * (1.0 / _fold_keys(e, jnp.add, jnp.sum))).astype(BF16))
    out_t = jnp.concatenate([_dot(vt_sc[_head_dims(hh), :], probs[hh]) for hh in range(n_heads)], axis=0)
    o_ref[0] = out_t.T.astype(o_ref.dtype)


def _mem_attention(qkv3, kv3, qcol):
    bsz, seq, _ = qkv3.shape
    n_mem = kv3.shape[1]
    width = N_MEM_HEADS * HEAD_DIM
    t = MEM_ROWS
    return pl.pallas_call(
        _mem_kernel,
        grid=(bsz, seq // t),
        in_specs=[pl.BlockSpec((1, t, width), lambda b, i: (b, i, qcol)),
                  pl.BlockSpec((1, n_mem, width), lambda b, i: (b, 0, 0)),
                  pl.BlockSpec((1, n_mem, width), lambda b, i: (b, 0, 1))],
        out_specs=pl.BlockSpec((1, t, width), lambda b, i: (b, i, 0)),
        out_shape=jax.ShapeDtypeStruct((bsz, seq, width), BF16),
        scratch_shapes=[pltpu.VMEM((width, n_mem), BF16)],
        compiler_params=_params(("arbitrary", "arbitrary"), 32),
        name="mem_attention",
    )(qkv3, kv3, kv3)


def _merge_kernel(ya_ref, yb_ref, ym_ref, x_ref, wg_ref, wa_ref, wb_ref, wm_ref, wo_ref,
                  lng_ref, lnb_ref, wrh_ref, br_ref, x1_ref, route_ref, tcnt_ref, h_sc, cnt_ref,
                  *, alpha, n_experts):
    step = pl.program_id(0)
    tm, d = x_ref.shape
    slot = lax.rem(step, 2)

    @pl.when(step == 0)
    def _():
        cnt_ref[...] = jnp.zeros(cnt_ref.shape, F32)
        h_sc[...] = jnp.zeros(h_sc.shape, F32)

    x = x_ref[...]
    xb = x.astype(BF16)
    branches = ((ya_ref, wa_ref), (yb_ref, wb_ref), (ym_ref, wm_ref))

    def branch_term(k):
        gate_logits = _dot(xb, wg_ref[:, k * d:(k + 1) * d])
        return gate_logits, _dot(branches[k][0][...], branches[k][1][...])

    x1 = _layer_norm(h_sc[1 - slot], lng_ref[...], lnb_ref[...])
    x1_ref[...] = x1
    xh = x1.astype(BF16)
    xl = (x1 - xh.astype(F32)).astype(BF16)

    first_term = branch_term(0)

    by_hi = _dot(xh, wrh_ref[...])
    logits = (by_hi[:, :LANES] + (by_hi[:, LANES:] + _dot(xl, wrh_ref[:, :LANES]))
              + br_ref[...])

    merged = None
    for k in range(len(branches)):
        gate_logits, y = first_term if k == 0 else branch_term(k)
        term = _sigmoid(gate_logits) * y
        merged = term if merged is None else merged + term
    h_sc[slot] = alpha * x + _dot(merged.astype(BF16), wo_ref[...])

    col = lax.broadcasted_iota(I32, (tm, LANES), 1)
    colf = col.astype(F32)
    big = float(LANES)
    gmask = (col >= n_experts) & (col < n_experts + N_GROUPS)
    lg = jnp.where(gmask, logits, -jnp.inf)
    gmax = jnp.max(lg, axis=1, keepdims=True)
    gidx = jnp.min(jnp.where(lg == gmax, colf, big), axis=1, keepdims=True) - n_experts
    g_p = 1.0 / jnp.sum(jnp.where(gmask, jnp.exp(logits - gmax), 0.0), axis=1, keepdims=True)
    lo_col = gidx * EXPERTS_PER_GROUP
    emask = (colf >= lo_col) & (colf < lo_col + EXPERTS_PER_GROUP)
    le = jnp.where(emask, logits, -jnp.inf)
    l1 = jnp.max(le, axis=1, keepdims=True)
    i1 = jnp.min(jnp.where(le == l1, colf, big), axis=1, keepdims=True)
    le2 = jnp.where(colf == i1, -jnp.inf, le)
    l2 = jnp.max(le2, axis=1, keepdims=True)
    i2 = jnp.min(jnp.where(le2 == l2, colf, big), axis=1, keepdims=True)
    e2 = jnp.exp(l2 - l1)
    gate1 = g_p * (1.0 / (1.0 + e2))
    gate2 = g_p * (e2 / (1.0 + e2))

    oh1 = colf == i1
    oh2 = colf == i2
    cnt = jnp.where((oh1 | oh2) & (step > 0), 1.0, 0.0)
    rr = lax.broadcasted_iota(I32, (tm, tm), 0)
    cc = lax.broadcasted_iota(I32, (tm, tm), 1)
    before = jnp.where(cc < rr, 1.0, 0.0).astype(BF16)
    base = _dot(before, cnt.astype(BF16)) + cnt_ref[0:1, :]
    rank1 = jnp.sum(jnp.where(oh1, base, 0.0), axis=1, keepdims=True)
    rank2 = jnp.sum(jnp.where(oh2, base, 0.0), axis=1, keepdims=True)
    tile_cnt = jnp.sum(cnt, axis=0, keepdims=True)
    cnt_ref[...] = cnt_ref[...] + tile_cnt
    tcnt_ref[0] = jnp.broadcast_to(tile_cnt, tcnt_ref.shape[1:])

    route = jnp.zeros((tm, LANES), F32)
    for k, val in enumerate((i1, i2, rank1, rank2, gate1, gate2)):
        route = jnp.where(col == k, val, route)
    route_ref[...] = route


def _merge(ya, yb, ym, x2, w_gates, wa, wb, wm, wo, ln_g, ln_b, wr_split, b_r, alpha, n_experts):
    n, d = x2.shape
    tm = MERGE_ROWS
    n_tiles = n // tm
    row_in = lambda w: pl.BlockSpec((tm, w), lambda i: (jnp.minimum(i, n_tiles - 1), 0))
    row_out = lambda w: pl.BlockSpec((tm, w), lambda i: (jnp.maximum(i - 1, 0), 0))
    full = lambda a: pl.BlockSpec(a.shape, lambda i: (0,) * a.ndim)
    return pl.pallas_call(
        functools.partial(_merge_kernel, alpha=alpha, n_experts=n_experts),
        grid=(n_tiles + 1,),
        in_specs=[row_in(ya.shape[1]), row_in(yb.shape[1]), row_in(ym.shape[1]), row_in(d),
                  full(w_gates), full(wa), full(wb), full(wm), full(wo), full(ln_g), full(ln_b),
                  full(wr_split), full(b_r)],
        out_specs=[row_out(d), row_out(LANES),
                   pl.BlockSpec((1, 8, LANES), lambda i: (jnp.maximum(i - 1, 0), 0, 0))],
        out_shape=[jax.ShapeDtypeStruct((n, d), F32),
                   jax.ShapeDtypeStruct((n, LANES), F32),
                   jax.ShapeDtypeStruct((n_tiles, 8, LANES), F32)],
        scratch_shapes=[pltpu.VMEM((2, tm, d), F32), pltpu.VMEM((8, LANES), F32)],
        compiler_params=_params(("arbitrary",), 56),
        name="merge",
    )(ya, yb, ym, x2, w_gates, wa, wb, wm, wo, ln_g, ln_b, wr_split, b_r)


def _row_copy(src_hbm, row, buf, sem, r):
    return pltpu.make_async_copy(src_hbm.at[pl.ds(row, 1), :], buf.at[pl.ds(r, 1), :], sem)


def _gather_rows(src_hbm, idx_ref, buf, sem, n_rows, unrolled):
    if unrolled:
        for r in range(n_rows):
            _row_copy(src_hbm, idx_ref[0, 0, r], buf, sem, r).start()
    else:
        def issue(r, _):
            _row_copy(src_hbm, idx_ref[0, 0, r], buf, sem, r).start()
            return 0
        lax.fori_loop(0, n_rows, issue, 0)


def _wait_rows(src_hbm, buf, sem, n_rows):
    pltpu.make_async_copy(src_hbm.at[pl.ds(0, n_rows), :], buf, sem).wait()


DISPATCH_ROWS = 512
CHUNK = 8


def _chunk_copy(buf, xpad_hbm, sem, src_row, dst_row):
    return pltpu.make_async_copy(buf.at[pl.ds(src_row, CHUNK), :], xpad_hbm.at[pl.ds(dst_row, CHUNK), :], sem)


def _dispatch_kernel(nch_ref, off_ref, dst_ref, zrow_ref, zflag_ref, nu_ref, base_ref, route_ref, x_ref,
                     xpad_hbm, dest_ref, sorted_even, sorted_odd, zero_buf, sem, zsem):
    i = pl.program_id(0)
    n_steps = pl.num_programs(0)
    n_exp = base_ref.shape[1]
    td = x_ref.shape[0]
    rb = sorted_even.shape[0]
    zrows = zero_buf.shape[0]
    bufs = (sorted_even, sorted_odd)
    parity = lax.rem(i, 2)

    @pl.when(i == 0)
    def _():
        zero_buf[...] = jnp.zeros(zero_buf.shape, F32)

        def zero_block(row, start):
            cp = pltpu.make_async_copy(zero_buf, xpad_hbm.at[pl.ds(pl.multiple_of(row, zrows), zrows), :], zsem)
            cp.start() if start else cp.wait()

        for start in (True, False):
            for e in range(n_exp):
                pl.when(zflag_ref[e] == 1)(functools.partial(zero_block, zrow_ref[e], start))

            def unused(blk, _):
                zero_block(blk * zrows, start)
                return 0
            lax.fori_loop(nu_ref[0], xpad_hbm.shape[0] // zrows, unused, 0)

    rt = route_ref[...].T
    eid = lax.broadcasted_iota(I32, (n_exp, td), 0).astype(F32)
    rowid = lax.broadcasted_iota(I32, (rb, td), 0).astype(F32)
    hit = None
    dests = []
    for k in range(2):
        mine = eid == rt[k:k + 1, :]
        rank = rt[2 + k:3 + k, :]
        pos = jnp.sum(jnp.where(mine, base_ref[0, :, 0:1], 0.0), axis=0, keepdims=True) + rank
        dests.append(jnp.sum(jnp.where(mine, base_ref[0, :, 1:2], 0.0), axis=0, keepdims=True) + rank)
        hit = (rowid == pos) if hit is None else hit | (rowid == pos)
    dest_ref[0] = jnp.concatenate(dests + [jnp.zeros((8 - len(dests), td), F32)], axis=0).astype(I32)
    perm = jnp.where(hit, 1.0, 0.0).astype(BF16)
    rows_sorted = _dot(perm, x_ref[...].astype(BF16))

    def runs(tile, buf, s, start):
        for e in range(n_exp):
            k = tile * n_exp + e
            src0, dst0 = off_ref[k], dst_ref[k]

            def one(j, _):
                cp = _chunk_copy(buf, xpad_hbm, s, pl.multiple_of(src0 + j * CHUNK, CHUNK),
                                 pl.multiple_of(dst0 + j * CHUNK, CHUNK))
                cp.start() if start else cp.wait()
                return 0
            lax.fori_loop(0, nch_ref[k], one, 0)

    def wait_runs(tile, buf, s):
        k_last = tile * n_exp + n_exp - 1
        total = pl.multiple_of(off_ref[k_last] + nch_ref[k_last] * CHUNK, CHUNK)
        pltpu.make_async_copy(buf.at[pl.ds(0, total), :], xpad_hbm.at[pl.ds(0, total), :], s).wait()

    def step(cur):
        bufs[cur][...] = rows_sorted

        @pl.when(i > 0)
        def _():
            wait_runs(i - 1, bufs[1 - cur], sem.at[1 - cur])

        runs(i, bufs[cur], sem.at[cur], True)

        @pl.when(i + 1 == n_steps)
        def _():
            wait_runs(i, bufs[cur], sem.at[cur])

    for cur in range(2):
        pl.when(parity == cur)(functools.partial(step, cur))


def _dispatch(x1, route, nch, off8, dst, zrow, zflag, n_used, base, n_rows_out):
    n, d = x1.shape
    td = DISPATCH_ROWS
    n_tiles = n // td
    n_exp = base.shape[1]
    rb = 2 * td + n_exp * CHUNK
    grid_spec = pltpu.PrefetchScalarGridSpec(
        num_scalar_prefetch=6,
        grid=(n_tiles,),
        in_specs=[pl.BlockSpec((1, n_exp, 2), lambda i, *_: (i, 0, 0)),
                  pl.BlockSpec((td, LANES), lambda i, *_: (i, 0)),
                  pl.BlockSpec((td, d), lambda i, *_: (i, 0))],
        out_specs=[pl.BlockSpec(memory_space=pl.ANY),
                   pl.BlockSpec((1, 8, td), lambda i, *_: (i, 0, 0))],
        scratch_shapes=[pltpu.VMEM((rb, d), F32), pltpu.VMEM((rb, d), F32),
                        pltpu.VMEM((MOE_BLOCK, d), F32),
                        pltpu.SemaphoreType.DMA((2,)), pltpu.SemaphoreType.DMA(())])
    return pl.pallas_call(
        _dispatch_kernel,
        grid_spec=grid_spec,
        out_shape=[jax.ShapeDtypeStruct((n_rows_out, d), F32),
                   jax.ShapeDtypeStruct((n_tiles, 8, td), I32)],
        compiler_params=_params(("arbitrary",), 48),
        name="dispatch",
    )(nch, off8, dst, zrow, zflag, n_used, base, route, x1)


def _experts_kernel(be_ref, nu_ref, x_hbm, wg_ref, wu_ref, wd_ref, o_ref, xbuf, sem, wg_sc, wu_sc, wd_sc):
    b = pl.program_id(0)
    n_used = nu_ref[0]
    rows = o_ref.shape[0]
    n_slots = xbuf.shape[0]

    def fetch(blk):
        slot = lax.rem(blk, n_slots)
        return pltpu.make_async_copy(x_hbm.at[pl.ds(pl.multiple_of(blk * rows, rows), rows), :],
                                     xbuf.at[slot], sem.at[slot])

    @pl.when(b == 0)
    def _():
        for blk in range(n_slots - 1):
            pl.when(blk < n_used)(lambda blk=blk: fetch(blk).start())

    @pl.when(b < n_used)
    def _():
        @pl.when(b + n_slots - 1 < n_used)
        def _():
            fetch(b + n_slots - 1).start()

        changed = (b == 0) | (be_ref[b] != be_ref[jnp.maximum(b - 1, 0)])

        @pl.when(changed)
        def _():
            wg_sc[...] = wg_ref[0].astype(BF16)
            wu_sc[...] = wu_ref[0].astype(BF16)
            wd_sc[...] = wd_ref[0].astype(BF16)

        fetch(b).wait()
        xb = xbuf[lax.rem(b, n_slots)].astype(BF16)
        g = _dot(xb, wg_sc[...])
        u = _dot(xb, wu_sc[...])
        h = (g * _sigmoid(g) * u).astype(BF16)
        o_ref[...] = _dot(h, wd_sc[...])

    @pl.when(b >= n_used)
    def _():
        o_ref[...] = jnp.zeros(o_ref.shape, F32)


def _experts(x_pad, block_expert, n_used, w_gate, w_up, w_down):
    d = x_pad.shape[1]
    n_blocks = block_expert.shape[0]
    rows = MOE_BLOCK
    d_exp = w_gate.shape[-1]
    grid_spec = pltpu.PrefetchScalarGridSpec(
        num_scalar_prefetch=2,
        grid=(n_blocks,),
        in_specs=[pl.BlockSpec(memory_space=pl.ANY),
                  pl.BlockSpec((1, d, d_exp), lambda b, be, nu: (be[b], 0, 0)),
                  pl.BlockSpec((1, d, d_exp), lambda b, be, nu: (be[b], 0, 0)),
                  pl.BlockSpec((1, d_exp, d), lambda b, be, nu: (be[b], 0, 0))],
        out_specs=pl.BlockSpec((rows, d), lambda b, be, nu: (b, 0)),
        scratch_shapes=[pltpu.VMEM((EXPERT_INPUT_SLOTS, rows, d), F32),
                        pltpu.SemaphoreType.DMA((EXPERT_INPUT_SLOTS,)),
                        pltpu.VMEM((d, d_exp), BF16),
                        pltpu.VMEM((d, d_exp), BF16),
                        pltpu.VMEM((d_exp, d), BF16)])
    return pl.pallas_call(
        _experts_kernel,
        grid_spec=grid_spec,
        out_shape=jax.ShapeDtypeStruct((n_blocks * rows, d), F32),
        compiler_params=_params(("arbitrary",), 44),
        name="experts",
    )(block_expert, n_used, x_pad, w_gate, w_up, w_down)


def _combine_kernel(idx_ref, idxn_ref, y_hbm, x1_ref, route_ref, lng_ref, lnb_ref, o_ref,
                    ybuf_even, ybuf_odd, sem, *, alpha):
    i = pl.program_id(0)
    n_steps = pl.num_programs(0)
    tm = x1_ref.shape[0]
    bufs = (ybuf_even, ybuf_odd)
    parity = lax.rem(i, 2)

    @pl.when(i == 0)
    def _():
        _gather_rows(y_hbm, idx_ref, bufs[0], sem.at[0], 2 * tm, False)

    def compute(cur):
        _gather_rows(y_hbm, idxn_ref, bufs[1 - cur], sem.at[1 - cur], 2 * tm, True)
        _wait_rows(y_hbm, bufs[cur], sem.at[cur], 2 * tm)
        ffn = (route_ref[:, 4:5] * bufs[cur][0:tm, :] + route_ref[:, 5:6] * bufs[cur][tm:2 * tm, :])
        o_ref[...] = _layer_norm(alpha * x1_ref[...] + ffn, lng_ref[...], lnb_ref[...])

        @pl.when(i + 1 == n_steps)
        def _():
            _wait_rows(y_hbm, bufs[1 - cur], sem.at[1 - cur], 2 * tm)

    for cur in range(2):
        pl.when(parity == cur)(functools.partial(compute, cur))


def _combine(y_pad, dest, x1, route, ln_g, ln_b, alpha):
    n, d = x1.shape
    tm = COMBINE_ROWS
    n_steps = n // tm
    idx3 = dest.reshape(n_steps, tm, 2).transpose(0, 2, 1).reshape(n_steps, 1, 2 * tm)
    smem_idx = lambda f: pl.BlockSpec((1, 1, 2 * tm), f, memory_space=pltpu.SMEM)
    return pl.pallas_call(
        functools.partial(_combine_kernel, alpha=alpha),
        grid=(n_steps,),
        in_specs=[smem_idx(lambda i: (i, 0, 0)),
                  smem_idx(lambda i: (jnp.minimum(i + 1, n_steps - 1), 0, 0)),
                  pl.BlockSpec(memory_space=pl.ANY),
                  pl.BlockSpec((tm, d), lambda i: (i, 0)),
                  pl.BlockSpec((tm, LANES), lambda i: (i, 0)),
                  pl.BlockSpec((1, d), lambda i: (0, 0)),
                  pl.BlockSpec((1, d), lambda i: (0, 0))],
        out_specs=pl.BlockSpec((tm, d), lambda i: (i, 0)),
        out_shape=jax.ShapeDtypeStruct((n, d), F32),
        scratch_shapes=[pltpu.VMEM((2 * tm, d), F32), pltpu.VMEM((2 * tm, d), F32),
                        pltpu.SemaphoreType.DMA((2,))],
        compiler_params=_params(("arbitrary",), 32),
        name="combine",
    )(idx3, idx3, y_pad, x1, route, ln_g, ln_b)


def _split_bf16(w):
    hi = w.astype(BF16)
    return hi, (w - hi.astype(F32)).astype(BF16)


def _layer(x, mem, tbl, rel_bias, w_in, w_mem_kv, w_br_moba, w_br_sb, w_br_mem, w_out, ln1_g, ln1_b,
           w_rg, b_rg, w_re, b_re, w_gate, w_up, w_down, ln2_g, ln2_b, alpha):
    bsz, seq, d = x.shape
    n = bsz * seq
    n_experts = w_re.shape[1]
    moba_w, sb_w, mem_w = N_MOBA_HEADS * HEAD_DIM, N_SB_HEADS * HEAD_DIM, N_MEM_HEADS * HEAD_DIM
    n_qkv = 3 * moba_w + 3 * sb_w + mem_w
    assert w_in.shape[1] == n_qkv + 3 * d and n_experts + N_GROUPS <= LANES and moba_w == sb_w

    x2 = x.reshape(n, d)
    qkv = _proj(x2, w_in[:, :n_qkv].astype(BF16))
    qkv3 = qkv.reshape(bsz, seq, n_qkv)
    y_a = _moba(qkv3, tbl, rel_bias, 0)
    y_b = _stickbreak(qkv3, 3 * moba_w // sb_w)
    kv = _memkv(mem.reshape(-1, d), w_mem_kv.astype(BF16)).reshape(bsz, mem.shape[1], 2 * mem_w)
    y_m = _mem_attention(qkv3, kv, (3 * moba_w + 3 * sb_w) // mem_w)

    w_r = jnp.zeros((d, LANES), F32).at[:, :n_experts].set(w_re).at[:, n_experts:n_experts + N_GROUPS].set(w_rg)
    b_r = jnp.zeros((1, LANES), F32).at[0, :n_experts].set(b_re).at[0, n_experts:n_experts + N_GROUPS].set(b_rg)
    wr_split = jnp.concatenate(_split_bf16(w_r), axis=1)
    x1, route, tile_counts = _merge(
        y_a.reshape(n, moba_w), y_b.reshape(n, sb_w), y_m.reshape(n, mem_w), x2, w_in[:, n_qkv:].astype(BF16),
        w_br_moba.astype(BF16), w_br_sb.astype(BF16), w_br_mem.astype(BF16), w_out.astype(BF16),
        ln1_g.reshape(1, d), ln1_b.reshape(1, d), wr_split, b_r, alpha, n_experts)

    rows, td = MOE_BLOCK, DISPATCH_ROWS
    assert td == MERGE_ROWS
    n_tiles = n // td
    tile_cnt = tile_counts[:, 0, :n_experts].astype(I32)
    n_chunks = (tile_cnt + CHUNK - 1) // CHUNK
    run_rows = n_chunks * CHUNK
    run_rank = jnp.cumsum(tile_cnt, axis=0) - tile_cnt
    run_off = jnp.cumsum(run_rows, axis=0) - run_rows
    sorted_off = jnp.cumsum(run_rows, axis=1) - run_rows
    expert_rows = jnp.sum(run_rows, axis=0)
    padded = (expert_rows + rows - 1) // rows * rows
    pend = jnp.cumsum(padded)
    pstart = pend - padded
    run_dst = pstart[None, :] + run_off
    n_blocks = (2 * n + n_tiles * n_experts * (CHUNK - 1)) // rows + n_experts
    block_start = jnp.arange(n_blocks, dtype=I32) * rows
    block_expert = jnp.minimum(jnp.sum((pend[None, :] <= block_start[:, None]).astype(I32), axis=1),
                               n_experts - 1)
    n_used = (pend[-1] // rows).astype(I32).reshape(1)
    base = jnp.stack([sorted_off - run_rank, run_dst - run_rank], axis=-1).astype(F32)
    x_pad, dest8 = _dispatch(x1, route, n_chunks.reshape(-1), sorted_off.reshape(-1), run_dst.reshape(-1),
                             jnp.maximum(pend - rows, 0), (padded > 0).astype(I32), n_used, base,
                             n_blocks * rows)
    dest = dest8[:, 0:2, :].transpose(0, 2, 1).reshape(n, 2)

    y_pad = _experts(x_pad, block_expert, n_used, w_gate, w_up, w_down)
    out = _combine(y_pad, dest, x1, route, ln2_g.reshape(1, d), ln2_b.reshape(1, d), alpha)
    return out.reshape(bsz, seq, d)


def kernel(x, mem, w_in, w_mem_kv, rel_bias, w_br_moba, w_br_sb, w_br_mem, w_out, ln1_g, ln1_b,
           w_router_group, b_router_group, w_router_expert, b_router_expert,
           w_gate, w_up, w_down, ln2_g, ln2_b):
    depth = w_in.shape[0]
    alpha = (2.0 * depth) ** 0.25
    tbl = _bias_table(rel_bias)
    for l in range(depth):
        x = _layer(x, mem, tbl, rel_bias, w_in[l], w_mem_kv[l], w_br_moba[l], w_br_sb[l], w_br_mem[l],
                   w_out[l], ln1_g[l], ln1_b[l], w_router_group[l], b_router_group[l],
                   w_router_expert[l], b_router_expert[l], w_gate[l], w_up[l], w_down[l],
                   ln2_g[l], ln2_b[l], alpha)
    return x
```

```python
import functools
import math

import jax
import jax.numpy as jnp
from jax import lax
from jax.experimental import pallas as pl
from jax.experimental.pallas import tpu as pltpu

F32, BF16, I32 = jnp.float32, jnp.bfloat16, jnp.int32

HEAD_DIM = 64
N_MOBA_HEADS = 6
N_SB_HEADS = 6
N_MEM_HEADS = 4
MOBA_BLOCK = 256
MOBA_TOPK = 3
N_BUCKETS = 32
MAX_DISTANCE = 128
N_GROUPS = 4
EXPERTS_PER_GROUP = 8
LN_EPS = 1e-5
NEG = -1e30

LANES = 128
VMEM_BYTES = 64 * 1024 * 1024
PAIR = LANES // HEAD_DIM

SB_ZERO_LOG = 110.0

PROJ_ROWS = 1024
MERGE_ROWS = 512
ATTN_ROWS = 256
MEM_ROWS = 1024
COMBINE_ROWS = 256
MOE_BLOCK = 512
INPUT_RING_SLOTS = 3


def _params(semantics, vmem_mb):
    return pltpu.CompilerParams(dimension_semantics=semantics,
                                vmem_limit_bytes=min(vmem_mb * 1024 * 1024, VMEM_BYTES))


def _dot(a, b):
    return jnp.dot(a, b, preferred_element_type=F32)


def _layer_norm(h, g, b):
    mu = jnp.mean(h, axis=-1, keepdims=True)
    d = h - mu
    var = jnp.mean(d * d, axis=-1, keepdims=True)
    return d * lax.rsqrt(var + LN_EPS) * g + b


def _sigmoid(x):
    return 1.0 / (1.0 + jnp.exp(-x))


def _head_rows(qt, row, hh, scale):
    keep = (row >= hh * HEAD_DIM) & (row < (hh + 1) * HEAD_DIM)
    return jnp.where(keep, qt * scale, 0.0).astype(BF16)


def _transposed_bf16(x):
    return x.astype(F32).T.astype(BF16)


def _proj_kernel(x_ref, wq_ref, qkv_ref):
    qkv_ref[...] = _dot(x_ref[...].astype(BF16), wq_ref[...]).astype(BF16)


def _proj(x2, w_qkv):
    n, d = x2.shape
    n_qkv = w_qkv.shape[1]
    tm = PROJ_ROWS
    return pl.pallas_call(
        _proj_kernel,
        grid=(n // tm,),
        in_specs=[pl.BlockSpec((tm, d), lambda i: (i, 0)),
                  pl.BlockSpec((d, n_qkv), lambda i: (0, 0))],
        out_specs=pl.BlockSpec((tm, n_qkv), lambda i: (i, 0)),
        out_shape=jax.ShapeDtypeStruct((n, n_qkv), BF16),
        compiler_params=_params(("arbitrary",), 40),
        name="proj",
    )(x2, w_qkv)


def _t5_bucket(rel):
    rel = jnp.maximum(rel, 0)
    max_exact = N_BUCKETS // 2
    rel_f = jnp.maximum(rel, 1).astype(F32)
    large = max_exact + (jnp.log(rel_f / max_exact) / math.log(MAX_DISTANCE / max_exact)
                         * (N_BUCKETS - max_exact)).astype(I32)
    large = jnp.minimum(large, N_BUCKETS - 1)
    return jnp.where(rel < max_exact, rel, large)


def _bias_table_kernel(rb_ref, o_ref):
    h = pl.program_id(0)
    blk = o_ref.shape[2]
    j = lax.broadcasted_iota(I32, (2 * blk, blk), 0)
    i = lax.broadcasted_iota(I32, (2 * blk, blk), 1)
    bucket = _t5_bucket(blk + i - j)
    acc = jnp.zeros((2 * blk, blk), F32)
    for b in range(N_BUCKETS):
        acc = jnp.where(bucket == b, rb_ref[b, h], acc)
    o_ref[0] = acc


def _bias_table(rel_bias):
    n_heads = rel_bias.shape[1]
    blk = MOBA_BLOCK
    return pl.pallas_call(
        _bias_table_kernel,
        grid=(n_heads,),
        in_specs=[pl.BlockSpec(memory_space=pltpu.SMEM)],
        out_specs=pl.BlockSpec((1, 2 * blk, blk), lambda h: (h, 0, 0)),
        out_shape=jax.ShapeDtypeStruct((n_heads, 2 * blk, blk), F32),
        compiler_params=_params(("arbitrary",), 32),
        name="bias_table",
    )(rel_bias)


def _fold_keys(x, op, final):
    while x.shape[0] > 8 and x.shape[0] % 2 == 0:
        half = x.shape[0] // 2
        x = op(x[:half], x[half:])
    return final(x, axis=0, keepdims=True)


def _head_queries(qt, row, hh, scale):
    p, sub = divmod(hh, PAIR)
    return _head_rows(qt[p * LANES:(p + 1) * LANES, :], row, sub, scale)


def _pair_lanes(hh):
    p = hh // PAIR
    return slice(p * LANES, (p + 1) * LANES)


def _head_dims(hh):
    return slice(hh * HEAD_DIM, (hh + 1) * HEAD_DIM)


def _moba_kernel(rb_ref, q_ref, k_ref, v_ref, tbl_ref, o_ref, kmean_sc, vt_sc, sel_sc, s_sc, p_sc, acc_sc):
    own = pl.program_id(1)
    blk, width = q_ref.shape[1], q_ref.shape[2]
    n_heads = width // HEAD_DIM
    nb = k_ref.shape[1] // blk
    scale = HEAD_DIM ** -0.5

    @pl.when(own == 0)
    def _():
        for n in range(nb):
            kb = k_ref[0, n * blk:(n + 1) * blk, :].astype(F32)
            kmean_sc[n:n + 1, :] = jnp.mean(kb, axis=0, keepdims=True)
            vt_sc[n] = _transposed_bf16(v_ref[0, n * blk:(n + 1) * blk, :])

    qt = q_ref[0].astype(F32).T
    row = lax.broadcasted_iota(I32, (LANES, blk), 0)
    blk_id = lax.broadcasted_iota(I32, (nb, blk), 0)
    valid = blk_id < own
    causal = (lax.broadcasted_iota(I32, (blk, blk), 0) <= lax.broadcasted_iota(I32, (blk, blk), 1))
    kmean = kmean_sc[...].astype(BF16)

    qs = [_head_queries(qt, row, hh, scale) for hh in range(n_heads)]
    gates = [_dot(kmean[:, _pair_lanes(hh)], _head_queries(qt, row, hh, 1.0)) for hh in range(n_heads)]

    def scores(n):
        offn = pl.multiple_of(n * blk, blk)
        return [_dot(k_ref[0, pl.ds(offn, blk), _pair_lanes(hh)], qs[hh]) for hh in range(n_heads)]

    own_scores = scores(own)
    prev_scores = scores(jnp.maximum(own - 1, 0))

    blk_f = blk_id.astype(F32)
    for hh in range(n_heads):
        gate = jnp.where(valid, gates[hh], NEG)
        picked = jnp.zeros((nb, blk), jnp.bool_)
        for _ in range(MOBA_TOPK):
            top = _fold_keys(gate, jnp.maximum, jnp.max)
            first = _fold_keys(jnp.where(gate == top, blk_f, float(nb)), jnp.minimum, jnp.min)
            hit = blk_f == first
            picked = picked | hit
            gate = jnp.where(hit, -jnp.inf, gate)
        sel_sc[hh] = jnp.where(valid & picked, 0.0, NEG)

    prev = jnp.maximum(own - 1, 0)
    no_prev = jnp.where(own > 0, 0.0, NEG)
    stats = []
    for hh, (s_own, s_prev) in enumerate(zip(own_scores, prev_scores)):
        s_own = jnp.where(causal, s_own + tbl_ref[hh, blk:, :], NEG)
        s_prev = s_prev + tbl_ref[hh, :blk, :]
        term = sel_sc[hh, pl.ds(prev, 1), :] + no_prev
        m0 = jnp.maximum(_fold_keys(s_own, jnp.maximum, jnp.max),
                         _fold_keys(s_prev, jnp.maximum, jnp.max) + term)
        p_own = jnp.exp(s_own - m0)
        p_prev = jnp.exp(s_prev - (m0 - term))
        l0 = _fold_keys(p_own, jnp.add, jnp.sum) + _fold_keys(p_prev, jnp.add, jnp.sum)
        stats.append((m0, l0, p_own.astype(BF16), p_prev.astype(BF16)))
    state = []
    for hh, (m0, l0, p_own, p_prev) in enumerate(stats):
        state += [m0, l0, _dot(vt_sc[own, _head_dims(hh), :], p_own) + _dot(vt_sc[prev, _head_dims(hh), :], p_prev)]
    state = tuple(state)

    def softmax_step(s, per_query, m_run, l_run):
        m_new = jnp.maximum(m_run, _fold_keys(s, jnp.maximum, jnp.max) + per_query)
        a = jnp.exp(m_run - m_new)
        pn = jnp.exp(s - (m_new - per_query))
        return m_new, a * l_run + _fold_keys(pn, jnp.add, jnp.sum), a, pn.astype(BF16)

    n_far = own - 1
    last_far = jnp.maximum(n_far - 1, 0)
    for hh, s in enumerate(scores(0)):
        s_sc[hh] = s
    p_sc[...] = jnp.zeros(p_sc.shape, BF16)

    for hh in range(n_heads):
        acc_sc[hh] = state[3 * hh + 2]

    def far_block(n, carry):
        ml, a_prev = carry[:2 * n_heads], carry[2 * n_heads:]
        n_prev = jnp.maximum(n - 1, 0)
        pv = [_dot(vt_sc[n_prev, _head_dims(hh), :], p_sc[hh]) for hh in range(n_heads)]
        s_next = scores(jnp.minimum(n + 1, last_far))
        stats = [softmax_step(s_sc[hh], rb_ref[N_BUCKETS - 1, hh] + sel_sc[hh, pl.ds(n, 1), :],
                              ml[2 * hh], ml[2 * hh + 1]) for hh in range(n_heads)]
        new, a_new = [], []
        for hh, (m_new, l_new, a, pn) in enumerate(stats):
            p_sc[hh] = pn
            s_sc[hh] = s_next[hh]
            acc_sc[hh] = a_prev[hh] * acc_sc[hh] + pv[hh]
            new += [m_new, l_new]
            a_new.append(a)
        return tuple(new) + tuple(a_new)

    ones = jnp.ones((1, blk), F32)
    ml0 = tuple(v for hh in range(n_heads) for v in state[3 * hh:3 * hh + 2])
    carry = lax.fori_loop(0, n_far, far_block, ml0 + (ones,) * n_heads)
    ml, a_prev = carry[:2 * n_heads], carry[2 * n_heads:]
    out_t = jnp.concatenate(
        [(a_prev[hh] * acc_sc[hh] + _dot(vt_sc[last_far, _head_dims(hh), :], p_sc[hh])) / ml[2 * hh + 1]
         for hh in range(n_heads)], axis=0)
    o_ref[0] = out_t.T.astype(o_ref.dtype)


def _moba(qkv3, tbl, rel_bias, col0):
    bsz, seq, _ = qkv3.shape
    assert MAX_DISTANCE <= MOBA_BLOCK and seq % MOBA_BLOCK == 0
    width = N_MOBA_HEADS * HEAD_DIM
    blk = MOBA_BLOCK
    nb = seq // blk
    return pl.pallas_call(
        _moba_kernel,
        grid=(bsz, nb),
        in_specs=[pl.BlockSpec(memory_space=pltpu.SMEM),
                  pl.BlockSpec((1, blk, width), lambda b, i: (b, i, col0)),
                  pl.BlockSpec((1, seq, width), lambda b, i: (b, 0, col0 + 1)),
                  pl.BlockSpec((1, seq, width), lambda b, i: (b, 0, col0 + 2)),
                  pl.BlockSpec((N_MOBA_HEADS, 2 * blk, blk), lambda b, i: (0, 0, 0))],
        out_specs=pl.BlockSpec((1, blk, width), lambda b, i: (b, i, 0)),
        out_shape=jax.ShapeDtypeStruct((bsz, seq, width), BF16),
        scratch_shapes=[pltpu.VMEM((nb, width), F32),
                        pltpu.VMEM((nb, width, blk), BF16),
                        pltpu.VMEM((N_MOBA_HEADS, nb, blk), F32),
                        pltpu.VMEM((N_MOBA_HEADS, blk, blk), F32),
                        pltpu.VMEM((N_MOBA_HEADS, blk, blk), BF16),
                        pltpu.VMEM((N_MOBA_HEADS, HEAD_DIM, blk), F32)],
        compiler_params=_params(("arbitrary", "arbitrary"), 48),
        name="moba",
    )(rel_bias, qkv3, qkv3, qkv3, tbl)


def _sb_kernel(q_ref, k_ref, v_ref, o_ref, vt_sc):
    qi = pl.program_id(1)
    t, width = q_ref.shape[1], q_ref.shape[2]
    n_heads = width // HEAD_DIM
    nb = k_ref.shape[1] // t
    scale = HEAD_DIM ** -0.5

    @pl.when(qi == 0)
    def _():
        for n in range(nb):
            vt_sc[n] = _transposed_bf16(v_ref[0, n * t:(n + 1) * t, :])

    qt = q_ref[0].astype(F32).T
    row = lax.broadcasted_iota(I32, (LANES, t), 0)
    key = lax.broadcasted_iota(I32, (t, t), 0)
    qry = lax.broadcasted_iota(I32, (t, t), 1)
    strict = key < qry
    tri = jnp.where(qry >= key, 1.0, 0.0).astype(BF16)
    qs = [_head_queries(qt, row, hh, scale) for hh in range(n_heads)]

    def blocks(tiles):
        zs = [[_dot(k_ref[0, pl.ds(pl.multiple_of(j * t, t), t), _pair_lanes(hh)], qs[hh])
               for hh in range(n_heads)] for j, _ in tiles]
        csums = []
        for (_, diagonal), zt in zip(tiles, zs):
            row_sums = []
            for z in zt:
                sp = jnp.maximum(z, 0.0) + jnp.log(1.0 + jnp.exp(-jnp.abs(z)))
                if diagonal:
                    sp = jnp.where(strict, sp, 0.0)
                hi = sp.astype(BF16)
                lo = (sp - hi.astype(F32)).astype(BF16)
                row_sums.append(_dot(tri, hi) + _dot(tri, lo))
            csums.append(row_sums)
        out = []
        for (j, diagonal), zt, ct in zip(tiles, zs, csums):
            pvs = []
            for hh in range(n_heads):
                a = jnp.exp(zt[hh] - ct[hh])
                if diagonal:
                    a = jnp.where(strict, a, 0.0)
                pvs.append(_dot(vt_sc[j, _head_dims(hh), :], a.astype(BF16)))
            out.append((pvs, [c[0:1, :] for c in ct]))
        return out

    has_prev = qi > 0
    (pv_d, tot_d), (pv_p, tot_p) = blocks([(qi, True), (jnp.maximum(qi - 1, 0), False)])
    accs = tuple(d + jnp.where(has_prev, p * jnp.exp(-c), 0.0) for d, p, c in zip(pv_d, pv_p, tot_d))
    carries = tuple(c + jnp.where(has_prev, p, 0.0) for c, p in zip(tot_d, tot_p))

    def lowest(carries):
        return jnp.min(functools.reduce(jnp.minimum, carries))

    def cond(state):
        j, cmin, _, _ = state
        return (j >= 0) & (cmin < SB_ZERO_LOG)

    def body(state):
        j, _, carries, accs = state
        (pvs, totals), = blocks([(j, False)])
        accs = tuple(acc + pv * jnp.exp(-c) for acc, pv, c in zip(accs, pvs, carries))
        carries = tuple(c + total for c, total in zip(carries, totals))
        return j - 1, lowest(carries), carries, accs

    state = lax.while_loop(cond, body, (qi - 2, lowest(carries), carries, accs))
    out_t = jnp.concatenate(list(state[3]), axis=0)
    o_ref[0] = out_t.T.astype(o_ref.dtype)


def _stickbreak(qkv3, col0):
    bsz, seq, _ = qkv3.shape
    width = N_SB_HEADS * HEAD_DIM
    t = ATTN_ROWS
    return pl.pallas_call(
        _sb_kernel,
        grid=(bsz, seq // t),
        in_specs=[pl.BlockSpec((1, t, width), lambda b, i: (b, i, col0)),
                  pl.BlockSpec((1, seq, width), lambda b, i: (b, 0, col0 + 1)),
                  pl.BlockSpec((1, seq, width), lambda b, i: (b, 0, col0 + 2))],
        out_specs=pl.BlockSpec((1, t, width), lambda b, i: (b, i, 0)),
        out_shape=jax.ShapeDtypeStruct((bsz, seq, width), BF16),
        scratch_shapes=[pltpu.VMEM((seq // t, width, t), BF16)],
        compiler_params=_params(("arbitrary", "arbitrary"), 48),
        name="stickbreak",
    )(qkv3, qkv3, qkv3)


def _memkv_kernel(m_ref, w_ref, o_ref):
    o_ref[...] = _dot(m_ref[...].astype(BF16), w_ref[...]).astype(BF16)


def _memkv(mem2, w_bf16):
    n, d = mem2.shape
    width = w_bf16.shape[1]
    tm = min(n, 512)
    return pl.pallas_call(
        _memkv_kernel,
        grid=(n // tm,),
        in_specs=[pl.BlockSpec((tm, d), lambda i: (i, 0)),
                  pl.BlockSpec((d, width), lambda i: (0, 0))],
        out_specs=pl.BlockSpec((tm, width), lambda i: (i, 0)),
        out_shape=jax.ShapeDtypeStruct((n, width), BF16),
        compiler_params=_params(("arbitrary",), 32),
        name="memkv",
    )(mem2, w_bf16)


def _mem_kernel(q_ref, k_ref, v_ref, o_ref, vt_sc):
    t, width = q_ref.shape[1], q_ref.shape[2]
    n_heads = width // HEAD_DIM
    scale = HEAD_DIM ** -0.5

    @pl.when(pl.program_id(1) == 0)
    def _():
        vt_sc[...] = _transposed_bf16(v_ref[0])

    qt = q_ref[0].astype(F32).T
    row = lax.broadcasted_iota(I32, (LANES, t), 0)
    qs = [_head_queries(qt, row, hh, scale) for hh in range(n_heads)]
    scores = [_dot(k_ref[0, :, _pair_lanes(hh)], qs[hh]) for hh in range(n_heads)]
    probs = []
    for s in scores:
        e = jnp.exp(s - _fold_keys(s, jnp.maximum, jnp.max))
        probs.append((e * (1.0 / _fold_keys(e, jnp.add, jnp.sum))).astype(BF16))
    out_t = jnp.concatenate([_dot(vt_sc[_head_dims(hh), :], probs[hh]) for hh in range(n_heads)], axis=0)
    o_ref[0] = out_t.T.astype(o_ref.dtype)


def _mem_attention(qkv3, kv3, qcol):
    bsz, seq, _ = qkv3.shape
    n_mem = kv3.shape[1]
    width = N_MEM_HEADS * HEAD_DIM
    t = MEM_ROWS
    return pl.pallas_call(
        _mem_kernel,
        grid=(bsz, seq // t),
        in_specs=[pl.BlockSpec((1, t, width), lambda b, i: (b, i, qcol)),
                  pl.BlockSpec((1, n_mem, width), lambda b, i: (b, 0, 0)),
                  pl.BlockSpec((1, n_mem, width), lambda b, i: (b, 0, 1))],
        out_specs=pl.BlockSpec((1, t, width), lambda b, i: (b, i, 0)),
        out_shape=jax.ShapeDtypeStruct((bsz, seq, width), BF16),
        scratch_shapes=[pltpu.VMEM((width, n_mem), BF16)],
        compiler_params=_params(("arbitrary", "arbitrary"), 32),
        name="mem_attention",
    )(qkv3, kv3, kv3)


def _merge_kernel(ya_ref, yb_ref, ym_ref, x_ref, wg_ref, wa_ref, wb_ref, wm_ref, wo_ref,
                  lng_ref, lnb_ref, wrh_ref, br_ref, x1_ref, route_ref, tcnt_ref, h_sc, cnt_ref,
                  *, alpha, n_experts):
    step = pl.program_id(0)
    tm, d = x_ref.shape
    slot = lax.rem(step, 2)

    @pl.when(step == 0)
    def _():
        cnt_ref[...] = jnp.zeros(cnt_ref.shape, F32)
        h_sc[...] = jnp.zeros(h_sc.shape, F32)

    x = x_ref[...]
    xb = x.astype(BF16)
    branches = ((ya_ref, wa_ref), (yb_ref, wb_ref), (ym_ref, wm_ref))

    def branch_term(k):
        gate_logits = _dot(xb, wg_ref[:, k * d:(k + 1) * d])
        return gate_logits, _dot(branches[k][0][...], branches[k][1][...])

    x1 = _layer_norm(h_sc[1 - slot], lng_ref[...], lnb_ref[...])
    x1_ref[...] = x1
    xh = x1.astype(BF16)
    xl = (x1 - xh.astype(F32)).astype(BF16)

    first_term = branch_term(0)

    by_hi = _dot(xh, wrh_ref[...])
    logits = (by_hi[:, :LANES] + (by_hi[:, LANES:] + _dot(xl, wrh_ref[:, :LANES]))
              + br_ref[...])

    merged = None
    for k in range(len(branches)):
        gate_logits, y = first_term if k == 0 else branch_term(k)
        term = _sigmoid(gate_logits) * y
        merged = term if merged is None else merged + term
    h_sc[slot] = alpha * x + _dot(merged.astype(BF16), wo_ref[...])

    col = lax.broadcasted_iota(I32, (tm, LANES), 1)
    colf = col.astype(F32)
    big = float(LANES)
    gmask = (col >= n_experts) & (col < n_experts + N_GROUPS)
    lg = jnp.where(gmask, logits, -jnp.inf)
    gmax = jnp.max(lg, axis=1, keepdims=True)
    gidx = jnp.min(jnp.where(lg == gmax, colf, big), axis=1, keepdims=True) - n_experts
    g_p = 1.0 / jnp.sum(jnp.where(gmask, jnp.exp(logits - gmax), 0.0), axis=1, keepdims=True)
    lo_col = gidx * EXPERTS_PER_GROUP
    emask = (colf >= lo_col) & (colf < lo_col + EXPERTS_PER_GROUP)
    le = jnp.where(emask, logits, -jnp.inf)
    l1 = jnp.max(le, axis=1, keepdims=True)
    i1 = jnp.min(jnp.where(le == l1, colf, big), axis=1, keepdims=True)
    le2 = jnp.where(colf == i1, -jnp.inf, le)
    l2 = jnp.max(le2, axis=1, keepdims=True)
    i2 = jnp.min(jnp.where(le2 == l2, colf, big), axis=1, keepdims=True)
    e2 = jnp.exp(l2 - l1)
    gate1 = g_p * (1.0 / (1.0 + e2))
    gate2 = g_p * (e2 / (1.0 + e2))

    oh1 = colf == i1
    oh2 = colf == i2
    cnt = jnp.where((oh1 | oh2) & (step > 0), 1.0, 0.0)
    rr = lax.broadcasted_iota(I32, (tm, tm), 0)
    cc = lax.broadcasted_iota(I32, (tm, tm), 1)
    before = jnp.where(cc < rr, 1.0, 0.0).astype(BF16)
    base = _dot(before, cnt.astype(BF16)) + cnt_ref[0:1, :]
    rank1 = jnp.sum(jnp.where(oh1, base, 0.0), axis=1, keepdims=True)
    rank2 = jnp.sum(jnp.where(oh2, base, 0.0), axis=1, keepdims=True)
    tile_cnt = jnp.sum(cnt, axis=0, keepdims=True)
    cnt_ref[...] = cnt_ref[...] + tile_cnt
    tcnt_ref[0] = jnp.broadcast_to(tile_cnt, tcnt_ref.shape[1:])

    route = jnp.zeros((tm, LANES), F32)
    for k, val in enumerate((i1, i2, rank1, rank2, gate1, gate2)):
        route = jnp.where(col == k, val, route)
    route_ref[...] = route


def _merge(ya, yb, ym, x2, w_gates, wa, wb, wm, wo, ln_g, ln_b, wr_split, b_r, alpha, n_experts):
    n, d = x2.shape
    tm = MERGE_ROWS
    n_tiles = n // tm
    row_in = lambda w: pl.BlockSpec((tm, w), lambda i: (jnp.minimum(i, n_tiles - 1), 0))
    row_out = lambda w: pl.BlockSpec((tm, w), lambda i: (jnp.maximum(i - 1, 0), 0))
    full = lambda a: pl.BlockSpec(a.shape, lambda i: (0,) * a.ndim)
    return pl.pallas_call(
        functools.partial(_merge_kernel, alpha=alpha, n_experts=n_experts),
        grid=(n_tiles + 1,),
        in_specs=[row_in(ya.shape[1]), row_in(yb.shape[1]), row_in(ym.shape[1]), row_in(d),
                  full(w_gates), full(wa), full(wb), full(wm), full(wo), full(ln_g), full(ln_b),
                  full(wr_split), full(b_r)],
        out_specs=[row_out(d), row_out(LANES),
                   pl.BlockSpec((1, 8, LANES), lambda i: (jnp.maximum(i - 1, 0), 0, 0))],
        out_shape=[jax.ShapeDtypeStruct((n, d), F32),
                   jax.ShapeDtypeStruct((n, LANES), F32),
                   jax.ShapeDtypeStruct((n_tiles, 8, LANES), F32)],
        scratch_shapes=[pltpu.VMEM((2, tm, d), F32), pltpu.VMEM((8, LANES), F32)],
        compiler_params=_params(("arbitrary",), 56),
        name="merge",
    )(ya, yb, ym, x2, w_gates, wa, wb, wm, wo, ln_g, ln_b, wr_split, b_r)


def _row_copy(src_hbm, row, buf, sem, r):
    return pltpu.make_async_copy(src_hbm.at[pl.ds(row, 1), :], buf.at[pl.ds(r, 1), :], sem)


def _gather_rows(src_hbm, idx_ref, buf, sem, n_rows, unrolled):
    if unrolled:
        for r in range(n_rows):
            _row_copy(src_hbm, idx_ref[0, 0, r], buf, sem, r).start()
    else:
        def issue(r, _):
            _row_copy(src_hbm, idx_ref[0, 0, r], buf, sem, r).start()
            return 0
        lax.fori_loop(0, n_rows, issue, 0)


def _wait_rows(src_hbm, buf, sem, n_rows):
    pltpu.make_async_copy(src_hbm.at[pl.ds(0, n_rows), :], buf, sem).wait()


DISPATCH_ROWS = 512
CHUNK = 8


def _chunk_copy(buf, xpad_hbm, sem, src_row, dst_row):
    return pltpu.make_async_copy(buf.at[pl.ds(src_row, CHUNK), :], xpad_hbm.at[pl.ds(dst_row, CHUNK), :], sem)


def _dispatch_kernel(nch_ref, off_ref, dst_ref, zrow_ref, zflag_ref, nu_ref, base_ref, route_ref, x_hbm,
                     xpad_hbm, dest_ref, sorted_even, sorted_odd, zero_buf, xring, sem, zsem, xsem):
    i = pl.program_id(0)
    n_steps = pl.num_programs(0)
    n_exp = base_ref.shape[1]
    n_slots, td = xring.shape[0], xring.shape[1]
    rb = sorted_even.shape[0]
    zrows = zero_buf.shape[0]
    bufs = (sorted_even, sorted_odd)
    parity = lax.rem(i, 2)

    def fetch(tile):
        slot = lax.rem(tile, n_slots)
        return pltpu.make_async_copy(x_hbm.at[pl.ds(pl.multiple_of(tile * td, td), td), :],
                                     xring.at[slot], xsem.at[slot])

    @pl.when(i == 0)
    def _():
        for tile in range(n_slots - 1):
            pl.when(tile < n_steps)(lambda tile=tile: fetch(tile).start())

    @pl.when(i + n_slots - 1 < n_steps)
    def _():
        fetch(i + n_slots - 1).start()

    @pl.when(i == 0)
    def _():
        zero_buf[...] = jnp.zeros(zero_buf.shape, F32)

        def zero_block(row, start):
            cp = pltpu.make_async_copy(zero_buf, xpad_hbm.at[pl.ds(pl.multiple_of(row, zrows), zrows), :], zsem)
            cp.start() if start else cp.wait()

        for start in (True, False):
            for e in range(n_exp):
                pl.when(zflag_ref[e] == 1)(functools.partial(zero_block, zrow_ref[e], start))

            def unused(blk, _):
                zero_block(blk * zrows, start)
                return 0
            lax.fori_loop(nu_ref[0], xpad_hbm.shape[0] // zrows, unused, 0)

    rt = route_ref[...].T
    eid = lax.broadcasted_iota(I32, (n_exp, td), 0).astype(F32)
    rowid = lax.broadcasted_iota(I32, (rb, td), 0).astype(F32)
    hit = None
    dests = []
    for k in range(2):
        mine = eid == rt[k:k + 1, :]
        rank = rt[2 + k:3 + k, :]
        pos = jnp.sum(jnp.where(mine, base_ref[0, :, 0:1], 0.0), axis=0, keepdims=True) + rank
        dests.append(jnp.sum(jnp.where(mine, base_ref[0, :, 1:2], 0.0), axis=0, keepdims=True) + rank)
        hit = (rowid == pos) if hit is None else hit | (rowid == pos)
    dest_ref[0] = jnp.concatenate(dests + [jnp.zeros((8 - len(dests), td), F32)], axis=0).astype(I32)
    perm = jnp.where(hit, 1.0, 0.0).astype(BF16)
    fetch(i).wait()
    rows_sorted = _dot(perm, xring[lax.rem(i, n_slots)].astype(BF16))

    def start_runs(tile, buf, s):
        for e in range(n_exp):
            k = tile * n_exp + e
            src0, dst0 = off_ref[k], dst_ref[k]

            def one(j, _):
                _chunk_copy(buf, xpad_hbm, s, pl.multiple_of(src0 + j * CHUNK, CHUNK),
                            pl.multiple_of(dst0 + j * CHUNK, CHUNK)).start()
                return 0
            lax.fori_loop(0, nch_ref[k], one, 0)

    def wait_runs(tile, buf, s):
        k_last = tile * n_exp + n_exp - 1
        total = pl.multiple_of(off_ref[k_last] + nch_ref[k_last] * CHUNK, CHUNK)
        pltpu.make_async_copy(buf.at[pl.ds(0, total), :], xpad_hbm.at[pl.ds(0, total), :], s).wait()

    def step(cur):
        bufs[cur][...] = rows_sorted

        @pl.when(i > 0)
        def _():
            wait_runs(i - 1, bufs[1 - cur], sem.at[1 - cur])

        start_runs(i, bufs[cur], sem.at[cur])

        @pl.when(i + 1 == n_steps)
        def _():
            wait_runs(i, bufs[cur], sem.at[cur])

    for cur in range(2):
        pl.when(parity == cur)(functools.partial(step, cur))


def _dispatch(x1, route, nch, off8, dst, zrow, zflag, n_used, base, n_rows_out):
    n, d = x1.shape
    td = DISPATCH_ROWS
    n_tiles = n // td
    n_exp = base.shape[1]
    rb = 2 * td + n_exp * CHUNK
    grid_spec = pltpu.PrefetchScalarGridSpec(
        num_scalar_prefetch=6,
        grid=(n_tiles,),
        in_specs=[pl.BlockSpec((1, n_exp, 2), lambda i, *_: (i, 0, 0)),
                  pl.BlockSpec((td, LANES), lambda i, *_: (i, 0)),
                  pl.BlockSpec(memory_space=pl.ANY)],
        out_specs=[pl.BlockSpec(memory_space=pl.ANY),
                   pl.BlockSpec((1, 8, td), lambda i, *_: (i, 0, 0))],
        scratch_shapes=[pltpu.VMEM((rb, d), F32), pltpu.VMEM((rb, d), F32),
                        pltpu.VMEM((MOE_BLOCK, d), F32),
                        pltpu.VMEM((INPUT_RING_SLOTS, td, d), F32),
                        pltpu.SemaphoreType.DMA((2,)), pltpu.SemaphoreType.DMA(()),
                        pltpu.SemaphoreType.DMA((INPUT_RING_SLOTS,))])
    return pl.pallas_call(
        _dispatch_kernel,
        grid_spec=grid_spec,
        out_shape=[jax.ShapeDtypeStruct((n_rows_out, d), F32),
                   jax.ShapeDtypeStruct((n_tiles, 8, td), I32)],
        compiler_params=_params(("arbitrary",), 48),
        name="dispatch",
    )(nch, off8, dst, zrow, zflag, n_used, base, route, x1)


def _experts_kernel(be_ref, nu_ref, x_hbm, wg_ref, wu_ref, wd_ref, o_ref, xbuf, sem, wg_sc, wu_sc, wd_sc):
    b = pl.program_id(0)
    n_used = nu_ref[0]
    rows = o_ref.shape[0]
    n_slots = xbuf.shape[0]

    def fetch(blk):
        slot = lax.rem(blk, n_slots)
        return pltpu.make_async_copy(x_hbm.at[pl.ds(pl.multiple_of(blk * rows, rows), rows), :],
                                     xbuf.at[slot], sem.at[slot])

    @pl.when(b == 0)
    def _():
        for blk in range(n_slots - 1):
            pl.when(blk < n_used)(lambda blk=blk: fetch(blk).start())

    @pl.when(b < n_used)
    def _():
        @pl.when(b + n_slots - 1 < n_used)
        def _():
            fetch(b + n_slots - 1).start()

        changed = (b == 0) | (be_ref[b] != be_ref[jnp.maximum(b - 1, 0)])

        @pl.when(changed)
        def _():
            wg_sc[...] = wg_ref[0].astype(BF16)
            wu_sc[...] = wu_ref[0].astype(BF16)
            wd_sc[...] = wd_ref[0].astype(BF16)

        fetch(b).wait()
        xb = xbuf[lax.rem(b, n_slots)].astype(BF16)
        g = _dot(xb, wg_sc[...])
        u = _dot(xb, wu_sc[...])
        h = (g * _sigmoid(g) * u).astype(BF16)
        o_ref[...] = _dot(h, wd_sc[...])

    @pl.when(b >= n_used)
    def _():
        o_ref[...] = jnp.zeros(o_ref.shape, F32)


def _experts(x_pad, block_expert, n_used, w_gate, w_up, w_down):
    d = x_pad.shape[1]
    n_blocks = block_expert.shape[0]
    rows = MOE_BLOCK
    d_exp = w_gate.shape[-1]
    grid_spec = pltpu.PrefetchScalarGridSpec(
        num_scalar_prefetch=2,
        grid=(n_blocks,),
        in_specs=[pl.BlockSpec(memory_space=pl.ANY),
                  pl.BlockSpec((1, d, d_exp), lambda b, be, nu: (be[b], 0, 0)),
                  pl.BlockSpec((1, d, d_exp), lambda b, be, nu: (be[b], 0, 0)),
                  pl.BlockSpec((1, d_exp, d), lambda b, be, nu: (be[b], 0, 0))],
        out_specs=pl.BlockSpec((rows, d), lambda b, be, nu: (b, 0)),
        scratch_shapes=[pltpu.VMEM((INPUT_RING_SLOTS, rows, d), F32),
                        pltpu.SemaphoreType.DMA((INPUT_RING_SLOTS,)),
                        pltpu.VMEM((d, d_exp), BF16),
                        pltpu.VMEM((d, d_exp), BF16),
                        pltpu.VMEM((d_exp, d), BF16)])
    return pl.pallas_call(
        _experts_kernel,
        grid_spec=grid_spec,
        out_shape=jax.ShapeDtypeStruct((n_blocks * rows, d), F32),
        compiler_params=_params(("arbitrary",), 44),
        name="experts",
    )(block_expert, n_used, x_pad, w_gate, w_up, w_down)


def _combine_kernel(idx_ref, idxn_ref, y_hbm, x1_ref, route_ref, lng_ref, lnb_ref, o_ref,
                    ybuf_even, ybuf_odd, sem, *, alpha):
    i = pl.program_id(0)
    n_steps = pl.num_programs(0)
    tm = x1_ref.shape[0]
    bufs = (ybuf_even, ybuf_odd)
    parity = lax.rem(i, 2)

    @pl.when(i == 0)
    def _():
        _gather_rows(y_hbm, idx_ref, bufs[0], sem.at[0], 2 * tm, False)

    def compute(cur):
        _gather_rows(y_hbm, idxn_ref, bufs[1 - cur], sem.at[1 - cur], 2 * tm, True)
        _wait_rows(y_hbm, bufs[cur], sem.at[cur], 2 * tm)
        ffn = (route_ref[:, 4:5] * bufs[cur][0:tm, :] + route_ref[:, 5:6] * bufs[cur][tm:2 * tm, :])
        o_ref[...] = _layer_norm(alpha * x1_ref[...] + ffn, lng_ref[...], lnb_ref[...])

        @pl.when(i + 1 == n_steps)
        def _():
            _wait_rows(y_hbm, bufs[1 - cur], sem.at[1 - cur], 2 * tm)

    for cur in range(2):
        pl.when(parity == cur)(functools.partial(compute, cur))


def _combine(y_pad, dest, x1, route, ln_g, ln_b, alpha):
    n, d = x1.shape
    tm = COMBINE_ROWS
    n_steps = n // tm
    idx3 = dest.reshape(n_steps, tm, 2).transpose(0, 2, 1).reshape(n_steps, 1, 2 * tm)
    smem_idx = lambda f: pl.BlockSpec((1, 1, 2 * tm), f, memory_space=pltpu.SMEM)
    return pl.pallas_call(
        functools.partial(_combine_kernel, alpha=alpha),
        grid=(n_steps,),
        in_specs=[smem_idx(lambda i: (i, 0, 0)),
                  smem_idx(lambda i: (jnp.minimum(i + 1, n_steps - 1), 0, 0)),
                  pl.BlockSpec(memory_space=pl.ANY),
                  pl.BlockSpec((tm, d), lambda i: (i, 0)),
                  pl.BlockSpec((tm, LANES), lambda i: (i, 0)),
                  pl.BlockSpec((1, d), lambda i: (0, 0)),
                  pl.BlockSpec((1, d), lambda i: (0, 0))],
        out_specs=pl.BlockSpec((tm, d), lambda i: (i, 0)),
        out_shape=jax.ShapeDtypeStruct((n, d), F32),
        scratch_shapes=[pltpu.VMEM((2 * tm, d), F32), pltpu.VMEM((2 * tm, d), F32),
                        pltpu.SemaphoreType.DMA((2,))],
        compiler_params=_params(("arbitrary",), 32),
        name="combine",
    )(idx3, idx3, y_pad, x1, route, ln_g, ln_b)


def _split_bf16(w):
    hi = w.astype(BF16)
    return hi, (w - hi.astype(F32)).astype(BF16)


def _layer(x, mem, tbl, rel_bias, w_in, w_mem_kv, w_br_moba, w_br_sb, w_br_mem, w_out, ln1_g, ln1_b,
           w_rg, b_rg, w_re, b_re, w_gate, w_up, w_down, ln2_g, ln2_b, alpha):
    bsz, seq, d = x.shape
    n = bsz * seq
    n_experts = w_re.shape[1]
    moba_w, sb_w, mem_w = N_MOBA_HEADS * HEAD_DIM, N_SB_HEADS * HEAD_DIM, N_MEM_HEADS * HEAD_DIM
    n_qkv = 3 * moba_w + 3 * sb_w + mem_w
    assert w_in.shape[1] == n_qkv + 3 * d and n_experts + N_GROUPS <= LANES and moba_w == sb_w

    x2 = x.reshape(n, d)
    qkv = _proj(x2, w_in[:, :n_qkv].astype(BF16))
    qkv3 = qkv.reshape(bsz, seq, n_qkv)
    y_a = _moba(qkv3, tbl, rel_bias, 0)
    y_b = _stickbreak(qkv3, 3 * moba_w // sb_w)
    kv = _memkv(mem.reshape(-1, d), w_mem_kv.astype(BF16)).reshape(bsz, mem.shape[1], 2 * mem_w)
    y_m = _mem_attention(qkv3, kv, (3 * moba_w + 3 * sb_w) // mem_w)

    w_r = jnp.zeros((d, LANES), F32).at[:, :n_experts].set(w_re).at[:, n_experts:n_experts + N_GROUPS].set(w_rg)
    b_r = jnp.zeros((1, LANES), F32).at[0, :n_experts].set(b_re).at[0, n_experts:n_experts + N_GROUPS].set(b_rg)
    wr_split = jnp.concatenate(_split_bf16(w_r), axis=1)
    x1, route, tile_counts = _merge(
        y_a.reshape(n, moba_w), y_b.reshape(n, sb_w), y_m.reshape(n, mem_w), x2, w_in[:, n_qkv:].astype(BF16),
        w_br_moba.astype(BF16), w_br_sb.astype(BF16), w_br_mem.astype(BF16), w_out.astype(BF16),
        ln1_g.reshape(1, d), ln1_b.reshape(1, d), wr_split, b_r, alpha, n_experts)

    rows, td = MOE_BLOCK, DISPATCH_ROWS
    assert td == MERGE_ROWS
    n_tiles = n // td
    tile_cnt = tile_counts[:, 0, :n_experts].astype(I32)
    n_chunks = (tile_cnt + CHUNK - 1) // CHUNK
    run_rows = n_chunks * CHUNK
    run_rank = jnp.cumsum(tile_cnt, axis=0) - tile_cnt
    run_off = jnp.cumsum(run_rows, axis=0) - run_rows
    sorted_off = jnp.cumsum(run_rows, axis=1) - run_rows
    expert_rows = jnp.sum(run_rows, axis=0)
    padded = (expert_rows + rows - 1) // rows * rows
    pend = jnp.cumsum(padded)
    pstart = pend - padded
    run_dst = pstart[None, :] + run_off
    n_blocks = (2 * n + n_tiles * n_experts * (CHUNK - 1)) // rows + n_experts
    block_start = jnp.arange(n_blocks, dtype=I32) * rows
    block_expert = jnp.minimum(jnp.sum((pend[None, :] <= block_start[:, None]).astype(I32), axis=1),
                               n_experts - 1)
    n_used = (pend[-1] // rows).astype(I32).reshape(1)
    base = jnp.stack([sorted_off - run_rank, run_dst - run_rank], axis=-1).astype(F32)
    x_pad, dest8 = _dispatch(x1, route, n_chunks.reshape(-1), sorted_off.reshape(-1), run_dst.reshape(-1),
                             jnp.maximum(pend - rows, 0), (padded > 0).astype(I32), n_used, base,
                             n_blocks * rows)
    dest = dest8[:, 0:2, :].transpose(0, 2, 1).reshape(n, 2)

    y_pad = _experts(x_pad, block_expert, n_used, w_gate, w_up, w_down)
    out = _combine(y_pad, dest, x1, route, ln2_g.reshape(1, d), ln2_b.reshape(1, d), alpha)
    return out.reshape(bsz, seq, d)


def kernel(x, mem, w_in, w_mem_kv, rel_bias, w_br_moba, w_br_sb, w_br_mem, w_out, ln1_g, ln1_b,
           w_router_group, b_router_group, w_router_expert, b_router_expert,
           w_gate, w_up, w_down, ln2_g, ln2_b):
    depth = w_in.shape[0]
    alpha = (2.0 * depth) ** 0.25
    tbl = _bias_table(rel_bias)
    for l in range(depth):
        x = _layer(x, mem, tbl, rel_bias, w_in[l], w_mem_kv[l], w_br_moba[l], w_br_sb[l], w_br_mem[l],
                   w_out[l], ln1_g[l], ln1_b[l], w_router_group[l], b_router_group[l],
                   w_router_expert[l], b_router_expert[l], w_gate[l], w_up[l], w_down[l],
                   ln2_g[l], ln2_b[l], alpha)
    return x
```

```python
import functools
import math

import jax
import jax.numpy as jnp
from jax import lax
from jax.experimental import pallas as pl
from jax.experimental.pallas import tpu as pltpu

F32, BF16, I32 = jnp.float32, jnp.bfloat16, jnp.int32

HEAD_DIM = 64
N_MOBA_HEADS = 6
N_SB_HEADS = 6
N_MEM_HEADS = 4
MOBA_BLOCK = 256
MOBA_TOPK = 3
N_BUCKETS = 32
MAX_DISTANCE = 128
N_GROUPS = 4
EXPERTS_PER_GROUP = 8
LN_EPS = 1e-5
NEG = -1e30

LANES = 128
VMEM_BYTES = 64 * 1024 * 1024
PAIR = LANES // HEAD_DIM

SB_ZERO_LOG = 110.0

PROJ_ROWS = 1024
MERGE_ROWS = 512
ATTN_ROWS = 256
MEM_ROWS = 1024
COMBINE_ROWS = 256
MOE_BLOCK = 512
EXPERT_INPUT_SLOTS = 3


def _params(semantics, vmem_mb):
    return pltpu.CompilerParams(dimension_semantics=semantics,
                                vmem_limit_bytes=min(vmem_mb * 1024 * 1024, VMEM_BYTES))


def _dot(a, b):
    return jnp.dot(a, b, preferred_element_type=F32)


def _layer_norm(h, g, b):
    mu = jnp.mean(h, axis=-1, keepdims=True)
    d = h - mu
    var = jnp.mean(d * d, axis=-1, keepdims=True)
    return d * lax.rsqrt(var + LN_EPS) * g + b


def _sigmoid(x):
    return 1.0 / (1.0 + jnp.exp(-x))


def _head_rows(qt, row, hh, scale):
    keep = (row >= hh * HEAD_DIM) & (row < (hh + 1) * HEAD_DIM)
    return jnp.where(keep, qt * scale, 0.0).astype(BF16)


def _transposed_bf16(x):
    return x.astype(F32).T.astype(BF16)


def _proj_kernel(x_ref, wq_ref, qkv_ref):
    qkv_ref[...] = _dot(x_ref[...].astype(BF16), wq_ref[...]).astype(BF16)


def _proj(x2, w_qkv):
    n, d = x2.shape
    n_qkv = w_qkv.shape[1]
    tm = PROJ_ROWS
    return pl.pallas_call(
        _proj_kernel,
        grid=(n // tm,),
        in_specs=[pl.BlockSpec((tm, d), lambda i: (i, 0)),
                  pl.BlockSpec((d, n_qkv), lambda i: (0, 0))],
        out_specs=pl.BlockSpec((tm, n_qkv), lambda i: (i, 0)),
        out_shape=jax.ShapeDtypeStruct((n, n_qkv), BF16),
        compiler_params=_params(("arbitrary",), 40),
        name="proj",
    )(x2, w_qkv)


def _t5_bucket(rel):
    rel = jnp.maximum(rel, 0)
    max_exact = N_BUCKETS // 2
    rel_f = jnp.maximum(rel, 1).astype(F32)
    large = max_exact + (jnp.log(rel_f / max_exact) / math.log(MAX_DISTANCE / max_exact)
                         * (N_BUCKETS - max_exact)).astype(I32)
    large = jnp.minimum(large, N_BUCKETS - 1)
    return jnp.where(rel < max_exact, rel, large)


def _bias_table_kernel(rb_ref, o_ref):
    h = pl.program_id(0)
    blk = o_ref.shape[2]
    j = lax.broadcasted_iota(I32, (2 * blk, blk), 0)
    i = lax.broadcasted_iota(I32, (2 * blk, blk), 1)
    bucket = _t5_bucket(blk + i - j)
    acc = jnp.zeros((2 * blk, blk), F32)
    for b in range(N_BUCKETS):
        acc = jnp.where(bucket == b, rb_ref[b, h], acc)
    o_ref[0] = acc


def _bias_table(rel_bias):
    n_heads = rel_bias.shape[1]
    blk = MOBA_BLOCK
    return pl.pallas_call(
        _bias_table_kernel,
        grid=(n_heads,),
        in_specs=[pl.BlockSpec(memory_space=pltpu.SMEM)],
        out_specs=pl.BlockSpec((1, 2 * blk, blk), lambda h: (h, 0, 0)),
        out_shape=jax.ShapeDtypeStruct((n_heads, 2 * blk, blk), F32),
        compiler_params=_params(("arbitrary",), 32),
        name="bias_table",
    )(rel_bias)


def _fold_keys(x, op, final):
    while x.shape[0] > 8 and x.shape[0] % 2 == 0:
        half = x.shape[0] // 2
        x = op(x[:half], x[half:])
    return final(x, axis=0, keepdims=True)


def _head_queries(qt, row, hh, scale):
    p, sub = divmod(hh, PAIR)
    return _head_rows(qt[p * LANES:(p + 1) * LANES, :], row, sub, scale)


def _pair_lanes(hh):
    p = hh // PAIR
    return slice(p * LANES, (p + 1) * LANES)


def _head_dims(hh):
    return slice(hh * HEAD_DIM, (hh + 1) * HEAD_DIM)


def _moba_kernel(rb_ref, q_ref, k_ref, v_ref, tbl_ref, o_ref, kmean_sc, vt_sc, sel_sc, s_sc, p_sc, acc_sc):
    own = pl.program_id(1)
    blk, width = q_ref.shape[1], q_ref.shape[2]
    n_heads = width // HEAD_DIM
    nb = k_ref.shape[1] // blk
    scale = HEAD_DIM ** -0.5

    @pl.when(own == 0)
    def _():
        for n in range(nb):
            kb = k_ref[0, n * blk:(n + 1) * blk, :].astype(F32)
            kmean_sc[n:n + 1, :] = jnp.mean(kb, axis=0, keepdims=True)
            vt_sc[n] = _transposed_bf16(v_ref[0, n * blk:(n + 1) * blk, :])

    qt = q_ref[0].astype(F32).T
    row = lax.broadcasted_iota(I32, (LANES, blk), 0)
    blk_id = lax.broadcasted_iota(I32, (nb, blk), 0)
    valid = blk_id < own
    causal = (lax.broadcasted_iota(I32, (blk, blk), 0) <= lax.broadcasted_iota(I32, (blk, blk), 1))
    kmean = kmean_sc[...].astype(BF16)

    qs = [_head_queries(qt, row, hh, scale) for hh in range(n_heads)]
    gates = [_dot(kmean[:, _pair_lanes(hh)], _head_queries(qt, row, hh, 1.0)) for hh in range(n_heads)]

    def scores(n):
        offn = pl.multiple_of(n * blk, blk)
        return [_dot(k_ref[0, pl.ds(offn, blk), _pair_lanes(hh)], qs[hh]) for hh in range(n_heads)]

    own_scores = scores(own)
    prev_scores = scores(jnp.maximum(own - 1, 0))

    blk_f = blk_id.astype(F32)
    for hh in range(n_heads):
        gate = jnp.where(valid, gates[hh], NEG)
        picked = jnp.zeros((nb, blk), jnp.bool_)
        for _ in range(MOBA_TOPK):
            top = _fold_keys(gate, jnp.maximum, jnp.max)
            first = _fold_keys(jnp.where(gate == top, blk_f, float(nb)), jnp.minimum, jnp.min)
            hit = blk_f == first
            picked = picked | hit
            gate = jnp.where(hit, -jnp.inf, gate)
        sel_sc[hh] = jnp.where(valid & picked, 0.0, NEG)

    prev = jnp.maximum(own - 1, 0)
    no_prev = jnp.where(own > 0, 0.0, NEG)
    stats = []
    for hh, (s_own, s_prev) in enumerate(zip(own_scores, prev_scores)):
        s_own = jnp.where(causal, s_own + tbl_ref[hh, blk:, :], NEG)
        s_prev = s_prev + tbl_ref[hh, :blk, :]
        term = sel_sc[hh, pl.ds(prev, 1), :] + no_prev
        m0 = jnp.maximum(_fold_keys(s_own, jnp.maximum, jnp.max),
                         _fold_keys(s_prev, jnp.maximum, jnp.max) + term)
        p_own = jnp.exp(s_own - m0)
        p_prev = jnp.exp(s_prev - (m0 - term))
        l0 = _fold_keys(p_own, jnp.add, jnp.sum) + _fold_keys(p_prev, jnp.add, jnp.sum)
        stats.append((m0, l0, p_own.astype(BF16), p_prev.astype(BF16)))
    state = []
    for hh, (m0, l0, p_own, p_prev) in enumerate(stats):
        state += [m0, l0, _dot(vt_sc[own, _head_dims(hh), :], p_own) + _dot(vt_sc[prev, _head_dims(hh), :], p_prev)]
    state = tuple(state)

    def softmax_step(s, per_query, m_run, l_run):
        m_new = jnp.maximum(m_run, _fold_keys(s, jnp.maximum, jnp.max) + per_query)
        a = jnp.exp(m_run - m_new)
        pn = jnp.exp(s - (m_new - per_query))
        return m_new, a * l_run + _fold_keys(pn, jnp.add, jnp.sum), a, pn.astype(BF16)

    n_far = own - 1
    last_far = jnp.maximum(n_far - 1, 0)
    for hh, s in enumerate(scores(0)):
        s_sc[hh] = s
    p_sc[...] = jnp.zeros(p_sc.shape, BF16)

    for hh in range(n_heads):
        acc_sc[hh] = state[3 * hh + 2]

    def far_block(n, carry):
        ml, a_prev = carry[:2 * n_heads], carry[2 * n_heads:]
        n_prev = jnp.maximum(n - 1, 0)
        pv = [_dot(vt_sc[n_prev, _head_dims(hh), :], p_sc[hh]) for hh in range(n_heads)]
        s_next = scores(jnp.minimum(n + 1, last_far))
        stats = [softmax_step(s_sc[hh], rb_ref[N_BUCKETS - 1, hh] + sel_sc[hh, pl.ds(n, 1), :],
                              ml[2 * hh], ml[2 * hh + 1]) for hh in range(n_heads)]
        new, a_new = [], []
        for hh, (m_new, l_new, a, pn) in enumerate(stats):
            p_sc[hh] = pn
            s_sc[hh] = s_next[hh]
            acc_sc[hh] = a_prev[hh] * acc_sc[hh] + pv[hh]
            new += [m_new, l_new]
            a_new.append(a)
        return tuple(new) + tuple(a_new)

    ones = jnp.ones((1, blk), F32)
    ml0 = tuple(v for hh in range(n_heads) for v in state[3 * hh:3 * hh + 2])
    carry = lax.fori_loop(0, n_far, far_block, ml0 + (ones,) * n_heads)
    ml, a_prev = carry[:2 * n_heads], carry[2 * n_heads:]
    out_t = jnp.concatenate(
        [(a_prev[hh] * acc_sc[hh] + _dot(vt_sc[last_far, _head_dims(hh), :], p_sc[hh])) / ml[2 * hh + 1]
         for hh in range(n_heads)], axis=0)
    o_ref[0] = out_t.T.astype(o_ref.dtype)


def _moba(qkv3, tbl, rel_bias, col0):
    bsz, seq, _ = qkv3.shape
    assert MAX_DISTANCE <= MOBA_BLOCK and seq % MOBA_BLOCK == 0
    width = N_MOBA_HEADS * HEAD_DIM
    blk = MOBA_BLOCK
    nb = seq // blk
    return pl.pallas_call(
        _moba_kernel,
        grid=(bsz, nb),
        in_specs=[pl.BlockSpec(memory_space=pltpu.SMEM),
                  pl.BlockSpec((1, blk, width), lambda b, i: (b, i, col0)),
                  pl.BlockSpec((1, seq, width), lambda b, i: (b, 0, col0 + 1)),
                  pl.BlockSpec((1, seq, width), lambda b, i: (b, 0, col0 + 2)),
                  pl.BlockSpec((N_MOBA_HEADS, 2 * blk, blk), lambda b, i: (0, 0, 0))],
        out_specs=pl.BlockSpec((1, blk, width), lambda b, i: (b, i, 0)),
        out_shape=jax.ShapeDtypeStruct((bsz, seq, width), BF16),
        scratch_shapes=[pltpu.VMEM((nb, width), F32),
                        pltpu.VMEM((nb, width, blk), BF16),
                        pltpu.VMEM((N_MOBA_HEADS, nb, blk), F32),
                        pltpu.VMEM((N_MOBA_HEADS, blk, blk), F32),
                        pltpu.VMEM((N_MOBA_HEADS, blk, blk), BF16),
                        pltpu.VMEM((N_MOBA_HEADS, HEAD_DIM, blk), F32)],
        compiler_params=_params(("arbitrary", "arbitrary"), 48),
        name="moba",
    )(rel_bias, qkv3, qkv3, qkv3, tbl)


def _sb_kernel(q_ref, k_ref, v_ref, o_ref, vt_sc, acc_sc):
    qi = pl.program_id(1)
    t, width = q_ref.shape[1], q_ref.shape[2]
    n_heads = width // HEAD_DIM
    nb = k_ref.shape[1] // t
    scale = HEAD_DIM ** -0.5

    @pl.when(qi == 0)
    def _():
        for n in range(nb):
            vt_sc[n] = _transposed_bf16(v_ref[0, n * t:(n + 1) * t, :])

    qt = q_ref[0].astype(F32).T
    row = lax.broadcasted_iota(I32, (LANES, t), 0)
    key = lax.broadcasted_iota(I32, (t, t), 0)
    qry = lax.broadcasted_iota(I32, (t, t), 1)
    strict = key < qry
    tri = jnp.where(qry >= key, 1.0, 0.0).astype(BF16)
    qs = [_head_queries(qt, row, hh, scale) for hh in range(n_heads)]

    def blocks(tiles):
        zs = [[_dot(k_ref[0, pl.ds(pl.multiple_of(j * t, t), t), _pair_lanes(hh)], qs[hh])
               for hh in range(n_heads)] for j, _ in tiles]
        csums = []
        for (_, diagonal), zt in zip(tiles, zs):
            row_sums = []
            for z in zt:
                sp = jnp.maximum(z, 0.0) + jnp.log(1.0 + jnp.exp(-jnp.abs(z)))
                if diagonal:
                    sp = jnp.where(strict, sp, 0.0)
                hi = sp.astype(BF16)
                lo = (sp - hi.astype(F32)).astype(BF16)
                row_sums.append(_dot(tri, hi) + _dot(tri, lo))
            csums.append(row_sums)
        out = []
        for (j, diagonal), zt, ct in zip(tiles, zs, csums):
            pvs = []
            for hh in range(n_heads):
                a = jnp.exp(zt[hh] - ct[hh])
                if diagonal:
                    a = jnp.where(strict, a, 0.0)
                pvs.append(_dot(vt_sc[j, _head_dims(hh), :], a.astype(BF16)))
            out.append((pvs, [c[0:1, :] for c in ct]))
        return out

    has_prev = qi > 0
    (pv_d, tot_d), (pv_p, tot_p) = blocks([(qi, True), (jnp.maximum(qi - 1, 0), False)])
    for hh, (d, p, c) in enumerate(zip(pv_d, pv_p, tot_d)):
        acc_sc[_head_dims(hh), :] = d + jnp.where(has_prev, p * jnp.exp(-c), 0.0)
    carries = tuple(c + jnp.where(has_prev, p, 0.0) for c, p in zip(tot_d, tot_p))

    def lowest(carries):
        return jnp.min(functools.reduce(jnp.minimum, carries))

    def cond(state):
        j, cmin, _ = state
        return (j >= 0) & (cmin < SB_ZERO_LOG)

    def body(state):
        j, _, carries = state
        (pvs, totals), = blocks([(j, False)])
        for hh, (pv, c) in enumerate(zip(pvs, carries)):
            acc_sc[_head_dims(hh), :] = acc_sc[_head_dims(hh), :] + pv * jnp.exp(-c)
        carries = tuple(c + total for c, total in zip(carries, totals))
        return j - 1, lowest(carries), carries

    lax.while_loop(cond, body, (qi - 2, lowest(carries), carries))
    o_ref[0] = acc_sc[...].T.astype(o_ref.dtype)


def _stickbreak(qkv3, col0):
    bsz, seq, _ = qkv3.shape
    width = N_SB_HEADS * HEAD_DIM
    t = ATTN_ROWS
    return pl.pallas_call(
        _sb_kernel,
        grid=(bsz, seq // t),
        in_specs=[pl.BlockSpec((1, t, width), lambda b, i: (b, i, col0)),
                  pl.BlockSpec((1, seq, width), lambda b, i: (b, 0, col0 + 1)),
                  pl.BlockSpec((1, seq, width), lambda b, i: (b, 0, col0 + 2))],
        out_specs=pl.BlockSpec((1, t, width), lambda b, i: (b, i, 0)),
        out_shape=jax.ShapeDtypeStruct((bsz, seq, width), BF16),
        scratch_shapes=[pltpu.VMEM((seq // t, width, t), BF16), pltpu.VMEM((width, t), F32)],
        compiler_params=_params(("arbitrary", "arbitrary"), 48),
        name="stickbreak",
    )(qkv3, qkv3, qkv3)


def _memkv_kernel(m_ref, w_ref, o_ref):
    o_ref[...] = _dot(m_ref[...].astype(BF16), w_ref[...]).astype(BF16)


def _memkv(mem2, w_bf16):
    n, d = mem2.shape
    width = w_bf16.shape[1]
    tm = min(n, 512)
    return pl.pallas_call(
        _memkv_kernel,
        grid=(n // tm,),
        in_specs=[pl.BlockSpec((tm, d), lambda i: (i, 0)),
                  pl.BlockSpec((d, width), lambda i: (0, 0))],
        out_specs=pl.BlockSpec((tm, width), lambda i: (i, 0)),
        out_shape=jax.ShapeDtypeStruct((n, width), BF16),
        compiler_params=_params(("arbitrary",), 32),
        name="memkv",
    )(mem2, w_bf16)


def _mem_kernel(q_ref, k_ref, v_ref, o_ref, vt_sc):
    t, width = q_ref.shape[1], q_ref.shape[2]
    n_heads = width // HEAD_DIM
    scale = HEAD_DIM ** -0.5

    @pl.when(pl.program_id(1) == 0)
    def _():
        vt_sc[...] = _transposed_bf16(v_ref[0])

    qt = q_ref[0].astype(F32).T
    row = lax.broadcasted_iota(I32, (LANES, t), 0)
    qs = [_head_queries(qt, row, hh, scale) for hh in range(n_heads)]
    scores = [_dot(k_ref[0, :, _pair_lanes(hh)], qs[hh]) for hh in range(n_heads)]
    probs = []
    for s in scores:
        e = jnp.exp(s - _fold_keys(s, jnp.maximum, jnp.max))
        probs.append((e * (1.0 / _fold_keys(e, jnp.add, jnp.sum))).astype(BF16))
    out_t = jnp.concatenate([_dot(vt_sc[_head_dims(hh), :], probs[hh]) for hh in range(n_heads)], axis=0)
    o_ref[0] = out_t.T.astype(o_ref.dtype)


def _mem_attention(qkv3, kv3, qcol):
    bsz, seq, _ = qkv3.shape
    n_mem = kv3.shape[1]
    width = N_MEM_HEADS * HEAD_DIM
    t = MEM_ROWS
    return pl.pallas_call(
        _mem_kernel,
        grid=(bsz, seq // t),
        in_specs=[pl.BlockSpec((1, t, width), lambda b, i: (b, i, qcol)),
                  pl.BlockSpec((1, n_mem, width), lambda b, i: (b, 0, 0)),
                  pl.BlockSpec((1, n_mem, width), lambda b, i: (b, 0, 1))],
        out_specs=pl.BlockSpec((1, t, width), lambda b, i: (b, i, 0)),
        out_shape=jax.ShapeDtypeStruct((bsz, seq, width), BF16),
        scratch_shapes=[pltpu.VMEM((width, n_mem), BF16)],
        compiler_params=_params(("arbitrary", "arbitrary"), 32),
        name="mem_attention",
    )(qkv3, kv3, kv3)


def _merge_kernel(ya_ref, yb_ref, ym_ref, x_ref, wg_ref, wa_ref, wb_ref, wm_ref, wo_ref,
                  lng_ref, lnb_ref, wrh_ref, br_ref, x1_ref, route_ref, tcnt_ref, h_sc, cnt_ref,
                  *, alpha, n_experts):
    step = pl.program_id(0)
    tm, d = x_ref.shape
    slot = lax.rem(step, 2)

    @pl.when(step == 0)
    def _():
        cnt_ref[...] = jnp.zeros(cnt_ref.shape, F32)
        h_sc[...] = jnp.zeros(h_sc.shape, F32)

    x = x_ref[...]
    xb = x.astype(BF16)
    branches = ((ya_ref, wa_ref), (yb_ref, wb_ref), (ym_ref, wm_ref))

    def branch_term(k):
        gate_logits = _dot(xb, wg_ref[:, k * d:(k + 1) * d])
        return gate_logits, _dot(branches[k][0][...], branches[k][1][...])

    x1 = _layer_norm(h_sc[1 - slot], lng_ref[...], lnb_ref[...])
    x1_ref[...] = x1
    xh = x1.astype(BF16)
    xl = (x1 - xh.astype(F32)).astype(BF16)

    first_term = branch_term(0)

    by_hi = _dot(xh, wrh_ref[...])
    logits = (by_hi[:, :LANES] + (by_hi[:, LANES:] + _dot(xl, wrh_ref[:, :LANES]))
              + br_ref[...])

    merged = None
    for k in range(len(branches)):
        gate_logits, y = first_term if k == 0 else branch_term(k)
        term = _sigmoid(gate_logits) * y
        merged = term if merged is None else merged + term
    h_sc[slot] = alpha * x + _dot(merged.astype(BF16), wo_ref[...])

    col = lax.broadcasted_iota(I32, (tm, LANES), 1)
    colf = col.astype(F32)
    big = float(LANES)
    gmask = (col >= n_experts) & (col < n_experts + N_GROUPS)
    lg = jnp.where(gmask, logits, -jnp.inf)
    gmax = jnp.max(lg, axis=1, keepdims=True)
    gidx = jnp.min(jnp.where(lg == gmax, colf, big), axis=1, keepdims=True) - n_experts
    g_p = 1.0 / jnp.sum(jnp.where(gmask, jnp.exp(logits - gmax), 0.0), axis=1, keepdims=True)
    lo_col = gidx * EXPERTS_PER_GROUP
    emask = (colf >= lo_col) & (colf < lo_col + EXPERTS_PER_GROUP)
    le = jnp.where(emask, logits, -jnp.inf)
    l1 = jnp.max(le, axis=1, keepdims=True)
    i1 = jnp.min(jnp.where(le == l1, colf, big), axis=1, keepdims=True)
    le2 = jnp.where(colf == i1, -jnp.inf, le)
    l2 = jnp.max(le2, axis=1, keepdims=True)
    i2 = jnp.min(jnp.where(le2 == l2, colf, big), axis=1, keepdims=True)
    e2 = jnp.exp(l2 - l1)
    gate1 = g_p * (1.0 / (1.0 + e2))
    gate2 = g_p * (e2 / (1.0 + e2))

    oh1 = colf == i1
    oh2 = colf == i2
    cnt = jnp.where((oh1 | oh2) & (step > 0), 1.0, 0.0)
    rr = lax.broadcasted_iota(I32, (tm, tm), 0)
    cc = lax.broadcasted_iota(I32, (tm, tm), 1)
    before = jnp.where(cc < rr, 1.0, 0.0).astype(BF16)
    base = _dot(before, cnt.astype(BF16)) + cnt_ref[0:1, :]
    rank1 = jnp.sum(jnp.where(oh1, base, 0.0), axis=1, keepdims=True)
    rank2 = jnp.sum(jnp.where(oh2, base, 0.0), axis=1, keepdims=True)
    tile_cnt = jnp.sum(cnt, axis=0, keepdims=True)
    cnt_ref[...] = cnt_ref[...] + tile_cnt
    tcnt_ref[0] = jnp.broadcast_to(tile_cnt, tcnt_ref.shape[1:])

    route = jnp.zeros((tm, LANES), F32)
    for k, val in enumerate((i1, i2, rank1, rank2, gate1, gate2)):
        route = jnp.where(col == k, val, route)
    route_ref[...] = route


def _merge(ya, yb, ym, x2, w_gates, wa, wb, wm, wo, ln_g, ln_b, wr_split, b_r, alpha, n_experts):
    n, d = x2.shape
    tm = MERGE_ROWS
    n_tiles = n // tm
    row_in = lambda w: pl.BlockSpec((tm, w), lambda i: (jnp.minimum(i, n_tiles - 1), 0))
    row_out = lambda w: pl.BlockSpec((tm, w), lambda i: (jnp.maximum(i - 1, 0), 0))
    full = lambda a: pl.BlockSpec(a.shape, lambda i: (0,) * a.ndim)
    return pl.pallas_call(
        functools.partial(_merge_kernel, alpha=alpha, n_experts=n_experts),
        grid=(n_tiles + 1,),
        in_specs=[row_in(ya.shape[1]), row_in(yb.shape[1]), row_in(ym.shape[1]), row_in(d),
                  full(w_gates), full(wa), full(wb), full(wm), full(wo), full(ln_g), full(ln_b),
                  full(wr_split), full(b_r)],
        out_specs=[row_out(d), row_out(LANES),
                   pl.BlockSpec((1, 8, LANES), lambda i: (jnp.maximum(i - 1, 0), 0, 0))],
        out_shape=[jax.ShapeDtypeStruct((n, d), F32),
                   jax.ShapeDtypeStruct((n, LANES), F32),
                   jax.ShapeDtypeStruct((n_tiles, 8, LANES), F32)],
        scratch_shapes=[pltpu.VMEM((2, tm, d), F32), pltpu.VMEM((8, LANES), F32)],
        compiler_params=_params(("arbitrary",), 56),
        name="merge",
    )(ya, yb, ym, x2, w_gates, wa, wb, wm, wo, ln_g, ln_b, wr_split, b_r)


def _row_copy(src_hbm, row, buf, sem, r):
    return pltpu.make_async_copy(src_hbm.at[pl.ds(row, 1), :], buf.at[pl.ds(r, 1), :], sem)


def _gather_rows(src_hbm, idx_ref, buf, sem, n_rows, unrolled):
    if unrolled:
        for r in range(n_rows):
            _row_copy(src_hbm, idx_ref[0, 0, r], buf, sem, r).start()
    else:
        def issue(r, _):
            _row_copy(src_hbm, idx_ref[0, 0, r], buf, sem, r).start()
            return 0
        lax.fori_loop(0, n_rows, issue, 0)


def _wait_rows(src_hbm, buf, sem, n_rows):
    pltpu.make_async_copy(src_hbm.at[pl.ds(0, n_rows), :], buf, sem).wait()


DISPATCH_ROWS = 512
CHUNK = 8


def _chunk_copy(buf, xpad_hbm, sem, src_row, dst_row):
    return pltpu.make_async_copy(buf.at[pl.ds(src_row, CHUNK), :], xpad_hbm.at[pl.ds(dst_row, CHUNK), :], sem)


def _dispatch_kernel(nch_ref, off_ref, dst_ref, zrow_ref, zflag_ref, nu_ref, base_ref, route_ref, x_ref,
                     xpad_hbm, dest_ref, sorted_even, sorted_odd, zero_buf, sem, zsem):
    i = pl.program_id(0)
    n_steps = pl.num_programs(0)
    n_exp = base_ref.shape[1]
    td = x_ref.shape[0]
    rb = sorted_even.shape[0]
    zrows = zero_buf.shape[0]
    bufs = (sorted_even, sorted_odd)
    parity = lax.rem(i, 2)

    @pl.when(i == 0)
    def _():
        zero_buf[...] = jnp.zeros(zero_buf.shape, F32)

        def zero_block(row, start):
            cp = pltpu.make_async_copy(zero_buf, xpad_hbm.at[pl.ds(pl.multiple_of(row, zrows), zrows), :], zsem)
            cp.start() if start else cp.wait()

        for start in (True, False):
            for e in range(n_exp):
                pl.when(zflag_ref[e] == 1)(functools.partial(zero_block, zrow_ref[e], start))

            def unused(blk, _):
                zero_block(blk * zrows, start)
                return 0
            lax.fori_loop(nu_ref[0], xpad_hbm.shape[0] // zrows, unused, 0)

    rt = route_ref[...].T
    eid = lax.broadcasted_iota(I32, (n_exp, td), 0).astype(F32)
    rowid = lax.broadcasted_iota(I32, (rb, td), 0).astype(F32)
    hit = None
    dests = []
    for k in range(2):
        mine = eid == rt[k:k + 1, :]
        rank = rt[2 + k:3 + k, :]
        pos = jnp.sum(jnp.where(mine, base_ref[0, :, 0:1], 0.0), axis=0, keepdims=True) + rank
        dests.append(jnp.sum(jnp.where(mine, base_ref[0, :, 1:2], 0.0), axis=0, keepdims=True) + rank)
        hit = (rowid == pos) if hit is None else hit | (rowid == pos)
    dest_ref[0] = jnp.concatenate(dests + [jnp.zeros((8 - len(dests), td), F32)], axis=0).astype(I32)
    perm = jnp.where(hit, 1.0, 0.0).astype(BF16)
    rows_sorted = _dot(perm, x_ref[...].astype(BF16))

    def runs(tile, buf, s, start):
        for e in range(n_exp):
            k = tile * n_exp + e
            src0, dst0 = off_ref[k], dst_ref[k]

            def one(j, _):
                cp = _chunk_copy(buf, xpad_hbm, s, pl.multiple_of(src0 + j * CHUNK, CHUNK),
                                 pl.multiple_of(dst0 + j * CHUNK, CHUNK))
                cp.start() if start else cp.wait()
                return 0
            lax.fori_loop(0, nch_ref[k], one, 0)

    def wait_runs(tile, buf, s):
        k_last = tile * n_exp + n_exp - 1
        total = pl.multiple_of(off_ref[k_last] + nch_ref[k_last] * CHUNK, CHUNK)
        pltpu.make_async_copy(buf.at[pl.ds(0, total), :], xpad_hbm.at[pl.ds(0, total), :], s).wait()

    def step(cur):
        bufs[cur][...] = rows_sorted

        @pl.when(i > 0)
        def _():
            wait_runs(i - 1, bufs[1 - cur], sem.at[1 - cur])

        runs(i, bufs[cur], sem.at[cur], True)

        @pl.when(i + 1 == n_steps)
        def _():
            wait_runs(i, bufs[cur], sem.at[cur])

    for cur in range(2):
        pl.when(parity == cur)(functools.partial(step, cur))


def _dispatch(x1, route, nch, off8, dst, zrow, zflag, n_used, base, n_rows_out):
    n, d = x1.shape
    td = DISPATCH_ROWS
    n_tiles = n // td
    n_exp = base.shape[1]
    rb = 2 * td + n_exp * CHUNK
    grid_spec = pltpu.PrefetchScalarGridSpec(
        num_scalar_prefetch=6,
        grid=(n_tiles,),
        in_specs=[pl.BlockSpec((1, n_exp, 2), lambda i, *_: (i, 0, 0)),
                  pl.BlockSpec((td, LANES), lambda i, *_: (i, 0)),
                  pl.BlockSpec((td, d), lambda i, *_: (i, 0))],
        out_specs=[pl.BlockSpec(memory_space=pl.ANY),
                   pl.BlockSpec((1, 8, td), lambda i, *_: (i, 0, 0))],
        scratch_shapes=[pltpu.VMEM((rb, d), F32), pltpu.VMEM((rb, d), F32),
                        pltpu.VMEM((MOE_BLOCK, d), F32),
                        pltpu.SemaphoreType.DMA((2,)), pltpu.SemaphoreType.DMA(())])
    return pl.pallas_call(
        _dispatch_kernel,
        grid_spec=grid_spec,
        out_shape=[jax.ShapeDtypeStruct((n_rows_out, d), F32),
                   jax.ShapeDtypeStruct((n_tiles, 8, td), I32)],
        compiler_params=_params(("arbitrary",), 48),
        name="dispatch",
    )(nch, off8, dst, zrow, zflag, n_used, base, route, x1)


def _experts_kernel(be_ref, nu_ref, x_hbm, wg_ref, wu_ref, wd_ref, o_ref, xbuf, sem, wg_sc, wu_sc, wd_sc):
    b = pl.program_id(0)
    n_used = nu_ref[0]
    rows = o_ref.shape[0]
    n_slots = xbuf.shape[0]

    def fetch(blk):
        slot = lax.rem(blk, n_slots)
        return pltpu.make_async_copy(x_hbm.at[pl.ds(pl.multiple_of(blk * rows, rows), rows), :],
                                     xbuf.at[slot], sem.at[slot])

    @pl.when(b == 0)
    def _():
        for blk in range(n_slots - 1):
            pl.when(blk < n_used)(lambda blk=blk: fetch(blk).start())

    @pl.when(b < n_used)
    def _():
        @pl.when(b + n_slots - 1 < n_used)
        def _():
            fetch(b + n_slots - 1).start()

        changed = (b == 0) | (be_ref[b] != be_ref[jnp.maximum(b - 1, 0)])

        @pl.when(changed)
        def _():
            wg_sc[...] = wg_ref[0].astype(BF16)
            wu_sc[...] = wu_ref[0].astype(BF16)
            wd_sc[...] = wd_ref[0].astype(BF16)

        fetch(b).wait()
        xb = xbuf[lax.rem(b, n_slots)].astype(BF16)
        g = _dot(xb, wg_sc[...])
        u = _dot(xb, wu_sc[...])
        h = (g * _sigmoid(g) * u).astype(BF16)
        o_ref[...] = _dot(h, wd_sc[...])

    @pl.when(b >= n_used)
    def _():
        o_ref[...] = jnp.zeros(o_ref.shape, F32)


def _experts(x_pad, block_expert, n_used, w_gate, w_up, w_down):
    d = x_pad.shape[1]
    n_blocks = block_expert.shape[0]
    rows = MOE_BLOCK
    d_exp = w_gate.shape[-1]
    grid_spec = pltpu.PrefetchScalarGridSpec(
        num_scalar_prefetch=2,
        grid=(n_blocks,),
        in_specs=[pl.BlockSpec(memory_space=pl.ANY),
                  pl.BlockSpec((1, d, d_exp), lambda b, be, nu: (be[b], 0, 0)),
                  pl.BlockSpec((1, d, d_exp), lambda b, be, nu: (be[b], 0, 0)),
                  pl.BlockSpec((1, d_exp, d), lambda b, be, nu: (be[b], 0, 0))],
        out_specs=pl.BlockSpec((rows, d), lambda b, be, nu: (b, 0)),
        scratch_shapes=[pltpu.VMEM((EXPERT_INPUT_SLOTS, rows, d), F32),
                        pltpu.SemaphoreType.DMA((EXPERT_INPUT_SLOTS,)),
                        pltpu.VMEM((d, d_exp), BF16),
                        pltpu.VMEM((d, d_exp), BF16),
                        pltpu.VMEM((d_exp, d), BF16)])
    return pl.pallas_call(
        _experts_kernel,
        grid_spec=grid_spec,
        out_shape=jax.ShapeDtypeStruct((n_blocks * rows, d), F32),
        compiler_params=_params(("arbitrary",), 44),
        name="experts",
    )(block_expert, n_used, x_pad, w_gate, w_up, w_down)


def _combine_kernel(idx_ref, idxn_ref, y_hbm, x1_ref, route_ref, lng_ref, lnb_ref, o_ref,
                    ybuf_even, ybuf_odd, sem, *, alpha):
    i = pl.program_id(0)
    n_steps = pl.num_programs(0)
    tm = x1_ref.shape[0]
    bufs = (ybuf_even, ybuf_odd)
    parity = lax.rem(i, 2)

    @pl.when(i == 0)
    def _():
        _gather_rows(y_hbm, idx_ref, bufs[0], sem.at[0], 2 * tm, False)

    def compute(cur):
        _gather_rows(y_hbm, idxn_ref, bufs[1 - cur], sem.at[1 - cur], 2 * tm, True)
        _wait_rows(y_hbm, bufs[cur], sem.at[cur], 2 * tm)
        ffn = (route_ref[:, 4:5] * bufs[cur][0:tm, :] + route_ref[:, 5:6] * bufs[cur][tm:2 * tm, :])
        o_ref[...] = _layer_norm(alpha * x1_ref[...] + ffn, lng_ref[...], lnb_ref[...])

        @pl.when(i + 1 == n_steps)
        def _():
            _wait_rows(y_hbm, bufs[1 - cur], sem.at[1 - cur], 2 * tm)

    for cur in range(2):
        pl.when(parity == cur)(functools.partial(compute, cur))


def _combine(y_pad, dest, x1, route, ln_g, ln_b, alpha):
    n, d = x1.shape
    tm = COMBINE_ROWS
    n_steps = n // tm
    idx3 = dest.reshape(n_steps, tm, 2).transpose(0, 2, 1).reshape(n_steps, 1, 2 * tm)
    smem_idx = lambda f: pl.BlockSpec((1, 1, 2 * tm), f, memory_space=pltpu.SMEM)
    return pl.pallas_call(
        functools.partial(_combine_kernel, alpha=alpha),
        grid=(n_steps,),
        in_specs=[smem_idx(lambda i: (i, 0, 0)),
                  smem_idx(lambda i: (jnp.minimum(i + 1, n_steps - 1), 0, 0)),
                  pl.BlockSpec(memory_space=pl.ANY),
                  pl.BlockSpec((tm, d), lambda i: (i, 0)),
                  pl.BlockSpec((tm, LANES), lambda i: (i, 0)),
                  pl.BlockSpec((1, d), lambda i: (0, 0)),
                  pl.BlockSpec((1, d), lambda i: (0, 0))],
        out_specs=pl.BlockSpec((tm, d), lambda i: (i, 0)),
        out_shape=jax.ShapeDtypeStruct((n, d), F32),
        scratch_shapes=[pltpu.VMEM((2 * tm, d), F32), pltpu.VMEM((2 * tm, d), F32),
                        pltpu.SemaphoreType.DMA((2,))],
        compiler_params=_params(("arbitrary",), 32),
        name="combine",
    )(idx3, idx3, y_pad, x1, route, ln_g, ln_b)


def _split_bf16(w):
    hi = w.astype(BF16)
    return hi, (w - hi.astype(F32)).astype(BF16)


def _layer(x, mem, tbl, rel_bias, w_in, w_mem_kv, w_br_moba, w_br_sb, w_br_mem, w_out, ln1_g, ln1_b,
           w_rg, b_rg, w_re, b_re, w_gate, w_up, w_down, ln2_g, ln2_b, alpha):
    bsz, seq, d = x.shape
    n = bsz * seq
    n_experts = w_re.shape[1]
    moba_w, sb_w, mem_w = N_MOBA_HEADS * HEAD_DIM, N_SB_HEADS * HEAD_DIM, N_MEM_HEADS * HEAD_DIM
    n_qkv = 3 * moba_w + 3 * sb_w + mem_w
    assert w_in.shape[1] == n_qkv + 3 * d and n_experts + N_GROUPS <= LANES and moba_w == sb_w

    x2 = x.reshape(n, d)
    qkv = _proj(x2, w_in[:, :n_qkv].astype(BF16))
    qkv3 = qkv.reshape(bsz, seq, n_qkv)
    y_a = _moba(qkv3, tbl, rel_bias, 0)
    y_b = _stickbreak(qkv3, 3 * moba_w // sb_w)
    kv = _memkv(mem.reshape(-1, d), w_mem_kv.astype(BF16)).reshape(bsz, mem.shape[1], 2 * mem_w)
    y_m = _mem_attention(qkv3, kv, (3 * moba_w + 3 * sb_w) // mem_w)

    w_r = jnp.zeros((d, LANES), F32).at[:, :n_experts].set(w_re).at[:, n_experts:n_experts + N_GROUPS].set(w_rg)
    b_r = jnp.zeros((1, LANES), F32).at[0, :n_experts].set(b_re).at[0, n_experts:n_experts + N_GROUPS].set(b_rg)
    wr_split = jnp.concatenate(_split_bf16(w_r), axis=1)
    x1, route, tile_counts = _merge(
        y_a.reshape(n, moba_w), y_b.reshape(n, sb_w), y_m.reshape(n, mem_w), x2, w_in[:, n_qkv:].astype(BF16),
        w_br_moba.astype(BF16), w_br_sb.astype(BF16), w_br_mem.astype(BF16), w_out.astype(BF16),
        ln1_g.reshape(1, d), ln1_b.reshape(1, d), wr_split, b_r, alpha, n_experts)

    rows, td = MOE_BLOCK, DISPATCH_ROWS
    assert td == MERGE_ROWS
    n_tiles = n // td
    tile_cnt = tile_counts[:, 0, :n_experts].astype(I32)
    n_chunks = (tile_cnt + CHUNK - 1) // CHUNK
    run_rows = n_chunks * CHUNK
    run_rank = jnp.cumsum(tile_cnt, axis=0) - tile_cnt
    run_off = jnp.cumsum(run_rows, axis=0) - run_rows
    sorted_off = jnp.cumsum(run_rows, axis=1) - run_rows
    expert_rows = jnp.sum(run_rows, axis=0)
    padded = (expert_rows + rows - 1) // rows * rows
    pend = jnp.cumsum(padded)
    pstart = pend - padded
    run_dst = pstart[None, :] + run_off
    n_blocks = (2 * n + n_tiles * n_experts * (CHUNK - 1)) // rows + n_experts
    block_start = jnp.arange(n_blocks, dtype=I32) * rows
    block_expert = jnp.minimum(jnp.sum((pend[None, :] <= block_start[:, None]).astype(I32), axis=1),
                               n_experts - 1)
    n_used = (pend[-1] // rows).astype(I32).reshape(1)
    base = jnp.stack([sorted_off - run_rank, run_dst - run_rank], axis=-1).astype(F32)
    x_pad, dest8 = _dispatch(x1, route, n_chunks.reshape(-1), sorted_off.reshape(-1), run_dst.reshape(-1),
                             jnp.maximum(pend - rows, 0), (padded > 0).astype(I32), n_used, base,
                             n_blocks * rows)
    dest = dest8[:, 0:2, :].transpose(0, 2, 1).reshape(n, 2)

    y_pad = _experts(x_pad, block_expert, n_used, w_gate, w_up, w_down)
    out = _combine(y_pad, dest, x1, route, ln2_g.reshape(1, d), ln2_b.reshape(1, d), alpha)
    return out.reshape(bsz, seq, d)


def kernel(x, mem, w_in, w_mem_kv, rel_bias, w_br_moba, w_br_sb, w_br_mem, w_out, ln1_g, ln1_b,
           w_router_group, b_router_group, w_router_expert, b_router_expert,
           w_gate, w_up, w_down, ln2_g, ln2_b):
    depth = w_in.shape[0]
    alpha = (2.0 * depth) ** 0.25
    tbl = _bias_table(rel_bias)
    for l in range(depth):
        x = _layer(x, mem, tbl, rel_bias, w_in[l], w_mem_kv[l], w_br_moba[l], w_br_sb[l], w_br_mem[l],
                   w_out[l], ln1_g[l], ln1_b[l], w_router_group[l], b_router_group[l],
                   w_router_expert[l], b_router_expert[l], w_gate[l], w_up[l], w_down[l],
                   ln2_g[l], ln2_b[l], alpha)
    return x
```

```python
import functools
import math

import jax
import jax.numpy as jnp
from jax import lax
from jax.experimental import pallas as pl
from jax.experimental.pallas import tpu as pltpu

F32, BF16, I32, U32 = jnp.float32, jnp.bfloat16, jnp.int32, jnp.uint32

HEAD_DIM = 64
N_MOBA_HEADS = 6
N_SB_HEADS = 6
N_MEM_HEADS = 4
MOBA_BLOCK = 256
MOBA_TOPK = 3
N_BUCKETS = 32
MAX_DISTANCE = 128
N_GROUPS = 4
EXPERTS_PER_GROUP = 8
LN_EPS = 1e-5
NEG = -1e30

LANES = 128
VMEM_BYTES = 64 * 1024 * 1024
PAIR = LANES // HEAD_DIM

SB_ZERO_LOG = 110.0

PROJ_ROWS = 1024
MERGE_ROWS = 512
ATTN_ROWS = 256
MEM_ROWS = 1024
COMBINE_ROWS = 256
MOE_BLOCK = 512
EXPERT_INPUT_SLOTS = 3


def _params(semantics, vmem_mb):
    return pltpu.CompilerParams(dimension_semantics=semantics,
                                vmem_limit_bytes=min(vmem_mb * 1024 * 1024, VMEM_BYTES))


def _dot(a, b):
    return jnp.dot(a, b, preferred_element_type=F32)


def _layer_norm(h, g, b):
    mu = jnp.mean(h, axis=-1, keepdims=True)
    d = h - mu
    var = jnp.mean(d * d, axis=-1, keepdims=True)
    return d * lax.rsqrt(var + LN_EPS) * g + b


def _sigmoid(x):
    return 1.0 / (1.0 + jnp.exp(-x))


def _head_rows(qt, row, hh, scale):
    keep = (row >= hh * HEAD_DIM) & (row < (hh + 1) * HEAD_DIM)
    return jnp.where(keep, qt * scale, 0.0).astype(BF16)


def _transposed_bf16(x):
    return x.astype(F32).T.astype(BF16)


def _proj_kernel(x_ref, wq_ref, qkv_ref):
    qkv_ref[...] = _dot(x_ref[...].astype(BF16), wq_ref[...]).astype(BF16)


def _proj(x2, w_qkv):
    n, d = x2.shape
    n_qkv = w_qkv.shape[1]
    tm = PROJ_ROWS
    return pl.pallas_call(
        _proj_kernel,
        grid=(n // tm,),
        in_specs=[pl.BlockSpec((tm, d), lambda i: (i, 0)),
                  pl.BlockSpec((d, n_qkv), lambda i: (0, 0))],
        out_specs=pl.BlockSpec((tm, n_qkv), lambda i: (i, 0)),
        out_shape=jax.ShapeDtypeStruct((n, n_qkv), BF16),
        compiler_params=_params(("arbitrary",), 40),
        name="proj",
    )(x2, w_qkv)


def _t5_bucket(rel):
    rel = jnp.maximum(rel, 0)
    max_exact = N_BUCKETS // 2
    rel_f = jnp.maximum(rel, 1).astype(F32)
    large = max_exact + (jnp.log(rel_f / max_exact) / math.log(MAX_DISTANCE / max_exact)
                         * (N_BUCKETS - max_exact)).astype(I32)
    large = jnp.minimum(large, N_BUCKETS - 1)
    return jnp.where(rel < max_exact, rel, large)


def _bias_table_kernel(rb_ref, o_ref):
    h = pl.program_id(0)
    blk = o_ref.shape[2]
    j = lax.broadcasted_iota(I32, (2 * blk, blk), 0)
    i = lax.broadcasted_iota(I32, (2 * blk, blk), 1)
    bucket = _t5_bucket(blk + i - j)
    acc = jnp.zeros((2 * blk, blk), F32)
    for b in range(N_BUCKETS):
        acc = jnp.where(bucket == b, rb_ref[b, h], acc)
    o_ref[0] = acc


def _bias_table(rel_bias):
    n_heads = rel_bias.shape[1]
    blk = MOBA_BLOCK
    return pl.pallas_call(
        _bias_table_kernel,
        grid=(n_heads,),
        in_specs=[pl.BlockSpec(memory_space=pltpu.SMEM)],
        out_specs=pl.BlockSpec((1, 2 * blk, blk), lambda h: (h, 0, 0)),
        out_shape=jax.ShapeDtypeStruct((n_heads, 2 * blk, blk), F32),
        compiler_params=_params(("arbitrary",), 32),
        name="bias_table",
    )(rel_bias)


def _fold_keys(x, op, final):
    while x.shape[0] > 8 and x.shape[0] % 2 == 0:
        half = x.shape[0] // 2
        x = op(x[:half], x[half:])
    return final(x, axis=0, keepdims=True)


def _head_queries(qt, row, hh, scale):
    p, sub = divmod(hh, PAIR)
    return _head_rows(qt[p * LANES:(p + 1) * LANES, :], row, sub, scale)


def _pair_lanes(hh):
    p = hh // PAIR
    return slice(p * LANES, (p + 1) * LANES)


def _head_dims(hh):
    return slice(hh * HEAD_DIM, (hh + 1) * HEAD_DIM)


def _moba_kernel(rb_ref, q_ref, k_ref, v_ref, tbl_ref, o_ref, kmean_sc, vt_sc, sel_sc, s_sc, p_sc, acc_sc):
    own = pl.program_id(1)
    blk, width = q_ref.shape[1], q_ref.shape[2]
    n_heads = width // HEAD_DIM
    nb = k_ref.shape[1] // blk
    scale = HEAD_DIM ** -0.5

    @pl.when(own == 0)
    def _():
        for n in range(nb):
            kb = k_ref[0, n * blk:(n + 1) * blk, :].astype(F32)
            kmean_sc[n:n + 1, :] = jnp.mean(kb, axis=0, keepdims=True)
            vt_sc[n] = _transposed_bf16(v_ref[0, n * blk:(n + 1) * blk, :])

    qt = q_ref[0].astype(F32).T
    row = lax.broadcasted_iota(I32, (LANES, blk), 0)
    blk_id = lax.broadcasted_iota(I32, (nb, blk), 0)
    valid = blk_id < own
    causal = (lax.broadcasted_iota(I32, (blk, blk), 0) <= lax.broadcasted_iota(I32, (blk, blk), 1))
    kmean = kmean_sc[...].astype(BF16)

    qs = [_head_queries(qt, row, hh, scale) for hh in range(n_heads)]
    gates = [_dot(kmean[:, _pair_lanes(hh)], _head_queries(qt, row, hh, 1.0)) for hh in range(n_heads)]

    def scores(n):
        offn = pl.multiple_of(n * blk, blk)
        return [_dot(k_ref[0, pl.ds(offn, blk), _pair_lanes(hh)], qs[hh]) for hh in range(n_heads)]

    own_scores = scores(own)
    prev_scores = scores(jnp.maximum(own - 1, 0))

    blk_f = blk_id.astype(F32)
    for hh in range(n_heads):
        gate = jnp.where(valid, gates[hh], NEG)
        picked = jnp.zeros((nb, blk), jnp.bool_)
        for _ in range(MOBA_TOPK):
            top = _fold_keys(gate, jnp.maximum, jnp.max)
            first = _fold_keys(jnp.where(gate == top, blk_f, float(nb)), jnp.minimum, jnp.min)
            hit = blk_f == first
            picked = picked | hit
            gate = jnp.where(hit, -jnp.inf, gate)
        sel_sc[hh] = jnp.where(valid & picked, 0.0, NEG)

    prev = jnp.maximum(own - 1, 0)
    no_prev = jnp.where(own > 0, 0.0, NEG)
    stats = []
    for hh, (s_own, s_prev) in enumerate(zip(own_scores, prev_scores)):
        s_own = jnp.where(causal, s_own + tbl_ref[hh, blk:, :], NEG)
        s_prev = s_prev + tbl_ref[hh, :blk, :]
        term = sel_sc[hh, pl.ds(prev, 1), :] + no_prev
        m0 = jnp.maximum(_fold_keys(s_own, jnp.maximum, jnp.max),
                         _fold_keys(s_prev, jnp.maximum, jnp.max) + term)
        p_own = jnp.exp(s_own - m0)
        p_prev = jnp.exp(s_prev - (m0 - term))
        l0 = _fold_keys(p_own, jnp.add, jnp.sum) + _fold_keys(p_prev, jnp.add, jnp.sum)
        stats.append((m0, l0, p_own.astype(BF16), p_prev.astype(BF16)))
    state = []
    for hh, (m0, l0, p_own, p_prev) in enumerate(stats):
        state += [m0, l0, _dot(vt_sc[own, _head_dims(hh), :], p_own) + _dot(vt_sc[prev, _head_dims(hh), :], p_prev)]
    state = tuple(state)

    def softmax_step(s, per_query, m_run, l_run):
        m_new = jnp.maximum(m_run, _fold_keys(s, jnp.maximum, jnp.max) + per_query)
        a = jnp.exp(m_run - m_new)
        pn = jnp.exp(s - (m_new - per_query))
        return m_new, a * l_run + _fold_keys(pn, jnp.add, jnp.sum), a, pn.astype(BF16)

    n_far = own - 1
    last_far = jnp.maximum(n_far - 1, 0)
    for hh, s in enumerate(scores(0)):
        s_sc[hh] = s
    p_sc[...] = jnp.zeros(p_sc.shape, BF16)

    for hh in range(n_heads):
        acc_sc[hh] = state[3 * hh + 2]

    def far_block(n, carry):
        ml, a_prev = carry[:2 * n_heads], carry[2 * n_heads:]
        n_prev = jnp.maximum(n - 1, 0)
        pv = [_dot(vt_sc[n_prev, _head_dims(hh), :], p_sc[hh]) for hh in range(n_heads)]
        s_next = scores(jnp.minimum(n + 1, last_far))
        stats = [softmax_step(s_sc[hh], rb_ref[N_BUCKETS - 1, hh] + sel_sc[hh, pl.ds(n, 1), :],
                              ml[2 * hh], ml[2 * hh + 1]) for hh in range(n_heads)]
        new, a_new = [], []
        for hh, (m_new, l_new, a, pn) in enumerate(stats):
            p_sc[hh] = pn
            s_sc[hh] = s_next[hh]
            acc_sc[hh] = a_prev[hh] * acc_sc[hh] + pv[hh]
            new += [m_new, l_new]
            a_new.append(a)
        return tuple(new) + tuple(a_new)

    ones = jnp.ones((1, blk), F32)
    ml0 = tuple(v for hh in range(n_heads) for v in state[3 * hh:3 * hh + 2])
    carry = lax.fori_loop(0, n_far, far_block, ml0 + (ones,) * n_heads)
    ml, a_prev = carry[:2 * n_heads], carry[2 * n_heads:]
    out_t = jnp.concatenate(
        [(a_prev[hh] * acc_sc[hh] + _dot(vt_sc[last_far, _head_dims(hh), :], p_sc[hh])) / ml[2 * hh + 1]
         for hh in range(n_heads)], axis=0)
    o_ref[0] = out_t.T.astype(o_ref.dtype)


def _moba(qkv3, tbl, rel_bias, col0):
    bsz, seq, _ = qkv3.shape
    assert MAX_DISTANCE <= MOBA_BLOCK and seq % MOBA_BLOCK == 0
    width = N_MOBA_HEADS * HEAD_DIM
    blk = MOBA_BLOCK
    nb = seq // blk
    return pl.pallas_call(
        _moba_kernel,
        grid=(bsz, nb),
        in_specs=[pl.BlockSpec(memory_space=pltpu.SMEM),
                  pl.BlockSpec((1, blk, width), lambda b, i: (b, i, col0)),
                  pl.BlockSpec((1, seq, width), lambda b, i: (b, 0, col0 + 1)),
                  pl.BlockSpec((1, seq, width), lambda b, i: (b, 0, col0 + 2)),
                  pl.BlockSpec((N_MOBA_HEADS, 2 * blk, blk), lambda b, i: (0, 0, 0))],
        out_specs=pl.BlockSpec((1, blk, width), lambda b, i: (b, i, 0)),
        out_shape=jax.ShapeDtypeStruct((bsz, seq, width), BF16),
        scratch_shapes=[pltpu.VMEM((nb, width), F32),
                        pltpu.VMEM((nb, width, blk), BF16),
                        pltpu.VMEM((N_MOBA_HEADS, nb, blk), F32),
                        pltpu.VMEM((N_MOBA_HEADS, blk, blk), F32),
                        pltpu.VMEM((N_MOBA_HEADS, blk, blk), BF16),
                        pltpu.VMEM((N_MOBA_HEADS, HEAD_DIM, blk), F32)],
        compiler_params=_params(("arbitrary", "arbitrary"), 48),
        name="moba",
    )(rel_bias, qkv3, qkv3, qkv3, tbl)


def _sb_kernel(q_ref, k_ref, v_ref, o_ref, vt_sc, acc_sc):
    qi = pl.program_id(1)
    t, width = q_ref.shape[1], q_ref.shape[2]
    n_heads = width // HEAD_DIM
    nb = k_ref.shape[1] // t
    scale = HEAD_DIM ** -0.5

    @pl.when(qi == 0)
    def _():
        for n in range(nb):
            vt_sc[n] = _transposed_bf16(v_ref[0, n * t:(n + 1) * t, :])

    qt = q_ref[0].astype(F32).T
    row = lax.broadcasted_iota(I32, (LANES, t), 0)
    key = lax.broadcasted_iota(I32, (t, t), 0)
    qry = lax.broadcasted_iota(I32, (t, t), 1)
    strict = key < qry
    tri = jnp.where(qry >= key, 1.0, 0.0).astype(BF16)
    qs = [_head_queries(qt, row, hh, scale) for hh in range(n_heads)]

    def blocks(tiles):
        zs = [[_dot(k_ref[0, pl.ds(pl.multiple_of(j * t, t), t), _pair_lanes(hh)], qs[hh])
               for hh in range(n_heads)] for j, _ in tiles]
        csums = []
        for (_, diagonal), zt in zip(tiles, zs):
            row_sums = []
            for z in zt:
                sp = jnp.maximum(z, 0.0) + jnp.log(1.0 + jnp.exp(-jnp.abs(z)))
                if diagonal:
                    sp = jnp.where(strict, sp, 0.0)
                hi = sp.astype(BF16)
                lo = (sp - hi.astype(F32)).astype(BF16)
                row_sums.append(_dot(tri, hi) + _dot(tri, lo))
            csums.append(row_sums)
        out = []
        for (j, diagonal), zt, ct in zip(tiles, zs, csums):
            pvs = []
            for hh in range(n_heads):
                a = jnp.exp(zt[hh] - ct[hh])
                if diagonal:
                    a = jnp.where(strict, a, 0.0)
                pvs.append(_dot(vt_sc[j, _head_dims(hh), :], a.astype(BF16)))
            out.append((pvs, [c[0:1, :] for c in ct]))
        return out

    has_prev = qi > 0
    (pv_d, tot_d), (pv_p, tot_p) = blocks([(qi, True), (jnp.maximum(qi - 1, 0), False)])
    for hh, (d, p, c) in enumerate(zip(pv_d, pv_p, tot_d)):
        acc_sc[_head_dims(hh), :] = d + jnp.where(has_prev, p * jnp.exp(-c), 0.0)
    carries = tuple(c + jnp.where(has_prev, p, 0.0) for c, p in zip(tot_d, tot_p))

    def lowest(carries):
        return jnp.min(functools.reduce(jnp.minimum, carries))

    def cond(state):
        j, cmin, _ = state
        return (j >= 0) & (cmin < SB_ZERO_LOG)

    def body(state):
        j, _, carries = state
        (pvs, totals), = blocks([(j, False)])
        for hh, (pv, c) in enumerate(zip(pvs, carries)):
            acc_sc[_head_dims(hh), :] = acc_sc[_head_dims(hh), :] + pv * jnp.exp(-c)
        carries = tuple(c + total for c, total in zip(carries, totals))
        return j - 1, lowest(carries), carries

    lax.while_loop(cond, body, (qi - 2, lowest(carries), carries))
    o_ref[0] = acc_sc[...].T.astype(o_ref.dtype)


def _stickbreak(qkv3, col0):
    bsz, seq, _ = qkv3.shape
    width = N_SB_HEADS * HEAD_DIM
    t = ATTN_ROWS
    return pl.pallas_call(
        _sb_kernel,
        grid=(bsz, seq // t),
        in_specs=[pl.BlockSpec((1, t, width), lambda b, i: (b, i, col0)),
                  pl.BlockSpec((1, seq, width), lambda b, i: (b, 0, col0 + 1)),
                  pl.BlockSpec((1, seq, width), lambda b, i: (b, 0, col0 + 2))],
        out_specs=pl.BlockSpec((1, t, width), lambda b, i: (b, i, 0)),
        out_shape=jax.ShapeDtypeStruct((bsz, seq, width), BF16),
        scratch_shapes=[pltpu.VMEM((seq // t, width, t), BF16), pltpu.VMEM((width, t), F32)],
        compiler_params=_params(("arbitrary", "arbitrary"), 48),
        name="stickbreak",
    )(qkv3, qkv3, qkv3)


def _memkv_kernel(m_ref, w_ref, o_ref):
    o_ref[...] = _dot(m_ref[...].astype(BF16), w_ref[...]).astype(BF16)


def _memkv(mem2, w_bf16):
    n, d = mem2.shape
    width = w_bf16.shape[1]
    tm = min(n, 512)
    return pl.pallas_call(
        _memkv_kernel,
        grid=(n // tm,),
        in_specs=[pl.BlockSpec((tm, d), lambda i: (i, 0)),
                  pl.BlockSpec((d, width), lambda i: (0, 0))],
        out_specs=pl.BlockSpec((tm, width), lambda i: (i, 0)),
        out_shape=jax.ShapeDtypeStruct((n, width), BF16),
        compiler_params=_params(("arbitrary",), 32),
        name="memkv",
    )(mem2, w_bf16)


def _mem_kernel(q_ref, k_ref, v_ref, o_ref, vt_sc):
    t, width = q_ref.shape[1], q_ref.shape[2]
    n_heads = width // HEAD_DIM
    scale = HEAD_DIM ** -0.5

    @pl.when(pl.program_id(1) == 0)
    def _():
        vt_sc[...] = _transposed_bf16(v_ref[0])

    qt = q_ref[0].astype(F32).T
    row = lax.broadcasted_iota(I32, (LANES, t), 0)
    qs = [_head_queries(qt, row, hh, scale) for hh in range(n_heads)]
    scores = [_dot(k_ref[0, :, _pair_lanes(hh)], qs[hh]) for hh in range(n_heads)]
    probs = []
    for s in scores:
        e = jnp.exp(s - _fold_keys(s, jnp.maximum, jnp.max))
        probs.append((e * (1.0 / _fold_keys(e, jnp.add, jnp.sum))).astype(BF16))
    out_t = jnp.concatenate([_dot(vt_sc[_head_dims(hh), :], probs[hh]) for hh in range(n_heads)], axis=0)
    o_ref[0] = out_t.T.astype(o_ref.dtype)


def _mem_attention(qkv3, kv3, qcol):
    bsz, seq, _ = qkv3.shape
    n_mem = kv3.shape[1]
    width = N_MEM_HEADS * HEAD_DIM
    t = MEM_ROWS
    return pl.pallas_call(
        _mem_kernel,
        grid=(bsz, seq // t),
        in_specs=[pl.BlockSpec((1, t, width), lambda b, i: (b, i, qcol)),
                  pl.BlockSpec((1, n_mem, width), lambda b, i: (b, 0, 0)),
                  pl.BlockSpec((1, n_mem, width), lambda b, i: (b, 0, 1))],
        out_specs=pl.BlockSpec((1, t, width), lambda b, i: (b, i, 0)),
        out_shape=jax.ShapeDtypeStruct((bsz, seq, width), BF16),
        scratch_shapes=[pltpu.VMEM((width, n_mem), BF16)],
        compiler_params=_params(("arbitrary", "arbitrary"), 32),
        name="mem_attention",
    )(qkv3, kv3, kv3)


def _merge_kernel(ya_ref, yb_ref, ym_ref, x_ref, wg_ref, wa_ref, wb_ref, wm_ref, wo_ref,
                  lng_ref, lnb_ref, wrh_ref, br_ref, x1_ref, route_ref, tcnt_ref, h_sc, cnt_ref,
                  *, alpha, n_experts):
    step = pl.program_id(0)
    tm, d = x_ref.shape
    slot = lax.rem(step, 2)

    @pl.when(step == 0)
    def _():
        cnt_ref[...] = jnp.zeros(cnt_ref.shape, F32)
        h_sc[...] = jnp.zeros(h_sc.shape, F32)

    x = x_ref[...]
    xb = x.astype(BF16)
    branches = ((ya_ref, wa_ref), (yb_ref, wb_ref), (ym_ref, wm_ref))

    def branch_term(k):
        gate_logits = _dot(xb, wg_ref[:, k * d:(k + 1) * d])
        return gate_logits, _dot(branches[k][0][...], branches[k][1][...])

    x1 = _layer_norm(h_sc[1 - slot], lng_ref[...], lnb_ref[...])
    x1_ref[...] = x1
    xh = x1.astype(BF16)
    xl = (x1 - xh.astype(F32)).astype(BF16)

    first_term = branch_term(0)

    by_hi = _dot(xh, wrh_ref[...])
    logits = (by_hi[:, :LANES] + (by_hi[:, LANES:] + _dot(xl, wrh_ref[:, :LANES]))
              + br_ref[...])

    merged = None
    for k in range(len(branches)):
        gate_logits, y = first_term if k == 0 else branch_term(k)
        term = _sigmoid(gate_logits) * y
        merged = term if merged is None else merged + term
    h_sc[slot] = alpha * x + _dot(merged.astype(BF16), wo_ref[...])

    col = lax.broadcasted_iota(I32, (tm, LANES), 1)
    colf = col.astype(F32)
    big = float(LANES)
    gmask = (col >= n_experts) & (col < n_experts + N_GROUPS)
    lg = jnp.where(gmask, logits, -jnp.inf)
    gmax = jnp.max(lg, axis=1, keepdims=True)
    gidx = jnp.min(jnp.where(lg == gmax, colf, big), axis=1, keepdims=True) - n_experts
    g_p = 1.0 / jnp.sum(jnp.where(gmask, jnp.exp(logits - gmax), 0.0), axis=1, keepdims=True)
    lo_col = gidx * EXPERTS_PER_GROUP
    emask = (colf >= lo_col) & (colf < lo_col + EXPERTS_PER_GROUP)
    le = jnp.where(emask, logits, -jnp.inf)
    l1 = jnp.max(le, axis=1, keepdims=True)
    i1 = jnp.min(jnp.where(le == l1, colf, big), axis=1, keepdims=True)
    le2 = jnp.where(colf == i1, -jnp.inf, le)
    l2 = jnp.max(le2, axis=1, keepdims=True)
    i2 = jnp.min(jnp.where(le2 == l2, colf, big), axis=1, keepdims=True)
    e2 = jnp.exp(l2 - l1)
    gate1 = g_p * (1.0 / (1.0 + e2))
    gate2 = g_p * (e2 / (1.0 + e2))

    oh1 = colf == i1
    oh2 = colf == i2
    cnt = jnp.where((oh1 | oh2) & (step > 0), 1.0, 0.0)
    rr = lax.broadcasted_iota(I32, (tm, tm), 0)
    cc = lax.broadcasted_iota(I32, (tm, tm), 1)
    before = jnp.where(cc < rr, 1.0, 0.0).astype(BF16)
    base = _dot(before, cnt.astype(BF16)) + cnt_ref[0:1, :]
    rank1 = jnp.sum(jnp.where(oh1, base, 0.0), axis=1, keepdims=True)
    rank2 = jnp.sum(jnp.where(oh2, base, 0.0), axis=1, keepdims=True)
    tile_cnt = jnp.sum(cnt, axis=0, keepdims=True)
    cnt_ref[...] = cnt_ref[...] + tile_cnt
    tcnt_ref[0] = jnp.broadcast_to(tile_cnt, tcnt_ref.shape[1:])

    route = jnp.zeros((tm, LANES), F32)
    for k, val in enumerate((i1, i2, rank1, rank2, gate1, gate2)):
        route = jnp.where(col == k, val, route)
    route_ref[...] = route


def _merge(ya, yb, ym, x2, w_gates, wa, wb, wm, wo, ln_g, ln_b, wr_split, b_r, alpha, n_experts):
    n, d = x2.shape
    tm = MERGE_ROWS
    n_tiles = n // tm
    row_in = lambda w: pl.BlockSpec((tm, w), lambda i: (jnp.minimum(i, n_tiles - 1), 0))
    row_out = lambda w: pl.BlockSpec((tm, w), lambda i: (jnp.maximum(i - 1, 0), 0))
    full = lambda a: pl.BlockSpec(a.shape, lambda i: (0,) * a.ndim)
    return pl.pallas_call(
        functools.partial(_merge_kernel, alpha=alpha, n_experts=n_experts),
        grid=(n_tiles + 1,),
        in_specs=[row_in(ya.shape[1]), row_in(yb.shape[1]), row_in(ym.shape[1]), row_in(d),
                  full(w_gates), full(wa), full(wb), full(wm), full(wo), full(ln_g), full(ln_b),
                  full(wr_split), full(b_r)],
        out_specs=[row_out(d), row_out(LANES),
                   pl.BlockSpec((1, 8, LANES), lambda i: (jnp.maximum(i - 1, 0), 0, 0))],
        out_shape=[jax.ShapeDtypeStruct((n, d), F32),
                   jax.ShapeDtypeStruct((n, LANES), F32),
                   jax.ShapeDtypeStruct((n_tiles, 8, LANES), F32)],
        scratch_shapes=[pltpu.VMEM((2, tm, d), F32), pltpu.VMEM((8, LANES), F32)],
        compiler_params=_params(("arbitrary",), 56),
        name="merge",
    )(ya, yb, ym, x2, w_gates, wa, wb, wm, wo, ln_g, ln_b, wr_split, b_r)


def _row_copy(src_hbm, row, buf, sem, r):
    return pltpu.make_async_copy(src_hbm.at[pl.ds(row, 1), :], buf.at[pl.ds(r, 1), :], sem)


def _gather_rows(src_hbm, idx_ref, buf, sem, n_rows, unrolled):
    if unrolled:
        for r in range(n_rows):
            _row_copy(src_hbm, idx_ref[0, 0, r], buf, sem, r).start()
    else:
        def issue(r, _):
            _row_copy(src_hbm, idx_ref[0, 0, r], buf, sem, r).start()
            return 0
        lax.fori_loop(0, n_rows, issue, 0)


def _wait_rows(src_hbm, buf, sem, n_rows):
    pltpu.make_async_copy(src_hbm.at[pl.ds(0, n_rows), :], buf, sem).wait()


DISPATCH_ROWS = 512
CHUNK = 8
HIGH_HALF = 0xFFFF0000


def _pack_bf16_pairs(x):
    half = x.shape[1] // 2
    hi = lax.bitcast_convert_type(x[:, :half], U32) & jnp.uint32(HIGH_HALF)
    lo = lax.shift_right_logical(lax.bitcast_convert_type(x[:, half:], U32), jnp.uint32(16))
    return hi | lo


def _unpack_bf16_pairs(p):
    left = lax.bitcast_convert_type(p & jnp.uint32(HIGH_HALF), F32)
    right = lax.bitcast_convert_type(lax.shift_left(p, jnp.uint32(16)), F32)
    return jnp.concatenate([left, right], axis=1).astype(BF16)


def _chunk_copy(buf, xpad_hbm, sem, src_row, dst_row):
    return pltpu.make_async_copy(buf.at[pl.ds(src_row, CHUNK), :], xpad_hbm.at[pl.ds(dst_row, CHUNK), :], sem)


def _dispatch_kernel(nch_ref, off_ref, dst_ref, zrow_ref, zflag_ref, nu_ref, base_ref, route_ref, x_ref,
                     xpad_hbm, dest_ref, sorted_even, sorted_odd, zero_buf, sem, zsem):
    i = pl.program_id(0)
    n_steps = pl.num_programs(0)
    n_exp = base_ref.shape[1]
    td = x_ref.shape[0]
    rb = sorted_even.shape[0]
    zrows = zero_buf.shape[0]
    bufs = (sorted_even, sorted_odd)
    parity = lax.rem(i, 2)

    @pl.when(i == 0)
    def _():
        zero_buf[...] = jnp.zeros(zero_buf.shape, U32)

        def zero_block(row, start):
            cp = pltpu.make_async_copy(zero_buf, xpad_hbm.at[pl.ds(pl.multiple_of(row, zrows), zrows), :], zsem)
            cp.start() if start else cp.wait()

        for start in (True, False):
            for e in range(n_exp):
                pl.when(zflag_ref[e] == 1)(functools.partial(zero_block, zrow_ref[e], start))

            def unused(blk, _):
                zero_block(blk * zrows, start)
                return 0
            lax.fori_loop(nu_ref[0], xpad_hbm.shape[0] // zrows, unused, 0)

    rt = route_ref[...].T
    eid = lax.broadcasted_iota(I32, (n_exp, td), 0).astype(F32)
    rowid = lax.broadcasted_iota(I32, (rb, td), 0).astype(F32)
    hit = None
    dests = []
    for k in range(2):
        mine = eid == rt[k:k + 1, :]
        rank = rt[2 + k:3 + k, :]
        pos = jnp.sum(jnp.where(mine, base_ref[0, :, 0:1], 0.0), axis=0, keepdims=True) + rank
        dests.append(jnp.sum(jnp.where(mine, base_ref[0, :, 1:2], 0.0), axis=0, keepdims=True) + rank)
        hit = (rowid == pos) if hit is None else hit | (rowid == pos)
    dest_ref[0] = jnp.concatenate(dests + [jnp.zeros((8 - len(dests), td), F32)], axis=0).astype(I32)
    perm = jnp.where(hit, 1.0, 0.0).astype(BF16)
    rows_sorted = _pack_bf16_pairs(_dot(perm, x_ref[...].astype(BF16)))

    def runs(tile, buf, s, start):
        for e in range(n_exp):
            k = tile * n_exp + e
            src0, dst0 = off_ref[k], dst_ref[k]

            def one(j, _):
                cp = _chunk_copy(buf, xpad_hbm, s, pl.multiple_of(src0 + j * CHUNK, CHUNK),
                                 pl.multiple_of(dst0 + j * CHUNK, CHUNK))
                cp.start() if start else cp.wait()
                return 0
            lax.fori_loop(0, nch_ref[k], one, 0)

    def wait_runs(tile, buf, s):
        k_last = tile * n_exp + n_exp - 1
        total = pl.multiple_of(off_ref[k_last] + nch_ref[k_last] * CHUNK, CHUNK)
        pltpu.make_async_copy(buf.at[pl.ds(0, total), :], xpad_hbm.at[pl.ds(0, total), :], s).wait()

    def step(cur):
        bufs[cur][...] = rows_sorted

        @pl.when(i > 0)
        def _():
            wait_runs(i - 1, bufs[1 - cur], sem.at[1 - cur])

        runs(i, bufs[cur], sem.at[cur], True)

        @pl.when(i + 1 == n_steps)
        def _():
            wait_runs(i, bufs[cur], sem.at[cur])

    for cur in range(2):
        pl.when(parity == cur)(functools.partial(step, cur))


def _dispatch(x1, route, nch, off8, dst, zrow, zflag, n_used, base, n_rows_out):
    n, d = x1.shape
    td = DISPATCH_ROWS
    n_tiles = n // td
    n_exp = base.shape[1]
    rb = 2 * td + n_exp * CHUNK
    grid_spec = pltpu.PrefetchScalarGridSpec(
        num_scalar_prefetch=6,
        grid=(n_tiles,),
        in_specs=[pl.BlockSpec((1, n_exp, 2), lambda i, *_: (i, 0, 0)),
                  pl.BlockSpec((td, LANES), lambda i, *_: (i, 0)),
                  pl.BlockSpec((td, d), lambda i, *_: (i, 0))],
        out_specs=[pl.BlockSpec(memory_space=pl.ANY),
                   pl.BlockSpec((1, 8, td), lambda i, *_: (i, 0, 0))],
        scratch_shapes=[pltpu.VMEM((rb, d // 2), U32), pltpu.VMEM((rb, d // 2), U32),
                        pltpu.VMEM((MOE_BLOCK, d // 2), U32),
                        pltpu.SemaphoreType.DMA((2,)), pltpu.SemaphoreType.DMA(())])
    return pl.pallas_call(
        _dispatch_kernel,
        grid_spec=grid_spec,
        out_shape=[jax.ShapeDtypeStruct((n_rows_out, d // 2), U32),
                   jax.ShapeDtypeStruct((n_tiles, 8, td), I32)],
        compiler_params=_params(("arbitrary",), 48),
        name="dispatch",
    )(nch, off8, dst, zrow, zflag, n_used, base, route, x1)


def _experts_kernel(be_ref, nu_ref, x_hbm, wg_ref, wu_ref, wd_ref, o_ref, xbuf, sem, wg_sc, wu_sc, wd_sc):
    b = pl.program_id(0)
    n_used = nu_ref[0]
    rows = o_ref.shape[0]
    n_slots = xbuf.shape[0]

    def fetch(blk):
        slot = lax.rem(blk, n_slots)
        return pltpu.make_async_copy(x_hbm.at[pl.ds(pl.multiple_of(blk * rows, rows), rows), :],
                                     xbuf.at[slot], sem.at[slot])

    @pl.when(b == 0)
    def _():
        for blk in range(n_slots - 1):
            pl.when(blk < n_used)(lambda blk=blk: fetch(blk).start())

    @pl.when(b < n_used)
    def _():
        @pl.when(b + n_slots - 1 < n_used)
        def _():
            fetch(b + n_slots - 1).start()

        changed = (b == 0) | (be_ref[b] != be_ref[jnp.maximum(b - 1, 0)])

        @pl.when(changed)
        def _():
            wg_sc[...] = wg_ref[0].astype(BF16)
            wu_sc[...] = wu_ref[0].astype(BF16)
            wd_sc[...] = wd_ref[0].astype(BF16)

        fetch(b).wait()
        xb = _unpack_bf16_pairs(xbuf[lax.rem(b, n_slots)])
        g = _dot(xb, wg_sc[...])
        u = _dot(xb, wu_sc[...])
        h = (g * _sigmoid(g) * u).astype(BF16)
        o_ref[...] = _dot(h, wd_sc[...])

    @pl.when(b >= n_used)
    def _():
        o_ref[...] = jnp.zeros(o_ref.shape, F32)


def _experts(x_pad, block_expert, n_used, w_gate, w_up, w_down):
    d = w_gate.shape[1]
    n_blocks = block_expert.shape[0]
    rows = MOE_BLOCK
    d_exp = w_gate.shape[-1]
    grid_spec = pltpu.PrefetchScalarGridSpec(
        num_scalar_prefetch=2,
        grid=(n_blocks,),
        in_specs=[pl.BlockSpec(memory_space=pl.ANY),
                  pl.BlockSpec((1, d, d_exp), lambda b, be, nu: (be[b], 0, 0)),
                  pl.BlockSpec((1, d, d_exp), lambda b, be, nu: (be[b], 0, 0)),
                  pl.BlockSpec((1, d_exp, d), lambda b, be, nu: (be[b], 0, 0))],
        out_specs=pl.BlockSpec((rows, d), lambda b, be, nu: (b, 0)),
        scratch_shapes=[pltpu.VMEM((EXPERT_INPUT_SLOTS, rows, d // 2), U32),
                        pltpu.SemaphoreType.DMA((EXPERT_INPUT_SLOTS,)),
                        pltpu.VMEM((d, d_exp), BF16),
                        pltpu.VMEM((d, d_exp), BF16),
                        pltpu.VMEM((d_exp, d), BF16)])
    return pl.pallas_call(
        _experts_kernel,
        grid_spec=grid_spec,
        out_shape=jax.ShapeDtypeStruct((n_blocks * rows, d), F32),
        compiler_params=_params(("arbitrary",), 44),
        name="experts",
    )(block_expert, n_used, x_pad, w_gate, w_up, w_down)


def _combine_kernel(idx_ref, idxn_ref, y_hbm, x1_ref, route_ref, lng_ref, lnb_ref, o_ref,
                    ybuf_even, ybuf_odd, sem, *, alpha):
    i = pl.program_id(0)
    n_steps = pl.num_programs(0)
    tm = x1_ref.shape[0]
    bufs = (ybuf_even, ybuf_odd)
    parity = lax.rem(i, 2)

    @pl.when(i == 0)
    def _():
        _gather_rows(y_hbm, idx_ref, bufs[0], sem.at[0], 2 * tm, False)

    def compute(cur):
        _gather_rows(y_hbm, idxn_ref, bufs[1 - cur], sem.at[1 - cur], 2 * tm, True)
        _wait_rows(y_hbm, bufs[cur], sem.at[cur], 2 * tm)
        ffn = (route_ref[:, 4:5] * bufs[cur][0:tm, :] + route_ref[:, 5:6] * bufs[cur][tm:2 * tm, :])
        o_ref[...] = _layer_norm(alpha * x1_ref[...] + ffn, lng_ref[...], lnb_ref[...])

        @pl.when(i + 1 == n_steps)
        def _():
            _wait_rows(y_hbm, bufs[1 - cur], sem.at[1 - cur], 2 * tm)

    for cur in range(2):
        pl.when(parity == cur)(functools.partial(compute, cur))


def _combine(y_pad, dest, x1, route, ln_g, ln_b, alpha):
    n, d = x1.shape
    tm = COMBINE_ROWS
    n_steps = n // tm
    idx3 = dest.reshape(n_steps, tm, 2).transpose(0, 2, 1).reshape(n_steps, 1, 2 * tm)
    smem_idx = lambda f: pl.BlockSpec((1, 1, 2 * tm), f, memory_space=pltpu.SMEM)
    return pl.pallas_call(
        functools.partial(_combine_kernel, alpha=alpha),
        grid=(n_steps,),
        in_specs=[smem_idx(lambda i: (i, 0, 0)),
                  smem_idx(lambda i: (jnp.minimum(i + 1, n_steps - 1), 0, 0)),
                  pl.BlockSpec(memory_space=pl.ANY),
                  pl.BlockSpec((tm, d), lambda i: (i, 0)),
                  pl.BlockSpec((tm, LANES), lambda i: (i, 0)),
                  pl.BlockSpec((1, d), lambda i: (0, 0)),
                  pl.BlockSpec((1, d), lambda i: (0, 0))],
        out_specs=pl.BlockSpec((tm, d), lambda i: (i, 0)),
        out_shape=jax.ShapeDtypeStruct((n, d), F32),
        scratch_shapes=[pltpu.VMEM((2 * tm, d), F32), pltpu.VMEM((2 * tm, d), F32),
                        pltpu.SemaphoreType.DMA((2,))],
        compiler_params=_params(("arbitrary",), 32),
        name="combine",
    )(idx3, idx3, y_pad, x1, route, ln_g, ln_b)


def _split_bf16(w):
    hi = w.astype(BF16)
    return hi, (w - hi.astype(F32)).astype(BF16)


def _layer(x, mem, tbl, rel_bias, w_in, w_mem_kv, w_br_moba, w_br_sb, w_br_mem, w_out, ln1_g, ln1_b,
           w_rg, b_rg, w_re, b_re, w_gate, w_up, w_down, ln2_g, ln2_b, alpha):
    bsz, seq, d = x.shape
    n = bsz * seq
    n_experts = w_re.shape[1]
    moba_w, sb_w, mem_w = N_MOBA_HEADS * HEAD_DIM, N_SB_HEADS * HEAD_DIM, N_MEM_HEADS * HEAD_DIM
    n_qkv = 3 * moba_w + 3 * sb_w + mem_w
    assert w_in.shape[1] == n_qkv + 3 * d and n_experts + N_GROUPS <= LANES and moba_w == sb_w

    x2 = x.reshape(n, d)
    qkv = _proj(x2, w_in[:, :n_qkv].astype(BF16))
    qkv3 = qkv.reshape(bsz, seq, n_qkv)
    y_a = _moba(qkv3, tbl, rel_bias, 0)
    y_b = _stickbreak(qkv3, 3 * moba_w // sb_w)
    kv = _memkv(mem.reshape(-1, d), w_mem_kv.astype(BF16)).reshape(bsz, mem.shape[1], 2 * mem_w)
    y_m = _mem_attention(qkv3, kv, (3 * moba_w + 3 * sb_w) // mem_w)

    w_r = jnp.zeros((d, LANES), F32).at[:, :n_experts].set(w_re).at[:, n_experts:n_experts + N_GROUPS].set(w_rg)
    b_r = jnp.zeros((1, LANES), F32).at[0, :n_experts].set(b_re).at[0, n_experts:n_experts + N_GROUPS].set(b_rg)
    wr_split = jnp.concatenate(_split_bf16(w_r), axis=1)
    x1, route, tile_counts = _merge(
        y_a.reshape(n, moba_w), y_b.reshape(n, sb_w), y_m.reshape(n, mem_w), x2, w_in[:, n_qkv:].astype(BF16),
        w_br_moba.astype(BF16), w_br_sb.astype(BF16), w_br_mem.astype(BF16), w_out.astype(BF16),
        ln1_g.reshape(1, d), ln1_b.reshape(1, d), wr_split, b_r, alpha, n_experts)

    rows, td = MOE_BLOCK, DISPATCH_ROWS
    assert td == MERGE_ROWS
    n_tiles = n // td
    tile_cnt = tile_counts[:, 0, :n_experts].astype(I32)
    n_chunks = (tile_cnt + CHUNK - 1) // CHUNK
    run_rows = n_chunks * CHUNK
    run_rank = jnp.cumsum(tile_cnt, axis=0) - tile_cnt
    run_off = jnp.cumsum(run_rows, axis=0) - run_rows
    sorted_off = jnp.cumsum(run_rows, axis=1) - run_rows
    expert_rows = jnp.sum(run_rows, axis=0)
    padded = (expert_rows + rows - 1) // rows * rows
    pend = jnp.cumsum(padded)
    pstart = pend - padded
    run_dst = pstart[None, :] + run_off
    n_blocks = (2 * n + n_tiles * n_experts * (CHUNK - 1)) // rows + n_experts
    block_start = jnp.arange(n_blocks, dtype=I32) * rows
    block_expert = jnp.minimum(jnp.sum((pend[None, :] <= block_start[:, None]).astype(I32), axis=1),
                               n_experts - 1)
    n_used = (pend[-1] // rows).astype(I32).reshape(1)
    base = jnp.stack([sorted_off - run_rank, run_dst - run_rank], axis=-1).astype(F32)
    x_pad, dest8 = _dispatch(x1, route, n_chunks.reshape(-1), sorted_off.reshape(-1), run_dst.reshape(-1),
                             jnp.maximum(pend - rows, 0), (padded > 0).astype(I32), n_used, base,
                             n_blocks * rows)
    dest = dest8[:, 0:2, :].transpose(0, 2, 1).reshape(n, 2)

    y_pad = _experts(x_pad, block_expert, n_used, w_gate, w_up, w_down)
    out = _combine(y_pad, dest, x1, route, ln2_g.reshape(1, d), ln2_b.reshape(1, d), alpha)
    return out.reshape(bsz, seq, d)


def kernel(x, mem, w_in, w_mem_kv, rel_bias, w_br_moba, w_br_sb, w_br_mem, w_out, ln1_g, ln1_b,
           w_router_group, b_router_group, w_router_expert, b_router_expert,
           w_gate, w_up, w_down, ln2_g, ln2_b):
    depth = w_in.shape[0]
    alpha = (2.0 * depth) ** 0.25
    tbl = _bias_table(rel_bias)
    for l in range(depth):
        x = _layer(x, mem, tbl, rel_bias, w_in[l], w_mem_kv[l], w_br_moba[l], w_br_sb[l], w_br_mem[l],
                   w_out[l], ln1_g[l], ln1_b[l], w_router_group[l], b_router_group[l],
                   w_router_expert[l], b_router_expert[l], w_gate[l], w_up[l], w_down[l],
                   ln2_g[l], ln2_b[l], alpha)
    return x
```
